```python
import jax
import jax.numpy as jnp
from jax import lax
import numpy as np

D_MODEL = 1024
BATCH = 8
SEQ = 2048
DEPTH = 1

GRID_W = 64
CTX_LEN = 256
FNET_DIM = 512
FNET_GROUPS = 4
FNET_GROUP_DIM = FNET_DIM // FNET_GROUPS
HG_DIM = 512
HG_HEADS = 4
HG_HEAD_DIM = HG_DIM // HG_HEADS
HG_CHUNK = 16
D_FF = 2816
CONV_K = 3
N_MOD = 6
EPS = 1e-6

OFF_FNET = 0
OFF_Q = OFF_FNET + FNET_DIM
OFF_FF = OFF_Q + HG_DIM
OFF_FB = OFF_FF + HG_DIM
OFF_I = OFF_FB + HG_DIM
OFF_G = OFF_I + HG_DIM
OFF_GA = OFF_G + HG_DIM
OFF_GB = OFF_GA + D_MODEL
IN_DIM = OFF_GB + D_MODEL

kernel_name = 'fnet_hgrn2_convffn_hybrid_dit'


def rmsnorm(x, g):
    xf = x.astype(jnp.float32)
    y = xf * lax.rsqrt(jnp.mean(xf * xf, axis=-1, keepdims=True) + EPS) * g.astype(jnp.float32)
    return y.astype(x.dtype)


def modulate(h, shift, scale):
    return h * (1.0 + scale) + shift


def fourier_mix(u):
    b, l, _ = u.shape
    ug = u.astype(jnp.float32).reshape(b, l, FNET_GROUPS, FNET_GROUP_DIM)
    y = jnp.fft.fft2(ug, axes=(1, 3), norm='ortho').real
    return y.reshape(b, l, FNET_DIM).astype(u.dtype)


def hgrn_chunk_scan(q, k, v, logf, s0):
    b, l, h, _ = q.shape
    dv = v.shape[-1]
    n = l // HG_CHUNK

    def to_chunks(t):
        return t.reshape(b, n, HG_CHUNK, h, t.shape[-1]).transpose(1, 0, 3, 2, 4)

    qc, kc, vc, gc = to_chunks(q), to_chunks(k), to_chunks(v), to_chunks(logf)
    lower = jnp.tril(jnp.ones((HG_CHUNK, HG_CHUNK), dtype=bool))[:, :, None]

    def step(state, inp):
        qi, ki, vi, gi = inp
        cum = jnp.cumsum(gi, axis=-2)
        total = cum[..., -1:, :]
        rel = cum[..., :, None, :] - cum[..., None, :, :]
        decay = jnp.exp(jnp.where(lower, rel, -jnp.inf))
        attn = jnp.einsum('bhtd,bhsd,bhtsd->bhts', qi, ki, decay)
        o = (jnp.einsum('bhts,bhsv->bhtv', attn, vi)
             + jnp.einsum('bhtd,bhdv->bhtv', qi * jnp.exp(cum), state))
        k_dec = ki * jnp.exp(total - cum)
        new_state = state * jnp.exp(total)[..., 0, :, None] + jnp.einsum('bhsd,bhsv->bhdv', k_dec, vi)
        return new_state, o

    s_final, o = lax.scan(step, s0, (qc, kc, vc, gc))
    o = o.transpose(1, 0, 3, 2, 4).reshape(b, l, h, dv)
    return o, s_final


def hgrn_inputs(u, lb):
    uf = u.astype(jnp.float32)
    b, l, _ = u.shape

    def heads(t):
        return t.reshape(b, l, HG_HEADS, HG_HEAD_DIM)

    q = heads(jax.nn.silu(uf[..., OFF_Q:OFF_Q + HG_DIM]))
    v = heads(uf[..., OFF_I:OFF_I + HG_DIM])
    f_fwd = lb + (1.0 - lb) * jax.nn.sigmoid(uf[..., OFF_FF:OFF_FF + HG_DIM])
    f_bwd = lb + (1.0 - lb) * jax.nn.sigmoid(uf[..., OFF_FB:OFF_FB + HG_DIM])
    return (q, v, heads(1.0 - f_fwd), heads(jnp.log(f_fwd)), heads(1.0 - f_bwd), heads(jnp.log(f_bwd)))


def hgrn_bidir(q, v, k_f, g_f, k_b, g_b, s0_f, s0_b):
    o_f, s_f = hgrn_chunk_scan(q, k_f, v, g_f, s0_f)
    flip = lambda t: jnp.flip(t, axis=1)
    o_b, s_b = hgrn_chunk_scan(flip(q), flip(k_b), flip(v), flip(g_b), s0_b)
    return o_f + flip(o_b), s_f, s_b


def hgrn_readout(u, o, onorm_g):
    b, l = o.shape[0], o.shape[1]
    on = o * lax.rsqrt(jnp.mean(o * o, axis=-1, keepdims=True) + EPS)
    on = on.reshape(b, l, HG_DIM) * onorm_g.astype(jnp.float32)
    on = on * jax.nn.silu(u[..., OFF_G:OFF_G + HG_DIM].astype(jnp.float32))
    return on.astype(u.dtype)


def merge_branches(u, y_hgrn, w_a, w_b, w_out):
    y_a = fourier_mix(u[..., OFF_FNET:OFF_FNET + FNET_DIM]) @ w_a
    y_b = y_hgrn @ w_b
    g_a = jax.nn.sigmoid(u[..., OFF_GA:OFF_GB])
    g_b = jax.nn.sigmoid(u[..., OFF_GB:IN_DIM])
    return (g_a * y_a + g_b * y_b) @ w_out


def conv_ffn(h, w_up, conv_w, conv_b, w_down, rows, cols):
    b, l, _ = h.shape
    z = (h @ w_up).reshape(b, rows, cols, 2 * D_FF)
    z = lax.conv_general_dilated(z, conv_w[:, :, None, :], window_strides=(1, 1), padding='SAME',
                                 dimension_numbers=('NHWC', 'HWIO', 'NHWC'),
                                 feature_group_count=2 * D_FF)
    z = z.reshape(b, l, 2 * D_FF) + conv_b
    return (jax.nn.silu(z[..., :D_FF]) * z[..., D_FF:]) @ w_down


def setup_inputs(seed: int = 0) -> dict:
    key = jax.random.key(seed)
    ks = jax.random.split(key, 19)

    def nrm(k, shape):
        return jax.random.normal(k, shape, jnp.float32)

    def w(k, shape, fan_in, scale=1.0):
        return nrm(k, shape) * (scale * fan_in ** -0.5)

    def gain(k, shape):
        return 1.0 + 0.1 * nrm(k, shape)

    return {
        'x': nrm(ks[0], (BATCH, SEQ, D_MODEL)),
        'c': nrm(ks[1], (BATCH, D_MODEL)),
        'ctx': nrm(ks[2], (BATCH, CTX_LEN, D_MODEL)),
        'c_ctx': nrm(ks[3], (D_MODEL,)),
        'ada_w': w(ks[4], (DEPTH, D_MODEL, N_MOD * D_MODEL), D_MODEL, 0.5),
        'ada_b': 0.02 * nrm(ks[5], (DEPTH, N_MOD * D_MODEL)),
        'norm1_g': gain(ks[6], (DEPTH, D_MODEL)),
        'w_in': w(ks[7], (DEPTH, D_MODEL, IN_DIM), D_MODEL),
        'hg_lb': 0.1 * nrm(ks[8], (DEPTH + 1, HG_DIM)),
        'hg_onorm_g': gain(ks[9], (DEPTH, HG_DIM)),
        'w_a': w(ks[10], (DEPTH, FNET_DIM, D_MODEL), FNET_DIM),
        'w_b': w(ks[11], (DEPTH, HG_DIM, D_MODEL), HG_DIM),
        'w_out': w(ks[12], (DEPTH, D_MODEL, D_MODEL), D_MODEL),
        'norm2_g': gain(ks[13], (DEPTH, D_MODEL)),
        'ffn_up': w(ks[14], (DEPTH, D_MODEL, 2 * D_FF), D_MODEL),
        'ffn_conv_w': w(ks[15], (DEPTH, CONV_K, CONV_K, 2 * D_FF), CONV_K * CONV_K),
        'ffn_conv_b': 0.02 * nrm(ks[16], (DEPTH, 2 * D_FF)),
        'ffn_down': w(ks[17], (DEPTH, D_FF, D_MODEL), D_FF),
        'final_g': gain(ks[18], (D_MODEL,)),
    }


def reference(x, c, ctx, c_ctx, ada_w, ada_b, norm1_g, w_in, hg_lb, hg_onorm_g, w_a, w_b, w_out,
              norm2_g, ffn_up, ffn_conv_w, ffn_conv_b, ffn_down, final_g):
    b, seq, _ = x.shape
    rows = seq // GRID_W
    ctx_len = ctx.shape[1]
    lb_all = jnp.cumsum(jax.nn.softmax(hg_lb.astype(jnp.float32), axis=0), axis=0)
    s_zero = jnp.zeros((b, HG_HEADS, HG_HEAD_DIM, HG_HEAD_DIM), jnp.float32)
    for layer in range(DEPTH):
        last = layer == DEPTH - 1
        mx = (jax.nn.silu(c) @ ada_w[layer] + ada_b[layer]).reshape(b, 1, N_MOD, D_MODEL)
        mc = (jax.nn.silu(c_ctx) @ ada_w[layer] + ada_b[layer]).reshape(N_MOD, D_MODEL)
        lb = lb_all[layer]

        hx = modulate(rmsnorm(x, norm1_g[layer]), mx[:, :, 0], mx[:, :, 1])
        hc = modulate(rmsnorm(ctx, norm1_g[layer]), mc[0], mc[1])
        ux = hx @ w_in[layer]
        uc = hc @ w_in[layer]
        oc, sc_f, sc_b = hgrn_bidir(*hgrn_inputs(uc, lb), s_zero, s_zero)
        ox, _, _ = hgrn_bidir(*hgrn_inputs(ux, lb), sc_f, sc_b)
        yx = merge_branches(ux, hgrn_readout(ux, ox, hg_onorm_g[layer]), w_a[layer], w_b[layer], w_out[layer])
        x = x + mx[:, :, 2] * yx
        if not last:
            yc = merge_branches(uc, hgrn_readout(uc, oc, hg_onorm_g[layer]), w_a[layer], w_b[layer], w_out[layer])
            ctx = ctx + mc[2] * yc

        h2 = modulate(rmsnorm(x, norm2_g[layer]), mx[:, :, 3], mx[:, :, 4])
        x = x + mx[:, :, 5] * conv_ffn(h2, ffn_up[layer], ffn_conv_w[layer], ffn_conv_b[layer],
                                       ffn_down[layer], rows, GRID_W)
        if not last:
            h2c = modulate(rmsnorm(ctx, norm2_g[layer]), mc[3], mc[4])
            ctx = ctx + mc[5] * conv_ffn(h2c, ffn_up[layer], ffn_conv_w[layer], ffn_conv_b[layer],
                                         ffn_down[layer], 1, ctx_len)
    return rmsnorm(x, final_g)
```

```python
import functools

import numpy as np
import jax
import jax.numpy as jnp
from jax import lax
from jax.experimental import pallas as pl
from jax.experimental.pallas import tpu as pltpu

D_MODEL = 1024
GRID_W = 64
FNET_DIM = 512
FNET_GROUPS = 4
FNET_GROUP_DIM = FNET_DIM // FNET_GROUPS
HG_DIM = 512
HG_HEADS = 4
HG_HEAD_DIM = HG_DIM // HG_HEADS
D_FF = 2816
N_MOD = 6
EPS = 1e-6

COL_FNET, COL_Q, COL_FF, COL_FB, COL_I, COL_G = 0, 1, 2, 3, 4, 5
COL_GA, COL_GB = 3, 4
IN_DIM = 5 * HG_DIM + FNET_DIM + 2 * D_MODEL

HG_CHUNK = 64
HG_SUB = 16
FFN_TILE = 256

BF16 = jnp.bfloat16
F32 = jnp.float32
VMEM_LIMIT = 56 * 1024 * 1024


def _sigmoid(x):
    return 1.0 / (1.0 + jnp.exp(-x))


def _silu(x):
    return x * _sigmoid(x)


def _dot(a, b):
    return jnp.dot(a, b, preferred_element_type=F32)


def _dot_nt(a, b):
    return lax.dot_general(a, b, (((1,), (1,)), ((), ())), preferred_element_type=F32)


def _params(*sem):
    return pltpu.CompilerParams(dimension_semantics=sem, vmem_limit_bytes=VMEM_LIMIT)


def _ada_kernel(c_ref, w_ref, b_ref, o_ref):
    a = _silu(c_ref[...])
    o_ref[...] = _dot(a.astype(BF16), w_ref[...].astype(BF16)) + b_ref[...]


def _ada(rows, ada_w, ada_b):
    m, d = rows.shape
    n = ada_w.shape[1]
    tn = 1536
    return pl.pallas_call(
        _ada_kernel,
        grid=(n // tn,),
        in_specs=[pl.BlockSpec((m, d), lambda j: (0, 0)),
                  pl.BlockSpec((d, tn), lambda j: (0, j)),
                  pl.BlockSpec((1, tn), lambda j: (0, j))],
        out_specs=pl.BlockSpec((m, tn), lambda j: (0, j)),
        out_shape=jax.ShapeDtypeStruct((m, n), F32),
        compiler_params=_params("arbitrary"),
        name="ada",
    )(rows, ada_w, ada_b.reshape(1, n))


def _modulated_norm(x, g, shift, scale):
    ms = jnp.mean(x * x, axis=-1, keepdims=True)
    return (x * lax.rsqrt(ms + EPS) * g) * (1.0 + scale) + shift


def _inproj_kernel(x_ref, mod_ref, g_ref, w_ref, o_ref):
    h = _modulated_norm(x_ref[0], g_ref[...], mod_ref[0, 0:1, :], mod_ref[0, 1:2, :])
    o_ref[0] = _dot(h.astype(BF16), w_ref[...])


def _inproj(x, mod, g, w, *, tm, per_batch_mod):
    b, l, d = x.shape
    n = w.shape[1]
    mod_map = (lambda i, j: (i, 0, 0)) if per_batch_mod else (lambda i, j: (0, 0, 0))
    return pl.pallas_call(
        _inproj_kernel,
        grid=(b, l // tm),
        in_specs=[pl.BlockSpec((1, tm, d), lambda i, j: (i, j, 0)),
                  pl.BlockSpec((1, N_MOD, d), mod_map),
                  pl.BlockSpec((1, d), lambda i, j: (0, 0)),
                  pl.BlockSpec((d, n), lambda i, j: (0, 0))],
        out_specs=pl.BlockSpec((1, tm, n), lambda i, j: (i, j, 0)),
        out_shape=jax.ShapeDtypeStruct((b, l, n), F32),
        compiler_params=_params("arbitrary", "arbitrary"),
        name="inproj",
    )(x, mod, g, w)


def _hgrn_direction(q_raw, g_raw, v, lb, st_ref, base, reverse, o_ref):
    c = HG_CHUNK
    f = lb + (1.0 - lb) * _sigmoid(g_raw)
    logf = jnp.log(f)
    k = 1.0 - f
    q = _silu(q_raw)

    r_idx = lax.broadcasted_iota(jnp.int32, (c, c), 0)
    c_idx = lax.broadcasted_iota(jnp.int32, (c, c), 1)
    tri = jnp.where((c_idx >= r_idx) if reverse else (c_idx <= r_idx), 1.0, 0.0).astype(BF16)
    hi = logf.astype(BF16)
    lo = (logf - hi.astype(F32)).astype(BF16)
    cum = _dot(tri, hi) + _dot(tri, lo)
    last = 0 if reverse else c - 1
    total = cum[last:last + 1, :]

    q_in = (q * jnp.exp(cum)).astype(BF16)
    k_out = (k * jnp.exp(total - cum)).astype(BF16)
    v_b = v.astype(BF16)
    v_t = v.T
    e_tot = jnp.exp(total)

    for h in range(HG_HEADS):
        hs = slice(h * HG_HEAD_DIM, (h + 1) * HG_HEAD_DIM)
        st = st_ref[base + h]
        st_b = st.astype(BF16)
        cum_h, q_h, k_h = cum[:, hs], q[:, hs], k[:, hs]
        if o_ref is not None:
            for i in range(c // HG_SUB):
                r0 = i * HG_SUB
                rows = slice(r0, r0 + HG_SUB)
                cols = slice(r0, c) if reverse else slice(0, r0 + HG_SUB)
                mid = r0 + HG_SUB // 2 if reverse else r0 + HG_SUB // 2 - 1
                m = cum_h[mid:mid + 1, :]
                q_t = (q_h[rows] * jnp.exp(cum_h[rows] - m)).astype(BF16)
                k_t = (k_h[cols] * jnp.exp(m - cum_h[cols])).astype(BF16)
                a = _dot_nt(q_t, k_t)
                n_cols = cols.stop - cols.start
                row_g = r0 + lax.broadcasted_iota(jnp.int32, (HG_SUB, n_cols), 0)
                col_g = cols.start + lax.broadcasted_iota(jnp.int32, (HG_SUB, n_cols), 1)
                a = jnp.where((col_g >= row_g) if reverse else (col_g <= row_g), a, 0.0)
                o = _dot(a.astype(BF16), v_b[cols, hs]) + _dot_nt(q_in[rows, hs], st_b)
                o_ref[0, rows, hs] = o
        st_ref[base + h] = st * e_tot[:, hs] + _dot(v_t[hs, :].astype(BF16), k_out[:, hs])


def _hgrn_kernel(*refs, has_s0, emit_o):
    it = iter(refs)
    qf_ref, gf_ref, vf_ref, qb_ref, gb_ref, vb_ref, lb_ref = (next(it) for _ in range(7))
    s0_ref = next(it) if has_s0 else None
    of_ref = next(it) if emit_o else None
    ob_ref = next(it) if emit_o else None
    sout_ref = None if emit_o else next(it)
    st_ref = next(it)
    j = pl.program_id(1)

    @pl.when(j == 0)
    def _():
        if has_s0:
            st_ref[...] = s0_ref[0]
        else:
            st_ref[...] = jnp.zeros_like(st_ref)

    lb = lb_ref[...]
    _hgrn_direction(qf_ref[0], gf_ref[0], vf_ref[0], lb, st_ref, 0, False, of_ref)
    _hgrn_direction(qb_ref[0], gb_ref[0], vb_ref[0], lb, st_ref, HG_HEADS, True, ob_ref)

    if not emit_o:
        @pl.when(j == pl.num_programs(1) - 1)
        def _():
            sout_ref[0] = st_ref[...]


def _hgrn(u, lb, s0, *, col0, emit_o):
    b, l, _ = u.shape
    c = HG_CHUNK
    n = l // c
    blk = (1, c, HG_DIM)
    fwd = lambda col: pl.BlockSpec(blk, lambda i, j: (i, j, col))
    bwd = lambda col: pl.BlockSpec(blk, lambda i, j: (i, n - 1 - j, col))
    state_spec = pl.BlockSpec((1, 2 * HG_HEADS, HG_HEAD_DIM, HG_HEAD_DIM), lambda i, j: (i, 0, 0, 0))
    state_shape = jax.ShapeDtypeStruct((b, 2 * HG_HEADS, HG_HEAD_DIM, HG_HEAD_DIM), F32)
    in_specs = [fwd(col0), fwd(col0 + 1), fwd(col0 + 3), bwd(col0), bwd(col0 + 2), bwd(col0 + 3),
                pl.BlockSpec((1, HG_DIM), lambda i, j: (0, 0))]
    args = [u, u, u, u, u, u, lb]
    if s0 is not None:
        in_specs.append(state_spec)
        args.append(s0)
    if emit_o:
        o_shape = jax.ShapeDtypeStruct((b, l, HG_DIM), F32)
        out_shape = (o_shape, o_shape)
        out_specs = (pl.BlockSpec(blk, lambda i, j: (i, j, 0)),
                     pl.BlockSpec(blk, lambda i, j: (i, n - 1 - j, 0)))
    else:
        out_shape = state_shape
        out_specs = state_spec
    return pl.pallas_call(
        functools.partial(_hgrn_kernel, has_s0=s0 is not None, emit_o=emit_o),
        grid=(b, n),
        in_specs=in_specs,
        out_specs=out_specs,
        out_shape=out_shape,
        scratch_shapes=[pltpu.VMEM((2 * HG_HEADS, HG_HEAD_DIM, HG_HEAD_DIM), F32)],
        compiler_params=_params("arbitrary", "arbitrary"),
        name="hgrn_x" if emit_o else "hgrn_ctx",
    )(*args)


def _dft_tables(seq):
    gd = FNET_GROUP_DIM
    kc = (np.arange(gd)[:, None] * np.arange(gd)[None, :]) % gd
    ang_c = 2.0 * np.pi * kc / gd
    scale = 1.0 / np.sqrt(float(seq) * gd)
    chan = np.concatenate([np.cos(ang_c), np.sin(ang_c)], axis=1) * scale
    kl = (np.arange(seq)[:, None] * np.arange(seq)[None, :]) % seq
    ang_l = 2.0 * np.pi * kl / seq
    pos = np.concatenate([np.cos(ang_l), -np.sin(ang_l)], axis=1)
    return chan.astype(np.float32), pos.astype(np.float32)


def _fnet_kernel(u_ref, chan_ref, pos_ref, y_ref, pq_ref):
    seq = u_ref.shape[1]

    @pl.when(pl.program_id(1) == 0)
    def _():
        for g in range(FNET_GROUPS):
            gs = slice(g * FNET_GROUP_DIM, (g + 1) * FNET_GROUP_DIM)
            pq = _dot(u_ref[0, :, gs].astype(BF16), chan_ref[...])
            pq_ref[0:seq, gs] = pq[:, :FNET_GROUP_DIM].astype(BF16)
            pq_ref[seq:2 * seq, gs] = pq[:, FNET_GROUP_DIM:].astype(BF16)

    y_ref[0] = _dot(pos_ref[...], pq_ref[...]).astype(BF16)


def _fnet(u, chan, pos, *, tm):
    b, seq, _ = u.shape
    return pl.pallas_call(
        _fnet_kernel,
        grid=(b, seq // tm),
        in_specs=[pl.BlockSpec((1, seq, FNET_DIM), lambda i, j: (i, 0, COL_FNET)),
                  pl.BlockSpec((FNET_GROUP_DIM, 2 * FNET_GROUP_DIM), lambda i, j: (0, 0)),
                  pl.BlockSpec((tm, 2 * seq), lambda i, j: (j, 0))],
        out_specs=pl.BlockSpec((1, tm, FNET_DIM), lambda i, j: (i, j, 0)),
        out_shape=jax.ShapeDtypeStruct((b, seq, FNET_DIM), BF16),
        scratch_shapes=[pltpu.VMEM((2 * seq, FNET_DIM), BF16)],
        compiler_params=_params("arbitrary", "arbitrary"),
        name="fnet",
    )(u, chan, pos)


def _merge_kernel(x_ref, of_ref, ob_ref, ug_ref, uga_ref, ugb_ref, yf_ref, mod_ref, og_ref, n2g_ref,
                  wa_ref, wb_ref, wo_ref, x1_ref, h2_ref):
    o = of_ref[0] + ob_ref[0]
    parts = []
    for h in range(HG_HEADS):
        oh = o[:, h * HG_HEAD_DIM:(h + 1) * HG_HEAD_DIM]
        parts.append(oh * lax.rsqrt(jnp.mean(oh * oh, axis=-1, keepdims=True) + EPS))
    on = jnp.concatenate(parts, axis=-1) * og_ref[...] * _silu(ug_ref[0])
    y_b = _dot(on.astype(BF16), wb_ref[...])
    y_a = _dot(yf_ref[0], wa_ref[...])
    m = _sigmoid(uga_ref[0]) * y_a + _sigmoid(ugb_ref[0]) * y_b
    yx = _dot(m.astype(BF16), wo_ref[...])
    x1 = x_ref[0] + mod_ref[0, 2:3, :] * yx
    x1_ref[0] = x1
    h2 = _modulated_norm(x1, n2g_ref[...], mod_ref[0, 3:4, :], mod_ref[0, 4:5, :])
    h2_ref[0] = h2.astype(BF16)


def _merge(x, o_f, o_b, u, y_f, mod, og, n2g, w_a, w_b, w_out, *, tm):
    b, l, d = x.shape
    tok = lambda w, col: pl.BlockSpec((1, tm, w), lambda i, j: (i, j, col))
    const = lambda shape: pl.BlockSpec(shape, lambda i, j: tuple(0 for _ in shape))
    return pl.pallas_call(
        _merge_kernel,
        grid=(b, l // tm),
        in_specs=[tok(d, 0), tok(HG_DIM, 0), tok(HG_DIM, 0), tok(HG_DIM, COL_G),
                  tok(d, COL_GA), tok(d, COL_GB), tok(FNET_DIM, 0),
                  pl.BlockSpec((1, N_MOD, d), lambda i, j: (i, 0, 0)),
                  const((1, HG_DIM)), const((1, d)),
                  const((FNET_DIM, d)), const((HG_DIM, d)), const((d, d))],
        out_specs=(tok(d, 0), tok(d, 0)),
        out_shape=(jax.ShapeDtypeStruct((b, l, d), F32), jax.ShapeDtypeStruct((b, l, d), BF16)),
        compiler_params=_params("arbitrary", "arbitrary"),
        name="merge",
    )(x, o_f, o_b, u, u, u, y_f, mod, og, n2g, w_a, w_b, w_out)


def _conv_rows(z_ref, w_ref, r, col):
    acc = None
    for kh in range(3):
        row = z_ref[r + kh]
        left = jnp.where(col == 0, 0.0, pltpu.roll(row, 1, axis=0))
        right = jnp.where(col == GRID_W - 1, 0.0, pltpu.roll(row, GRID_W - 1, axis=0))
        t = (left * w_ref[3 * kh:3 * kh + 1, :] + row * w_ref[3 * kh + 1:3 * kh + 2, :]
             + right * w_ref[3 * kh + 2:3 * kh + 3, :])
        acc = t if acc is None else acc + t
    return acc


def _ffn_kernel(h_ref, up1_ref, up2_ref, cw1_ref, cw2_ref, cb1_ref, cb2_ref, dn_ref, y_ref,
                z1_ref, z2_ref, a_ref):
    t = pl.program_id(1)
    rows = z1_ref.shape[0] - 2
    tc = z1_ref.shape[2]
    h = h_ref[0]
    zero_row = jnp.zeros((GRID_W, tc), F32)
    for z_ref, up_ref in ((z1_ref, up1_ref), (z2_ref, up2_ref)):
        z_ref[0] = zero_row
        z_ref[rows + 1] = zero_row
        z_ref[1:rows + 1] = _dot(h, up_ref[...]).reshape(rows, GRID_W, tc)

    col = lax.broadcasted_iota(jnp.int32, (GRID_W, tc), 0)

    def body(r, carry):
        c1 = _conv_rows(z1_ref, cw1_ref, r, col) + cb1_ref[...]
        c2 = _conv_rows(z2_ref, cw2_ref, r, col) + cb2_ref[...]
        a_ref[r] = (_silu(c1) * c2).astype(BF16)
        return carry

    lax.fori_loop(0, rows, body, 0)
    part = _dot(a_ref[...].reshape(rows * GRID_W, tc), dn_ref[...])

    @pl.when(t == 0)
    def _():
        y_ref[0] = part

    @pl.when(t > 0)
    def _():
        y_ref[0] += part


def _ffn(h2, up, conv_w, conv_b, down):
    b, l, d = h2.shape
    rows = l // GRID_W
    tc = FFN_TILE
    nt = D_FF // tc
    return pl.pallas_call(
        _ffn_kernel,
        grid=(b, nt),
        in_specs=[pl.BlockSpec((1, l, d), lambda i, t: (i, 0, 0)),
                  pl.BlockSpec((d, tc), lambda i, t: (0, t)),
                  pl.BlockSpec((d, tc), lambda i, t: (0, nt + t)),
                  pl.BlockSpec((9, tc), lambda i, t: (0, t)),
                  pl.BlockSpec((9, tc), lambda i, t: (0, nt + t)),
                  pl.BlockSpec((1, tc), lambda i, t: (0, t)),
                  pl.BlockSpec((1, tc), lambda i, t: (0, nt + t)),
                  pl.BlockSpec((tc, d), lambda i, t: (t, 0))],
        out_specs=pl.BlockSpec((1, l, d), lambda i, t: (i, 0, 0)),
        out_shape=jax.ShapeDtypeStruct((b, l, d), F32),
        scratch_shapes=[pltpu.VMEM((rows + 2, GRID_W, tc), F32),
                        pltpu.VMEM((rows + 2, GRID_W, tc), F32),
                        pltpu.VMEM((rows, GRID_W, tc), BF16)],
        compiler_params=_params("arbitrary", "arbitrary"),
        name="ffn",
    )(h2, up, up, conv_w, conv_w, conv_b, conv_b, down)


def _final_kernel(x1_ref, y_ref, mod_ref, g_ref, o_ref):
    x = x1_ref[0] + mod_ref[0, 5:6, :] * y_ref[0]
    ms = jnp.mean(x * x, axis=-1, keepdims=True)
    o_ref[0] = x * lax.rsqrt(ms + EPS) * g_ref[...]


def _final(x1, y, mod, g, *, tm):
    b, l, d = x1.shape
    tok = pl.BlockSpec((1, tm, d), lambda i, j: (i, j, 0))
    return pl.pallas_call(
        _final_kernel,
        grid=(b, l // tm),
        in_specs=[tok, tok, pl.BlockSpec((1, N_MOD, d), lambda i, j: (i, 0, 0)),
                  pl.BlockSpec((1, d), lambda i, j: (0, 0))],
        out_specs=tok,
        out_shape=jax.ShapeDtypeStruct((b, l, d), F32),
        compiler_params=_params("arbitrary", "arbitrary"),
        name="final",
    )(x1, y, mod, g)


def kernel(x, c, ctx, c_ctx, ada_w, ada_b, norm1_g, w_in, hg_lb, hg_onorm_g, w_a, w_b, w_out,
           norm2_g, ffn_up, ffn_conv_w, ffn_conv_b, ffn_down, final_g):
    b, seq, d = x.shape
    assert ada_w.shape[0] == 1, "single-layer kernel"
    assert (d, seq % GRID_W) == (D_MODEL, 0)

    lb = jnp.cumsum(jax.nn.softmax(hg_lb.astype(F32), axis=0), axis=0)[0].reshape(1, HG_DIM)

    n_rows = 16
    rows = jnp.concatenate([c, c_ctx[None, :], jnp.zeros((n_rows - b - 1, d), F32)], axis=0)
    mod = _ada(rows, ada_w[0], ada_b[0])
    mod_x = mod[:b].reshape(b, N_MOD, d)
    mod_c = mod[b:b + 1].reshape(1, N_MOD, d)

    w_in_b = w_in[0].astype(BF16)
    g1 = norm1_g[0].reshape(1, d)
    u_x = _inproj(x, mod_x, g1, w_in_b, tm=256, per_batch_mod=True)
    w_ctx = w_in_b[:, COL_Q * HG_DIM:(COL_I + 1) * HG_DIM]
    u_c = _inproj(ctx, mod_c, g1, w_ctx, tm=256, per_batch_mod=False)

    s_ctx = _hgrn(u_c, lb, None, col0=0, emit_o=False)
    o_f, o_b = _hgrn(u_x, lb, s_ctx, col0=COL_Q, emit_o=True)

    chan, pos = _dft_tables(seq)
    y_f = _fnet(u_x, jnp.asarray(chan).astype(BF16), jnp.asarray(pos).astype(BF16), tm=512)

    x1, h2 = _merge(x, o_f, o_b, u_x, y_f, mod_x, hg_onorm_g[0].reshape(1, HG_DIM),
                    norm2_g[0].reshape(1, d), w_a[0].astype(BF16), w_b[0].astype(BF16),
                    w_out[0].astype(BF16), tm=256)

    y = _ffn(h2, ffn_up[0].astype(BF16), ffn_conv_w[0].reshape(9, 2 * D_FF),
             ffn_conv_b[0].reshape(1, 2 * D_FF), ffn_down[0].astype(BF16))
    return _final(x1, y, mod_x, final_g.reshape(1, d), tm=512)
```

```python
import functools

import numpy as np
import jax
import jax.numpy as jnp
from jax import lax
from jax.experimental import pallas as pl
from jax.experimental.pallas import tpu as pltpu

D_MODEL = 1024
GRID_W = 64
FNET_DIM = 512
FNET_GROUPS = 4
FNET_GROUP_DIM = FNET_DIM // FNET_GROUPS
HG_DIM = 512
HG_HEADS = 4
HG_HEAD_DIM = HG_DIM // HG_HEADS
D_FF = 2816
N_MOD = 6
EPS = 1e-6

COL_FNET, COL_Q, COL_FF, COL_FB, COL_I, COL_G = 0, 1, 2, 3, 4, 5

HG_BLOCK = 256
HG_CHUNK = 64
HG_SUB = 16
FFN_TILE = 256

BF16 = jnp.bfloat16
F32 = jnp.float32
VMEM_LIMIT = 56 * 1024 * 1024


def _sigmoid(x):
    return 1.0 / (1.0 + jnp.exp(-x))


def _silu(x):
    return x * _sigmoid(x)


def _dot(a, b):
    return jnp.dot(a, b, preferred_element_type=F32)


def _dot_nt(a, b):
    return lax.dot_general(a, b, (((1,), (1,)), ((), ())), preferred_element_type=F32)


def _params(*sem):
    return pltpu.CompilerParams(dimension_semantics=sem, vmem_limit_bytes=VMEM_LIMIT)


def _ada_kernel(c_ref, w_ref, b_ref, o_ref):
    a = _silu(c_ref[...])
    o_ref[...] = _dot(a.astype(BF16), w_ref[...].astype(BF16)) + b_ref[...]


def _ada(rows, ada_w, ada_b):
    m, d = rows.shape
    n = ada_w.shape[1]
    tn = 1536
    return pl.pallas_call(
        _ada_kernel,
        grid=(n // tn,),
        in_specs=[pl.BlockSpec((m, d), lambda j: (0, 0)),
                  pl.BlockSpec((d, tn), lambda j: (0, j)),
                  pl.BlockSpec((1, tn), lambda j: (0, j))],
        out_specs=pl.BlockSpec((m, tn), lambda j: (0, j)),
        out_shape=jax.ShapeDtypeStruct((m, n), F32),
        compiler_params=_params("arbitrary"),
        name="ada",
    )(rows, ada_w, ada_b.reshape(1, n))


def _modulated_norm(x, g, shift, scale):
    ms = jnp.mean(x * x, axis=-1, keepdims=True)
    return (x * lax.rsqrt(ms + EPS) * g) * (1.0 + scale) + shift


def _inproj_kernel(x_ref, mod_ref, g_ref, lb_ref, w_ref, *o_refs, hg_col0, full):
    hb = _modulated_norm(x_ref[0], g_ref[...], mod_ref[0, 0:1, :], mod_ref[0, 1:2, :]).astype(BF16)
    lb = lb_ref[...]

    def proj(col, width=HG_DIM):
        return _dot(hb, w_ref[:, col * HG_DIM:col * HG_DIM + width])

    hg_ref = o_refs[0]
    hg_ref[0, :, 0:HG_DIM] = _silu(proj(hg_col0))
    for n in (1, 2):
        f = lb + (1.0 - lb) * _sigmoid(proj(hg_col0 + n))
        hg_ref[0, :, n * HG_DIM:(n + 1) * HG_DIM] = jnp.log2(f)
    hg_ref[0, :, 3 * HG_DIM:4 * HG_DIM] = proj(hg_col0 + 3)
    if full:
        uf_ref, gate_ref = o_refs[1], o_refs[2]
        uf_ref[0] = proj(COL_FNET).astype(BF16)
        gate_ref[0, :, 0:2 * D_MODEL] = _sigmoid(proj(COL_G + 1, 2 * D_MODEL)).astype(BF16)
        gate_ref[0, :, 2 * D_MODEL:2 * D_MODEL + HG_DIM] = _silu(proj(COL_G)).astype(BF16)


def _inproj(x, mod, g, lb, w, *, tm, full):
    b, l, d = x.shape
    n = w.shape[1]
    mod_map = (lambda i, j: (i, 0, 0)) if full else (lambda i, j: (0, 0, 0))
    tok = lambda width, dtype: (pl.BlockSpec((1, tm, width), lambda i, j: (i, j, 0)),
                                jax.ShapeDtypeStruct((b, l, width), dtype))
    outs = [tok(4 * HG_DIM, F32)]
    if full:
        outs += [tok(FNET_DIM, BF16), tok(HG_DIM + 2 * D_MODEL, BF16)]
    return pl.pallas_call(
        functools.partial(_inproj_kernel, hg_col0=COL_Q if full else 0, full=full),
        grid=(b, l // tm),
        in_specs=[pl.BlockSpec((1, tm, d), lambda i, j: (i, j, 0)),
                  pl.BlockSpec((1, N_MOD, d), mod_map),
                  pl.BlockSpec((1, d), lambda i, j: (0, 0)),
                  pl.BlockSpec((1, HG_DIM), lambda i, j: (0, 0)),
                  pl.BlockSpec((d, n), lambda i, j: (0, 0))],
        out_specs=tuple(o[0] for o in outs),
        out_shape=tuple(o[1] for o in outs),
        compiler_params=_params("arbitrary", "arbitrary"),
        name="inproj_x" if full else "inproj_ctx",
    )(x, mod, g, lb, w)


def _hgrn_step(dirs, st_ref):
    c = HG_CHUNK
    r_idx = lax.broadcasted_iota(jnp.int32, (c, c), 0)
    c_idx = lax.broadcasted_iota(jnp.int32, (c, c), 1)
    heads = [slice(h * HG_HEAD_DIM, (h + 1) * HG_HEAD_DIM) for h in range(HG_HEADS)]

    cums = []
    for q, l2f, v, reverse, o_ref, row0, base in dirs:
        tri = jnp.where((c_idx >= r_idx) if reverse else (c_idx <= r_idx), 1.0, 0.0).astype(BF16)
        hi = l2f.astype(BF16)
        lo = (l2f - hi.astype(F32)).astype(BF16)
        cums.append(_dot(tri, hi) + _dot(tri, lo))

    state_in, state_upd, intra = [], [], []
    for (q, l2f, v, reverse, o_ref, row0, base), cum in zip(dirs, cums):
        k = 1.0 - jnp.exp2(l2f)
        last = 0 if reverse else c - 1
        total = cum[last:last + 1, :]
        q_in = (q * jnp.exp2(cum)).astype(BF16)
        k_out = (k * jnp.exp2(total - cum)).astype(BF16)
        e_tot = jnp.exp2(total)
        v_b = v.astype(BF16)
        v_t = v.T.astype(BF16)
        for h, hs in enumerate(heads):
            state_in.append(q_in[:, hs])
            state_upd.append((base + h, e_tot[:, hs], v_t[hs, :], k_out[:, hs]))
            if o_ref is None:
                continue
            cum_h, q_h, k_h = cum[:, hs], q[:, hs], k[:, hs]
            for i in range(c // HG_SUB):
                r0 = i * HG_SUB
                rows = slice(r0, r0 + HG_SUB)
                cols = slice(r0, c) if reverse else slice(0, r0 + HG_SUB)
                mid = r0 + HG_SUB // 2 if reverse else r0 + HG_SUB // 2 - 1
                m = cum_h[mid:mid + 1, :]
                q_t = (q_h[rows] * jnp.exp2(cum_h[rows] - m)).astype(BF16)
                k_t = (k_h[cols] * jnp.exp2(m - cum_h[cols])).astype(BF16)
                intra.append((o_ref, row0, rows, cols, hs, reverse, len(state_in) - 1,
                              q_t, k_t, v_b[cols, hs]))

    states = [st_ref[idx] for idx, _, _, _ in state_upd]
    inter = [_dot_nt(q_in_h, st.astype(BF16)) if intra else None for q_in_h, st in zip(state_in, states)]
    scores = [_dot_nt(item[7], item[8]) for item in intra]
    for (idx, e_tot_h, v_t_h, k_out_h), st in zip(state_upd, states):
        st_ref[idx] = st * e_tot_h + _dot(v_t_h, k_out_h)

    for (o_ref, row0, rows, cols, hs, reverse, chain, _, _, v_c), a in zip(intra, scores):
        n_cols = cols.stop - cols.start
        row_g = rows.start + lax.broadcasted_iota(jnp.int32, (HG_SUB, n_cols), 0)
        col_g = cols.start + lax.broadcasted_iota(jnp.int32, (HG_SUB, n_cols), 1)
        a = jnp.where((col_g >= row_g) if reverse else (col_g <= row_g), a, 0.0)
        o_ref[0, pl.ds(row0 + rows.start, HG_SUB), hs] = _dot(a.astype(BF16), v_c) + inter[chain][rows]


def _hgrn_kernel(*refs, has_s0, emit_o):
    it = iter(refs)
    qf_ref, gf_ref, vf_ref, qb_ref, gb_ref, vb_ref = (next(it) for _ in range(6))
    s0_ref = next(it) if has_s0 else None
    of_ref = next(it) if emit_o else None
    ob_ref = next(it) if emit_o else None
    sout_ref = None if emit_o else next(it)
    st_ref = next(it)
    j = pl.program_id(1)
    n_inner = qf_ref.shape[1] // HG_CHUNK

    @pl.when(j == 0)
    def _():
        if has_s0:
            st_ref[...] = s0_ref[0]
        else:
            st_ref[...] = jnp.zeros_like(st_ref)

    def body(jj, carry):
        rf = pl.multiple_of(jj * HG_CHUNK, HG_CHUNK)
        rb = pl.multiple_of((n_inner - 1 - jj) * HG_CHUNK, HG_CHUNK)
        sf, sb = pl.ds(rf, HG_CHUNK), pl.ds(rb, HG_CHUNK)
        _hgrn_step([(qf_ref[0, sf, :], gf_ref[0, sf, :], vf_ref[0, sf, :], False, of_ref, rf, 0),
                    (qb_ref[0, sb, :], gb_ref[0, sb, :], vb_ref[0, sb, :], True, ob_ref, rb, HG_HEADS)],
                   st_ref)
        return carry

    lax.fori_loop(0, n_inner, body, 0)

    if not emit_o:
        @pl.when(j == pl.num_programs(1) - 1)
        def _():
            sout_ref[0] = st_ref[...]


def _hgrn(u, s0, *, emit_o):
    b, l, _ = u.shape
    c = min(HG_BLOCK, l)
    n = l // c
    col0 = 0
    blk = (1, c, HG_DIM)
    fwd = lambda col: pl.BlockSpec(blk, lambda i, j: (i, j, col))
    bwd = lambda col: pl.BlockSpec(blk, lambda i, j: (i, n - 1 - j, col))
    state_spec = pl.BlockSpec((1, 2 * HG_HEADS, HG_HEAD_DIM, HG_HEAD_DIM), lambda i, j: (i, 0, 0, 0))
    state_shape = jax.ShapeDtypeStruct((b, 2 * HG_HEADS, HG_HEAD_DIM, HG_HEAD_DIM), F32)
    in_specs = [fwd(col0), fwd(col0 + 1), fwd(col0 + 3), bwd(col0), bwd(col0 + 2), bwd(col0 + 3)]
    args = [u, u, u, u, u, u]
    if s0 is not None:
        in_specs.append(state_spec)
        args.append(s0)
    if emit_o:
        o_shape = jax.ShapeDtypeStruct((b, l, HG_DIM), F32)
        out_shape = (o_shape, o_shape)
        out_specs = (pl.BlockSpec(blk, lambda i, j: (i, j, 0)),
                     pl.BlockSpec(blk, lambda i, j: (i, n - 1 - j, 0)))
    else:
        out_shape = state_shape
        out_specs = state_spec
    return pl.pallas_call(
        functools.partial(_hgrn_kernel, has_s0=s0 is not None, emit_o=emit_o),
        grid=(b, n),
        in_specs=in_specs,
        out_specs=out_specs,
        out_shape=out_shape,
        scratch_shapes=[pltpu.VMEM((2 * HG_HEADS, HG_HEAD_DIM, HG_HEAD_DIM), F32)],
        compiler_params=_params("arbitrary", "arbitrary"),
        name="hgrn_x" if emit_o else "hgrn_ctx",
    )(*args)


def _dft_tables(seq):
    gd = FNET_GROUP_DIM
    kc = (np.arange(gd)[:, None] * np.arange(gd)[None, :]) % gd
    ang_c = 2.0 * np.pi * kc / gd
    scale = 1.0 / np.sqrt(float(seq) * gd)
    chan = np.concatenate([np.cos(ang_c), np.sin(ang_c)], axis=1) * scale
    kl = (np.arange(seq)[:, None] * np.arange(seq)[None, :]) % seq
    ang_l = 2.0 * np.pi * kl / seq
    pos = np.concatenate([np.cos(ang_l), -np.sin(ang_l)], axis=1)
    return chan.astype(np.float32), pos.astype(np.float32)


def _fnet_kernel(u_ref, chan_ref, pos_ref, y_ref, pq_ref):
    seq = u_ref.shape[1]

    @pl.when(pl.program_id(1) == 0)
    def _():
        for g in range(FNET_GROUPS):
            gs = slice(g * FNET_GROUP_DIM, (g + 1) * FNET_GROUP_DIM)
            pq = _dot(u_ref[0, :, gs], chan_ref[...])
            pq_ref[0:seq, gs] = pq[:, :FNET_GROUP_DIM].astype(BF16)
            pq_ref[seq:2 * seq, gs] = pq[:, FNET_GROUP_DIM:].astype(BF16)

    y_ref[0] = _dot(pos_ref[...], pq_ref[...]).astype(BF16)


def _fnet(u, chan, pos, *, tm):
    b, seq, _ = u.shape
    return pl.pallas_call(
        _fnet_kernel,
        grid=(b, seq // tm),
        in_specs=[pl.BlockSpec((1, seq, FNET_DIM), lambda i, j: (i, 0, 0)),
                  pl.BlockSpec((FNET_GROUP_DIM, 2 * FNET_GROUP_DIM), lambda i, j: (0, 0)),
                  pl.BlockSpec((tm, 2 * seq), lambda i, j: (j, 0))],
        out_specs=pl.BlockSpec((1, tm, FNET_DIM), lambda i, j: (i, j, 0)),
        out_shape=jax.ShapeDtypeStruct((b, seq, FNET_DIM), BF16),
        scratch_shapes=[pltpu.VMEM((2 * seq, FNET_DIM), BF16)],
        compiler_params=_params("arbitrary", "arbitrary"),
        name="fnet",
    )(u, chan, pos)


def _merge_kernel(x_ref, of_ref, ob_ref, sg_ref, ga_ref, gb_ref, yf_ref, mod_ref, og_ref, n2g_ref,
                  wa_ref, wb_ref, wo_ref, x1_ref, h2_ref):
    o = of_ref[0] + ob_ref[0]
    parts = []
    for h in range(HG_HEADS):
        oh = o[:, h * HG_HEAD_DIM:(h + 1) * HG_HEAD_DIM]
        parts.append(oh * lax.rsqrt(jnp.mean(oh * oh, axis=-1, keepdims=True) + EPS))
    on = jnp.concatenate(parts, axis=-1) * og_ref[...] * sg_ref[0].astype(F32)
    y_b = _dot(on.astype(BF16), wb_ref[...])
    y_a = _dot(yf_ref[0], wa_ref[...])
    m = ga_ref[0].astype(F32) * y_a + gb_ref[0].astype(F32) * y_b
    yx = _dot(m.astype(BF16), wo_ref[...])
    x1 = x_ref[0] + mod_ref[0, 2:3, :] * yx
    x1_ref[0] = x1
    h2 = _modulated_norm(x1, n2g_ref[...], mod_ref[0, 3:4, :], mod_ref[0, 4:5, :])
    h2_ref[0] = h2.astype(BF16)


def _merge(x, o_f, o_b, gates, y_f, mod, og, n2g, w_a, w_b, w_out, *, tm):
    b, l, d = x.shape
    tok = lambda w, col: pl.BlockSpec((1, tm, w), lambda i, j: (i, j, col))
    const = lambda shape: pl.BlockSpec(shape, lambda i, j: tuple(0 for _ in shape))
    return pl.pallas_call(
        _merge_kernel,
        grid=(b, l // tm),
        in_specs=[tok(d, 0), tok(HG_DIM, 0), tok(HG_DIM, 0), tok(HG_DIM, 2 * d // HG_DIM),
                  tok(d, 0), tok(d, 1), tok(FNET_DIM, 0),
                  pl.BlockSpec((1, N_MOD, d), lambda i, j: (i, 0, 0)),
                  const((1, HG_DIM)), const((1, d)),
                  const((FNET_DIM, d)), const((HG_DIM, d)), const((d, d))],
        out_specs=(tok(d, 0), tok(d, 0)),
        out_shape=(jax.ShapeDtypeStruct((b, l, d), F32), jax.ShapeDtypeStruct((b, l, d), BF16)),
        compiler_params=_params("arbitrary", "arbitrary"),
        name="merge",
    )(x, o_f, o_b, gates, gates, gates, y_f, mod, og, n2g, w_a, w_b, w_out)


def _conv_tile(z_ref, wb_ref, half, s, r, lanes):
    zk = [[z_ref[s, r + kh, 8 * b:8 * b + 8, :] for b in range(8)] for kh in range(3)]

    def vertical(kw):
        taps = [wb_ref[half, 3 * kh + kw, :, lanes] for kh in range(3)]
        return [zk[0][b] * taps[0] + zk[1][b] * taps[1] + zk[2][b] * taps[2] for b in range(8)]

    v0, v1, v2 = vertical(0), vertical(1), vertical(2)
    sub = lax.broadcasted_iota(jnp.int32, (8, 128), 0)
    left = [jnp.where(sub == 0, 0.0, pltpu.roll(v0[7], 1, axis=0))] + v0[0:7]
    right = v2[1:8] + [jnp.where(sub == 7, 0.0, pltpu.roll(v2[0], 7, axis=0))]
    bias = wb_ref[half, 9, :, lanes]
    return [left[b] + v1[b] + right[b] + bias for b in range(8)]


def _ffn_kernel(h_ref, up1_ref, up2_ref, cw1_ref, cw2_ref, cb1_ref, cb2_ref, dn_ref, y_ref,
                z_ref, wb_ref, a_ref):
    t = pl.program_id(1)
    nl = a_ref.shape[0]
    rows = a_ref.shape[1]
    h = h_ref[0]
    zero_row = jnp.zeros((GRID_W, 128), F32)
    for half, (up_ref, cw_ref, cb_ref) in enumerate(((up1_ref, cw1_ref, cb1_ref),
                                                     (up2_ref, cw2_ref, cb2_ref))):
        wb_ref[half, 0:9] = jnp.broadcast_to(cw_ref[...][:, None, :], (9, 8, nl * 128))
        wb_ref[half, 9] = jnp.broadcast_to(cb_ref[...], (8, nl * 128))
        z = _dot(h, up_ref[...])
        for lt in range(nl):
            s = half * nl + lt
            z_ref[s, 0] = zero_row
            z_ref[s, rows + 1] = zero_row
            for r in range(rows):
                for a in range(8):
                    r0 = r * GRID_W + 8 * a
                    z_ref[s, r + 1, pl.ds(a, 8, stride=8), :] = z[r0:r0 + 8, lt * 128:(lt + 1) * 128]

    def body(r, carry):
        for lt in range(nl):
            lanes = slice(lt * 128, (lt + 1) * 128)
            c1 = _conv_tile(z_ref, wb_ref, 0, lt, r, lanes)
            c2 = _conv_tile(z_ref, wb_ref, 1, nl + lt, r, lanes)
            for b in range(8):
                a_ref[lt, r, pl.ds(b, 8, stride=8), :] = _silu(c1[b]) * c2[b]
        return carry

    lax.fori_loop(0, rows, body, 0)
    act = jnp.concatenate([a_ref[lt].reshape(rows * GRID_W, 128) for lt in range(nl)], axis=1)
    part = _dot(act.astype(BF16), dn_ref[...])

    @pl.when(t == 0)
    def _():
        y_ref[0] = part

    @pl.when(t > 0)
    def _():
        y_ref[0] += part


def _ffn(h2, up, conv_w, conv_b, down):
    b, l, d = h2.shape
    rows = l // GRID_W
    tc = FFN_TILE
    nl = tc // 128
    nt = D_FF // tc
    return pl.pallas_call(
        _ffn_kernel,
        grid=(b, nt),
        in_specs=[pl.BlockSpec((1, l, d), lambda i, t: (i, 0, 0)),
                  pl.BlockSpec((d, tc), lambda i, t: (0, t)),
                  pl.BlockSpec((d, tc), lambda i, t: (0, nt + t)),
                  pl.BlockSpec((9, tc), lambda i, t: (0, t)),
                  pl.BlockSpec((9, tc), lambda i, t: (0, nt + t)),
                  pl.BlockSpec((1, tc), lambda i, t: (0, t)),
                  pl.BlockSpec((1, tc), lambda i, t: (0, nt + t)),
                  pl.BlockSpec((tc, d), lambda i, t: (t, 0))],
        out_specs=pl.BlockSpec((1, l, d), lambda i, t: (i, 0, 0)),
        out_shape=jax.ShapeDtypeStruct((b, l, d), F32),
        scratch_shapes=[pltpu.VMEM((2 * nl, rows + 2, GRID_W, 128), F32),
                        pltpu.VMEM((2, 10, 8, tc), F32),
                        pltpu.VMEM((nl, rows, GRID_W, 128), F32)],
        compiler_params=_params("arbitrary", "arbitrary"),
        name="ffn",
    )(h2, up, up, conv_w, conv_w, conv_b, conv_b, down)


def _final_kernel(x1_ref, y_ref, mod_ref, g_ref, o_ref):
    x = x1_ref[0] + mod_ref[0, 5:6, :] * y_ref[0]
    ms = jnp.mean(x * x, axis=-1, keepdims=True)
    o_ref[0] = x * lax.rsqrt(ms + EPS) * g_ref[...]


def _final(x1, y, mod, g, *, tm):
    b, l, d = x1.shape
    tok = pl.BlockSpec((1, tm, d), lambda i, j: (i, j, 0))
    return pl.pallas_call(
        _final_kernel,
        grid=(b, l // tm),
        in_specs=[tok, tok, pl.BlockSpec((1, N_MOD, d), lambda i, j: (i, 0, 0)),
                  pl.BlockSpec((1, d), lambda i, j: (0, 0))],
        out_specs=tok,
        out_shape=jax.ShapeDtypeStruct((b, l, d), F32),
        compiler_params=_params("arbitrary", "arbitrary"),
        name="final",
    )(x1, y, mod, g)


def kernel(x, c, ctx, c_ctx, ada_w, ada_b, norm1_g, w_in, hg_lb, hg_onorm_g, w_a, w_b, w_out,
           norm2_g, ffn_up, ffn_conv_w, ffn_conv_b, ffn_down, final_g):
    b, seq, d = x.shape
    assert ada_w.shape[0] == 1, "single-layer kernel"
    assert (d, seq % GRID_W) == (D_MODEL, 0)

    lb = jnp.cumsum(jax.nn.softmax(hg_lb.astype(F32), axis=0), axis=0)[0].reshape(1, HG_DIM)

    n_rows = 16
    rows = jnp.concatenate([c, c_ctx[None, :], jnp.zeros((n_rows - b - 1, d), F32)], axis=0)
    mod = _ada(rows, ada_w[0], ada_b[0])
    mod_x = mod[:b].reshape(b, N_MOD, d)
    mod_c = mod[b:b + 1].reshape(1, N_MOD, d)

    w_in_b = w_in[0].astype(BF16)
    g1 = norm1_g[0].reshape(1, d)
    hg_x, u_f, gates = _inproj(x, mod_x, g1, lb, w_in_b, tm=256, full=True)
    w_ctx = w_in_b[:, COL_Q * HG_DIM:(COL_I + 1) * HG_DIM]
    (hg_c,) = _inproj(ctx, mod_c, g1, lb, w_ctx, tm=256, full=False)

    s_ctx = _hgrn(hg_c, None, emit_o=False)
    o_f, o_b = _hgrn(hg_x, s_ctx, emit_o=True)

    chan, pos = _dft_tables(seq)
    y_f = _fnet(u_f, jnp.asarray(chan).astype(BF16), jnp.asarray(pos).astype(BF16), tm=512)

    x1, h2 = _merge(x, o_f, o_b, gates, y_f, mod_x, hg_onorm_g[0].reshape(1, HG_DIM),
                    norm2_g[0].reshape(1, d), w_a[0].astype(BF16), w_b[0].astype(BF16),
                    w_out[0].astype(BF16), tm=256)

    y = _ffn(h2, ffn_up[0].astype(BF16), ffn_conv_w[0].reshape(9, 2 * D_FF),
             ffn_conv_b[0].reshape(1, 2 * D_FF), ffn_down[0].astype(BF16))
    return _final(x1, y, mod_x, final_g.reshape(1, d), tm=512)
```

```python
import functools

import numpy as np
import jax
import jax.numpy as jnp
from jax import lax
from jax.experimental import pallas as pl
from jax.experimental.pallas import tpu as pltpu

D_MODEL = 1024
GRID_W = 64
FNET_DIM = 512
FNET_GROUPS = 4
FNET_GROUP_DIM = FNET_DIM // FNET_GROUPS
HG_DIM = 512
HG_HEADS = 4
HG_HEAD_DIM = HG_DIM // HG_HEADS
D_FF = 2816
N_MOD = 6
EPS = 1e-6

COL_FNET, COL_Q, COL_FF, COL_FB, COL_I, COL_G = 0, 1, 2, 3, 4, 5

HG_BLOCK = 256
HG_CHUNK = 64
HG_SUB = 16
FFN_TILE = 256

BF16 = jnp.bfloat16
F32 = jnp.float32
VMEM_LIMIT = 56 * 1024 * 1024


def _sigmoid(x):
    return 1.0 / (1.0 + jnp.exp(-x))


def _silu(x):
    return x * _sigmoid(x)


def _dot(a, b):
    return jnp.dot(a, b, preferred_element_type=F32)


def _dot_nt(a, b):
    return lax.dot_general(a, b, (((1,), (1,)), ((), ())), preferred_element_type=F32)


def _params(*sem, flags=None):
    return pltpu.CompilerParams(dimension_semantics=sem, vmem_limit_bytes=VMEM_LIMIT, flags=flags)


def _ada_kernel(c_ref, w_ref, b_ref, o_ref):
    a = _silu(c_ref[...])
    o_ref[...] = _dot(a.astype(BF16), w_ref[...].astype(BF16)) + b_ref[...]


def _ada(rows, ada_w, ada_b):
    m, d = rows.shape
    n = ada_w.shape[1]
    tn = 1536
    return pl.pallas_call(
        _ada_kernel,
        grid=(n // tn,),
        in_specs=[pl.BlockSpec((m, d), lambda j: (0, 0)),
                  pl.BlockSpec((d, tn), lambda j: (0, j)),
                  pl.BlockSpec((1, tn), lambda j: (0, j))],
        out_specs=pl.BlockSpec((m, tn), lambda j: (0, j)),
        out_shape=jax.ShapeDtypeStruct((m, n), F32),
        compiler_params=_params("arbitrary"),
        name="ada",
    )(rows, ada_w, ada_b.reshape(1, n))


def _modulated_norm(x, g, shift, scale):
    ms = jnp.mean(x * x, axis=-1, keepdims=True)
    return (x * lax.rsqrt(ms + EPS) * g) * (1.0 + scale) + shift


def _inproj_kernel(x_ref, mod_ref, g_ref, lb_ref, w_ref, *o_refs, hg_col0, full):
    hb = _modulated_norm(x_ref[0], g_ref[...], mod_ref[0, 0:1, :], mod_ref[0, 1:2, :]).astype(BF16)
    lb = lb_ref[...]

    def proj(col, width=HG_DIM):
        return _dot(hb, w_ref[:, col * HG_DIM:col * HG_DIM + width])

    hg_ref = o_refs[0]
    hg_ref[0, :, 0:HG_DIM] = _silu(proj(hg_col0))
    for n in (1, 2):
        f = lb + (1.0 - lb) * _sigmoid(proj(hg_col0 + n))
        hg_ref[0, :, n * HG_DIM:(n + 1) * HG_DIM] = jnp.log2(f)
    hg_ref[0, :, 3 * HG_DIM:4 * HG_DIM] = proj(hg_col0 + 3)
    if full:
        uf_ref, gate_ref = o_refs[1], o_refs[2]
        uf_ref[0] = proj(COL_FNET).astype(BF16)
        gate_ref[0, :, 0:2 * D_MODEL] = _sigmoid(proj(COL_G + 1, 2 * D_MODEL)).astype(BF16)
        gate_ref[0, :, 2 * D_MODEL:2 * D_MODEL + HG_DIM] = _silu(proj(COL_G)).astype(BF16)


def _inproj(x, mod, g, lb, w, *, tm, full):
    b, l, d = x.shape
    n = w.shape[1]
    mod_map = (lambda i, j: (i, 0, 0)) if full else (lambda i, j: (0, 0, 0))
    tok = lambda width, dtype: (pl.BlockSpec((1, tm, width), lambda i, j: (i, j, 0)),
                                jax.ShapeDtypeStruct((b, l, width), dtype))
    outs = [tok(4 * HG_DIM, F32)]
    if full:
        outs += [tok(FNET_DIM, BF16), tok(HG_DIM + 2 * D_MODEL, BF16)]
    return pl.pallas_call(
        functools.partial(_inproj_kernel, hg_col0=COL_Q if full else 0, full=full),
        grid=(b, l // tm),
        in_specs=[pl.BlockSpec((1, tm, d), lambda i, j: (i, j, 0)),
                  pl.BlockSpec((1, N_MOD, d), mod_map),
                  pl.BlockSpec((1, d), lambda i, j: (0, 0)),
                  pl.BlockSpec((1, HG_DIM), lambda i, j: (0, 0)),
                  pl.BlockSpec((d, n), lambda i, j: (0, 0))],
        out_specs=tuple(o[0] for o in outs),
        out_shape=tuple(o[1] for o in outs),
        compiler_params=_params("arbitrary", "arbitrary"),
        name="inproj_x" if full else "inproj_ctx",
    )(x, mod, g, lb, w)


def _hgrn_step(dirs, st_ref):
    c = HG_CHUNK
    r_idx = lax.broadcasted_iota(jnp.int32, (c, c), 0)
    c_idx = lax.broadcasted_iota(jnp.int32, (c, c), 1)
    heads = [slice(h * HG_HEAD_DIM, (h + 1) * HG_HEAD_DIM) for h in range(HG_HEADS)]

    cums = []
    for q, l2f, v, reverse, o_ref, row0, base in dirs:
        tri = jnp.where((c_idx >= r_idx) if reverse else (c_idx <= r_idx), 1.0, 0.0).astype(BF16)
        hi = l2f.astype(BF16)
        lo = (l2f - hi.astype(F32)).astype(BF16)
        cums.append(_dot(tri, hi) + _dot(tri, lo))

    state_in, state_upd, intra = [], [], []
    for (q, l2f, v, reverse, o_ref, row0, base), cum in zip(dirs, cums):
        k = 1.0 - jnp.exp2(l2f)
        last = 0 if reverse else c - 1
        total = cum[last:last + 1, :]
        q_in = (q * jnp.exp2(cum)).astype(BF16)
        k_out = (k * jnp.exp2(total - cum)).astype(BF16)
        e_tot = jnp.exp2(total)
        v_b = v.astype(BF16)
        v_t = v.T.astype(BF16)
        for h, hs in enumerate(heads):
            state_in.append(q_in[:, hs])
            state_upd.append((base + h, e_tot[:, hs], v_t[hs, :], k_out[:, hs]))
            if o_ref is None:
                continue
            cum_h, q_h, k_h = cum[:, hs], q[:, hs], k[:, hs]
            for i in range(c // HG_SUB):
                r0 = i * HG_SUB
                rows = slice(r0, r0 + HG_SUB)
                cols = slice(r0, c) if reverse else slice(0, r0 + HG_SUB)
                mid = r0 + HG_SUB // 2 if reverse else r0 + HG_SUB // 2 - 1
                m = cum_h[mid:mid + 1, :]
                q_t = (q_h[rows] * jnp.exp2(cum_h[rows] - m)).astype(BF16)
                k_t = (k_h[cols] * jnp.exp2(m - cum_h[cols])).astype(BF16)
                intra.append((o_ref, row0, rows, cols, hs, reverse, len(state_in) - 1,
                              q_t, k_t, v_b[cols, hs]))

    states = [st_ref[idx] for idx, _, _, _ in state_upd]
    inter = [_dot_nt(q_in_h, st.astype(BF16)) if intra else None for q_in_h, st in zip(state_in, states)]
    scores = [_dot_nt(item[7], item[8]) for item in intra]
    for (idx, e_tot_h, v_t_h, k_out_h), st in zip(state_upd, states):
        st_ref[idx] = st * e_tot_h + _dot(v_t_h, k_out_h)

    for (o_ref, row0, rows, cols, hs, reverse, chain, _, _, v_c), a in zip(intra, scores):
        n_cols = cols.stop - cols.start
        row_g = rows.start + lax.broadcasted_iota(jnp.int32, (HG_SUB, n_cols), 0)
        col_g = cols.start + lax.broadcasted_iota(jnp.int32, (HG_SUB, n_cols), 1)
        a = jnp.where((col_g >= row_g) if reverse else (col_g <= row_g), a, 0.0)
        o_ref[0, pl.ds(row0 + rows.start, HG_SUB), hs] = _dot(a.astype(BF16), v_c) + inter[chain][rows]


def _hgrn_kernel(*refs, has_s0, emit_o):
    it = iter(refs)
    qf_ref, gf_ref, vf_ref, qb_ref, gb_ref, vb_ref = (next(it) for _ in range(6))
    s0_ref = next(it) if has_s0 else None
    of_ref = next(it) if emit_o else None
    ob_ref = next(it) if emit_o else None
    sout_ref = None if emit_o else next(it)
    st_ref = next(it)
    j = pl.program_id(1)
    n_inner = qf_ref.shape[1] // HG_CHUNK

    @pl.when(j == 0)
    def _():
        if has_s0:
            st_ref[...] = s0_ref[0]
        else:
            st_ref[...] = jnp.zeros_like(st_ref)

    def body(jj, carry):
        rf = pl.multiple_of(jj * HG_CHUNK, HG_CHUNK)
        rb = pl.multiple_of((n_inner - 1 - jj) * HG_CHUNK, HG_CHUNK)
        sf, sb = pl.ds(rf, HG_CHUNK), pl.ds(rb, HG_CHUNK)
        _hgrn_step([(qf_ref[0, sf, :], gf_ref[0, sf, :], vf_ref[0, sf, :], False, of_ref, rf, 0),
                    (qb_ref[0, sb, :], gb_ref[0, sb, :], vb_ref[0, sb, :], True, ob_ref, rb, HG_HEADS)],
                   st_ref)
        return carry

    lax.fori_loop(0, n_inner, body, 0)

    if not emit_o:
        @pl.when(j == pl.num_programs(1) - 1)
        def _():
            sout_ref[0] = st_ref[...]


def _hgrn(u, s0, *, emit_o):
    b, l, _ = u.shape
    c = min(HG_BLOCK, l)
    n = l // c
    col0 = 0
    blk = (1, c, HG_DIM)
    fwd = lambda col: pl.BlockSpec(blk, lambda i, j: (i, j, col))
    bwd = lambda col: pl.BlockSpec(blk, lambda i, j: (i, n - 1 - j, col))
    state_spec = pl.BlockSpec((1, 2 * HG_HEADS, HG_HEAD_DIM, HG_HEAD_DIM), lambda i, j: (i, 0, 0, 0))
    state_shape = jax.ShapeDtypeStruct((b, 2 * HG_HEADS, HG_HEAD_DIM, HG_HEAD_DIM), F32)
    in_specs = [fwd(col0), fwd(col0 + 1), fwd(col0 + 3), bwd(col0), bwd(col0 + 2), bwd(col0 + 3)]
    args = [u, u, u, u, u, u]
    if s0 is not None:
        in_specs.append(state_spec)
        args.append(s0)
    if emit_o:
        o_shape = jax.ShapeDtypeStruct((b, l, HG_DIM), F32)
        out_shape = (o_shape, o_shape)
        out_specs = (pl.BlockSpec(blk, lambda i, j: (i, j, 0)),
                     pl.BlockSpec(blk, lambda i, j: (i, n - 1 - j, 0)))
    else:
        out_shape = state_shape
        out_specs = state_spec
    return pl.pallas_call(
        functools.partial(_hgrn_kernel, has_s0=s0 is not None, emit_o=emit_o),
        grid=(b, n),
        in_specs=in_specs,
        out_specs=out_specs,
        out_shape=out_shape,
        scratch_shapes=[pltpu.VMEM((2 * HG_HEADS, HG_HEAD_DIM, HG_HEAD_DIM), F32)],
        compiler_params=_params("arbitrary", "arbitrary"),
        name="hgrn_x" if emit_o else "hgrn_ctx",
    )(*args)


def _dft_tables(seq):
    gd = FNET_GROUP_DIM
    kc = (np.arange(gd)[:, None] * np.arange(gd)[None, :]) % gd
    ang_c = 2.0 * np.pi * kc / gd
    scale = 1.0 / np.sqrt(float(seq) * gd)
    chan = np.concatenate([np.cos(ang_c), np.sin(ang_c)], axis=1) * scale
    kl = (np.arange(seq)[:, None] * np.arange(seq)[None, :]) % seq
    ang_l = 2.0 * np.pi * kl / seq
    pos = np.concatenate([np.cos(ang_l), -np.sin(ang_l)], axis=1)
    return chan.astype(np.float32), pos.astype(np.float32)


def _fnet_kernel(u_ref, chan_ref, pos_ref, y_ref, pq_ref):
    seq = u_ref.shape[1]

    @pl.when(pl.program_id(1) == 0)
    def _():
        for g in range(FNET_GROUPS):
            gs = slice(g * FNET_GROUP_DIM, (g + 1) * FNET_GROUP_DIM)
            pq = _dot(u_ref[0, :, gs], chan_ref[...])
            pq_ref[0:seq, gs] = pq[:, :FNET_GROUP_DIM].astype(BF16)
            pq_ref[seq:2 * seq, gs] = pq[:, FNET_GROUP_DIM:].astype(BF16)

    y_ref[0] = _dot(pos_ref[...], pq_ref[...]).astype(BF16)


def _fnet(u, chan, pos, *, tm):
    b, seq, _ = u.shape
    return pl.pallas_call(
        _fnet_kernel,
        grid=(b, seq // tm),
        in_specs=[pl.BlockSpec((1, seq, FNET_DIM), lambda i, j: (i, 0, 0)),
                  pl.BlockSpec((FNET_GROUP_DIM, 2 * FNET_GROUP_DIM), lambda i, j: (0, 0)),
                  pl.BlockSpec((tm, 2 * seq), lambda i, j: (j, 0))],
        out_specs=pl.BlockSpec((1, tm, FNET_DIM), lambda i, j: (i, j, 0)),
        out_shape=jax.ShapeDtypeStruct((b, seq, FNET_DIM), BF16),
        scratch_shapes=[pltpu.VMEM((2 * seq, FNET_DIM), BF16)],
        compiler_params=_params("arbitrary", "arbitrary"),
        name="fnet",
    )(u, chan, pos)


def _merge_kernel(x_ref, of_ref, ob_ref, sg_ref, ga_ref, gb_ref, yf_ref, mod_ref, og_ref, n2g_ref,
                  wa_ref, wb_ref, wo_ref, x1_ref, h2_ref):
    o = of_ref[0] + ob_ref[0]
    parts = []
    for h in range(HG_HEADS):
        oh = o[:, h * HG_HEAD_DIM:(h + 1) * HG_HEAD_DIM]
        parts.append(oh * lax.rsqrt(jnp.mean(oh * oh, axis=-1, keepdims=True) + EPS))
    on = jnp.concatenate(parts, axis=-1) * og_ref[...] * sg_ref[0].astype(F32)
    y_b = _dot(on.astype(BF16), wb_ref[...])
    y_a = _dot(yf_ref[0], wa_ref[...])
    m = ga_ref[0].astype(F32) * y_a + gb_ref[0].astype(F32) * y_b
    yx = _dot(m.astype(BF16), wo_ref[...])
    x1 = x_ref[0] + mod_ref[0, 2:3, :] * yx
    x1_ref[0] = x1
    h2 = _modulated_norm(x1, n2g_ref[...], mod_ref[0, 3:4, :], mod_ref[0, 4:5, :])
    h2_ref[0] = h2.astype(BF16)


def _merge(x, o_f, o_b, gates, y_f, mod, og, n2g, w_a, w_b, w_out, *, tm):
    b, l, d = x.shape
    tok = lambda w, col: pl.BlockSpec((1, tm, w), lambda i, j: (i, j, col))
    const = lambda shape: pl.BlockSpec(shape, lambda i, j: tuple(0 for _ in shape))
    return pl.pallas_call(
        _merge_kernel,
        grid=(b, l // tm),
        in_specs=[tok(d, 0), tok(HG_DIM, 0), tok(HG_DIM, 0), tok(HG_DIM, 2 * d // HG_DIM),
                  tok(d, 0), tok(d, 1), tok(FNET_DIM, 0),
                  pl.BlockSpec((1, N_MOD, d), lambda i, j: (i, 0, 0)),
                  const((1, HG_DIM)), const((1, d)),
                  const((FNET_DIM, d)), const((HG_DIM, d)), const((d, d))],
        out_specs=(tok(d, 0), tok(d, 0)),
        out_shape=(jax.ShapeDtypeStruct((b, l, d), F32), jax.ShapeDtypeStruct((b, l, d), BF16)),
        compiler_params=_params("arbitrary", "arbitrary"),
        name="merge",
    )(x, o_f, o_b, gates, gates, gates, y_f, mod, og, n2g, w_a, w_b, w_out)


FFN_GROUP = 8


def _ffn_permute_tokens(h_ref, hp_ref, tmp_ref):
    n_tok = tmp_ref.shape[1]

    def body(g, carry):
        tok = pl.ds(pl.multiple_of(g * n_tok, n_tok), n_tok)
        hf = h_ref[0, tok, :].astype(F32)
        for lt in range(tmp_ref.shape[0]):
            for r in range(n_tok // GRID_W):
                for a in range(8):
                    r0 = r * GRID_W + 8 * a
                    tmp_ref[lt, pl.ds(r * GRID_W + a, 8, stride=8), :] = (
                        hf[r0:r0 + 8, lt * 128:(lt + 1) * 128])
        for lt in range(tmp_ref.shape[0]):
            hp_ref[tok, lt * 128:(lt + 1) * 128] = tmp_ref[lt].astype(BF16)
        return carry

    lax.fori_loop(0, h_ref.shape[1] // n_tok, body, 0)


def _ffn_up(hp_ref, up_refs, row0, n_rows):
    h = hp_ref[row0 * GRID_W:(row0 + n_rows) * GRID_W, :]
    return [_dot(h, up_ref[...]) for up_ref in up_refs]


def _ffn_row_terms(zs, wb_ref, hk_ref, row0, n_rows):
    nl = hk_ref.shape[1] // 2
    sub = lax.broadcasted_iota(jnp.int32, (8, 128), 0)
    for half, z in enumerate(zs):
        for lt in range(nl):
            lanes = slice(lt * 128, (lt + 1) * 128)
            taps = [wb_ref[half, k, :, lanes] for k in range(9)]
            for r in range(n_rows):
                zt = [z[r * GRID_W + 8 * b:r * GRID_W + 8 * b + 8, lanes] for b in range(8)]
                left = [jnp.where(sub == 0, 0.0, pltpu.roll(zt[7], 1, axis=0))] + zt[0:7]
                right = zt[1:8] + [jnp.where(sub == 7, 0.0, pltpu.roll(zt[0], 7, axis=0))]
                for kh in range(3):
                    for b in range(8):
                        hk_ref[kh, half * nl + lt, row0 + r + 2 - kh, 8 * b:8 * b + 8, :] = (
                            left[b] * taps[3 * kh] + zt[b] * taps[3 * kh + 1] + right[b] * taps[3 * kh + 2])


def _ffn_down(hk_ref, wb_ref, a_ref, dn_ref, y_ref, row0, n_rows):
    nl = hk_ref.shape[1] // 2
    for lt in range(nl):
        lanes = slice(lt * 128, (lt + 1) * 128)
        for r in range(n_rows):
            for b in range(8):
                rows_b = slice(8 * b, 8 * b + 8)
                c = [hk_ref[0, s, row0 + r + 1, rows_b, :] + hk_ref[1, s, row0 + r + 1, rows_b, :]
                     + hk_ref[2, s, row0 + r + 1, rows_b, :] + wb_ref[half, 9, :, lanes]
                     for half, s in ((0, lt), (1, nl + lt))]
                a_ref[(row0 + r) * GRID_W + 8 * b:(row0 + r) * GRID_W + 8 * b + 8, lanes] = (
                    _silu(c[0]) * c[1])
    tok = slice(row0 * GRID_W, (row0 + n_rows) * GRID_W)
    part = _dot(a_ref[tok, :].astype(BF16), dn_ref[...])
    for j in range(y_ref.shape[1]):
        y_ref[0, j, tok, :] += part[:, j * 128:(j + 1) * 128]


def _ffn_kernel(h_ref, up1_ref, up2_ref, cw1_ref, cw2_ref, cb1_ref, cb2_ref, dn_ref, y_ref,
                hp_ref, hk_ref, a_ref, wb_ref, tmp_ref):
    t = pl.program_id(1)
    nl = hk_ref.shape[1] // 2
    rows = hk_ref.shape[2] - 2
    grp = FFN_GROUP

    @pl.when(t == 0)
    def _():
        _ffn_permute_tokens(h_ref, hp_ref, tmp_ref)
        y_ref[...] = jnp.zeros_like(y_ref)
        zero_row = jnp.zeros((GRID_W, 128), F32)
        for s in range(2 * nl):
            hk_ref[0, s, 1] = zero_row
            hk_ref[2, s, rows] = zero_row

    for half, (cw_ref, cb_ref) in enumerate(((cw1_ref, cb1_ref), (cw2_ref, cb2_ref))):
        wb_ref[half, 0:9] = jnp.broadcast_to(cw_ref[...][:, None, :], (9, 8, nl * 128))
        wb_ref[half, 9] = jnp.broadcast_to(cb_ref[...], (8, nl * 128))

    z_next = _ffn_up(hp_ref, (up1_ref, up2_ref), 0, grp)
    for row0 in range(0, rows, grp):
        z_cur = z_next
        if row0 + grp < rows:
            z_next = _ffn_up(hp_ref, (up1_ref, up2_ref), row0 + grp, grp)
        _ffn_row_terms(z_cur, wb_ref, hk_ref, row0, grp)
        if row0 > 0:
            _ffn_down(hk_ref, wb_ref, a_ref, dn_ref, y_ref, row0 - grp, grp)
    _ffn_down(hk_ref, wb_ref, a_ref, dn_ref, y_ref, rows - grp, grp)


def _ffn(h2, up, conv_w, conv_b, down):
    b, l, d = h2.shape
    rows = l // GRID_W
    tc = FFN_TILE
    nl = tc // 128
    nt = D_FF // tc
    assert rows % FFN_GROUP == 0
    return pl.pallas_call(
        _ffn_kernel,
        grid=(b, nt),
        in_specs=[pl.BlockSpec((1, l, d), lambda i, t: (i, 0, 0)),
                  pl.BlockSpec((d, tc), lambda i, t: (0, t)),
                  pl.BlockSpec((d, tc), lambda i, t: (0, nt + t)),
                  pl.BlockSpec((9, tc), lambda i, t: (0, t)),
                  pl.BlockSpec((9, tc), lambda i, t: (0, nt + t)),
                  pl.BlockSpec((1, tc), lambda i, t: (0, t)),
                  pl.BlockSpec((1, tc), lambda i, t: (0, nt + t)),
                  pl.BlockSpec((tc, d), lambda i, t: (t, 0))],
        out_specs=pl.BlockSpec((1, d // 128, l, 128), lambda i, t: (i, 0, 0, 0)),
        out_shape=jax.ShapeDtypeStruct((b, d // 128, l, 128), F32),
        scratch_shapes=[pltpu.VMEM((l, d), BF16),
                        pltpu.VMEM((3, 2 * nl, rows + 2, GRID_W, 128), F32),
                        pltpu.VMEM((l, tc), F32),
                        pltpu.VMEM((2, 10, 8, tc), F32),
                        pltpu.VMEM((d // 128, FFN_GROUP * GRID_W, 128), F32)],
        compiler_params=_params("arbitrary", "arbitrary"),
        name="ffn",
    )(h2, up, up, conv_w, conv_w, conv_b, conv_b, down)


def _final_kernel(x1_ref, y_ref, mod_ref, g_ref, o_ref):
    n_slab = y_ref.shape[1]
    for r in range(y_ref.shape[2] // GRID_W):
        for a in range(8):
            tok = slice(r * GRID_W + 8 * a, r * GRID_W + 8 * a + 8)
            y = jnp.concatenate([y_ref[0, j, pl.ds(r * GRID_W + a, 8, stride=8), :]
                                 for j in range(n_slab)], axis=-1)
            x = x1_ref[0, tok, :] + mod_ref[0, 5:6, :] * y
            ms = jnp.mean(x * x, axis=-1, keepdims=True)
            o_ref[0, tok, :] = x * lax.rsqrt(ms + EPS) * g_ref[...]


def _final(x1, y, mod, g, *, tm):
    b, l, d = x1.shape
    tok = pl.BlockSpec((1, tm, d), lambda i, j: (i, j, 0))
    return pl.pallas_call(
        _final_kernel,
        grid=(b, l // tm),
        in_specs=[tok, pl.BlockSpec((1, d // 128, tm, 128), lambda i, j: (i, 0, j, 0)),
                  pl.BlockSpec((1, N_MOD, d), lambda i, j: (i, 0, 0)),
                  pl.BlockSpec((1, d), lambda i, j: (0, 0))],
        out_specs=tok,
        out_shape=jax.ShapeDtypeStruct((b, l, d), F32),
        compiler_params=_params("arbitrary", "arbitrary"),
        name="final",
    )(x1, y, mod, g)


def kernel(x, c, ctx, c_ctx, ada_w, ada_b, norm1_g, w_in, hg_lb, hg_onorm_g, w_a, w_b, w_out,
           norm2_g, ffn_up, ffn_conv_w, ffn_conv_b, ffn_down, final_g):
    b, seq, d = x.shape
    assert ada_w.shape[0] == 1, "single-layer kernel"
    assert (d, seq % GRID_W) == (D_MODEL, 0)

    lb = jnp.cumsum(jax.nn.softmax(hg_lb.astype(F32), axis=0), axis=0)[0].reshape(1, HG_DIM)

    n_rows = 16
    rows = jnp.concatenate([c, c_ctx[None, :], jnp.zeros((n_rows - b - 1, d), F32)], axis=0)
    mod = _ada(rows, ada_w[0], ada_b[0])
    mod_x = mod[:b].reshape(b, N_MOD, d)
    mod_c = mod[b:b + 1].reshape(1, N_MOD, d)

    w_in_b = w_in[0].astype(BF16)
    g1 = norm1_g[0].reshape(1, d)
    hg_x, u_f, gates = _inproj(x, mod_x, g1, lb, w_in_b, tm=256, full=True)
    w_ctx = w_in_b[:, COL_Q * HG_DIM:(COL_I + 1) * HG_DIM]
    (hg_c,) = _inproj(ctx, mod_c, g1, lb, w_ctx, tm=256, full=False)

    s_ctx = _hgrn(hg_c, None, emit_o=False)
    o_f, o_b = _hgrn(hg_x, s_ctx, emit_o=True)

    chan, pos = _dft_tables(seq)
    y_f = _fnet(u_f, jnp.asarray(chan).astype(BF16), jnp.asarray(pos).astype(BF16), tm=512)

    x1, h2 = _merge(x, o_f, o_b, gates, y_f, mod_x, hg_onorm_g[0].reshape(1, HG_DIM),
                    norm2_g[0].reshape(1, d), w_a[0].astype(BF16), w_b[0].astype(BF16),
                    w_out[0].astype(BF16), tm=256)

    y = _ffn(h2, ffn_up[0].astype(BF16), ffn_conv_w[0].reshape(9, 2 * D_FF),
             ffn_conv_b[0].reshape(1, 2 * D_FF), ffn_down[0].astype(BF16))
    return _final(x1, y, mod_x, final_g.reshape(1, d), tm=512)
```

```python
import functools

import numpy as np
import jax
import jax.numpy as jnp
from jax import lax
from jax.experimental import pallas as pl
from jax.experimental.pallas import tpu as pltpu

D_MODEL = 1024
GRID_W = 64
FNET_DIM = 512
FNET_GROUPS = 4
FNET_GROUP_DIM = FNET_DIM // FNET_GROUPS
HG_DIM = 512
HG_HEADS = 4
HG_HEAD_DIM = HG_DIM // HG_HEADS
D_FF = 2816
N_MOD = 6
EPS = 1e-6

COL_FNET, COL_Q, COL_FF, COL_FB, COL_I, COL_G = 0, 1, 2, 3, 4, 5

HG_BLOCK = 256
HG_CHUNK = 64
HG_UNROLL = 2
HG_SUB = 16
FFN_TILE = 256

BF16 = jnp.bfloat16
F32 = jnp.float32
VMEM_LIMIT = 56 * 1024 * 1024


def _sigmoid(x):
    return 1.0 / (1.0 + jnp.exp(-x))


def _silu(x):
    return x * _sigmoid(x)


def _dot(a, b):
    return jnp.dot(a, b, preferred_element_type=F32)


def _dot_nt(a, b):
    return lax.dot_general(a, b, (((1,), (1,)), ((), ())), preferred_element_type=F32)


def _params(*sem, flags=None):
    return pltpu.CompilerParams(dimension_semantics=sem, vmem_limit_bytes=VMEM_LIMIT, flags=flags)


def _ada_kernel(c_ref, w_ref, b_ref, o_ref):
    a = _silu(c_ref[...])
    o_ref[...] = _dot(a.astype(BF16), w_ref[...].astype(BF16)) + b_ref[...]


def _ada(rows, ada_w, ada_b):
    m, d = rows.shape
    n = ada_w.shape[1]
    tn = 1536
    return pl.pallas_call(
        _ada_kernel,
        grid=(n // tn,),
        in_specs=[pl.BlockSpec((m, d), lambda j: (0, 0)),
                  pl.BlockSpec((d, tn), lambda j: (0, j)),
                  pl.BlockSpec((1, tn), lambda j: (0, j))],
        out_specs=pl.BlockSpec((m, tn), lambda j: (0, j)),
        out_shape=jax.ShapeDtypeStruct((m, n), F32),
        compiler_params=_params("arbitrary"),
        name="ada",
    )(rows, ada_w, ada_b.reshape(1, n))


def _modulated_norm(x, g, shift, scale):
    ms = jnp.mean(x * x, axis=-1, keepdims=True)
    return (x * lax.rsqrt(ms + EPS) * g) * (1.0 + scale) + shift


def _inproj_kernel(x_ref, mod_ref, g_ref, lb_ref, w_ref, *o_refs, hg_col0, full):
    hb = _modulated_norm(x_ref[0], g_ref[...], mod_ref[0, 0:1, :], mod_ref[0, 1:2, :]).astype(BF16)
    lb = lb_ref[...]

    def proj(col, width=HG_DIM):
        return _dot(hb, w_ref[:, col * HG_DIM:col * HG_DIM + width])

    hg_ref = o_refs[0]
    hg_ref[0, :, 0:HG_DIM] = _silu(proj(hg_col0))
    for n in (1, 2):
        f = lb + (1.0 - lb) * _sigmoid(proj(hg_col0 + n))
        hg_ref[0, :, n * HG_DIM:(n + 1) * HG_DIM] = jnp.log2(f)
    hg_ref[0, :, 3 * HG_DIM:4 * HG_DIM] = proj(hg_col0 + 3)
    if full:
        uf_ref, gate_ref = o_refs[1], o_refs[2]
        uf_ref[0] = proj(COL_FNET).astype(BF16)
        gate_ref[0, :, 0:2 * D_MODEL] = _sigmoid(proj(COL_G + 1, 2 * D_MODEL)).astype(BF16)
        gate_ref[0, :, 2 * D_MODEL:2 * D_MODEL + HG_DIM] = _silu(proj(COL_G)).astype(BF16)


def _inproj(x, mod, g, lb, w, *, tm, full):
    b, l, d = x.shape
    n = w.shape[1]
    mod_map = (lambda i, j: (i, 0, 0)) if full else (lambda i, j: (0, 0, 0))
    tok = lambda width, dtype: (pl.BlockSpec((1, tm, width), lambda i, j: (i, j, 0)),
                                jax.ShapeDtypeStruct((b, l, width), dtype))
    outs = [tok(4 * HG_DIM, F32)]
    if full:
        outs += [tok(FNET_DIM, BF16), tok(HG_DIM + 2 * D_MODEL, BF16)]
    return pl.pallas_call(
        functools.partial(_inproj_kernel, hg_col0=COL_Q if full else 0, full=full),
        grid=(b, l // tm),
        in_specs=[pl.BlockSpec((1, tm, d), lambda i, j: (i, j, 0)),
                  pl.BlockSpec((1, N_MOD, d), mod_map),
                  pl.BlockSpec((1, d), lambda i, j: (0, 0)),
                  pl.BlockSpec((1, HG_DIM), lambda i, j: (0, 0)),
                  pl.BlockSpec((d, n), lambda i, j: (0, 0))],
        out_specs=tuple(o[0] for o in outs),
        out_shape=tuple(o[1] for o in outs),
        compiler_params=_params("arbitrary", "arbitrary"),
        name="inproj_x" if full else "inproj_ctx",
    )(x, mod, g, lb, w)


def _hgrn_step(dirs, st_ref):
    c = HG_CHUNK
    r_idx = lax.broadcasted_iota(jnp.int32, (c, c), 0)
    c_idx = lax.broadcasted_iota(jnp.int32, (c, c), 1)
    heads = [slice(h * HG_HEAD_DIM, (h + 1) * HG_HEAD_DIM) for h in range(HG_HEADS)]

    cums = []
    for q, l2f, v, reverse, o_ref, row0, base in dirs:
        tri = jnp.where((c_idx >= r_idx) if reverse else (c_idx <= r_idx), 1.0, 0.0).astype(BF16)
        hi = l2f.astype(BF16)
        lo = (l2f - hi.astype(F32)).astype(BF16)
        cums.append(_dot(tri, hi) + _dot(tri, lo))

    state_in, state_upd, intra = [], [], []
    for (q, l2f, v, reverse, o_ref, row0, base), cum in zip(dirs, cums):
        k = 1.0 - jnp.exp2(l2f)
        last = 0 if reverse else c - 1
        total = cum[last:last + 1, :]
        q_in = (q * jnp.exp2(cum)).astype(BF16)
        k_out = (k * jnp.exp2(total - cum)).astype(BF16)
        e_tot = jnp.exp2(total)
        v_b = v.astype(BF16)
        v_t = v.T.astype(BF16)
        for h, hs in enumerate(heads):
            state_in.append(q_in[:, hs])
            state_upd.append((base + h, e_tot[:, hs], v_t[hs, :], k_out[:, hs]))
            if o_ref is None:
                continue
            cum_h, q_h, k_h = cum[:, hs], q[:, hs], k[:, hs]
            for i in range(c // HG_SUB):
                r0 = i * HG_SUB
                rows = slice(r0, r0 + HG_SUB)
                cols = slice(r0, c) if reverse else slice(0, r0 + HG_SUB)
                mid = r0 + HG_SUB // 2 if reverse else r0 + HG_SUB // 2 - 1
                m = cum_h[mid:mid + 1, :]
                q_t = (q_h[rows] * jnp.exp2(cum_h[rows] - m)).astype(BF16)
                k_t = (k_h[cols] * jnp.exp2(m - cum_h[cols])).astype(BF16)
                intra.append((o_ref, row0, rows, cols, hs, reverse, len(state_in) - 1,
                              q_t, k_t, v_b[cols, hs]))

    scores = [_dot_nt(item[7], item[8]) for item in intra]
    kv = [_dot(v_t_h, k_out_h) for _, _, v_t_h, k_out_h in state_upd]
    states = {}
    inter = []
    for n, (idx, e_tot_h, _, _) in enumerate(state_upd):
        st = states[idx] if idx in states else st_ref[idx]
        inter.append(_dot_nt(state_in[n], st.astype(BF16)) if intra else None)
        states[idx] = st * e_tot_h + kv[n]
    for idx, st in states.items():
        st_ref[idx] = st

    for (o_ref, row0, rows, cols, hs, reverse, chain, _, _, v_c), a in zip(intra, scores):
        n_cols = cols.stop - cols.start
        row_g = rows.start + lax.broadcasted_iota(jnp.int32, (HG_SUB, n_cols), 0)
        col_g = cols.start + lax.broadcasted_iota(jnp.int32, (HG_SUB, n_cols), 1)
        a = jnp.where((col_g >= row_g) if reverse else (col_g <= row_g), a, 0.0)
        o_ref[0, pl.ds(row0 + rows.start, HG_SUB), hs] = _dot(a.astype(BF16), v_c) + inter[chain][rows]


def _hgrn_kernel(*refs, has_s0, emit_o):
    it = iter(refs)
    qf_ref, gf_ref, vf_ref, qb_ref, gb_ref, vb_ref = (next(it) for _ in range(6))
    s0_ref = next(it) if has_s0 else None
    of_ref = next(it) if emit_o else None
    ob_ref = next(it) if emit_o else None
    sout_ref = None if emit_o else next(it)
    st_ref = next(it)
    j = pl.program_id(1)
    n_inner = qf_ref.shape[1] // HG_CHUNK

    @pl.when(j == 0)
    def _():
        if has_s0:
            st_ref[...] = s0_ref[0]
        else:
            st_ref[...] = jnp.zeros_like(st_ref)

    span = HG_UNROLL * HG_CHUNK

    def body(it, carry):
        dirs = []
        for jj in range(HG_UNROLL):
            rf = pl.multiple_of(it * span, span) + jj * HG_CHUNK
            rb = pl.multiple_of((n_inner // HG_UNROLL - 1 - it) * span, span) + (HG_UNROLL - 1 - jj) * HG_CHUNK
            sf, sb = pl.ds(rf, HG_CHUNK), pl.ds(rb, HG_CHUNK)
            dirs.append((qf_ref[0, sf, :], gf_ref[0, sf, :], vf_ref[0, sf, :], False, of_ref, rf, 0))
            dirs.append((qb_ref[0, sb, :], gb_ref[0, sb, :], vb_ref[0, sb, :], True, ob_ref, rb, HG_HEADS))
        _hgrn_step(dirs, st_ref)
        return carry

    lax.fori_loop(0, n_inner // HG_UNROLL, body, 0)

    if not emit_o:
        @pl.when(j == pl.num_programs(1) - 1)
        def _():
            sout_ref[0] = st_ref[...]


def _hgrn(u, s0, *, emit_o):
    b, l, _ = u.shape
    c = min(HG_BLOCK, l)
    n = l // c
    col0 = 0
    blk = (1, c, HG_DIM)
    fwd = lambda col: pl.BlockSpec(blk, lambda i, j: (i, j, col))
    bwd = lambda col: pl.BlockSpec(blk, lambda i, j: (i, n - 1 - j, col))
    state_spec = pl.BlockSpec((1, 2 * HG_HEADS, HG_HEAD_DIM, HG_HEAD_DIM), lambda i, j: (i, 0, 0, 0))
    state_shape = jax.ShapeDtypeStruct((b, 2 * HG_HEADS, HG_HEAD_DIM, HG_HEAD_DIM), F32)
    in_specs = [fwd(col0), fwd(col0 + 1), fwd(col0 + 3), bwd(col0), bwd(col0 + 2), bwd(col0 + 3)]
    args = [u, u, u, u, u, u]
    if s0 is not None:
        in_specs.append(state_spec)
        args.append(s0)
    if emit_o:
        o_shape = jax.ShapeDtypeStruct((b, l, HG_DIM), F32)
        out_shape = (o_shape, o_shape)
        out_specs = (pl.BlockSpec(blk, lambda i, j: (i, j, 0)),
                     pl.BlockSpec(blk, lambda i, j: (i, n - 1 - j, 0)))
    else:
        out_shape = state_shape
        out_specs = state_spec
    return pl.pallas_call(
        functools.partial(_hgrn_kernel, has_s0=s0 is not None, emit_o=emit_o),
        grid=(b, n),
        in_specs=in_specs,
        out_specs=out_specs,
        out_shape=out_shape,
        scratch_shapes=[pltpu.VMEM((2 * HG_HEADS, HG_HEAD_DIM, HG_HEAD_DIM), F32)],
        compiler_params=_params("arbitrary", "arbitrary"),
        name="hgrn_x" if emit_o else "hgrn_ctx",
    )(*args)


def _dft_tables(seq):
    gd = FNET_GROUP_DIM
    kc = (np.arange(gd)[:, None] * np.arange(gd)[None, :]) % gd
    ang_c = 2.0 * np.pi * kc / gd
    scale = 1.0 / np.sqrt(float(seq) * gd)
    chan = np.concatenate([np.cos(ang_c), np.sin(ang_c)], axis=1) * scale
    kl = (np.arange(seq)[:, None] * np.arange(seq)[None, :]) % seq
    ang_l = 2.0 * np.pi * kl / seq
    pos = np.concatenate([np.cos(ang_l), -np.sin(ang_l)], axis=1)
    return chan.astype(np.float32), pos.astype(np.float32)


def _fnet_kernel(u_ref, chan_ref, pos_ref, y_ref, pq_ref):
    seq = u_ref.shape[1]

    @pl.when(pl.program_id(1) == 0)
    def _():
        for g in range(FNET_GROUPS):
            gs = slice(g * FNET_GROUP_DIM, (g + 1) * FNET_GROUP_DIM)
            pq = _dot(u_ref[0, :, gs], chan_ref[...])
            pq_ref[0:seq, gs] = pq[:, :FNET_GROUP_DIM].astype(BF16)
            pq_ref[seq:2 * seq, gs] = pq[:, FNET_GROUP_DIM:].astype(BF16)

    y_ref[0] = _dot(pos_ref[...], pq_ref[...]).astype(BF16)


def _fnet(u, chan, pos, *, tm):
    b, seq, _ = u.shape
    return pl.pallas_call(
        _fnet_kernel,
        grid=(b, seq // tm),
        in_specs=[pl.BlockSpec((1, seq, FNET_DIM), lambda i, j: (i, 0, 0)),
                  pl.BlockSpec((FNET_GROUP_DIM, 2 * FNET_GROUP_DIM), lambda i, j: (0, 0)),
                  pl.BlockSpec((tm, 2 * seq), lambda i, j: (j, 0))],
        out_specs=pl.BlockSpec((1, tm, FNET_DIM), lambda i, j: (i, j, 0)),
        out_shape=jax.ShapeDtypeStruct((b, seq, FNET_DIM), BF16),
        scratch_shapes=[pltpu.VMEM((2 * seq, FNET_DIM), BF16)],
        compiler_params=_params("arbitrary", "arbitrary"),
        name="fnet",
    )(u, chan, pos)


def _merge_kernel(x_ref, of_ref, ob_ref, sg_ref, ga_ref, gb_ref, yf_ref, mod_ref, og_ref, n2g_ref,
                  wa_ref, wb_ref, wo_ref, x1_ref, h2_ref):
    o = of_ref[0] + ob_ref[0]
    parts = []
    for h in range(HG_HEADS):
        oh = o[:, h * HG_HEAD_DIM:(h + 1) * HG_HEAD_DIM]
        parts.append(oh * lax.rsqrt(jnp.mean(oh * oh, axis=-1, keepdims=True) + EPS))
    on = jnp.concatenate(parts, axis=-1) * og_ref[...] * sg_ref[0].astype(F32)
    y_b = _dot(on.astype(BF16), wb_ref[...])
    y_a = _dot(yf_ref[0], wa_ref[...])
    m = ga_ref[0].astype(F32) * y_a + gb_ref[0].astype(F32) * y_b
    yx = _dot(m.astype(BF16), wo_ref[...])
    x1 = x_ref[0] + mod_ref[0, 2:3, :] * yx
    x1_ref[0] = x1
    h2 = _modulated_norm(x1, n2g_ref[...], mod_ref[0, 3:4, :], mod_ref[0, 4:5, :])
    h2_ref[0] = h2.astype(BF16)


def _merge(x, o_f, o_b, gates, y_f, mod, og, n2g, w_a, w_b, w_out, *, tm):
    b, l, d = x.shape
    tok = lambda w, col: pl.BlockSpec((1, tm, w), lambda i, j: (i, j, col))
    const = lambda shape: pl.BlockSpec(shape, lambda i, j: tuple(0 for _ in shape))
    return pl.pallas_call(
        _merge_kernel,
        grid=(b, l // tm),
        in_specs=[tok(d, 0), tok(HG_DIM, 0), tok(HG_DIM, 0), tok(HG_DIM, 2 * d // HG_DIM),
                  tok(d, 0), tok(d, 1), tok(FNET_DIM, 0),
                  pl.BlockSpec((1, N_MOD, d), lambda i, j: (i, 0, 0)),
                  const((1, HG_DIM)), const((1, d)),
                  const((FNET_DIM, d)), const((HG_DIM, d)), const((d, d))],
        out_specs=(tok(d, 0), tok(d, 0)),
        out_shape=(jax.ShapeDtypeStruct((b, l, d), F32), jax.ShapeDtypeStruct((b, l, d), BF16)),
        compiler_params=_params("arbitrary", "arbitrary"),
        name="merge",
    )(x, o_f, o_b, gates, gates, gates, y_f, mod, og, n2g, w_a, w_b, w_out)


FFN_GROUP = 8


def _ffn_permute_tokens(h_ref, hp_ref, tmp_ref):
    n_tok = tmp_ref.shape[1]

    def body(g, carry):
        tok = pl.ds(pl.multiple_of(g * n_tok, n_tok), n_tok)
        hf = h_ref[0, tok, :].astype(F32)
        for lt in range(tmp_ref.shape[0]):
            for r in range(n_tok // GRID_W):
                for a in range(8):
                    r0 = r * GRID_W + 8 * a
                    tmp_ref[lt, pl.ds(r * GRID_W + a, 8, stride=8), :] = (
                        hf[r0:r0 + 8, lt * 128:(lt + 1) * 128])
        for lt in range(tmp_ref.shape[0]):
            hp_ref[tok, lt * 128:(lt + 1) * 128] = tmp_ref[lt].astype(BF16)
        return carry

    lax.fori_loop(0, h_ref.shape[1] // n_tok, body, 0)


def _ffn_up(hp_ref, up_refs, row0, n_rows):
    h = hp_ref[row0 * GRID_W:(row0 + n_rows) * GRID_W, :]
    return [_dot(h, up_ref[...]) for up_ref in up_refs]


def _ffn_row_terms(zs, wb_ref, hk_ref, row0, n_rows):
    nl = hk_ref.shape[1] // 2
    sub = lax.broadcasted_iota(jnp.int32, (8, 128), 0)
    for half, z in enumerate(zs):
        for lt in range(nl):
            lanes = slice(lt * 128, (lt + 1) * 128)
            taps = [wb_ref[half, k, :, lanes] for k in range(9)]
            for r in range(n_rows):
                zt = [z[r * GRID_W + 8 * b:r * GRID_W + 8 * b + 8, lanes] for b in range(8)]
                left = [jnp.where(sub == 0, 0.0, pltpu.roll(zt[7], 1, axis=0))] + zt[0:7]
                right = zt[1:8] + [jnp.where(sub == 7, 0.0, pltpu.roll(zt[0], 7, axis=0))]
                for kh in range(3):
                    for b in range(8):
                        hk_ref[kh, half * nl + lt, row0 + r + 2 - kh, 8 * b:8 * b + 8, :] = (
                            left[b] * taps[3 * kh] + zt[b] * taps[3 * kh + 1] + right[b] * taps[3 * kh + 2])


def _ffn_down(hk_ref, wb_ref, a_ref, dn_ref, y_ref, row0, n_rows):
    nl = hk_ref.shape[1] // 2
    for lt in range(nl):
        lanes = slice(lt * 128, (lt + 1) * 128)
        for r in range(n_rows):
            for b in range(8):
                rows_b = slice(8 * b, 8 * b + 8)
                c = [hk_ref[0, s, row0 + r + 1, rows_b, :] + hk_ref[1, s, row0 + r + 1, rows_b, :]
                     + hk_ref[2, s, row0 + r + 1, rows_b, :] + wb_ref[half, 9, :, lanes]
                     for half, s in ((0, lt), (1, nl + lt))]
                a_ref[(row0 + r) * GRID_W + 8 * b:(row0 + r) * GRID_W + 8 * b + 8, lanes] = (
                    _silu(c[0]) * c[1])
    tok = slice(row0 * GRID_W, (row0 + n_rows) * GRID_W)
    part = _dot(a_ref[tok, :].astype(BF16), dn_ref[...])
    for j in range(y_ref.shape[1]):
        y_ref[0, j, tok, :] += part[:, j * 128:(j + 1) * 128]


def _ffn_kernel(h_ref, up1_ref, up2_ref, cw1_ref, cw2_ref, cb1_ref, cb2_ref, dn_ref, y_ref,
                hp_ref, hk_ref, a_ref, wb_ref, tmp_ref):
    t = pl.program_id(1)
    nl = hk_ref.shape[1] // 2
    rows = hk_ref.shape[2] - 2
    grp = FFN_GROUP

    @pl.when(t == 0)
    def _():
        _ffn_permute_tokens(h_ref, hp_ref, tmp_ref)
        y_ref[...] = jnp.zeros_like(y_ref)
        zero_row = jnp.zeros((GRID_W, 128), F32)
        for s in range(2 * nl):
            hk_ref[0, s, 1] = zero_row
            hk_ref[2, s, rows] = zero_row

    for half, (cw_ref, cb_ref) in enumerate(((cw1_ref, cb1_ref), (cw2_ref, cb2_ref))):
        wb_ref[half, 0:9] = jnp.broadcast_to(cw_ref[...][:, None, :], (9, 8, nl * 128))
        wb_ref[half, 9] = jnp.broadcast_to(cb_ref[...], (8, nl * 128))

    z_next = _ffn_up(hp_ref, (up1_ref, up2_ref), 0, grp)
    for row0 in range(0, rows, grp):
        z_cur = z_next
        if row0 + grp < rows:
            z_next = _ffn_up(hp_ref, (up1_ref, up2_ref), row0 + grp, grp)
        _ffn_row_terms(z_cur, wb_ref, hk_ref, row0, grp)
        if row0 > 0:
            _ffn_down(hk_ref, wb_ref, a_ref, dn_ref, y_ref, row0 - grp, grp)
    _ffn_down(hk_ref, wb_ref, a_ref, dn_ref, y_ref, rows - grp, grp)


def _ffn(h2, up, conv_w, conv_b, down):
    b, l, d = h2.shape
    rows = l // GRID_W
    tc = FFN_TILE
    nl = tc // 128
    nt = D_FF // tc
    assert rows % FFN_GROUP == 0
    return pl.pallas_call(
        _ffn_kernel,
        grid=(b, nt),
        in_specs=[pl.BlockSpec((1, l, d), lambda i, t: (i, 0, 0)),
                  pl.BlockSpec((d, tc), lambda i, t: (0, t)),
                  pl.BlockSpec((d, tc), lambda i, t: (0, nt + t)),
                  pl.BlockSpec((9, tc), lambda i, t: (0, t)),
                  pl.BlockSpec((9, tc), lambda i, t: (0, nt + t)),
                  pl.BlockSpec((1, tc), lambda i, t: (0, t)),
                  pl.BlockSpec((1, tc), lambda i, t: (0, nt + t)),
                  pl.BlockSpec((tc, d), lambda i, t: (t, 0))],
        out_specs=pl.BlockSpec((1, d // 128, l, 128), lambda i, t: (i, 0, 0, 0)),
        out_shape=jax.ShapeDtypeStruct((b, d // 128, l, 128), F32),
        scratch_shapes=[pltpu.VMEM((l, d), BF16),
                        pltpu.VMEM((3, 2 * nl, rows + 2, GRID_W, 128), F32),
                        pltpu.VMEM((l, tc), F32),
                        pltpu.VMEM((2, 10, 8, tc), F32),
                        pltpu.VMEM((d // 128, FFN_GROUP * GRID_W, 128), F32)],
        compiler_params=_params("arbitrary", "arbitrary"),
        name="ffn",
    )(h2, up, up, conv_w, conv_w, conv_b, conv_b, down)


def _final_kernel(x1_ref, y_ref, mod_ref, g_ref, o_ref):
    n_slab = y_ref.shape[1]
    for r in range(y_ref.shape[2] // GRID_W):
        for a in range(8):
            tok = slice(r * GRID_W + 8 * a, r * GRID_W + 8 * a + 8)
            y = jnp.concatenate([y_ref[0, j, pl.ds(r * GRID_W + a, 8, stride=8), :]
                                 for j in range(n_slab)], axis=-1)
            x = x1_ref[0, tok, :] + mod_ref[0, 5:6, :] * y
            ms = jnp.mean(x * x, axis=-1, keepdims=True)
            o_ref[0, tok, :] = x * lax.rsqrt(ms + EPS) * g_ref[...]


def _final(x1, y, mod, g, *, tm):
    b, l, d = x1.shape
    tok = pl.BlockSpec((1, tm, d), lambda i, j: (i, j, 0))
    return pl.pallas_call(
        _final_kernel,
        grid=(b, l // tm),
        in_specs=[tok, pl.BlockSpec((1, d // 128, tm, 128), lambda i, j: (i, 0, j, 0)),
                  pl.BlockSpec((1, N_MOD, d), lambda i, j: (i, 0, 0)),
                  pl.BlockSpec((1, d), lambda i, j: (0, 0))],
        out_specs=tok,
        out_shape=jax.ShapeDtypeStruct((b, l, d), F32),
        compiler_params=_params("arbitrary", "arbitrary"),
        name="final",
    )(x1, y, mod, g)


def kernel(x, c, ctx, c_ctx, ada_w, ada_b, norm1_g, w_in, hg_lb, hg_onorm_g, w_a, w_b, w_out,
           norm2_g, ffn_up, ffn_conv_w, ffn_conv_b, ffn_down, final_g):
    b, seq, d = x.shape
    assert ada_w.shape[0] == 1, "single-layer kernel"
    assert (d, seq % GRID_W) == (D_MODEL, 0)

    lb = jnp.cumsum(jax.nn.softmax(hg_lb.astype(F32), axis=0), axis=0)[0].reshape(1, HG_DIM)

    n_rows = 16
    rows = jnp.concatenate([c, c_ctx[None, :], jnp.zeros((n_rows - b - 1, d), F32)], axis=0)
    mod = _ada(rows, ada_w[0], ada_b[0])
    mod_x = mod[:b].reshape(b, N_MOD, d)
    mod_c = mod[b:b + 1].reshape(1, N_MOD, d)

    w_in_b = w_in[0].astype(BF16)
    g1 = norm1_g[0].reshape(1, d)
    hg_x, u_f, gates = _inproj(x, mod_x, g1, lb, w_in_b, tm=256, full=True)
    w_ctx = w_in_b[:, COL_Q * HG_DIM:(COL_I + 1) * HG_DIM]
    (hg_c,) = _inproj(ctx, mod_c, g1, lb, w_ctx, tm=256, full=False)

    s_ctx = _hgrn(hg_c, None, emit_o=False)
    o_f, o_b = _hgrn(hg_x, s_ctx, emit_o=True)

    chan, pos = _dft_tables(seq)
    y_f = _fnet(u_f, jnp.asarray(chan).astype(BF16), jnp.asarray(pos).astype(BF16), tm=512)

    x1, h2 = _merge(x, o_f, o_b, gates, y_f, mod_x, hg_onorm_g[0].reshape(1, HG_DIM),
                    norm2_g[0].reshape(1, d), w_a[0].astype(BF16), w_b[0].astype(BF16),
                    w_out[0].astype(BF16), tm=512)

    y = _ffn(h2, ffn_up[0].astype(BF16), ffn_conv_w[0].reshape(9, 2 * D_FF),
             ffn_conv_b[0].reshape(1, 2 * D_FF), ffn_down[0].astype(BF16))
    return _final(x1, y, mod_x, final_g.reshape(1, d), tm=512)
```

```python
import functools

import numpy as np
import jax
import jax.numpy as jnp
from jax import lax
from jax.experimental import pallas as pl
from jax.experimental.pallas import tpu as pltpu

D_MODEL = 1024
GRID_W = 64
FNET_DIM = 512
FNET_GROUPS = 4
FNET_GROUP_DIM = FNET_DIM // FNET_GROUPS
HG_DIM = 512
HG_HEADS = 4
HG_HEAD_DIM = HG_DIM // HG_HEADS
D_FF = 2816
N_MOD = 6
EPS = 1e-6

COL_FNET, COL_Q, COL_FF, COL_FB, COL_I, COL_G = 0, 1, 2, 3, 4, 5

HG_BLOCK = 256
HG_CHUNK = 64
HG_UNROLL = 2
HG_SUB = 16
FFN_TILE = 256

BF16 = jnp.bfloat16
F32 = jnp.float32
VMEM_LIMIT = 56 * 1024 * 1024


def _sigmoid(x):
    return 1.0 / (1.0 + jnp.exp(-x))


def _silu(x):
    return x * _sigmoid(x)


def _dot(a, b):
    return jnp.dot(a, b, preferred_element_type=F32)


def _dot_nt(a, b):
    return lax.dot_general(a, b, (((1,), (1,)), ((), ())), preferred_element_type=F32)


def _params(*sem, flags=None):
    return pltpu.CompilerParams(dimension_semantics=sem, vmem_limit_bytes=VMEM_LIMIT, flags=flags)


def _ada_kernel(c_ref, w_ref, b_ref, o_ref):
    a = _silu(c_ref[...])
    o_ref[...] = _dot(a.astype(BF16), w_ref[...].astype(BF16)) + b_ref[...]


def _ada(rows, ada_w, ada_b):
    m, d = rows.shape
    n = ada_w.shape[1]
    tn = 1536
    return pl.pallas_call(
        _ada_kernel,
        grid=(n // tn,),
        in_specs=[pl.BlockSpec((m, d), lambda j: (0, 0)),
                  pl.BlockSpec((d, tn), lambda j: (0, j)),
                  pl.BlockSpec((1, tn), lambda j: (0, j))],
        out_specs=pl.BlockSpec((m, tn), lambda j: (0, j)),
        out_shape=jax.ShapeDtypeStruct((m, n), F32),
        compiler_params=_params("arbitrary"),
        name="ada",
    )(rows, ada_w, ada_b.reshape(1, n))


def _modulated_norm(x, g, shift, scale):
    ms = jnp.mean(x * x, axis=-1, keepdims=True)
    return (x * lax.rsqrt(ms + EPS) * g) * (1.0 + scale) + shift


def _inproj_kernel(x_ref, mod_ref, g_ref, lb_ref, w_ref, *o_refs, hg_col0, full):
    hb = _modulated_norm(x_ref[0], g_ref[...], mod_ref[0, 0:1, :], mod_ref[0, 1:2, :]).astype(BF16)
    lb = lb_ref[...]

    def proj(col, width=HG_DIM):
        return _dot(hb, w_ref[:, col * HG_DIM:col * HG_DIM + width])

    hg_ref = o_refs[0]
    hg_ref[0, :, 0:HG_DIM] = _silu(proj(hg_col0))
    for n in (1, 2):
        f = lb + (1.0 - lb) * _sigmoid(proj(hg_col0 + n))
        hg_ref[0, :, n * HG_DIM:(n + 1) * HG_DIM] = jnp.log2(f)
    hg_ref[0, :, 3 * HG_DIM:4 * HG_DIM] = proj(hg_col0 + 3)
    if full:
        uf_ref, gate_ref = o_refs[1], o_refs[2]
        uf_ref[0] = proj(COL_FNET).astype(BF16)
        gate_ref[0, :, 0:2 * D_MODEL] = _sigmoid(proj(COL_G + 1, 2 * D_MODEL)).astype(BF16)
        gate_ref[0, :, 2 * D_MODEL:2 * D_MODEL + HG_DIM] = _silu(proj(COL_G)).astype(BF16)


def _inproj(x, mod, g, lb, w, *, tm, full):
    b, l, d = x.shape
    n = w.shape[1]
    mod_map = (lambda i, j: (i, 0, 0)) if full else (lambda i, j: (0, 0, 0))
    tok = lambda width, dtype: (pl.BlockSpec((1, tm, width), lambda i, j: (i, j, 0)),
                                jax.ShapeDtypeStruct((b, l, width), dtype))
    outs = [tok(4 * HG_DIM, F32)]
    if full:
        outs += [tok(FNET_DIM, BF16), tok(HG_DIM + 2 * D_MODEL, BF16)]
    return pl.pallas_call(
        functools.partial(_inproj_kernel, hg_col0=COL_Q if full else 0, full=full),
        grid=(b, l // tm),
        in_specs=[pl.BlockSpec((1, tm, d), lambda i, j: (i, j, 0)),
                  pl.BlockSpec((1, N_MOD, d), mod_map),
                  pl.BlockSpec((1, d), lambda i, j: (0, 0)),
                  pl.BlockSpec((1, HG_DIM), lambda i, j: (0, 0)),
                  pl.BlockSpec((d, n), lambda i, j: (0, 0))],
        out_specs=tuple(o[0] for o in outs),
        out_shape=tuple(o[1] for o in outs),
        compiler_params=_params("arbitrary", "arbitrary"),
        name="inproj_x" if full else "inproj_ctx",
    )(x, mod, g, lb, w)


def _hgrn_step(dirs, st_ref):
    c = HG_CHUNK
    r_idx = lax.broadcasted_iota(jnp.int32, (c, c), 0)
    c_idx = lax.broadcasted_iota(jnp.int32, (c, c), 1)
    heads = [slice(h * HG_HEAD_DIM, (h + 1) * HG_HEAD_DIM) for h in range(HG_HEADS)]

    cums = []
    for q, l2f, v, reverse, o_ref, row0, base in dirs:
        tri = jnp.where((c_idx >= r_idx) if reverse else (c_idx <= r_idx), 1.0, 0.0).astype(BF16)
        hi = l2f.astype(BF16)
        lo = (l2f - hi.astype(F32)).astype(BF16)
        cums.append(_dot(tri, hi) + _dot(tri, lo))

    state_in, state_upd, intra = [], [], []
    for (q, l2f, v, reverse, o_ref, row0, base), cum in zip(dirs, cums):
        k = 1.0 - jnp.exp2(l2f)
        last = 0 if reverse else c - 1
        total = cum[last:last + 1, :]
        q_in = (q * jnp.exp2(cum)).astype(BF16)
        k_out = (k * jnp.exp2(total - cum)).astype(BF16)
        e_tot = jnp.exp2(total)
        v_b = v.astype(BF16)
        v_t = v.T.astype(BF16)
        for h, hs in enumerate(heads):
            state_in.append(q_in[:, hs])
            state_upd.append((base + h, e_tot[:, hs], v_t[hs, :], k_out[:, hs]))
            if o_ref is None:
                continue
            cum_h, q_h, k_h = cum[:, hs], q[:, hs], k[:, hs]
            for i in range(c // HG_SUB):
                r0 = i * HG_SUB
                rows = slice(r0, r0 + HG_SUB)
                cols = slice(r0, c) if reverse else slice(0, r0 + HG_SUB)
                mid = r0 + HG_SUB // 2 if reverse else r0 + HG_SUB // 2 - 1
                m = cum_h[mid:mid + 1, :]
                q_t = (q_h[rows] * jnp.exp2(cum_h[rows] - m)).astype(BF16)
                k_t = (k_h[cols] * jnp.exp2(m - cum_h[cols])).astype(BF16)
                intra.append((o_ref, row0, rows, cols, hs, reverse, len(state_in) - 1,
                              q_t, k_t, v_b[cols, hs]))

    scores = [_dot_nt(item[7], item[8]) for item in intra]
    kv = [_dot(v_t_h, k_out_h) for _, _, v_t_h, k_out_h in state_upd]
    states = {}
    inter = []
    for n, (idx, e_tot_h, _, _) in enumerate(state_upd):
        st = states[idx] if idx in states else st_ref[idx]
        inter.append(_dot_nt(state_in[n], st.astype(BF16)) if intra else None)
        states[idx] = st * e_tot_h + kv[n]
    for idx, st in states.items():
        st_ref[idx] = st

    for (o_ref, row0, rows, cols, hs, reverse, chain, _, _, v_c), a in zip(intra, scores):
        n_cols = cols.stop - cols.start
        row_g = rows.start + lax.broadcasted_iota(jnp.int32, (HG_SUB, n_cols), 0)
        col_g = cols.start + lax.broadcasted_iota(jnp.int32, (HG_SUB, n_cols), 1)
        a = jnp.where((col_g >= row_g) if reverse else (col_g <= row_g), a, 0.0)
        o_ref[0, pl.ds(row0 + rows.start, HG_SUB), hs] = _dot(a.astype(BF16), v_c) + inter[chain][rows]


def _hgrn_kernel(*refs, has_s0, emit_o):
    it = iter(refs)
    qf_ref, gf_ref, vf_ref, qb_ref, gb_ref, vb_ref = (next(it) for _ in range(6))
    s0_ref = next(it) if has_s0 else None
    of_ref = next(it) if emit_o else None
    ob_ref = next(it) if emit_o else None
    sout_ref = None if emit_o else next(it)
    st_ref = next(it)
    j = pl.program_id(1)
    n_inner = qf_ref.shape[1] // HG_CHUNK

    @pl.when(j == 0)
    def _():
        if has_s0:
            st_ref[...] = s0_ref[0]
        else:
            st_ref[...] = jnp.zeros_like(st_ref)

    span = HG_UNROLL * HG_CHUNK

    def body(it, carry):
        dirs = []
        for jj in range(HG_UNROLL):
            rf = pl.multiple_of(it * span, span) + jj * HG_CHUNK
            rb = pl.multiple_of((n_inner // HG_UNROLL - 1 - it) * span, span) + (HG_UNROLL - 1 - jj) * HG_CHUNK
            sf, sb = pl.ds(rf, HG_CHUNK), pl.ds(rb, HG_CHUNK)
            dirs.append((qf_ref[0, sf, :], gf_ref[0, sf, :], vf_ref[0, sf, :], False, of_ref, rf, 0))
            dirs.append((qb_ref[0, sb, :], gb_ref[0, sb, :], vb_ref[0, sb, :], True, ob_ref, rb, HG_HEADS))
        _hgrn_step(dirs, st_ref)
        return carry

    lax.fori_loop(0, n_inner // HG_UNROLL, body, 0)

    if not emit_o:
        @pl.when(j == pl.num_programs(1) - 1)
        def _():
            sout_ref[0] = st_ref[...]


def _hgrn(u, s0, *, emit_o):
    b, l, _ = u.shape
    c = min(HG_BLOCK, l)
    n = l // c
    col0 = 0
    blk = (1, c, HG_DIM)
    fwd = lambda col: pl.BlockSpec(blk, lambda i, j: (i, j, col))
    bwd = lambda col: pl.BlockSpec(blk, lambda i, j: (i, n - 1 - j, col))
    state_spec = pl.BlockSpec((1, 2 * HG_HEADS, HG_HEAD_DIM, HG_HEAD_DIM), lambda i, j: (i, 0, 0, 0))
    state_shape = jax.ShapeDtypeStruct((b, 2 * HG_HEADS, HG_HEAD_DIM, HG_HEAD_DIM), F32)
    in_specs = [fwd(col0), fwd(col0 + 1), fwd(col0 + 3), bwd(col0), bwd(col0 + 2), bwd(col0 + 3)]
    args = [u, u, u, u, u, u]
    if s0 is not None:
        in_specs.append(state_spec)
        args.append(s0)
    if emit_o:
        o_shape = jax.ShapeDtypeStruct((b, l, HG_DIM), F32)
        out_shape = (o_shape, o_shape)
        out_specs = (pl.BlockSpec(blk, lambda i, j: (i, j, 0)),
                     pl.BlockSpec(blk, lambda i, j: (i, n - 1 - j, 0)))
    else:
        out_shape = state_shape
        out_specs = state_spec
    return pl.pallas_call(
        functools.partial(_hgrn_kernel, has_s0=s0 is not None, emit_o=emit_o),
        grid=(b, n),
        in_specs=in_specs,
        out_specs=out_specs,
        out_shape=out_shape,
        scratch_shapes=[pltpu.VMEM((2 * HG_HEADS, HG_HEAD_DIM, HG_HEAD_DIM), F32)],
        compiler_params=_params("arbitrary", "arbitrary"),
        name="hgrn_x" if emit_o else "hgrn_ctx",
    )(*args)


def _dft_tables(seq):
    gd = FNET_GROUP_DIM
    kc = (np.arange(gd)[:, None] * np.arange(gd)[None, :]) % gd
    ang_c = 2.0 * np.pi * kc / gd
    scale = 1.0 / np.sqrt(float(seq) * gd)
    chan = np.concatenate([np.cos(ang_c), np.sin(ang_c)], axis=1) * scale
    kl = (np.arange(seq)[:, None] * np.arange(seq)[None, :]) % seq
    ang_l = 2.0 * np.pi * kl / seq
    pos = np.concatenate([np.cos(ang_l), -np.sin(ang_l)], axis=1)
    return chan.astype(np.float32), pos.astype(np.float32)


def _fnet_kernel(u_ref, chan_ref, pos_ref, y_ref, pq_ref):
    seq = u_ref.shape[1]

    @pl.when(pl.program_id(1) == 0)
    def _():
        for g in range(FNET_GROUPS):
            gs = slice(g * FNET_GROUP_DIM, (g + 1) * FNET_GROUP_DIM)
            pq = _dot(u_ref[0, :, gs], chan_ref[...])
            pq_ref[0:seq, gs] = pq[:, :FNET_GROUP_DIM].astype(BF16)
            pq_ref[seq:2 * seq, gs] = pq[:, FNET_GROUP_DIM:].astype(BF16)

    y_ref[0] = _dot(pos_ref[...], pq_ref[...]).astype(BF16)


def _fnet(u, chan, pos, *, tm):
    b, seq, _ = u.shape
    return pl.pallas_call(
        _fnet_kernel,
        grid=(b, seq // tm),
        in_specs=[pl.BlockSpec((1, seq, FNET_DIM), lambda i, j: (i, 0, 0)),
                  pl.BlockSpec((FNET_GROUP_DIM, 2 * FNET_GROUP_DIM), lambda i, j: (0, 0)),
                  pl.BlockSpec((tm, 2 * seq), lambda i, j: (j, 0))],
        out_specs=pl.BlockSpec((1, tm, FNET_DIM), lambda i, j: (i, j, 0)),
        out_shape=jax.ShapeDtypeStruct((b, seq, FNET_DIM), BF16),
        scratch_shapes=[pltpu.VMEM((2 * seq, FNET_DIM), BF16)],
        compiler_params=_params("arbitrary", "arbitrary"),
        name="fnet",
    )(u, chan, pos)


def _merge_kernel(x_ref, of_ref, ob_ref, sg_ref, ga_ref, gb_ref, yf_ref, mod_ref, og_ref, n2g_ref,
                  wa_ref, wb_ref, wo_ref, x1_ref, h2_ref):
    o = of_ref[0] + ob_ref[0]
    parts = []
    for h in range(HG_HEADS):
        oh = o[:, h * HG_HEAD_DIM:(h + 1) * HG_HEAD_DIM]
        parts.append(oh * lax.rsqrt(jnp.mean(oh * oh, axis=-1, keepdims=True) + EPS))
    on = jnp.concatenate(parts, axis=-1) * og_ref[...] * sg_ref[0].astype(F32)
    y_b = _dot(on.astype(BF16), wb_ref[...])
    y_a = _dot(yf_ref[0], wa_ref[...])
    m = ga_ref[0].astype(F32) * y_a + gb_ref[0].astype(F32) * y_b
    yx = _dot(m.astype(BF16), wo_ref[...])
    x1 = x_ref[0] + mod_ref[0, 2:3, :] * yx
    x1_ref[0] = x1
    h2 = _modulated_norm(x1, n2g_ref[...], mod_ref[0, 3:4, :], mod_ref[0, 4:5, :])
    h2_ref[0] = h2.astype(BF16)


def _merge(x, o_f, o_b, gates, y_f, mod, og, n2g, w_a, w_b, w_out, *, tm):
    b, l, d = x.shape
    tok = lambda w, col: pl.BlockSpec((1, tm, w), lambda i, j: (i, j, col))
    const = lambda shape: pl.BlockSpec(shape, lambda i, j: tuple(0 for _ in shape))
    return pl.pallas_call(
        _merge_kernel,
        grid=(b, l // tm),
        in_specs=[tok(d, 0), tok(HG_DIM, 0), tok(HG_DIM, 0), tok(HG_DIM, 2 * d // HG_DIM),
                  tok(d, 0), tok(d, 1), tok(FNET_DIM, 0),
                  pl.BlockSpec((1, N_MOD, d), lambda i, j: (i, 0, 0)),
                  const((1, HG_DIM)), const((1, d)),
                  const((FNET_DIM, d)), const((HG_DIM, d)), const((d, d))],
        out_specs=(tok(d, 0), tok(d, 0)),
        out_shape=(jax.ShapeDtypeStruct((b, l, d), F32), jax.ShapeDtypeStruct((b, l, d), BF16)),
        compiler_params=_params("arbitrary", "arbitrary"),
        name="merge",
    )(x, o_f, o_b, gates, gates, gates, y_f, mod, og, n2g, w_a, w_b, w_out)


FFN_GROUP = 8


def _ffn_permute_tokens(h_ref, hp_ref, tmp_ref):
    n_tok = tmp_ref.shape[1]

    def body(g, carry):
        tok = pl.ds(pl.multiple_of(g * n_tok, n_tok), n_tok)
        hf = h_ref[0, tok, :].astype(F32)
        for lt in range(tmp_ref.shape[0]):
            for r in range(n_tok // GRID_W):
                for a in range(8):
                    r0 = r * GRID_W + 8 * a
                    tmp_ref[lt, pl.ds(r * GRID_W + a, 8, stride=8), :] = (
                        hf[r0:r0 + 8, lt * 128:(lt + 1) * 128])
        for lt in range(tmp_ref.shape[0]):
            hp_ref[tok, lt * 128:(lt + 1) * 128] = tmp_ref[lt].astype(BF16)
        return carry

    lax.fori_loop(0, h_ref.shape[1] // n_tok, body, 0)


def _ffn_up(hp_ref, up_refs, row0, n_rows):
    h = hp_ref[row0 * GRID_W:(row0 + n_rows) * GRID_W, :]
    return [_dot(h, up_ref[...]) for up_ref in up_refs]


def _pair(lo, hi):
    return jnp.concatenate([lo, hi], axis=0).astype(BF16)


def _ffn_row_terms(zs, wp_ref, hk_ref, row0, n_rows):
    sub = lax.broadcasted_iota(jnp.int32, (8, 128), 0)
    for half, z in enumerate(zs):
        taps = [wp_ref[half, k] for k in range(9)]
        for r in range(n_rows):
            lo = [z[r * GRID_W + 8 * b:r * GRID_W + 8 * b + 8, 0:128] for b in range(8)]
            hi = [z[r * GRID_W + 8 * b:r * GRID_W + 8 * b + 8, 128:256] for b in range(8)]
            mid = [_pair(lo[b], hi[b]) for b in range(8)]
            down = lambda v: jnp.where(sub == 0, 0.0, pltpu.roll(v, 1, axis=0))
            left = [_pair(down(lo[7]), down(hi[7]))] + mid[0:7]
            up = lambda v: jnp.where(sub == 7, 0.0, pltpu.roll(v, 7, axis=0))
            right = mid[1:8] + [_pair(up(lo[0]), up(hi[0]))]
            for kh in range(3):
                for b in range(8):
                    hk_ref[kh, half, row0 + r + 2 - kh, b] = (
                        left[b] * taps[3 * kh] + mid[b] * taps[3 * kh + 1] + right[b] * taps[3 * kh + 2])


def _ffn_down(hk_ref, wp_ref, a_ref, dn_ref, y_ref, row0, n_rows):
    for r in range(n_rows):
        for b in range(8):
            c = [hk_ref[0, half, row0 + r + 1, b] + hk_ref[1, half, row0 + r + 1, b]
                 + hk_ref[2, half, row0 + r + 1, b] + wp_ref[half, 9] for half in range(2)]
            act = (_silu(c[0]) * c[1]).astype(F32)
            tok_b = slice((row0 + r) * GRID_W + 8 * b, (row0 + r) * GRID_W + 8 * b + 8)
            a_ref[tok_b, 0:128] = act[0:8]
            a_ref[tok_b, 128:256] = act[8:16]
    tok = slice(row0 * GRID_W, (row0 + n_rows) * GRID_W)
    part = _dot(a_ref[tok, :].astype(BF16), dn_ref[...])
    for j in range(y_ref.shape[1]):
        y_ref[0, j, tok, :] += part[:, j * 128:(j + 1) * 128]


def _ffn_kernel(h_ref, up1_ref, up2_ref, cw1_ref, cw2_ref, cb1_ref, cb2_ref, dn_ref, y_ref,
                hp_ref, hk_ref, a_ref, wp_ref, tmp_ref):
    t = pl.program_id(1)
    rows = hk_ref.shape[2] - 2
    grp = FFN_GROUP

    @pl.when(t == 0)
    def _():
        _ffn_permute_tokens(h_ref, hp_ref, tmp_ref)
        y_ref[...] = jnp.zeros_like(y_ref)
        for half in range(2):
            hk_ref[0, half, 1] = jnp.zeros(hk_ref.shape[3:], BF16)
            hk_ref[2, half, rows] = jnp.zeros(hk_ref.shape[3:], BF16)

    for half, (cw_ref, cb_ref) in enumerate(((cw1_ref, cb1_ref), (cw2_ref, cb2_ref))):
        for k in range(10):
            row = cw_ref[k:k + 1, :] if k < 9 else cb_ref[...]
            wp_ref[half, k] = _pair(jnp.broadcast_to(row[:, 0:128], (8, 128)),
                                    jnp.broadcast_to(row[:, 128:256], (8, 128)))

    z_next = _ffn_up(hp_ref, (up1_ref, up2_ref), 0, grp)
    for row0 in range(0, rows, grp):
        z_cur = z_next
        if row0 + grp < rows:
            z_next = _ffn_up(hp_ref, (up1_ref, up2_ref), row0 + grp, grp)
        _ffn_row_terms(z_cur, wp_ref, hk_ref, row0, grp)
        if row0 > 0:
            _ffn_down(hk_ref, wp_ref, a_ref, dn_ref, y_ref, row0 - grp, grp)
    _ffn_down(hk_ref, wp_ref, a_ref, dn_ref, y_ref, rows - grp, grp)


def _ffn(h2, up, conv_w, conv_b, down):
    b, l, d = h2.shape
    rows = l // GRID_W
    tc = FFN_TILE
    nt = D_FF // tc
    assert rows % FFN_GROUP == 0 and tc == 256, "the conv pairs the two lane tiles of a channel tile"
    return pl.pallas_call(
        _ffn_kernel,
        grid=(b, nt),
        in_specs=[pl.BlockSpec((1, l, d), lambda i, t: (i, 0, 0)),
                  pl.BlockSpec((d, tc), lambda i, t: (0, t)),
                  pl.BlockSpec((d, tc), lambda i, t: (0, nt + t)),
                  pl.BlockSpec((9, tc), lambda i, t: (0, t)),
                  pl.BlockSpec((9, tc), lambda i, t: (0, nt + t)),
                  pl.BlockSpec((1, tc), lambda i, t: (0, t)),
                  pl.BlockSpec((1, tc), lambda i, t: (0, nt + t)),
                  pl.BlockSpec((tc, d), lambda i, t: (t, 0))],
        out_specs=pl.BlockSpec((1, d // 128, l, 128), lambda i, t: (i, 0, 0, 0)),
        out_shape=jax.ShapeDtypeStruct((b, d // 128, l, 128), F32),
        scratch_shapes=[pltpu.VMEM((l, d), BF16),
                        pltpu.VMEM((3, 2, rows + 2, 8, 16, 128), BF16),
                        pltpu.VMEM((l, tc), F32),
                        pltpu.VMEM((2, 10, 16, 128), BF16),
                        pltpu.VMEM((d // 128, FFN_GROUP * GRID_W, 128), F32)],
        compiler_params=_params("arbitrary", "arbitrary"),
        name="ffn",
    )(h2, up, up, conv_w, conv_w, conv_b, conv_b, down)


def _final_kernel(x1_ref, y_ref, mod_ref, g_ref, o_ref):
    n_slab = y_ref.shape[1]
    for r in range(y_ref.shape[2] // GRID_W):
        for a in range(8):
            tok = slice(r * GRID_W + 8 * a, r * GRID_W + 8 * a + 8)
            y = jnp.concatenate([y_ref[0, j, pl.ds(r * GRID_W + a, 8, stride=8), :]
                                 for j in range(n_slab)], axis=-1)
            x = x1_ref[0, tok, :] + mod_ref[0, 5:6, :] * y
            ms = jnp.mean(x * x, axis=-1, keepdims=True)
            o_ref[0, tok, :] = x * lax.rsqrt(ms + EPS) * g_ref[...]


def _final(x1, y, mod, g, *, tm):
    b, l, d = x1.shape
    tok = pl.BlockSpec((1, tm, d), lambda i, j: (i, j, 0))
    return pl.pallas_call(
        _final_kernel,
        grid=(b, l // tm),
        in_specs=[tok, pl.BlockSpec((1, d // 128, tm, 128), lambda i, j: (i, 0, j, 0)),
                  pl.BlockSpec((1, N_MOD, d), lambda i, j: (i, 0, 0)),
                  pl.BlockSpec((1, d), lambda i, j: (0, 0))],
        out_specs=tok,
        out_shape=jax.ShapeDtypeStruct((b, l, d), F32),
        compiler_params=_params("arbitrary", "arbitrary"),
        name="final",
    )(x1, y, mod, g)


def kernel(x, c, ctx, c_ctx, ada_w, ada_b, norm1_g, w_in, hg_lb, hg_onorm_g, w_a, w_b, w_out,
           norm2_g, ffn_up, ffn_conv_w, ffn_conv_b, ffn_down, final_g):
    b, seq, d = x.shape
    assert ada_w.shape[0] == 1, "single-layer kernel"
    assert (d, seq % GRID_W) == (D_MODEL, 0)

    lb = jnp.cumsum(jax.nn.softmax(hg_lb.astype(F32), axis=0), axis=0)[0].reshape(1, HG_DIM)

    n_rows = 16
    rows = jnp.concatenate([c, c_ctx[None, :], jnp.zeros((n_rows - b - 1, d), F32)], axis=0)
    mod = _ada(rows, ada_w[0], ada_b[0])
    mod_x = mod[:b].reshape(b, N_MOD, d)
    mod_c = mod[b:b + 1].reshape(1, N_MOD, d)

    w_in_b = w_in[0].astype(BF16)
    g1 = norm1_g[0].reshape(1, d)
    hg_x, u_f, gates = _inproj(x, mod_x, g1, lb, w_in_b, tm=256, full=True)
    w_ctx = w_in_b[:, COL_Q * HG_DIM:(COL_I + 1) * HG_DIM]
    (hg_c,) = _inproj(ctx, mod_c, g1, lb, w_ctx, tm=256, full=False)

    s_ctx = _hgrn(hg_c, None, emit_o=False)
    o_f, o_b = _hgrn(hg_x, s_ctx, emit_o=True)

    chan, pos = _dft_tables(seq)
    y_f = _fnet(u_f, jnp.asarray(chan).astype(BF16), jnp.asarray(pos).astype(BF16), tm=512)

    x1, h2 = _merge(x, o_f, o_b, gates, y_f, mod_x, hg_onorm_g[0].reshape(1, HG_DIM),
                    norm2_g[0].reshape(1, d), w_a[0].astype(BF16), w_b[0].astype(BF16),
                    w_out[0].astype(BF16), tm=512)

    y = _ffn(h2, ffn_up[0].astype(BF16), ffn_conv_w[0].reshape(9, 2 * D_FF),
             ffn_conv_b[0].reshape(1, 2 * D_FF), ffn_down[0].astype(BF16))
    return _final(x1, y, mod_x, final_g.reshape(1, d), tm=512)
```

```python
import functools

import numpy as np
import jax
import jax.numpy as jnp
from jax import lax
from jax.experimental import pallas as pl
from jax.experimental.pallas import tpu as pltpu

D_MODEL = 1024
GRID_W = 64
FNET_DIM = 512
FNET_GROUPS = 4
FNET_GROUP_DIM = FNET_DIM // FNET_GROUPS
HG_DIM = 512
HG_HEADS = 4
HG_HEAD_DIM = HG_DIM // HG_HEADS
D_FF = 2816
N_MOD = 6
EPS = 1e-6

COL_FNET, COL_Q, COL_FF, COL_FB, COL_I, COL_G = 0, 1, 2, 3, 4, 5

HG_BLOCK = 256
HG_CHUNK = 64
HG_UNROLL = 2
HG_SUB = 16
FFN_TILE = 256

BF16 = jnp.bfloat16
F32 = jnp.float32
VMEM_LIMIT = 56 * 1024 * 1024


def _sigmoid(x):
    return 1.0 / (1.0 + jnp.exp(-x))


def _silu(x):
    return x * _sigmoid(x)


def _dot(a, b):
    return jnp.dot(a, b, preferred_element_type=F32)


def _dot_nt(a, b):
    return lax.dot_general(a, b, (((1,), (1,)), ((), ())), preferred_element_type=F32)


def _params(*sem, flags=None):
    return pltpu.CompilerParams(dimension_semantics=sem, vmem_limit_bytes=VMEM_LIMIT, flags=flags)


def _ada_kernel(c_ref, w_ref, b_ref, o_ref):
    a = _silu(c_ref[...])
    o_ref[...] = _dot(a.astype(BF16), w_ref[...].astype(BF16)) + b_ref[...]


def _ada(rows, ada_w, ada_b):
    m, d = rows.shape
    n = ada_w.shape[1]
    tn = 1536
    return pl.pallas_call(
        _ada_kernel,
        grid=(n // tn,),
        in_specs=[pl.BlockSpec((m, d), lambda j: (0, 0)),
                  pl.BlockSpec((d, tn), lambda j: (0, j)),
                  pl.BlockSpec((1, tn), lambda j: (0, j))],
        out_specs=pl.BlockSpec((m, tn), lambda j: (0, j)),
        out_shape=jax.ShapeDtypeStruct((m, n), F32),
        compiler_params=_params("arbitrary"),
        name="ada",
    )(rows, ada_w, ada_b.reshape(1, n))


def _modulated_norm(x, g, shift, scale):
    ms = jnp.mean(x * x, axis=-1, keepdims=True)
    return (x * lax.rsqrt(ms + EPS) * g) * (1.0 + scale) + shift


def _inproj_kernel(x_ref, mod_ref, g_ref, lb_ref, w_ref, *o_refs, hg_col0, full):
    hb = _modulated_norm(x_ref[0], g_ref[...], mod_ref[0, 0:1, :], mod_ref[0, 1:2, :]).astype(BF16)
    lb = lb_ref[...]

    def proj(col, width=HG_DIM):
        return _dot(hb, w_ref[:, col * HG_DIM:col * HG_DIM + width])

    hg_ref = o_refs[0]
    hg_ref[0, :, 0:HG_DIM] = _silu(proj(hg_col0))
    for n in (1, 2):
        f = lb + (1.0 - lb) * _sigmoid(proj(hg_col0 + n))
        hg_ref[0, :, n * HG_DIM:(n + 1) * HG_DIM] = jnp.log2(f)
    hg_ref[0, :, 3 * HG_DIM:4 * HG_DIM] = proj(hg_col0 + 3)
    if full:
        uf_ref, gate_ref = o_refs[1], o_refs[2]
        uf_ref[0] = proj(COL_FNET).astype(BF16)
        gate_ref[0, :, 0:2 * D_MODEL] = _sigmoid(proj(COL_G + 1, 2 * D_MODEL)).astype(BF16)
        gate_ref[0, :, 2 * D_MODEL:2 * D_MODEL + HG_DIM] = _silu(proj(COL_G)).astype(BF16)


def _inproj(x, mod, g, lb, w, *, tm, full):
    b, l, d = x.shape
    n = w.shape[1]
    mod_map = (lambda i, j: (i, 0, 0)) if full else (lambda i, j: (0, 0, 0))
    tok = lambda width, dtype: (pl.BlockSpec((1, tm, width), lambda i, j: (i, j, 0)),
                                jax.ShapeDtypeStruct((b, l, width), dtype))
    outs = [tok(4 * HG_DIM, F32)]
    if full:
        outs += [tok(FNET_DIM, BF16), tok(HG_DIM + 2 * D_MODEL, BF16)]
    return pl.pallas_call(
        functools.partial(_inproj_kernel, hg_col0=COL_Q if full else 0, full=full),
        grid=(b, l // tm),
        in_specs=[pl.BlockSpec((1, tm, d), lambda i, j: (i, j, 0)),
                  pl.BlockSpec((1, N_MOD, d), mod_map),
                  pl.BlockSpec((1, d), lambda i, j: (0, 0)),
                  pl.BlockSpec((1, HG_DIM), lambda i, j: (0, 0)),
                  pl.BlockSpec((d, n), lambda i, j: (0, 0))],
        out_specs=tuple(o[0] for o in outs),
        out_shape=tuple(o[1] for o in outs),
        compiler_params=_params("arbitrary", "arbitrary"),
        name="inproj_x" if full else "inproj_ctx",
    )(x, mod, g, lb, w)


def _hgrn_step(dirs, st_ref):
    c = HG_CHUNK
    r_idx = lax.broadcasted_iota(jnp.int32, (c, c), 0)
    c_idx = lax.broadcasted_iota(jnp.int32, (c, c), 1)
    heads = [slice(h * HG_HEAD_DIM, (h + 1) * HG_HEAD_DIM) for h in range(HG_HEADS)]

    cums = []
    for q, l2f, v, reverse, o_ref, row0, base in dirs:
        tri = jnp.where((c_idx >= r_idx) if reverse else (c_idx <= r_idx), 1.0, 0.0).astype(BF16)
        hi = l2f.astype(BF16)
        lo = (l2f - hi.astype(F32)).astype(BF16)
        cums.append(_dot(tri, hi) + _dot(tri, lo))

    state_in, state_upd, intra = [], [], []
    for (q, l2f, v, reverse, o_ref, row0, base), cum in zip(dirs, cums):
        k = 1.0 - jnp.exp2(l2f)
        last = 0 if reverse else c - 1
        total = cum[last:last + 1, :]
        q_in = (q * jnp.exp2(cum)).astype(BF16)
        k_out = (k * jnp.exp2(total - cum)).astype(BF16)
        e_tot = jnp.exp2(total)
        v_b = v.astype(BF16)
        v_t = v.T.astype(BF16)
        for h, hs in enumerate(heads):
            state_in.append(q_in[:, hs])
            state_upd.append((base + h, e_tot[:, hs], v_t[hs, :], k_out[:, hs]))
            if o_ref is None:
                continue
            cum_h, q_h, k_h = cum[:, hs], q[:, hs], k[:, hs]
            for i in range(c // HG_SUB):
                r0 = i * HG_SUB
                rows = slice(r0, r0 + HG_SUB)
                cols = slice(r0, c) if reverse else slice(0, r0 + HG_SUB)
                mid = r0 + HG_SUB // 2 if reverse else r0 + HG_SUB // 2 - 1
                m = cum_h[mid:mid + 1, :]
                q_t = (q_h[rows] * jnp.exp2(cum_h[rows] - m)).astype(BF16)
                k_t = (k_h[cols] * jnp.exp2(m - cum_h[cols])).astype(BF16)
                intra.append((o_ref, row0, rows, cols, hs, reverse, len(state_in) - 1,
                              q_t, k_t, v_b[cols, hs]))

    scores = [_dot_nt(item[7], item[8]) for item in intra]
    kv = [_dot(v_t_h, k_out_h) for _, _, v_t_h, k_out_h in state_upd]
    states = {}
    inter = []
    for n, (idx, e_tot_h, _, _) in enumerate(state_upd):
        st = states[idx] if idx in states else st_ref[idx]
        inter.append(_dot_nt(state_in[n], st.astype(BF16)) if intra else None)
        states[idx] = st * e_tot_h + kv[n]
    for idx, st in states.items():
        st_ref[idx] = st

    for (o_ref, row0, rows, cols, hs, reverse, chain, _, _, v_c), a in zip(intra, scores):
        n_cols = cols.stop - cols.start
        row_g = rows.start + lax.broadcasted_iota(jnp.int32, (HG_SUB, n_cols), 0)
        col_g = cols.start + lax.broadcasted_iota(jnp.int32, (HG_SUB, n_cols), 1)
        a = jnp.where((col_g >= row_g) if reverse else (col_g <= row_g), a, 0.0)
        o_ref[0, pl.ds(row0 + rows.start, HG_SUB), hs] = _dot(a.astype(BF16), v_c) + inter[chain][rows]


def _hgrn_kernel(*refs, has_s0, emit_o):
    it = iter(refs)
    qf_ref, gf_ref, vf_ref, qb_ref, gb_ref, vb_ref = (next(it) for _ in range(6))
    s0_ref = next(it) if has_s0 else None
    of_ref = next(it) if emit_o else None
    ob_ref = next(it) if emit_o else None
    sout_ref = None if emit_o else next(it)
    st_ref = next(it)
    j = pl.program_id(1)
    n_inner = qf_ref.shape[1] // HG_CHUNK

    @pl.when(j == 0)
    def _():
        if has_s0:
            st_ref[...] = s0_ref[0]
        else:
            st_ref[...] = jnp.zeros_like(st_ref)

    span = HG_UNROLL * HG_CHUNK

    def body(it, carry):
        dirs = []
        for jj in range(HG_UNROLL):
            rf = pl.multiple_of(it * span, span) + jj * HG_CHUNK
            rb = pl.multiple_of((n_inner // HG_UNROLL - 1 - it) * span, span) + (HG_UNROLL - 1 - jj) * HG_CHUNK
            sf, sb = pl.ds(rf, HG_CHUNK), pl.ds(rb, HG_CHUNK)
            dirs.append((qf_ref[0, sf, :], gf_ref[0, sf, :], vf_ref[0, sf, :], False, of_ref, rf, 0))
            dirs.append((qb_ref[0, sb, :], gb_ref[0, sb, :], vb_ref[0, sb, :], True, ob_ref, rb, HG_HEADS))
        _hgrn_step(dirs, st_ref)
        return carry

    lax.fori_loop(0, n_inner // HG_UNROLL, body, 0)

    if not emit_o:
        @pl.when(j == pl.num_programs(1) - 1)
        def _():
            sout_ref[0] = st_ref[...]


def _hgrn(u, s0, *, emit_o):
    b, l, _ = u.shape
    c = min(HG_BLOCK, l)
    n = l // c
    col0 = 0
    blk = (1, c, HG_DIM)
    fwd = lambda col: pl.BlockSpec(blk, lambda i, j: (i, j, col))
    bwd = lambda col: pl.BlockSpec(blk, lambda i, j: (i, n - 1 - j, col))
    state_spec = pl.BlockSpec((1, 2 * HG_HEADS, HG_HEAD_DIM, HG_HEAD_DIM), lambda i, j: (i, 0, 0, 0))
    state_shape = jax.ShapeDtypeStruct((b, 2 * HG_HEADS, HG_HEAD_DIM, HG_HEAD_DIM), F32)
    in_specs = [fwd(col0), fwd(col0 + 1), fwd(col0 + 3), bwd(col0), bwd(col0 + 2), bwd(col0 + 3)]
    args = [u, u, u, u, u, u]
    if s0 is not None:
        in_specs.append(state_spec)
        args.append(s0)
    if emit_o:
        o_shape = jax.ShapeDtypeStruct((b, l, HG_DIM), F32)
        out_shape = (o_shape, o_shape)
        out_specs = (pl.BlockSpec(blk, lambda i, j: (i, j, 0)),
                     pl.BlockSpec(blk, lambda i, j: (i, n - 1 - j, 0)))
    else:
        out_shape = state_shape
        out_specs = state_spec
    return pl.pallas_call(
        functools.partial(_hgrn_kernel, has_s0=s0 is not None, emit_o=emit_o),
        grid=(b, n),
        in_specs=in_specs,
        out_specs=out_specs,
        out_shape=out_shape,
        scratch_shapes=[pltpu.VMEM((2 * HG_HEADS, HG_HEAD_DIM, HG_HEAD_DIM), F32)],
        compiler_params=_params("arbitrary", "arbitrary"),
        name="hgrn_x" if emit_o else "hgrn_ctx",
    )(*args)


def _dft_tables(seq):
    gd = FNET_GROUP_DIM
    kc = (np.arange(gd)[:, None] * np.arange(gd)[None, :]) % gd
    ang_c = 2.0 * np.pi * kc / gd
    scale = 1.0 / np.sqrt(float(seq) * gd)
    chan = np.concatenate([np.cos(ang_c), np.sin(ang_c)], axis=1) * scale
    kl = (np.arange(seq)[:, None] * np.arange(seq)[None, :]) % seq
    ang_l = 2.0 * np.pi * kl / seq
    pos = np.concatenate([np.cos(ang_l), -np.sin(ang_l)], axis=1)
    return chan.astype(np.float32), pos.astype(np.float32)


def _fnet_kernel(u_ref, chan_ref, pos_ref, y_ref, pq_ref):
    seq = u_ref.shape[1]

    @pl.when(pl.program_id(1) == 0)
    def _():
        for g in range(FNET_GROUPS):
            gs = slice(g * FNET_GROUP_DIM, (g + 1) * FNET_GROUP_DIM)
            pq = _dot(u_ref[0, :, gs], chan_ref[...])
            pq_ref[0:seq, gs] = pq[:, :FNET_GROUP_DIM].astype(BF16)
            pq_ref[seq:2 * seq, gs] = pq[:, FNET_GROUP_DIM:].astype(BF16)

    y_ref[0] = _dot(pos_ref[...], pq_ref[...]).astype(BF16)


def _fnet(u, chan, pos, *, tm):
    b, seq, _ = u.shape
    return pl.pallas_call(
        _fnet_kernel,
        grid=(b, seq // tm),
        in_specs=[pl.BlockSpec((1, seq, FNET_DIM), lambda i, j: (i, 0, 0)),
                  pl.BlockSpec((FNET_GROUP_DIM, 2 * FNET_GROUP_DIM), lambda i, j: (0, 0)),
                  pl.BlockSpec((tm, 2 * seq), lambda i, j: (j, 0))],
        out_specs=pl.BlockSpec((1, tm, FNET_DIM), lambda i, j: (i, j, 0)),
        out_shape=jax.ShapeDtypeStruct((b, seq, FNET_DIM), BF16),
        scratch_shapes=[pltpu.VMEM((2 * seq, FNET_DIM), BF16)],
        compiler_params=_params("arbitrary", "arbitrary"),
        name="fnet",
    )(u, chan, pos)


def _merge_kernel(x_ref, of_ref, ob_ref, sg_ref, ga_ref, gb_ref, yf_ref, mod_ref, og_ref, n2g_ref,
                  wa_ref, wb_ref, wo_ref, x1_ref, h2_ref):
    o = of_ref[0] + ob_ref[0]
    parts = []
    for h in range(HG_HEADS):
        oh = o[:, h * HG_HEAD_DIM:(h + 1) * HG_HEAD_DIM]
        parts.append(oh * lax.rsqrt(jnp.mean(oh * oh, axis=-1, keepdims=True) + EPS))
    on = jnp.concatenate(parts, axis=-1) * og_ref[...] * sg_ref[0].astype(F32)
    y_b = _dot(on.astype(BF16), wb_ref[...])
    y_a = _dot(yf_ref[0], wa_ref[...])
    m = ga_ref[0].astype(F32) * y_a + gb_ref[0].astype(F32) * y_b
    yx = _dot(m.astype(BF16), wo_ref[...])
    x1 = x_ref[0] + mod_ref[0, 2:3, :] * yx
    x1_ref[0] = x1
    h2 = _modulated_norm(x1, n2g_ref[...], mod_ref[0, 3:4, :], mod_ref[0, 4:5, :])
    h2_ref[0] = h2.astype(BF16)


def _merge(x, o_f, o_b, gates, y_f, mod, og, n2g, w_a, w_b, w_out, *, tm):
    b, l, d = x.shape
    tok = lambda w, col: pl.BlockSpec((1, tm, w), lambda i, j: (i, j, col))
    const = lambda shape: pl.BlockSpec(shape, lambda i, j: tuple(0 for _ in shape))
    return pl.pallas_call(
        _merge_kernel,
        grid=(b, l // tm),
        in_specs=[tok(d, 0), tok(HG_DIM, 0), tok(HG_DIM, 0), tok(HG_DIM, 2 * d // HG_DIM),
                  tok(d, 0), tok(d, 1), tok(FNET_DIM, 0),
                  pl.BlockSpec((1, N_MOD, d), lambda i, j: (i, 0, 0)),
                  const((1, HG_DIM)), const((1, d)),
                  const((FNET_DIM, d)), const((HG_DIM, d)), const((d, d))],
        out_specs=(tok(d, 0), tok(d, 0)),
        out_shape=(jax.ShapeDtypeStruct((b, l, d), F32), jax.ShapeDtypeStruct((b, l, d), BF16)),
        compiler_params=_params("arbitrary", "arbitrary"),
        name="merge",
    )(x, o_f, o_b, gates, gates, gates, y_f, mod, og, n2g, w_a, w_b, w_out)


FFN_GROUP = 8


def _ffn_permute_tokens(h_ref, hp_ref, tmp_ref):
    n_tok = tmp_ref.shape[1]

    def body(g, carry):
        tok = pl.ds(pl.multiple_of(g * n_tok, n_tok), n_tok)
        hf = h_ref[0, tok, :].astype(F32)
        for lt in range(tmp_ref.shape[0]):
            for r in range(n_tok // GRID_W):
                for a in range(8):
                    r0 = r * GRID_W + 8 * a
                    tmp_ref[lt, pl.ds(r * GRID_W + a, 8, stride=8), :] = (
                        hf[r0:r0 + 8, lt * 128:(lt + 1) * 128])
        for lt in range(tmp_ref.shape[0]):
            hp_ref[tok, lt * 128:(lt + 1) * 128] = tmp_ref[lt].astype(BF16)
        return carry

    lax.fori_loop(0, h_ref.shape[1] // n_tok, body, 0)


def _ffn_up(hp_ref, up_refs, row0, n_rows):
    h = hp_ref[row0 * GRID_W:(row0 + n_rows) * GRID_W, :]
    return [_dot(h, up_ref[...]) for up_ref in up_refs]


def _pair(lo, hi):
    return jnp.concatenate([lo, hi], axis=0).astype(BF16)


def _ffn_row_terms(zs, wp_ref, hk_ref, row0, n_rows):
    sub = lax.broadcasted_iota(jnp.int32, (8, 128), 0)
    for half, z in enumerate(zs):
        taps = [wp_ref[half, k] for k in range(9)]
        for r in range(n_rows):
            lo = [z[r * GRID_W + 8 * b:r * GRID_W + 8 * b + 8, 0:128] for b in range(8)]
            hi = [z[r * GRID_W + 8 * b:r * GRID_W + 8 * b + 8, 128:256] for b in range(8)]
            mid = [_pair(lo[b], hi[b]) for b in range(8)]
            down = lambda v: jnp.where(sub == 0, 0.0, pltpu.roll(v, 1, axis=0))
            left = [_pair(down(lo[7]), down(hi[7]))] + mid[0:7]
            up = lambda v: jnp.where(sub == 7, 0.0, pltpu.roll(v, 7, axis=0))
            right = mid[1:8] + [_pair(up(lo[0]), up(hi[0]))]
            for kh in range(3):
                for b in range(8):
                    hk_ref[kh, half, row0 + r + 2 - kh, b] = (
                        left[b] * taps[3 * kh] + mid[b] * taps[3 * kh + 1] + right[b] * taps[3 * kh + 2])


def _ffn_down(hk_ref, wp_ref, a_ref, dn_ref, y_ref, row0, n_rows):
    for r in range(n_rows):
        act = []
        for b in range(8):
            c = [hk_ref[0, half, row0 + r + 1, b] + hk_ref[1, half, row0 + r + 1, b]
                 + hk_ref[2, half, row0 + r + 1, b] + wp_ref[half, 9] for half in range(2)]
            act.append(_silu(c[0]) * c[1])
        for b in range(0, 8, 2):
            tok_b = slice((row0 + r) * GRID_W + 8 * b, (row0 + r) * GRID_W + 8 * b + 16)
            a_ref[tok_b, 0:128] = jnp.concatenate([act[b][0:8], act[b + 1][0:8]], axis=0)
            a_ref[tok_b, 128:256] = jnp.concatenate([act[b][8:16], act[b + 1][8:16]], axis=0)
    tok = slice(row0 * GRID_W, (row0 + n_rows) * GRID_W)
    part = _dot(a_ref[tok, :], dn_ref[...])
    for j in range(y_ref.shape[1]):
        y_ref[0, j, tok, :] += part[:, j * 128:(j + 1) * 128]


def _ffn_kernel(h_ref, up1_ref, up2_ref, cw1_ref, cw2_ref, cb1_ref, cb2_ref, dn_ref, y_ref,
                hp_ref, hk_ref, a_ref, wp_ref, tmp_ref):
    t = pl.program_id(1)
    rows = hk_ref.shape[2] - 2
    grp = FFN_GROUP

    @pl.when(t == 0)
    def _():
        _ffn_permute_tokens(h_ref, hp_ref, tmp_ref)
        y_ref[...] = jnp.zeros_like(y_ref)
        for half in range(2):
            hk_ref[0, half, 1] = jnp.zeros(hk_ref.shape[3:], BF16)
            hk_ref[2, half, rows] = jnp.zeros(hk_ref.shape[3:], BF16)

    for half, (cw_ref, cb_ref) in enumerate(((cw1_ref, cb1_ref), (cw2_ref, cb2_ref))):
        for k in range(10):
            row = cw_ref[k:k + 1, :] if k < 9 else cb_ref[...]
            wp_ref[half, k] = _pair(jnp.broadcast_to(row[:, 0:128], (8, 128)),
                                    jnp.broadcast_to(row[:, 128:256], (8, 128)))

    z_next = _ffn_up(hp_ref, (up1_ref, up2_ref), 0, grp)
    for row0 in range(0, rows, grp):
        z_cur = z_next
        if row0 + grp < rows:
            z_next = _ffn_up(hp_ref, (up1_ref, up2_ref), row0 + grp, grp)
        _ffn_row_terms(z_cur, wp_ref, hk_ref, row0, grp)
        if row0 > 0:
            _ffn_down(hk_ref, wp_ref, a_ref, dn_ref, y_ref, row0 - grp, grp)
    _ffn_down(hk_ref, wp_ref, a_ref, dn_ref, y_ref, rows - grp, grp)


def _ffn(h2, up, conv_w, conv_b, down):
    b, l, d = h2.shape
    rows = l // GRID_W
    tc = FFN_TILE
    nt = D_FF // tc
    assert rows % FFN_GROUP == 0 and tc == 256, "the conv pairs the two lane tiles of a channel tile"
    return pl.pallas_call(
        _ffn_kernel,
        grid=(b, nt),
        in_specs=[pl.BlockSpec((1, l, d), lambda i, t: (i, 0, 0)),
                  pl.BlockSpec((d, tc), lambda i, t: (0, t)),
                  pl.BlockSpec((d, tc), lambda i, t: (0, nt + t)),
                  pl.BlockSpec((9, tc), lambda i, t: (0, t)),
                  pl.BlockSpec((9, tc), lambda i, t: (0, nt + t)),
                  pl.BlockSpec((1, tc), lambda i, t: (0, t)),
                  pl.BlockSpec((1, tc), lambda i, t: (0, nt + t)),
                  pl.BlockSpec((tc, d), lambda i, t: (t, 0))],
        out_specs=pl.BlockSpec((1, d // 128, l, 128), lambda i, t: (i, 0, 0, 0)),
        out_shape=jax.ShapeDtypeStruct((b, d // 128, l, 128), F32),
        scratch_shapes=[pltpu.VMEM((l, d), BF16),
                        pltpu.VMEM((3, 2, rows + 2, 8, 16, 128), BF16),
                        pltpu.VMEM((l, tc), BF16),
                        pltpu.VMEM((2, 10, 16, 128), BF16),
                        pltpu.VMEM((d // 128, FFN_GROUP * GRID_W, 128), F32)],
        compiler_params=_params("arbitrary", "arbitrary"),
        name="ffn",
    )(h2, up, up, conv_w, conv_w, conv_b, conv_b, down)


def _final_kernel(x1_ref, y_ref, mod_ref, g_ref, o_ref):
    n_slab = y_ref.shape[1]
    for r in range(y_ref.shape[2] // GRID_W):
        for a in range(8):
            tok = slice(r * GRID_W + 8 * a, r * GRID_W + 8 * a + 8)
            y = jnp.concatenate([y_ref[0, j, pl.ds(r * GRID_W + a, 8, stride=8), :]
                                 for j in range(n_slab)], axis=-1)
            x = x1_ref[0, tok, :] + mod_ref[0, 5:6, :] * y
            ms = jnp.mean(x * x, axis=-1, keepdims=True)
            o_ref[0, tok, :] = x * lax.rsqrt(ms + EPS) * g_ref[...]


def _final(x1, y, mod, g, *, tm):
    b, l, d = x1.shape
    tok = pl.BlockSpec((1, tm, d), lambda i, j: (i, j, 0))
    return pl.pallas_call(
        _final_kernel,
        grid=(b, l // tm),
        in_specs=[tok, pl.BlockSpec((1, d // 128, tm, 128), lambda i, j: (i, 0, j, 0)),
                  pl.BlockSpec((1, N_MOD, d), lambda i, j: (i, 0, 0)),
                  pl.BlockSpec((1, d), lambda i, j: (0, 0))],
        out_specs=tok,
        out_shape=jax.ShapeDtypeStruct((b, l, d), F32),
        compiler_params=_params("arbitrary", "arbitrary"),
        name="final",
    )(x1, y, mod, g)


def kernel(x, c, ctx, c_ctx, ada_w, ada_b, norm1_g, w_in, hg_lb, hg_onorm_g, w_a, w_b, w_out,
           norm2_g, ffn_up, ffn_conv_w, ffn_conv_b, ffn_down, final_g):
    b, seq, d = x.shape
    assert ada_w.shape[0] == 1, "single-layer kernel"
    assert (d, seq % GRID_W) == (D_MODEL, 0)

    lb = jnp.cumsum(jax.nn.softmax(hg_lb.astype(F32), axis=0), axis=0)[0].reshape(1, HG_DIM)

    n_rows = 16
    rows = jnp.concatenate([c, c_ctx[None, :], jnp.zeros((n_rows - b - 1, d), F32)], axis=0)
    mod = _ada(rows, ada_w[0], ada_b[0])
    mod_x = mod[:b].reshape(b, N_MOD, d)
    mod_c = mod[b:b + 1].reshape(1, N_MOD, d)

    w_in_b = w_in[0].astype(BF16)
    g1 = norm1_g[0].reshape(1, d)
    hg_x, u_f, gates = _inproj(x, mod_x, g1, lb, w_in_b, tm=256, full=True)
    w_ctx = w_in_b[:, COL_Q * HG_DIM:(COL_I + 1) * HG_DIM]
    (hg_c,) = _inproj(ctx, mod_c, g1, lb, w_ctx, tm=256, full=False)

    s_ctx = _hgrn(hg_c, None, emit_o=False)
    o_f, o_b = _hgrn(hg_x, s_ctx, emit_o=True)

    chan, pos = _dft_tables(seq)
    y_f = _fnet(u_f, jnp.asarray(chan).astype(BF16), jnp.asarray(pos).astype(BF16), tm=512)

    x1, h2 = _merge(x, o_f, o_b, gates, y_f, mod_x, hg_onorm_g[0].reshape(1, HG_DIM),
                    norm2_g[0].reshape(1, d), w_a[0].astype(BF16), w_b[0].astype(BF16),
                    w_out[0].astype(BF16), tm=512)

    y = _ffn(h2, ffn_up[0].astype(BF16), ffn_conv_w[0].reshape(9, 2 * D_FF),
             ffn_conv_b[0].reshape(1, 2 * D_FF), ffn_down[0].astype(BF16))
    return _final(x1, y, mod_x, final_g.reshape(1, d), tm=512)
```

```python
import functools

import numpy as np
import jax
import jax.numpy as jnp
from jax import lax
from jax.experimental import pallas as pl
from jax.experimental.pallas import tpu as pltpu

D_MODEL = 1024
GRID_W = 64
FNET_DIM = 512
FNET_GROUPS = 4
FNET_GROUP_DIM = FNET_DIM // FNET_GROUPS
HG_DIM = 512
HG_HEADS = 4
HG_HEAD_DIM = HG_DIM // HG_HEADS
D_FF = 2816
N_MOD = 6
EPS = 1e-6

COL_FNET, COL_Q, COL_FF, COL_FB, COL_I, COL_G = 0, 1, 2, 3, 4, 5

HG_BLOCK = 256
HG_CHUNK = 64
HG_UNROLL = 2
HG_SUB = 16
FFN_TILE = 256

BF16 = jnp.bfloat16
F32 = jnp.float32
VMEM_LIMIT = 56 * 1024 * 1024


def _sigmoid(x):
    return 1.0 / (1.0 + jnp.exp(-x))


def _silu(x):
    return x * _sigmoid(x)


def _dot(a, b):
    return jnp.dot(a, b, preferred_element_type=F32)


def _dot_nt(a, b):
    return lax.dot_general(a, b, (((1,), (1,)), ((), ())), preferred_element_type=F32)


def _params(*sem, flags=None):
    return pltpu.CompilerParams(dimension_semantics=sem, vmem_limit_bytes=VMEM_LIMIT, flags=flags)


def _ada_kernel(c_ref, w_ref, b_ref, o_ref):
    a = _silu(c_ref[...])
    o_ref[...] = _dot(a.astype(BF16), w_ref[...].astype(BF16)) + b_ref[...]


def _ada(rows, ada_w, ada_b):
    m, d = rows.shape
    n = ada_w.shape[1]
    tn = 1536
    return pl.pallas_call(
        _ada_kernel,
        grid=(n // tn,),
        in_specs=[pl.BlockSpec((m, d), lambda j: (0, 0)),
                  pl.BlockSpec((d, tn), lambda j: (0, j)),
                  pl.BlockSpec((1, tn), lambda j: (0, j))],
        out_specs=pl.BlockSpec((m, tn), lambda j: (0, j)),
        out_shape=jax.ShapeDtypeStruct((m, n), F32),
        compiler_params=_params("arbitrary"),
        name="ada",
    )(rows, ada_w, ada_b.reshape(1, n))


def _modulated_norm(x, g, shift, scale):
    ms = jnp.mean(x * x, axis=-1, keepdims=True)
    return (x * lax.rsqrt(ms + EPS) * g) * (1.0 + scale) + shift


def _inproj_kernel(x_ref, mod_ref, g_ref, lb_ref, w_ref, *o_refs, hg_col0, full):
    hb = _modulated_norm(x_ref[0], g_ref[...], mod_ref[0, 0:1, :], mod_ref[0, 1:2, :]).astype(BF16)
    lb = lb_ref[...]

    def proj(col, width=HG_DIM):
        return _dot(hb, w_ref[:, col * HG_DIM:col * HG_DIM + width])

    hg_ref, hl_ref = o_refs[0], o_refs[1]
    hg_ref[0, :, 0:HG_DIM] = _silu(proj(hg_col0))
    for n in (1, 2):
        f = lb + (1.0 - lb) * _sigmoid(proj(hg_col0 + n))
        hg_ref[0, :, n * HG_DIM:(n + 1) * HG_DIM] = 1.0 - f
        l2f = jnp.log2(f)
        hi = l2f.astype(BF16)
        hl_ref[0, :, (2 * n - 2) * HG_DIM:(2 * n - 1) * HG_DIM] = hi
        hl_ref[0, :, (2 * n - 1) * HG_DIM:2 * n * HG_DIM] = (l2f - hi.astype(F32)).astype(BF16)
    hg_ref[0, :, 3 * HG_DIM:4 * HG_DIM] = proj(hg_col0 + 3)
    if full:
        uf_ref, gate_ref = o_refs[2], o_refs[3]
        uf_ref[0] = proj(COL_FNET).astype(BF16)
        gate_ref[0, :, 0:2 * D_MODEL] = _sigmoid(proj(COL_G + 1, 2 * D_MODEL)).astype(BF16)
        gate_ref[0, :, 2 * D_MODEL:2 * D_MODEL + HG_DIM] = _silu(proj(COL_G)).astype(BF16)


def _inproj(x, mod, g, lb, w, *, tm, full):
    b, l, d = x.shape
    n = w.shape[1]
    mod_map = (lambda i, j: (i, 0, 0)) if full else (lambda i, j: (0, 0, 0))
    tok = lambda width, dtype: (pl.BlockSpec((1, tm, width), lambda i, j: (i, j, 0)),
                                jax.ShapeDtypeStruct((b, l, width), dtype))
    outs = [tok(4 * HG_DIM, F32), tok(4 * HG_DIM, BF16)]
    if full:
        outs += [tok(FNET_DIM, BF16), tok(HG_DIM + 2 * D_MODEL, BF16)]
    return pl.pallas_call(
        functools.partial(_inproj_kernel, hg_col0=COL_Q if full else 0, full=full),
        grid=(b, l // tm),
        in_specs=[pl.BlockSpec((1, tm, d), lambda i, j: (i, j, 0)),
                  pl.BlockSpec((1, N_MOD, d), mod_map),
                  pl.BlockSpec((1, d), lambda i, j: (0, 0)),
                  pl.BlockSpec((1, HG_DIM), lambda i, j: (0, 0)),
                  pl.BlockSpec((d, n), lambda i, j: (0, 0))],
        out_specs=tuple(o[0] for o in outs),
        out_shape=tuple(o[1] for o in outs),
        compiler_params=_params("arbitrary", "arbitrary"),
        name="inproj_x" if full else "inproj_ctx",
    )(x, mod, g, lb, w)


def _hgrn_step(dirs, st_ref):
    c = HG_CHUNK
    hd = HG_HEAD_DIM
    r_idx = lax.broadcasted_iota(jnp.int32, (c, c), 0)
    c_idx = lax.broadcasted_iota(jnp.int32, (c, c), 1)
    pairs = [slice(p * 2 * hd, (p + 1) * 2 * hd) for p in range(HG_HEADS // 2)]

    def block_diag(a0, a1):
        z0, z1 = jnp.zeros(a0.shape, a0.dtype), jnp.zeros(a1.shape, a1.dtype)
        return jnp.concatenate([jnp.concatenate([a0, z0], axis=1),
                                jnp.concatenate([z1, a1], axis=1)], axis=0)

    cums = []
    for q, k, hi, lo, v, reverse, o_ref, row0, base in dirs:
        tri = jnp.where((c_idx >= r_idx) if reverse else (c_idx <= r_idx), 1.0, 0.0).astype(BF16)
        cums.append(_dot(tri, hi) + _dot(tri, lo))

    state_in, state_upd, intra = [], [], []
    for (q, k, hi, lo, v, reverse, o_ref, row0, base), cum in zip(dirs, cums):
        last = 0 if reverse else c - 1
        total = cum[last:last + 1, :]
        q_in = (q * jnp.exp2(cum)).astype(BF16)
        k_out = (k * jnp.exp2(total - cum)).astype(BF16)
        e_tot = jnp.exp2(total)
        v_b = v.astype(BF16)
        for p, ps in enumerate(pairs):
            h0, h1 = slice(ps.start, ps.start + hd), slice(ps.start + hd, ps.stop)
            v_t = jnp.concatenate([v[:, h0], v[:, h1]], axis=0).T.astype(BF16)
            state_in.append(q_in[:, ps])
            state_upd.append((base + p, e_tot[:, ps], v_t, block_diag(k_out[:, h0], k_out[:, h1])))
            if o_ref is None:
                continue
            for i in range(c // HG_SUB):
                r0 = i * HG_SUB
                rows = slice(r0, r0 + HG_SUB)
                cols = slice(r0, c) if reverse else slice(0, r0 + HG_SUB)
                mid = r0 + HG_SUB // 2 if reverse else r0 + HG_SUB // 2 - 1
                m = cum[mid:mid + 1, ps]
                q_t = (q[rows, ps] * jnp.exp2(cum[rows, ps] - m)).astype(BF16)
                k_t = (k[cols, ps] * jnp.exp2(m - cum[cols, ps])).astype(BF16)
                intra.append((o_ref, row0, rows, cols, ps, reverse, len(state_in) - 1, q_t,
                              block_diag(k_t[:, :hd], k_t[:, hd:]),
                              block_diag(v_b[cols, h0], v_b[cols, h1])))

    scores = [_dot_nt(item[7], item[8]) for item in intra]
    kv = [_dot(v_t, k_bd) for _, _, v_t, k_bd in state_upd]
    states = {}
    inter = []
    for n, (idx, e_tot_p, _, _) in enumerate(state_upd):
        st = states[idx] if idx in states else st_ref[idx]
        if intra:
            st_b = st.astype(BF16)
            inter.append(_dot_nt(state_in[n], block_diag(st_b[:, :hd], st_b[:, hd:])))
        states[idx] = st * e_tot_p + kv[n]
    for idx, st in states.items():
        st_ref[idx] = st

    for (o_ref, row0, rows, cols, ps, reverse, chain, _, _, v_bd), a in zip(intra, scores):
        n_cols = cols.stop - cols.start
        row_g = rows.start + lax.broadcasted_iota(jnp.int32, (HG_SUB, 2 * n_cols), 0)
        lane = lax.broadcasted_iota(jnp.int32, (HG_SUB, 2 * n_cols), 1)
        col_g = cols.start + jnp.where(lane >= n_cols, lane - n_cols, lane)
        a = jnp.where((col_g >= row_g) if reverse else (col_g <= row_g), a, 0.0)
        o_ref[0, pl.ds(row0 + rows.start, HG_SUB), ps] = _dot(a.astype(BF16), v_bd) + inter[chain][rows]


def _hgrn_kernel(*refs, has_s0, emit_o):
    it = iter(refs)
    fwd_refs = [next(it) for _ in range(5)]
    bwd_refs = [next(it) for _ in range(5)]
    qf_ref = fwd_refs[0]
    s0_ref = next(it) if has_s0 else None
    of_ref = next(it) if emit_o else None
    ob_ref = next(it) if emit_o else None
    sout_ref = None if emit_o else next(it)
    st_ref = next(it)
    j = pl.program_id(1)
    n_inner = qf_ref.shape[1] // HG_CHUNK

    @pl.when(j == 0)
    def _():
        if has_s0:
            st_ref[...] = s0_ref[0]
        else:
            st_ref[...] = jnp.zeros_like(st_ref)

    span = HG_UNROLL * HG_CHUNK

    def body(it, carry):
        dirs = []
        for jj in range(HG_UNROLL):
            rf = pl.multiple_of(it * span, span) + jj * HG_CHUNK
            rb = pl.multiple_of((n_inner // HG_UNROLL - 1 - it) * span, span) + (HG_UNROLL - 1 - jj) * HG_CHUNK
            sf, sb = pl.ds(rf, HG_CHUNK), pl.ds(rb, HG_CHUNK)
            dirs.append(tuple(r[0, sf, :] for r in fwd_refs) + (False, of_ref, rf, 0))
            dirs.append(tuple(r[0, sb, :] for r in bwd_refs) + (True, ob_ref, rb, HG_HEADS // 2))
        _hgrn_step(dirs, st_ref)
        return carry

    lax.fori_loop(0, n_inner // HG_UNROLL, body, 0)

    if not emit_o:
        @pl.when(j == pl.num_programs(1) - 1)
        def _():
            sout_ref[0] = st_ref[...]


def _hgrn(u, ul, s0, *, emit_o):
    b, l, _ = u.shape
    c = min(HG_BLOCK, l)
    n = l // c
    blk = (1, c, HG_DIM)
    fwd = lambda col: pl.BlockSpec(blk, lambda i, j: (i, j, col))
    bwd = lambda col: pl.BlockSpec(blk, lambda i, j: (i, n - 1 - j, col))
    st_dims = (HG_HEADS, HG_HEAD_DIM, 2 * HG_HEAD_DIM)
    state_spec = pl.BlockSpec((1,) + st_dims, lambda i, j: (i, 0, 0, 0))
    state_shape = jax.ShapeDtypeStruct((b,) + st_dims, F32)
    in_specs = [fwd(0), fwd(1), fwd(0), fwd(1), fwd(3), bwd(0), bwd(2), bwd(2), bwd(3), bwd(3)]
    args = [u, u, ul, ul, u, u, u, ul, ul, u]
    if s0 is not None:
        in_specs.append(state_spec)
        args.append(s0)
    if emit_o:
        o_shape = jax.ShapeDtypeStruct((b, l, HG_DIM), F32)
        out_shape = (o_shape, o_shape)
        out_specs = (pl.BlockSpec(blk, lambda i, j: (i, j, 0)),
                     pl.BlockSpec(blk, lambda i, j: (i, n - 1 - j, 0)))
    else:
        out_shape = state_shape
        out_specs = state_spec
    return pl.pallas_call(
        functools.partial(_hgrn_kernel, has_s0=s0 is not None, emit_o=emit_o),
        grid=(b, n),
        in_specs=in_specs,
        out_specs=out_specs,
        out_shape=out_shape,
        scratch_shapes=[pltpu.VMEM(st_dims, F32)],
        compiler_params=_params("arbitrary", "arbitrary"),
        name="hgrn_x" if emit_o else "hgrn_ctx",
    )(*args)


def _dft_tables(seq):
    gd = FNET_GROUP_DIM
    kc = (np.arange(gd)[:, None] * np.arange(gd)[None, :]) % gd
    ang_c = 2.0 * np.pi * kc / gd
    scale = 1.0 / np.sqrt(float(seq) * gd)
    chan = np.concatenate([np.cos(ang_c), np.sin(ang_c)], axis=1) * scale
    kl = (np.arange(seq)[:, None] * np.arange(seq)[None, :]) % seq
    ang_l = 2.0 * np.pi * kl / seq
    pos = np.concatenate([np.cos(ang_l), -np.sin(ang_l)], axis=1)
    return chan.astype(np.float32), pos.astype(np.float32)


def _fnet_kernel(u_ref, chan_ref, pos_ref, y_ref, pq_ref):
    seq = u_ref.shape[1]

    @pl.when(pl.program_id(1) == 0)
    def _():
        for g in range(FNET_GROUPS):
            gs = slice(g * FNET_GROUP_DIM, (g + 1) * FNET_GROUP_DIM)
            pq = _dot(u_ref[0, :, gs], chan_ref[...])
            pq_ref[0:seq, gs] = pq[:, :FNET_GROUP_DIM].astype(BF16)
            pq_ref[seq:2 * seq, gs] = pq[:, FNET_GROUP_DIM:].astype(BF16)

    y_ref[0] = _dot(pos_ref[...], pq_ref[...]).astype(BF16)


def _fnet(u, chan, pos, *, tm):
    b, seq, _ = u.shape
    return pl.pallas_call(
        _fnet_kernel,
        grid=(b, seq // tm),
        in_specs=[pl.BlockSpec((1, seq, FNET_DIM), lambda i, j: (i, 0, 0)),
                  pl.BlockSpec((FNET_GROUP_DIM, 2 * FNET_GROUP_DIM), lambda i, j: (0, 0)),
                  pl.BlockSpec((tm, 2 * seq), lambda i, j: (j, 0))],
        out_specs=pl.BlockSpec((1, tm, FNET_DIM), lambda i, j: (i, j, 0)),
        out_shape=jax.ShapeDtypeStruct((b, seq, FNET_DIM), BF16),
        scratch_shapes=[pltpu.VMEM((2 * seq, FNET_DIM), BF16)],
        compiler_params=_params("arbitrary", "arbitrary"),
        name="fnet",
    )(u, chan, pos)


def _merge_kernel(x_ref, of_ref, ob_ref, sg_ref, ga_ref, gb_ref, yf_ref, mod_ref, og_ref, n2g_ref,
                  wa_ref, wb_ref, wo_ref, x1_ref, h2_ref):
    o = of_ref[0] + ob_ref[0]
    parts = []
    for h in range(HG_HEADS):
        oh = o[:, h * HG_HEAD_DIM:(h + 1) * HG_HEAD_DIM]
        parts.append(oh * lax.rsqrt(jnp.mean(oh * oh, axis=-1, keepdims=True) + EPS))
    on = jnp.concatenate(parts, axis=-1) * og_ref[...] * sg_ref[0].astype(F32)
    y_b = _dot(on.astype(BF16), wb_ref[...])
    y_a = _dot(yf_ref[0], wa_ref[...])
    m = ga_ref[0].astype(F32) * y_a + gb_ref[0].astype(F32) * y_b
    yx = _dot(m.astype(BF16), wo_ref[...])
    x1 = x_ref[0] + mod_ref[0, 2:3, :] * yx
    x1_ref[0] = x1
    h2 = _modulated_norm(x1, n2g_ref[...], mod_ref[0, 3:4, :], mod_ref[0, 4:5, :])
    h2_ref[0] = h2.astype(BF16)


def _merge(x, o_f, o_b, gates, y_f, mod, og, n2g, w_a, w_b, w_out, *, tm):
    b, l, d = x.shape
    tok = lambda w, col: pl.BlockSpec((1, tm, w), lambda i, j: (i, j, col))
    const = lambda shape: pl.BlockSpec(shape, lambda i, j: tuple(0 for _ in shape))
    return pl.pallas_call(
        _merge_kernel,
        grid=(b, l // tm),
        in_specs=[tok(d, 0), tok(HG_DIM, 0), tok(HG_DIM, 0), tok(HG_DIM, 2 * d // HG_DIM),
                  tok(d, 0), tok(d, 1), tok(FNET_DIM, 0),
                  pl.BlockSpec((1, N_MOD, d), lambda i, j: (i, 0, 0)),
                  const((1, HG_DIM)), const((1, d)),
                  const((FNET_DIM, d)), const((HG_DIM, d)), const((d, d))],
        out_specs=(tok(d, 0), tok(d, 0)),
        out_shape=(jax.ShapeDtypeStruct((b, l, d), F32), jax.ShapeDtypeStruct((b, l, d), BF16)),
        compiler_params=_params("arbitrary", "arbitrary"),
        name="merge",
    )(x, o_f, o_b, gates, gates, gates, y_f, mod, og, n2g, w_a, w_b, w_out)


FFN_GROUP = 8


def _ffn_permute_tokens(h_ref, hp_ref, tmp_ref):
    n_tok = tmp_ref.shape[1]

    def body(g, carry):
        tok = pl.ds(pl.multiple_of(g * n_tok, n_tok), n_tok)
        hf = h_ref[0, tok, :].astype(F32)
        for lt in range(tmp_ref.shape[0]):
            for r in range(n_tok // GRID_W):
                for a in range(8):
                    r0 = r * GRID_W + 8 * a
                    tmp_ref[lt, pl.ds(r * GRID_W + a, 8, stride=8), :] = (
                        hf[r0:r0 + 8, lt * 128:(lt + 1) * 128])
        for lt in range(tmp_ref.shape[0]):
            hp_ref[tok, lt * 128:(lt + 1) * 128] = tmp_ref[lt].astype(BF16)
        return carry

    lax.fori_loop(0, h_ref.shape[1] // n_tok, body, 0)


def _ffn_up(hp_ref, up_refs, row0, n_rows):
    h = hp_ref[row0 * GRID_W:(row0 + n_rows) * GRID_W, :]
    return [_dot(h, up_ref[...]) for up_ref in up_refs]


def _pair(lo, hi):
    return jnp.concatenate([lo, hi], axis=0).astype(BF16)


def _ffn_row(z, r, taps, bias, first, last, sub):
    lo = [z[r * GRID_W + 8 * b:r * GRID_W + 8 * b + 8, 0:128] for b in range(8)]
    hi = [z[r * GRID_W + 8 * b:r * GRID_W + 8 * b + 8, 128:256] for b in range(8)]
    mid = [_pair(lo[b], hi[b]) for b in range(8)]
    down = lambda v: jnp.where(sub == 0, 0.0, pltpu.roll(v, 1, axis=0))
    left = [_pair(down(lo[7]), down(hi[7]))] + mid[0:7]
    up = lambda v: jnp.where(sub == 7, 0.0, pltpu.roll(v, 7, axis=0))
    right = mid[1:8] + [_pair(up(lo[0]), up(hi[0]))]
    term = lambda kh: [left[b] * taps[3 * kh] + mid[b] * taps[3 * kh + 1] + right[b] * taps[3 * kh + 2]
                       for b in range(8)]
    above = None if first else term(2)
    here = term(1)
    below = None if last else [t + bias for t in term(0)]
    return above, here, below


def _ffn_down(a_ref, dn_ref, y_ref, row0, n_rows):
    tok = slice(row0 * GRID_W, (row0 + n_rows) * GRID_W)
    part = _dot(a_ref[tok, :].astype(BF16), dn_ref[...])
    for j in range(y_ref.shape[1]):
        y_ref[0, j, tok, :] += part[:, j * 128:(j + 1) * 128]


def _ffn_kernel(h_ref, up1_ref, up2_ref, cw1_ref, cw2_ref, cb1_ref, cb2_ref, dn_ref, y_ref,
                hp_ref, a_ref, wp_ref, tmp_ref):
    t = pl.program_id(1)
    rows = hp_ref.shape[0] // GRID_W
    grp = FFN_GROUP

    @pl.when(t == 0)
    def _():
        _ffn_permute_tokens(h_ref, hp_ref, tmp_ref)
        y_ref[...] = jnp.zeros_like(y_ref)

    for half, (cw_ref, cb_ref) in enumerate(((cw1_ref, cb1_ref), (cw2_ref, cb2_ref))):
        for k in range(10):
            row = cw_ref[k:k + 1, :] if k < 9 else cb_ref[...]
            wp_ref[half, k] = _pair(jnp.broadcast_to(row[:, 0:128], (8, 128)),
                                    jnp.broadcast_to(row[:, 128:256], (8, 128)))
    taps = [[wp_ref[half, k] for k in range(9)] for half in range(2)]
    bias = [wp_ref[half, 9] for half in range(2)]
    sub = lax.broadcasted_iota(jnp.int32, (8, 128), 0)

    def gate(c, out_row):
        for b in range(8):
            act = (_silu(c[0][b]) * c[1][b]).astype(F32)
            tok_b = slice(out_row * GRID_W + 8 * b, out_row * GRID_W + 8 * b + 8)
            a_ref[tok_b, 0:128] = act[0:8]
            a_ref[tok_b, 128:256] = act[8:16]

    prev = [None, None]
    cur = [[bias[half]] * 8 for half in range(2)]
    z_next = _ffn_up(hp_ref, (up1_ref, up2_ref), 0, grp)
    for row0 in range(0, rows, grp):
        z_cur = z_next
        if row0 + grp < rows:
            z_next = _ffn_up(hp_ref, (up1_ref, up2_ref), row0 + grp, grp)
        for rr in range(grp):
            r = row0 + rr
            done = []
            for half in range(2):
                above, here, below = _ffn_row(z_cur[half], rr, taps[half], bias[half],
                                              r == 0, r == rows - 1, sub)
                if above is not None:
                    done.append([prev[half][b] + above[b] for b in range(8)])
                prev[half] = [cur[half][b] + here[b] for b in range(8)]
                cur[half] = below
            if done:
                gate(done, r - 1)
            if rr == 0 and row0 > 0:
                _ffn_down(a_ref, dn_ref, y_ref, row0 - grp, grp)
    gate(prev, rows - 1)
    _ffn_down(a_ref, dn_ref, y_ref, rows - grp, grp)


def _ffn(h2, up, conv_w, conv_b, down):
    b, l, d = h2.shape
    rows = l // GRID_W
    tc = FFN_TILE
    nt = D_FF // tc
    assert rows % FFN_GROUP == 0 and tc == 256, "the conv pairs the two lane tiles of a channel tile"
    return pl.pallas_call(
        _ffn_kernel,
        grid=(b, nt),
        in_specs=[pl.BlockSpec((1, l, d), lambda i, t: (i, 0, 0)),
                  pl.BlockSpec((d, tc), lambda i, t: (0, t)),
                  pl.BlockSpec((d, tc), lambda i, t: (0, nt + t)),
                  pl.BlockSpec((9, tc), lambda i, t: (0, t)),
                  pl.BlockSpec((9, tc), lambda i, t: (0, nt + t)),
                  pl.BlockSpec((1, tc), lambda i, t: (0, t)),
                  pl.BlockSpec((1, tc), lambda i, t: (0, nt + t)),
                  pl.BlockSpec((tc, d), lambda i, t: (t, 0))],
        out_specs=pl.BlockSpec((1, d // 128, l, 128), lambda i, t: (i, 0, 0, 0)),
        out_shape=jax.ShapeDtypeStruct((b, d // 128, l, 128), F32),
        scratch_shapes=[pltpu.VMEM((l, d), BF16),
                        pltpu.VMEM((l, tc), F32),
                        pltpu.VMEM((2, 10, 16, 128), BF16),
                        pltpu.VMEM((d // 128, FFN_GROUP * GRID_W, 128), F32)],
        compiler_params=_params("arbitrary", "arbitrary"),
        name="ffn",
    )(h2, up, up, conv_w, conv_w, conv_b, conv_b, down)


def _final_kernel(x1_ref, y_ref, mod_ref, g_ref, o_ref):
    n_slab = y_ref.shape[1]
    for r in range(y_ref.shape[2] // GRID_W):
        for a in range(8):
            tok = slice(r * GRID_W + 8 * a, r * GRID_W + 8 * a + 8)
            y = jnp.concatenate([y_ref[0, j, pl.ds(r * GRID_W + a, 8, stride=8), :]
                                 for j in range(n_slab)], axis=-1)
            x = x1_ref[0, tok, :] + mod_ref[0, 5:6, :] * y
            ms = jnp.mean(x * x, axis=-1, keepdims=True)
            o_ref[0, tok, :] = x * lax.rsqrt(ms + EPS) * g_ref[...]


def _final(x1, y, mod, g, *, tm):
    b, l, d = x1.shape
    tok = pl.BlockSpec((1, tm, d), lambda i, j: (i, j, 0))
    return pl.pallas_call(
        _final_kernel,
        grid=(b, l // tm),
        in_specs=[tok, pl.BlockSpec((1, d // 128, tm, 128), lambda i, j: (i, 0, j, 0)),
                  pl.BlockSpec((1, N_MOD, d), lambda i, j: (i, 0, 0)),
                  pl.BlockSpec((1, d), lambda i, j: (0, 0))],
        out_specs=tok,
        out_shape=jax.ShapeDtypeStruct((b, l, d), F32),
        compiler_params=_params("arbitrary", "arbitrary"),
        name="final",
    )(x1, y, mod, g)


def kernel(x, c, ctx, c_ctx, ada_w, ada_b, norm1_g, w_in, hg_lb, hg_onorm_g, w_a, w_b, w_out,
           norm2_g, ffn_up, ffn_conv_w, ffn_conv_b, ffn_down, final_g):
    b, seq, d = x.shape
    assert ada_w.shape[0] == 1, "single-layer kernel"
    assert (d, seq % GRID_W) == (D_MODEL, 0)

    lb = jnp.cumsum(jax.nn.softmax(hg_lb.astype(F32), axis=0), axis=0)[0].reshape(1, HG_DIM)

    n_rows = 16
    rows = jnp.concatenate([c, c_ctx[None, :], jnp.zeros((n_rows - b - 1, d), F32)], axis=0)
    mod = _ada(rows, ada_w[0], ada_b[0])
    mod_x = mod[:b].reshape(b, N_MOD, d)
    mod_c = mod[b:b + 1].reshape(1, N_MOD, d)

    w_in_b = w_in[0].astype(BF16)
    g1 = norm1_g[0].reshape(1, d)
    hg_x, hl_x, u_f, gates = _inproj(x, mod_x, g1, lb, w_in_b, tm=256, full=True)
    w_ctx = w_in_b[:, COL_Q * HG_DIM:(COL_I + 1) * HG_DIM]
    hg_c, hl_c = _inproj(ctx, mod_c, g1, lb, w_ctx, tm=256, full=False)

    s_ctx = _hgrn(hg_c, hl_c, None, emit_o=False)
    o_f, o_b = _hgrn(hg_x, hl_x, s_ctx, emit_o=True)

    chan, pos = _dft_tables(seq)
    y_f = _fnet(u_f, jnp.asarray(chan).astype(BF16), jnp.asarray(pos).astype(BF16), tm=512)

    x1, h2 = _merge(x, o_f, o_b, gates, y_f, mod_x, hg_onorm_g[0].reshape(1, HG_DIM),
                    norm2_g[0].reshape(1, d), w_a[0].astype(BF16), w_b[0].astype(BF16),
                    w_out[0].astype(BF16), tm=512)

    y = _ffn(h2, ffn_up[0].astype(BF16), ffn_conv_w[0].reshape(9, 2 * D_FF),
             ffn_conv_b[0].reshape(1, 2 * D_FF), ffn_down[0].astype(BF16))
    return _final(x1, y, mod_x, final_g.reshape(1, d), tm=512)
```

```python
import functools

import numpy as np
import jax
import jax.numpy as jnp
from jax import lax
from jax.experimental import pallas as pl
from jax.experimental.pallas import tpu as pltpu

D_MODEL = 1024
GRID_W = 64
FNET_DIM = 512
FNET_GROUPS = 4
FNET_GROUP_DIM = FNET_DIM // FNET_GROUPS
HG_DIM = 512
HG_HEADS = 4
HG_HEAD_DIM = HG_DIM // HG_HEADS
D_FF = 2816
N_MOD = 6
EPS = 1e-6

COL_FNET, COL_Q, COL_FF, COL_FB, COL_I, COL_G = 0, 1, 2, 3, 4, 5

HG_BLOCK = 256
HG_CHUNK = 64
HG_UNROLL = 2
HG_SUB = 16
FFN_TILE = 256

BF16 = jnp.bfloat16
F32 = jnp.float32
VMEM_LIMIT = 56 * 1024 * 1024


def _sigmoid(x):
    return 1.0 / (1.0 + jnp.exp(-x))


def _silu(x):
    return x * _sigmoid(x)


def _dot(a, b):
    return jnp.dot(a, b, preferred_element_type=F32)


def _dot_nt(a, b):
    return lax.dot_general(a, b, (((1,), (1,)), ((), ())), preferred_element_type=F32)


def _params(*sem, flags=None):
    return pltpu.CompilerParams(dimension_semantics=sem, vmem_limit_bytes=VMEM_LIMIT, flags=flags)


def _ada_kernel(c_ref, w_ref, b_ref, o_ref):
    a = _silu(c_ref[...])
    o_ref[...] = _dot(a.astype(BF16), w_ref[...].astype(BF16)) + b_ref[...]


def _ada(rows, ada_w, ada_b):
    m, d = rows.shape
    n = ada_w.shape[1]
    tn = 1536
    return pl.pallas_call(
        _ada_kernel,
        grid=(n // tn,),
        in_specs=[pl.BlockSpec((m, d), lambda j: (0, 0)),
                  pl.BlockSpec((d, tn), lambda j: (0, j)),
                  pl.BlockSpec((1, tn), lambda j: (0, j))],
        out_specs=pl.BlockSpec((m, tn), lambda j: (0, j)),
        out_shape=jax.ShapeDtypeStruct((m, n), F32),
        compiler_params=_params("arbitrary"),
        name="ada",
    )(rows, ada_w, ada_b.reshape(1, n))


def _modulated_norm(x, g, shift, scale):
    ms = jnp.mean(x * x, axis=-1, keepdims=True)
    return (x * lax.rsqrt(ms + EPS) * g) * (1.0 + scale) + shift


def _inproj_kernel(x_ref, mod_ref, g_ref, lb_ref, w_ref, *o_refs, hg_col0, full):
    hb = _modulated_norm(x_ref[0], g_ref[...], mod_ref[0, 0:1, :], mod_ref[0, 1:2, :]).astype(BF16)
    lb = lb_ref[...]

    def proj(col, width=HG_DIM):
        return _dot(hb, w_ref[:, col * HG_DIM:col * HG_DIM + width])

    hg_ref = o_refs[0]
    hg_ref[0, :, 0:HG_DIM] = _silu(proj(hg_col0))
    for n in (1, 2):
        f = lb + (1.0 - lb) * _sigmoid(proj(hg_col0 + n))
        hg_ref[0, :, n * HG_DIM:(n + 1) * HG_DIM] = jnp.log2(f)
    hg_ref[0, :, 3 * HG_DIM:4 * HG_DIM] = proj(hg_col0 + 3)
    if full:
        uf_ref, gate_ref = o_refs[1], o_refs[2]
        uf_ref[0] = proj(COL_FNET).astype(BF16)
        gate_ref[0, :, 0:2 * D_MODEL] = _sigmoid(proj(COL_G + 1, 2 * D_MODEL)).astype(BF16)
        gate_ref[0, :, 2 * D_MODEL:2 * D_MODEL + HG_DIM] = _silu(proj(COL_G)).astype(BF16)


def _inproj(x, mod, g, lb, w, *, tm, full):
    b, l, d = x.shape
    n = w.shape[1]
    mod_map = (lambda i, j: (i, 0, 0)) if full else (lambda i, j: (0, 0, 0))
    tok = lambda width, dtype: (pl.BlockSpec((1, tm, width), lambda i, j: (i, j, 0)),
                                jax.ShapeDtypeStruct((b, l, width), dtype))
    outs = [tok(4 * HG_DIM, F32)]
    if full:
        outs += [tok(FNET_DIM, BF16), tok(HG_DIM + 2 * D_MODEL, BF16)]
    return pl.pallas_call(
        functools.partial(_inproj_kernel, hg_col0=COL_Q if full else 0, full=full),
        grid=(b, l // tm),
        in_specs=[pl.BlockSpec((1, tm, d), lambda i, j: (i, j, 0)),
                  pl.BlockSpec((1, N_MOD, d), mod_map),
                  pl.BlockSpec((1, d), lambda i, j: (0, 0)),
                  pl.BlockSpec((1, HG_DIM), lambda i, j: (0, 0)),
                  pl.BlockSpec((d, n), lambda i, j: (0, 0))],
        out_specs=tuple(o[0] for o in outs),
        out_shape=tuple(o[1] for o in outs),
        compiler_params=_params("arbitrary", "arbitrary"),
        name="inproj_x" if full else "inproj_ctx",
    )(x, mod, g, lb, w)


def _hgrn_step(dirs, st_ref):
    c = HG_CHUNK
    hd = HG_HEAD_DIM
    r_idx = lax.broadcasted_iota(jnp.int32, (c, c), 0)
    c_idx = lax.broadcasted_iota(jnp.int32, (c, c), 1)
    pairs = [slice(p * 2 * hd, (p + 1) * 2 * hd) for p in range(HG_HEADS // 2)]

    def block_diag(a0, a1):
        z0, z1 = jnp.zeros(a0.shape, a0.dtype), jnp.zeros(a1.shape, a1.dtype)
        return jnp.concatenate([jnp.concatenate([a0, z0], axis=1),
                                jnp.concatenate([z1, a1], axis=1)], axis=0)

    cums = []
    for q, l2f, v, reverse, o_ref, row0, base in dirs:
        tri = jnp.where((c_idx >= r_idx) if reverse else (c_idx <= r_idx), 1.0, 0.0).astype(BF16)
        hi = l2f.astype(BF16)
        lo = (l2f - hi.astype(F32)).astype(BF16)
        cums.append(_dot(tri, hi) + _dot(tri, lo))

    state_in, state_upd, intra = [], [], []
    for (q, l2f, v, reverse, o_ref, row0, base), cum in zip(dirs, cums):
        k = 1.0 - jnp.exp2(l2f)
        last = 0 if reverse else c - 1
        total = cum[last:last + 1, :]
        q_in = (q * jnp.exp2(cum)).astype(BF16)
        k_out = (k * jnp.exp2(total - cum)).astype(BF16)
        e_tot = jnp.exp2(total)
        v_b = v.astype(BF16)
        for p, ps in enumerate(pairs):
            h0, h1 = slice(ps.start, ps.start + hd), slice(ps.start + hd, ps.stop)
            v_t = jnp.concatenate([v[:, h0], v[:, h1]], axis=0).T.astype(BF16)
            state_in.append(q_in[:, ps])
            state_upd.append((base + p, e_tot[:, ps], v_t, block_diag(k_out[:, h0], k_out[:, h1])))
            if o_ref is None:
                continue
            for i in range(c // HG_SUB):
                r0 = i * HG_SUB
                rows = slice(r0, r0 + HG_SUB)
                cols = slice(r0, c) if reverse else slice(0, r0 + HG_SUB)
                mid = r0 + HG_SUB // 2 if reverse else r0 + HG_SUB // 2 - 1
                m = cum[mid:mid + 1, ps]
                q_t = (q[rows, ps] * jnp.exp2(cum[rows, ps] - m)).astype(BF16)
                k_t = (k[cols, ps] * jnp.exp2(m - cum[cols, ps])).astype(BF16)
                intra.append((o_ref, row0, rows, cols, ps, reverse, len(state_in) - 1, q_t,
                              block_diag(k_t[:, :hd], k_t[:, hd:]),
                              block_diag(v_b[cols, h0], v_b[cols, h1])))

    scores = [_dot_nt(item[7], item[8]) for item in intra]
    kv = [_dot(v_t, k_bd) for _, _, v_t, k_bd in state_upd]
    states = {}
    inter = []
    for n, (idx, e_tot_p, _, _) in enumerate(state_upd):
        st = states[idx] if idx in states else st_ref[idx]
        if intra:
            st_b = st.astype(BF16)
            inter.append(_dot_nt(state_in[n], block_diag(st_b[:, :hd], st_b[:, hd:])))
        states[idx] = st * e_tot_p + kv[n]
    for idx, st in states.items():
        st_ref[idx] = st

    for (o_ref, row0, rows, cols, ps, reverse, chain, _, _, v_bd), a in zip(intra, scores):
        n_cols = cols.stop - cols.start
        row_g = rows.start + lax.broadcasted_iota(jnp.int32, (HG_SUB, 2 * n_cols), 0)
        lane = lax.broadcasted_iota(jnp.int32, (HG_SUB, 2 * n_cols), 1)
        col_g = cols.start + jnp.where(lane >= n_cols, lane - n_cols, lane)
        a = jnp.where((col_g >= row_g) if reverse else (col_g <= row_g), a, 0.0)
        o_ref[0, pl.ds(row0 + rows.start, HG_SUB), ps] = _dot(a.astype(BF16), v_bd) + inter[chain][rows]


def _hgrn_kernel(*refs, has_s0, emit_o):
    it = iter(refs)
    fwd_refs = [next(it) for _ in range(3)]
    bwd_refs = [next(it) for _ in range(3)]
    qf_ref = fwd_refs[0]
    s0_ref = next(it) if has_s0 else None
    of_ref = next(it) if emit_o else None
    ob_ref = next(it) if emit_o else None
    sout_ref = None if emit_o else next(it)
    st_ref = next(it)
    j = pl.program_id(1)
    n_inner = qf_ref.shape[1] // HG_CHUNK

    @pl.when(j == 0)
    def _():
        if has_s0:
            st_ref[...] = s0_ref[0]
        else:
            st_ref[...] = jnp.zeros_like(st_ref)

    span = HG_UNROLL * HG_CHUNK

    def body(it, carry):
        dirs = []
        for jj in range(HG_UNROLL):
            rf = pl.multiple_of(it * span, span) + jj * HG_CHUNK
            rb = pl.multiple_of((n_inner // HG_UNROLL - 1 - it) * span, span) + (HG_UNROLL - 1 - jj) * HG_CHUNK
            sf, sb = pl.ds(rf, HG_CHUNK), pl.ds(rb, HG_CHUNK)
            dirs.append(tuple(r[0, sf, :] for r in fwd_refs) + (False, of_ref, rf, 0))
            dirs.append(tuple(r[0, sb, :] for r in bwd_refs) + (True, ob_ref, rb, HG_HEADS // 2))
        _hgrn_step(dirs, st_ref)
        return carry

    lax.fori_loop(0, n_inner // HG_UNROLL, body, 0)

    if not emit_o:
        @pl.when(j == pl.num_programs(1) - 1)
        def _():
            sout_ref[0] = st_ref[...]


def _hgrn(u, s0, *, emit_o):
    b, l, _ = u.shape
    c = min(HG_BLOCK, l)
    n = l // c
    blk = (1, c, HG_DIM)
    fwd = lambda col: pl.BlockSpec(blk, lambda i, j: (i, j, col))
    bwd = lambda col: pl.BlockSpec(blk, lambda i, j: (i, n - 1 - j, col))
    st_dims = (HG_HEADS, HG_HEAD_DIM, 2 * HG_HEAD_DIM)
    state_spec = pl.BlockSpec((1,) + st_dims, lambda i, j: (i, 0, 0, 0))
    state_shape = jax.ShapeDtypeStruct((b,) + st_dims, F32)
    in_specs = [fwd(0), fwd(1), fwd(3), bwd(0), bwd(2), bwd(3)]
    args = [u, u, u, u, u, u]
    if s0 is not None:
        in_specs.append(state_spec)
        args.append(s0)
    if emit_o:
        o_shape = jax.ShapeDtypeStruct((b, l, HG_DIM), F32)
        out_shape = (o_shape, o_shape)
        out_specs = (pl.BlockSpec(blk, lambda i, j: (i, j, 0)),
                     pl.BlockSpec(blk, lambda i, j: (i, n - 1 - j, 0)))
    else:
        out_shape = state_shape
        out_specs = state_spec
    return pl.pallas_call(
        functools.partial(_hgrn_kernel, has_s0=s0 is not None, emit_o=emit_o),
        grid=(b, n),
        in_specs=in_specs,
        out_specs=out_specs,
        out_shape=out_shape,
        scratch_shapes=[pltpu.VMEM(st_dims, F32)],
        compiler_params=_params("arbitrary", "arbitrary"),
        name="hgrn_x" if emit_o else "hgrn_ctx",
    )(*args)


def _dft_tables(seq):
    gd = FNET_GROUP_DIM
    kc = (np.arange(gd)[:, None] * np.arange(gd)[None, :]) % gd
    ang_c = 2.0 * np.pi * kc / gd
    scale = 1.0 / np.sqrt(float(seq) * gd)
    chan = np.concatenate([np.cos(ang_c), np.sin(ang_c)], axis=1) * scale
    kl = (np.arange(seq)[:, None] * np.arange(seq)[None, :]) % seq
    ang_l = 2.0 * np.pi * kl / seq
    pos = np.concatenate([np.cos(ang_l), -np.sin(ang_l)], axis=1)
    return chan.astype(np.float32), pos.astype(np.float32)


def _fnet_kernel(u_ref, chan_ref, pos_ref, y_ref, pq_ref):
    seq = u_ref.shape[1]

    @pl.when(pl.program_id(1) == 0)
    def _():
        for g in range(FNET_GROUPS):
            gs = slice(g * FNET_GROUP_DIM, (g + 1) * FNET_GROUP_DIM)
            pq = _dot(u_ref[0, :, gs], chan_ref[...])
            pq_ref[0:seq, gs] = pq[:, :FNET_GROUP_DIM].astype(BF16)
            pq_ref[seq:2 * seq, gs] = pq[:, FNET_GROUP_DIM:].astype(BF16)

    y_ref[0] = _dot(pos_ref[...], pq_ref[...]).astype(BF16)


def _fnet(u, chan, pos, *, tm):
    b, seq, _ = u.shape
    return pl.pallas_call(
        _fnet_kernel,
        grid=(b, seq // tm),
        in_specs=[pl.BlockSpec((1, seq, FNET_DIM), lambda i, j: (i, 0, 0)),
                  pl.BlockSpec((FNET_GROUP_DIM, 2 * FNET_GROUP_DIM), lambda i, j: (0, 0)),
                  pl.BlockSpec((tm, 2 * seq), lambda i, j: (j, 0))],
        out_specs=pl.BlockSpec((1, tm, FNET_DIM), lambda i, j: (i, j, 0)),
        out_shape=jax.ShapeDtypeStruct((b, seq, FNET_DIM), BF16),
        scratch_shapes=[pltpu.VMEM((2 * seq, FNET_DIM), BF16)],
        compiler_params=_params("arbitrary", "arbitrary"),
        name="fnet",
    )(u, chan, pos)


def _merge_kernel(x_ref, of_ref, ob_ref, sg_ref, ga_ref, gb_ref, yf_ref, mod_ref, og_ref, n2g_ref,
                  wa_ref, wb_ref, wo_ref, x1_ref, h2_ref):
    o = of_ref[0] + ob_ref[0]
    parts = []
    for h in range(HG_HEADS):
        oh = o[:, h * HG_HEAD_DIM:(h + 1) * HG_HEAD_DIM]
        parts.append(oh * lax.rsqrt(jnp.mean(oh * oh, axis=-1, keepdims=True) + EPS))
    on = jnp.concatenate(parts, axis=-1) * og_ref[...] * sg_ref[0].astype(F32)
    y_b = _dot(on.astype(BF16), wb_ref[...])
    y_a = _dot(yf_ref[0], wa_ref[...])
    m = ga_ref[0].astype(F32) * y_a + gb_ref[0].astype(F32) * y_b
    yx = _dot(m.astype(BF16), wo_ref[...])
    x1 = x_ref[0] + mod_ref[0, 2:3, :] * yx
    x1_ref[0] = x1
    h2 = _modulated_norm(x1, n2g_ref[...], mod_ref[0, 3:4, :], mod_ref[0, 4:5, :])
    h2_ref[0] = h2.astype(BF16)


def _merge(x, o_f, o_b, gates, y_f, mod, og, n2g, w_a, w_b, w_out, *, tm):
    b, l, d = x.shape
    tok = lambda w, col: pl.BlockSpec((1, tm, w), lambda i, j: (i, j, col))
    const = lambda shape: pl.BlockSpec(shape, lambda i, j: tuple(0 for _ in shape))
    return pl.pallas_call(
        _merge_kernel,
        grid=(b, l // tm),
        in_specs=[tok(d, 0), tok(HG_DIM, 0), tok(HG_DIM, 0), tok(HG_DIM, 2 * d // HG_DIM),
                  tok(d, 0), tok(d, 1), tok(FNET_DIM, 0),
                  pl.BlockSpec((1, N_MOD, d), lambda i, j: (i, 0, 0)),
                  const((1, HG_DIM)), const((1, d)),
                  const((FNET_DIM, d)), const((HG_DIM, d)), const((d, d))],
        out_specs=(tok(d, 0), tok(d, 0)),
        out_shape=(jax.ShapeDtypeStruct((b, l, d), F32), jax.ShapeDtypeStruct((b, l, d), BF16)),
        compiler_params=_params("arbitrary", "arbitrary"),
        name="merge",
    )(x, o_f, o_b, gates, gates, gates, y_f, mod, og, n2g, w_a, w_b, w_out)


FFN_GROUP = 8


def _ffn_permute_tokens(h_ref, hp_ref, tmp_ref):
    n_tok = tmp_ref.shape[1]

    def body(g, carry):
        tok = pl.ds(pl.multiple_of(g * n_tok, n_tok), n_tok)
        hf = h_ref[0, tok, :].astype(F32)
        for lt in range(tmp_ref.shape[0]):
            for r in range(n_tok // GRID_W):
                for a in range(8):
                    r0 = r * GRID_W + 8 * a
                    tmp_ref[lt, pl.ds(r * GRID_W + a, 8, stride=8), :] = (
                        hf[r0:r0 + 8, lt * 128:(lt + 1) * 128])
        for lt in range(tmp_ref.shape[0]):
            hp_ref[tok, lt * 128:(lt + 1) * 128] = tmp_ref[lt].astype(BF16)
        return carry

    lax.fori_loop(0, h_ref.shape[1] // n_tok, body, 0)


def _ffn_up(hp_ref, up_refs, row0, n_rows):
    h = hp_ref[row0 * GRID_W:(row0 + n_rows) * GRID_W, :]
    return [_dot(h, up_ref[...]) for up_ref in up_refs]


def _pair(lo, hi):
    return jnp.concatenate([lo, hi], axis=0).astype(BF16)


def _ffn_row(z, r, taps, bias, first, last, sub):
    lo = [z[r * GRID_W + 8 * b:r * GRID_W + 8 * b + 8, 0:128] for b in range(8)]
    hi = [z[r * GRID_W + 8 * b:r * GRID_W + 8 * b + 8, 128:256] for b in range(8)]
    mid = [_pair(lo[b], hi[b]) for b in range(8)]
    down = lambda v: jnp.where(sub == 0, 0.0, pltpu.roll(v, 1, axis=0))
    left = [_pair(down(lo[7]), down(hi[7]))] + mid[0:7]
    up = lambda v: jnp.where(sub == 7, 0.0, pltpu.roll(v, 7, axis=0))
    right = mid[1:8] + [_pair(up(lo[0]), up(hi[0]))]
    term = lambda kh: [left[b] * taps[3 * kh] + mid[b] * taps[3 * kh + 1] + right[b] * taps[3 * kh + 2]
                       for b in range(8)]
    above = None if first else term(2)
    here = term(1)
    below = None if last else [t + bias for t in term(0)]
    return above, here, below


def _ffn_down(a_ref, dn_ref, y_ref, row0, n_rows):
    tok = slice(row0 * GRID_W, (row0 + n_rows) * GRID_W)
    part = _dot(a_ref[tok, :].astype(BF16), dn_ref[...])
    for j in range(y_ref.shape[1]):
        y_ref[0, j, tok, :] += part[:, j * 128:(j + 1) * 128]


def _ffn_kernel(h_ref, up1_ref, up2_ref, cw1_ref, cw2_ref, cb1_ref, cb2_ref, dn_ref, y_ref,
                hp_ref, a_ref, wp_ref, tmp_ref):
    t = pl.program_id(1)
    rows = hp_ref.shape[0] // GRID_W
    grp = FFN_GROUP

    @pl.when(t == 0)
    def _():
        _ffn_permute_tokens(h_ref, hp_ref, tmp_ref)
        y_ref[...] = jnp.zeros_like(y_ref)

    for half, (cw_ref, cb_ref) in enumerate(((cw1_ref, cb1_ref), (cw2_ref, cb2_ref))):
        for k in range(10):
            row = cw_ref[k:k + 1, :] if k < 9 else cb_ref[...]
            wp_ref[half, k] = _pair(jnp.broadcast_to(row[:, 0:128], (8, 128)),
                                    jnp.broadcast_to(row[:, 128:256], (8, 128)))
    taps = [[wp_ref[half, k] for k in range(9)] for half in range(2)]
    bias = [wp_ref[half, 9] for half in range(2)]
    sub = lax.broadcasted_iota(jnp.int32, (8, 128), 0)

    def gate(c, out_row):
        for b in range(8):
            act = (_silu(c[0][b]) * c[1][b]).astype(F32)
            tok_b = slice(out_row * GRID_W + 8 * b, out_row * GRID_W + 8 * b + 8)
            a_ref[tok_b, 0:128] = act[0:8]
            a_ref[tok_b, 128:256] = act[8:16]

    prev = [None, None]
    cur = [[bias[half]] * 8 for half in range(2)]
    z_next = _ffn_up(hp_ref, (up1_ref, up2_ref), 0, grp)
    for row0 in range(0, rows, grp):
        z_cur = z_next
        if row0 + grp < rows:
            z_next = _ffn_up(hp_ref, (up1_ref, up2_ref), row0 + grp, grp)
        for rr in range(grp):
            r = row0 + rr
            done = []
            for half in range(2):
                above, here, below = _ffn_row(z_cur[half], rr, taps[half], bias[half],
                                              r == 0, r == rows - 1, sub)
                if above is not None:
                    done.append([prev[half][b] + above[b] for b in range(8)])
                prev[half] = [cur[half][b] + here[b] for b in range(8)]
                cur[half] = below
            if done:
                gate(done, r - 1)
            if rr == 0 and row0 > 0:
                _ffn_down(a_ref, dn_ref, y_ref, row0 - grp, grp)
    gate(prev, rows - 1)
    _ffn_down(a_ref, dn_ref, y_ref, rows - grp, grp)


def _ffn(h2, up, conv_w, conv_b, down):
    b, l, d = h2.shape
    rows = l // GRID_W
    tc = FFN_TILE
    nt = D_FF // tc
    assert rows % FFN_GROUP == 0 and tc == 256, "the conv pairs the two lane tiles of a channel tile"
    return pl.pallas_call(
        _ffn_kernel,
        grid=(b, nt),
        in_specs=[pl.BlockSpec((1, l, d), lambda i, t: (i, 0, 0)),
                  pl.BlockSpec((d, tc), lambda i, t: (0, t)),
                  pl.BlockSpec((d, tc), lambda i, t: (0, nt + t)),
                  pl.BlockSpec((9, tc), lambda i, t: (0, t)),
                  pl.BlockSpec((9, tc), lambda i, t: (0, nt + t)),
                  pl.BlockSpec((1, tc), lambda i, t: (0, t)),
                  pl.BlockSpec((1, tc), lambda i, t: (0, nt + t)),
                  pl.BlockSpec((tc, d), lambda i, t: (t, 0))],
        out_specs=pl.BlockSpec((1, d // 128, l, 128), lambda i, t: (i, 0, 0, 0)),
        out_shape=jax.ShapeDtypeStruct((b, d // 128, l, 128), F32),
        scratch_shapes=[pltpu.VMEM((l, d), BF16),
                        pltpu.VMEM((l, tc), F32),
                        pltpu.VMEM((2, 10, 16, 128), BF16),
                        pltpu.VMEM((d // 128, FFN_GROUP * GRID_W, 128), F32)],
        compiler_params=_params("arbitrary", "arbitrary"),
        name="ffn",
    )(h2, up, up, conv_w, conv_w, conv_b, conv_b, down)


def _final_kernel(x1_ref, y_ref, mod_ref, g_ref, o_ref):
    n_slab = y_ref.shape[1]
    for r in range(y_ref.shape[2] // GRID_W):
        for a in range(8):
            tok = slice(r * GRID_W + 8 * a, r * GRID_W + 8 * a + 8)
            y = jnp.concatenate([y_ref[0, j, pl.ds(r * GRID_W + a, 8, stride=8), :]
                                 for j in range(n_slab)], axis=-1)
            x = x1_ref[0, tok, :] + mod_ref[0, 5:6, :] * y
            ms = jnp.mean(x * x, axis=-1, keepdims=True)
            o_ref[0, tok, :] = x * lax.rsqrt(ms + EPS) * g_ref[...]


def _final(x1, y, mod, g, *, tm):
    b, l, d = x1.shape
    tok = pl.BlockSpec((1, tm, d), lambda i, j: (i, j, 0))
    return pl.pallas_call(
        _final_kernel,
        grid=(b, l // tm),
        in_specs=[tok, pl.BlockSpec((1, d // 128, tm, 128), lambda i, j: (i, 0, j, 0)),
                  pl.BlockSpec((1, N_MOD, d), lambda i, j: (i, 0, 0)),
                  pl.BlockSpec((1, d), lambda i, j: (0, 0))],
        out_specs=tok,
        out_shape=jax.ShapeDtypeStruct((b, l, d), F32),
        compiler_params=_params("arbitrary", "arbitrary"),
        name="final",
    )(x1, y, mod, g)


def kernel(x, c, ctx, c_ctx, ada_w, ada_b, norm1_g, w_in, hg_lb, hg_onorm_g, w_a, w_b, w_out,
           norm2_g, ffn_up, ffn_conv_w, ffn_conv_b, ffn_down, final_g):
    b, seq, d = x.shape
    assert ada_w.shape[0] == 1, "single-layer kernel"
    assert (d, seq % GRID_W) == (D_MODEL, 0)

    lb = jnp.cumsum(jax.nn.softmax(hg_lb.astype(F32), axis=0), axis=0)[0].reshape(1, HG_DIM)

    n_rows = 16
    rows = jnp.concatenate([c, c_ctx[None, :], jnp.zeros((n_rows - b - 1, d), F32)], axis=0)
    mod = _ada(rows, ada_w[0], ada_b[0])
    mod_x = mod[:b].reshape(b, N_MOD, d)
    mod_c = mod[b:b + 1].reshape(1, N_MOD, d)

    w_in_b = w_in[0].astype(BF16)
    g1 = norm1_g[0].reshape(1, d)
    hg_x, u_f, gates = _inproj(x, mod_x, g1, lb, w_in_b, tm=256, full=True)
    w_ctx = w_in_b[:, COL_Q * HG_DIM:(COL_I + 1) * HG_DIM]
    (hg_c,) = _inproj(ctx, mod_c, g1, lb, w_ctx, tm=256, full=False)

    s_ctx = _hgrn(hg_c, None, emit_o=False)
    o_f, o_b = _hgrn(hg_x, s_ctx, emit_o=True)

    chan, pos = _dft_tables(seq)
    y_f = _fnet(u_f, jnp.asarray(chan).astype(BF16), jnp.asarray(pos).astype(BF16), tm=512)

    x1, h2 = _merge(x, o_f, o_b, gates, y_f, mod_x, hg_onorm_g[0].reshape(1, HG_DIM),
                    norm2_g[0].reshape(1, d), w_a[0].astype(BF16), w_b[0].astype(BF16),
                    w_out[0].astype(BF16), tm=512)

    y = _ffn(h2, ffn_up[0].astype(BF16), ffn_conv_w[0].reshape(9, 2 * D_FF),
             ffn_conv_b[0].reshape(1, 2 * D_FF), ffn_down[0].astype(BF16))
    return _final(x1, y, mod_x, final_g.reshape(1, d), tm=512)
```

```python
import functools

import numpy as np
import jax
import jax.numpy as jnp
from jax import lax
from jax.experimental import pallas as pl
from jax.experimental.pallas import tpu as pltpu

D_MODEL = 1024
GRID_W = 64
FNET_DIM = 512
FNET_GROUPS = 4
FNET_GROUP_DIM = FNET_DIM // FNET_GROUPS
HG_DIM = 512
HG_HEADS = 4
HG_HEAD_DIM = HG_DIM // HG_HEADS
D_FF = 2816
N_MOD = 6
EPS = 1e-6

COL_FNET, COL_Q, COL_FF, COL_FB, COL_I, COL_G = 0, 1, 2, 3, 4, 5

HG_BLOCK = 256
HG_CHUNK = 64
HG_UNROLL = 2
HG_SUB = 16
FFN_TILE = 256

BF16 = jnp.bfloat16
F32 = jnp.float32
VMEM_LIMIT = 56 * 1024 * 1024


def _sigmoid(x):
    return 1.0 / (1.0 + jnp.exp(-x))


def _silu(x):
    return x * _sigmoid(x)


def _dot(a, b):
    return jnp.dot(a, b, preferred_element_type=F32)


def _dot_nt(a, b):
    return lax.dot_general(a, b, (((1,), (1,)), ((), ())), preferred_element_type=F32)


def _params(*sem, flags=None):
    return pltpu.CompilerParams(dimension_semantics=sem, vmem_limit_bytes=VMEM_LIMIT, flags=flags)


def _ada_kernel(c_ref, w_ref, b_ref, o_ref):
    a = _silu(c_ref[...])
    o_ref[...] = _dot(a.astype(BF16), w_ref[...].astype(BF16)) + b_ref[...]


def _ada(rows, ada_w, ada_b):
    m, d = rows.shape
    n = ada_w.shape[1]
    tn = 1536
    return pl.pallas_call(
        _ada_kernel,
        grid=(n // tn,),
        in_specs=[pl.BlockSpec((m, d), lambda j: (0, 0)),
                  pl.BlockSpec((d, tn), lambda j: (0, j)),
                  pl.BlockSpec((1, tn), lambda j: (0, j))],
        out_specs=pl.BlockSpec((m, tn), lambda j: (0, j)),
        out_shape=jax.ShapeDtypeStruct((m, n), F32),
        compiler_params=_params("arbitrary"),
        name="ada",
    )(rows, ada_w, ada_b.reshape(1, n))


def _modulated_norm(x, g, shift, scale):
    ms = jnp.mean(x * x, axis=-1, keepdims=True)
    return (x * lax.rsqrt(ms + EPS) * g) * (1.0 + scale) + shift


def _inproj_kernel(x_ref, mod_ref, g_ref, lb_ref, w_ref, *o_refs, hg_col0, full):
    hb = _modulated_norm(x_ref[0], g_ref[...], mod_ref[0, 0:1, :], mod_ref[0, 1:2, :]).astype(BF16)
    lb = lb_ref[...]

    def proj(col, width=HG_DIM):
        return _dot(hb, w_ref[:, col * HG_DIM:col * HG_DIM + width])

    hg_ref = o_refs[0]
    hg_ref[0, :, 0:HG_DIM] = _silu(proj(hg_col0))
    for n in (1, 2):
        f = lb + (1.0 - lb) * _sigmoid(proj(hg_col0 + n))
        hg_ref[0, :, n * HG_DIM:(n + 1) * HG_DIM] = jnp.log2(f)
    hg_ref[0, :, 3 * HG_DIM:4 * HG_DIM] = proj(hg_col0 + 3)
    if full:
        uf_ref, gate_ref = o_refs[1], o_refs[2]
        uf_ref[0] = proj(COL_FNET).astype(BF16)
        gate_ref[0, :, 0:2 * D_MODEL] = _sigmoid(proj(COL_G + 1, 2 * D_MODEL)).astype(BF16)
        gate_ref[0, :, 2 * D_MODEL:2 * D_MODEL + HG_DIM] = _silu(proj(COL_G)).astype(BF16)


def _inproj(x, mod, g, lb, w, *, tm, full):
    b, l, d = x.shape
    n = w.shape[1]
    mod_map = (lambda i, j: (i, 0, 0)) if full else (lambda i, j: (0, 0, 0))
    tok = lambda width, dtype: (pl.BlockSpec((1, tm, width), lambda i, j: (i, j, 0)),
                                jax.ShapeDtypeStruct((b, l, width), dtype))
    outs = [tok(4 * HG_DIM, F32)]
    if full:
        outs += [tok(FNET_DIM, BF16), tok(HG_DIM + 2 * D_MODEL, BF16)]
    return pl.pallas_call(
        functools.partial(_inproj_kernel, hg_col0=COL_Q if full else 0, full=full),
        grid=(b, l // tm),
        in_specs=[pl.BlockSpec((1, tm, d), lambda i, j: (i, j, 0)),
                  pl.BlockSpec((1, N_MOD, d), mod_map),
                  pl.BlockSpec((1, d), lambda i, j: (0, 0)),
                  pl.BlockSpec((1, HG_DIM), lambda i, j: (0, 0)),
                  pl.BlockSpec((d, n), lambda i, j: (0, 0))],
        out_specs=tuple(o[0] for o in outs),
        out_shape=tuple(o[1] for o in outs),
        compiler_params=_params("arbitrary", "arbitrary"),
        name="inproj_x" if full else "inproj_ctx",
    )(x, mod, g, lb, w)


def _hgrn_step(dirs, st_ref):
    c = HG_CHUNK
    hd = HG_HEAD_DIM
    r_idx = lax.broadcasted_iota(jnp.int32, (c, c), 0)
    c_idx = lax.broadcasted_iota(jnp.int32, (c, c), 1)
    pairs = [slice(p * 2 * hd, (p + 1) * 2 * hd) for p in range(HG_HEADS // 2)]

    def block_diag(a0, a1):
        z0, z1 = jnp.zeros(a0.shape, a0.dtype), jnp.zeros(a1.shape, a1.dtype)
        return jnp.concatenate([jnp.concatenate([a0, z0], axis=1),
                                jnp.concatenate([z1, a1], axis=1)], axis=0)

    cums = []
    for q, l2f, v, reverse, o_ref, row0, base in dirs:
        tri = jnp.where((c_idx >= r_idx) if reverse else (c_idx <= r_idx), 1.0, 0.0).astype(BF16)
        hi = l2f.astype(BF16)
        lo = (l2f - hi.astype(F32)).astype(BF16)
        cums.append(_dot(tri, hi) + _dot(tri, lo))

    state_in, state_upd, intra = [], [], []
    for (q, l2f, v, reverse, o_ref, row0, base), cum in zip(dirs, cums):
        k = 1.0 - jnp.exp2(l2f)
        last = 0 if reverse else c - 1
        total = cum[last:last + 1, :]
        q_in = (q * jnp.exp2(cum)).astype(BF16)
        k_out = (k * jnp.exp2(total - cum)).astype(BF16)
        e_tot = jnp.exp2(total)
        v_b = v.astype(BF16)
        for p, ps in enumerate(pairs):
            h0, h1 = slice(ps.start, ps.start + hd), slice(ps.start + hd, ps.stop)
            v_t = jnp.concatenate([v[:, h0], v[:, h1]], axis=0).T.astype(BF16)
            state_in.append(q_in[:, ps])
            state_upd.append((base + p, e_tot[:, ps], v_t, block_diag(k_out[:, h0], k_out[:, h1])))
            if o_ref is None:
                continue
            for i in range(c // HG_SUB):
                r0 = i * HG_SUB
                rows = slice(r0, r0 + HG_SUB)
                cols = slice(r0, c) if reverse else slice(0, r0 + HG_SUB)
                mid = r0 + HG_SUB // 2 if reverse else r0 + HG_SUB // 2 - 1
                m = cum[mid:mid + 1, ps]
                q_t = (q[rows, ps] * jnp.exp2(cum[rows, ps] - m)).astype(BF16)
                k_t = (k[cols, ps] * jnp.exp2(m - cum[cols, ps])).astype(BF16)
                intra.append((o_ref, row0, rows, cols, ps, reverse, len(state_in) - 1, q_t,
                              block_diag(k_t[:, :hd], k_t[:, hd:]),
                              block_diag(v_b[cols, h0], v_b[cols, h1])))

    scores = [_dot_nt(item[7], item[8]) for item in intra]
    kv = [_dot(v_t, k_bd) for _, _, v_t, k_bd in state_upd]
    states = {}
    inter = []
    for n, (idx, e_tot_p, _, _) in enumerate(state_upd):
        st = states[idx] if idx in states else st_ref[idx]
        if intra:
            st_b = st.astype(BF16)
            inter.append(_dot_nt(state_in[n], block_diag(st_b[:, :hd], st_b[:, hd:])))
        states[idx] = st * e_tot_p + kv[n]
    for idx, st in states.items():
        st_ref[idx] = st

    for (o_ref, row0, rows, cols, ps, reverse, chain, _, _, v_bd), a in zip(intra, scores):
        n_cols = cols.stop - cols.start
        row_g = rows.start + lax.broadcasted_iota(jnp.int32, (HG_SUB, 2 * n_cols), 0)
        lane = lax.broadcasted_iota(jnp.int32, (HG_SUB, 2 * n_cols), 1)
        col_g = cols.start + jnp.where(lane >= n_cols, lane - n_cols, lane)
        a = jnp.where((col_g >= row_g) if reverse else (col_g <= row_g), a, 0.0)
        o_ref[0, pl.ds(row0 + rows.start, HG_SUB), ps] = _dot(a.astype(BF16), v_bd) + inter[chain][rows]


def _hgrn_kernel(*refs, has_s0, emit_o):
    it = iter(refs)
    fwd_refs = [next(it) for _ in range(3)]
    bwd_refs = [next(it) for _ in range(3)]
    qf_ref = fwd_refs[0]
    s0_ref = next(it) if has_s0 else None
    of_ref = next(it) if emit_o else None
    ob_ref = next(it) if emit_o else None
    sout_ref = None if emit_o else next(it)
    st_ref = next(it)
    j = pl.program_id(1)
    n_inner = qf_ref.shape[1] // HG_CHUNK

    @pl.when(j == 0)
    def _():
        if has_s0:
            st_ref[...] = s0_ref[0]
        else:
            st_ref[...] = jnp.zeros_like(st_ref)

    span = HG_UNROLL * HG_CHUNK

    def body(it, carry):
        dirs = []
        for jj in range(HG_UNROLL):
            rf = pl.multiple_of(it * span, span) + jj * HG_CHUNK
            rb = pl.multiple_of((n_inner // HG_UNROLL - 1 - it) * span, span) + (HG_UNROLL - 1 - jj) * HG_CHUNK
            sf, sb = pl.ds(rf, HG_CHUNK), pl.ds(rb, HG_CHUNK)
            dirs.append(tuple(r[0, sf, :] for r in fwd_refs) + (False, of_ref, rf, 0))
            dirs.append(tuple(r[0, sb, :] for r in bwd_refs) + (True, ob_ref, rb, HG_HEADS // 2))
        _hgrn_step(dirs, st_ref)
        return carry

    lax.fori_loop(0, n_inner // HG_UNROLL, body, 0)

    if not emit_o:
        @pl.when(j == pl.num_programs(1) - 1)
        def _():
            sout_ref[0] = st_ref[...]


def _hgrn(u, s0, *, emit_o):
    b, l, _ = u.shape
    c = min(HG_BLOCK, l)
    n = l // c
    blk = (1, c, HG_DIM)
    fwd = lambda col: pl.BlockSpec(blk, lambda i, j: (i, j, col))
    bwd = lambda col: pl.BlockSpec(blk, lambda i, j: (i, n - 1 - j, col))
    st_dims = (HG_HEADS, HG_HEAD_DIM, 2 * HG_HEAD_DIM)
    state_spec = pl.BlockSpec((1,) + st_dims, lambda i, j: (i, 0, 0, 0))
    state_shape = jax.ShapeDtypeStruct((b,) + st_dims, F32)
    in_specs = [fwd(0), fwd(1), fwd(3), bwd(0), bwd(2), bwd(3)]
    args = [u, u, u, u, u, u]
    if s0 is not None:
        in_specs.append(state_spec)
        args.append(s0)
    if emit_o:
        o_shape = jax.ShapeDtypeStruct((b, l, HG_DIM), F32)
        out_shape = (o_shape, o_shape)
        out_specs = (pl.BlockSpec(blk, lambda i, j: (i, j, 0)),
                     pl.BlockSpec(blk, lambda i, j: (i, n - 1 - j, 0)))
    else:
        out_shape = state_shape
        out_specs = state_spec
    return pl.pallas_call(
        functools.partial(_hgrn_kernel, has_s0=s0 is not None, emit_o=emit_o),
        grid=(b, n),
        in_specs=in_specs,
        out_specs=out_specs,
        out_shape=out_shape,
        scratch_shapes=[pltpu.VMEM(st_dims, F32)],
        compiler_params=_params("arbitrary", "arbitrary"),
        name="hgrn_x" if emit_o else "hgrn_ctx",
    )(*args)


def _dft_tables(seq):
    gd = FNET_GROUP_DIM
    kc = (np.arange(gd)[:, None] * np.arange(gd)[None, :]) % gd
    ang_c = 2.0 * np.pi * kc / gd
    scale = 1.0 / np.sqrt(float(seq) * gd)
    chan = np.concatenate([np.cos(ang_c), np.sin(ang_c)], axis=1) * scale
    kl = (np.arange(seq)[:, None] * np.arange(seq // 2)[None, :]) % seq
    ang_l = 2.0 * np.pi * kl / seq
    pos = np.concatenate([np.cos(ang_l), -np.sin(ang_l)], axis=1)
    return chan.astype(np.float32), pos.astype(np.float32)


def _fold_positions(v, sign):
    seq = v.shape[0]
    n_tiles = seq // 8
    sub = lax.broadcasted_iota(jnp.int32, (8, 128), 0)
    gather = jnp.where(sub == 0, 0, 8 - sub)
    tile = lambda k: v[8 * k:8 * k + 8, :]
    out = []
    for k in range(n_tiles // 2):
        tail = jnp.take_along_axis(tile(n_tiles - 1 - k), gather, axis=0)
        if k == 0:
            mirror = jnp.where(sub == 0, 0.0, tail)
        else:
            mirror = jnp.where(sub == 0, jnp.take_along_axis(tile(n_tiles - k), gather, axis=0), tail)
        out.append(tile(k) + mirror if sign > 0 else tile(k) - mirror)
    return jnp.concatenate(out, axis=0)


def _fnet_kernel(u_ref, chan_ref, pos_ref, y_ref, pq_ref, mid_ref):
    seq = u_ref.shape[1]
    half = seq // 2
    gd = FNET_GROUP_DIM

    @pl.when(pl.program_id(1) == 0)
    def _():
        for g in range(FNET_GROUPS):
            gs = slice(g * gd, (g + 1) * gd)
            pq = _dot(u_ref[0, :, gs], chan_ref[...])
            pq_ref[0:half, gs] = _fold_positions(pq[:, :gd], 1).astype(BF16)
            pq_ref[half:seq, gs] = _fold_positions(pq[:, gd:], -1).astype(BF16)
            mid_ref[:, gs] = pq[half:half + 8, :gd]

    y = _dot(pos_ref[...], pq_ref[...])
    row = lax.broadcasted_iota(jnp.int32, y.shape, 0)
    mid = mid_ref[0:1, :]
    y_ref[0] = (y + jnp.where(row % 2 == 0, mid, -mid)).astype(BF16)


def _fnet(u, chan, pos, *, tm):
    b, seq, _ = u.shape
    return pl.pallas_call(
        _fnet_kernel,
        grid=(b, seq // tm),
        in_specs=[pl.BlockSpec((1, seq, FNET_DIM), lambda i, j: (i, 0, 0)),
                  pl.BlockSpec((FNET_GROUP_DIM, 2 * FNET_GROUP_DIM), lambda i, j: (0, 0)),
                  pl.BlockSpec((tm, seq), lambda i, j: (j, 0))],
        out_specs=pl.BlockSpec((1, tm, FNET_DIM), lambda i, j: (i, j, 0)),
        out_shape=jax.ShapeDtypeStruct((b, seq, FNET_DIM), BF16),
        scratch_shapes=[pltpu.VMEM((seq, FNET_DIM), BF16), pltpu.VMEM((8, FNET_DIM), F32)],
        compiler_params=_params("arbitrary", "arbitrary"),
        name="fnet",
    )(u, chan, pos)


def _merge_kernel(x_ref, of_ref, ob_ref, sg_ref, ga_ref, gb_ref, yf_ref, mod_ref, og_ref, n2g_ref,
                  wa_ref, wb_ref, wo_ref, x1_ref, h2_ref):
    o = of_ref[0] + ob_ref[0]
    parts = []
    for h in range(HG_HEADS):
        oh = o[:, h * HG_HEAD_DIM:(h + 1) * HG_HEAD_DIM]
        parts.append(oh * lax.rsqrt(jnp.mean(oh * oh, axis=-1, keepdims=True) + EPS))
    on = jnp.concatenate(parts, axis=-1) * og_ref[...] * sg_ref[0].astype(F32)
    y_b = _dot(on.astype(BF16), wb_ref[...])
    y_a = _dot(yf_ref[0], wa_ref[...])
    m = ga_ref[0].astype(F32) * y_a + gb_ref[0].astype(F32) * y_b
    yx = _dot(m.astype(BF16), wo_ref[...])
    x1 = x_ref[0] + mod_ref[0, 2:3, :] * yx
    x1_ref[0] = x1
    h2 = _modulated_norm(x1, n2g_ref[...], mod_ref[0, 3:4, :], mod_ref[0, 4:5, :])
    h2_ref[0] = h2.astype(BF16)


def _merge(x, o_f, o_b, gates, y_f, mod, og, n2g, w_a, w_b, w_out, *, tm):
    b, l, d = x.shape
    tok = lambda w, col: pl.BlockSpec((1, tm, w), lambda i, j: (i, j, col))
    const = lambda shape: pl.BlockSpec(shape, lambda i, j: tuple(0 for _ in shape))
    return pl.pallas_call(
        _merge_kernel,
        grid=(b, l // tm),
        in_specs=[tok(d, 0), tok(HG_DIM, 0), tok(HG_DIM, 0), tok(HG_DIM, 2 * d // HG_DIM),
                  tok(d, 0), tok(d, 1), tok(FNET_DIM, 0),
                  pl.BlockSpec((1, N_MOD, d), lambda i, j: (i, 0, 0)),
                  const((1, HG_DIM)), const((1, d)),
                  const((FNET_DIM, d)), const((HG_DIM, d)), const((d, d))],
        out_specs=(tok(d, 0), tok(d, 0)),
        out_shape=(jax.ShapeDtypeStruct((b, l, d), F32), jax.ShapeDtypeStruct((b, l, d), BF16)),
        compiler_params=_params("arbitrary", "arbitrary"),
        name="merge",
    )(x, o_f, o_b, gates, gates, gates, y_f, mod, og, n2g, w_a, w_b, w_out)


FFN_GROUP = 8


def _ffn_permute_tokens(h_ref, hp_ref, tmp_ref):
    n_tok = tmp_ref.shape[1]

    def body(g, carry):
        tok = pl.ds(pl.multiple_of(g * n_tok, n_tok), n_tok)
        hf = h_ref[0, tok, :].astype(F32)
        for lt in range(tmp_ref.shape[0]):
            for r in range(n_tok // GRID_W):
                for a in range(8):
                    r0 = r * GRID_W + 8 * a
                    tmp_ref[lt, pl.ds(r * GRID_W + a, 8, stride=8), :] = (
                        hf[r0:r0 + 8, lt * 128:(lt + 1) * 128])
        for lt in range(tmp_ref.shape[0]):
            hp_ref[tok, lt * 128:(lt + 1) * 128] = tmp_ref[lt].astype(BF16)
        return carry

    lax.fori_loop(0, h_ref.shape[1] // n_tok, body, 0)


def _ffn_up(hp_ref, up_refs, row0, n_rows):
    h = hp_ref[row0 * GRID_W:(row0 + n_rows) * GRID_W, :]
    return [_dot(h, up_ref[...]) for up_ref in up_refs]


def _pair(lo, hi):
    return jnp.concatenate([lo, hi], axis=0).astype(BF16)


def _ffn_row(z, r, taps, bias, first, last, sub):
    lo = [z[r * GRID_W + 8 * b:r * GRID_W + 8 * b + 8, 0:128] for b in range(8)]
    hi = [z[r * GRID_W + 8 * b:r * GRID_W + 8 * b + 8, 128:256] for b in range(8)]
    mid = [_pair(lo[b], hi[b]) for b in range(8)]
    down = lambda v: jnp.where(sub == 0, 0.0, pltpu.roll(v, 1, axis=0))
    left = [_pair(down(lo[7]), down(hi[7]))] + mid[0:7]
    up = lambda v: jnp.where(sub == 7, 0.0, pltpu.roll(v, 7, axis=0))
    right = mid[1:8] + [_pair(up(lo[0]), up(hi[0]))]
    term = lambda kh: [left[b] * taps[3 * kh] + mid[b] * taps[3 * kh + 1] + right[b] * taps[3 * kh + 2]
                       for b in range(8)]
    above = None if first else term(2)
    here = term(1)
    below = None if last else [t + bias for t in term(0)]
    return above, here, below


def _ffn_down(a_ref, dn_ref, y_ref, row0, n_rows):
    tok = slice(row0 * GRID_W, (row0 + n_rows) * GRID_W)
    part = _dot(a_ref[tok, :].astype(BF16), dn_ref[...])
    for j in range(y_ref.shape[1]):
        y_ref[0, j, tok, :] += part[:, j * 128:(j + 1) * 128]


def _ffn_kernel(h_ref, up1_ref, up2_ref, cw1_ref, cw2_ref, cb1_ref, cb2_ref, dn_ref, y_ref,
                hp_ref, a_ref, wp_ref, tmp_ref):
    t = pl.program_id(1)
    rows = hp_ref.shape[0] // GRID_W
    grp = FFN_GROUP

    @pl.when(t == 0)
    def _():
        _ffn_permute_tokens(h_ref, hp_ref, tmp_ref)
        y_ref[...] = jnp.zeros_like(y_ref)

    for half, (cw_ref, cb_ref) in enumerate(((cw1_ref, cb1_ref), (cw2_ref, cb2_ref))):
        for k in range(10):
            row = cw_ref[k:k + 1, :] if k < 9 else cb_ref[...]
            wp_ref[half, k] = _pair(jnp.broadcast_to(row[:, 0:128], (8, 128)),
                                    jnp.broadcast_to(row[:, 128:256], (8, 128)))
    taps = [[wp_ref[half, k] for k in range(9)] for half in range(2)]
    bias = [wp_ref[half, 9] for half in range(2)]
    sub = lax.broadcasted_iota(jnp.int32, (8, 128), 0)

    def gate(c, out_row):
        for b in range(8):
            act = (_silu(c[0][b]) * c[1][b]).astype(F32)
            tok_b = slice(out_row * GRID_W + 8 * b, out_row * GRID_W + 8 * b + 8)
            a_ref[tok_b, 0:128] = act[0:8]
            a_ref[tok_b, 128:256] = act[8:16]

    prev = [None, None]
    cur = [[bias[half]] * 8 for half in range(2)]
    z_next = _ffn_up(hp_ref, (up1_ref, up2_ref), 0, grp)
    for row0 in range(0, rows, grp):
        z_cur = z_next
        if row0 + grp < rows:
            z_next = _ffn_up(hp_ref, (up1_ref, up2_ref), row0 + grp, grp)
        for rr in range(grp):
            r = row0 + rr
            done = []
            for half in range(2):
                above, here, below = _ffn_row(z_cur[half], rr, taps[half], bias[half],
                                              r == 0, r == rows - 1, sub)
                if above is not None:
                    done.append([prev[half][b] + above[b] for b in range(8)])
                prev[half] = [cur[half][b] + here[b] for b in range(8)]
                cur[half] = below
            if done:
                gate(done, r - 1)
            if rr == 0 and row0 > 0:
                _ffn_down(a_ref, dn_ref, y_ref, row0 - grp, grp)
    gate(prev, rows - 1)
    _ffn_down(a_ref, dn_ref, y_ref, rows - grp, grp)


def _ffn(h2, up, conv_w, conv_b, down):
    b, l, d = h2.shape
    rows = l // GRID_W
    tc = FFN_TILE
    nt = D_FF // tc
    assert rows % FFN_GROUP == 0 and tc == 256, "the conv pairs the two lane tiles of a channel tile"
    return pl.pallas_call(
        _ffn_kernel,
        grid=(b, nt),
        in_specs=[pl.BlockSpec((1, l, d), lambda i, t: (i, 0, 0)),
                  pl.BlockSpec((d, tc), lambda i, t: (0, t)),
                  pl.BlockSpec((d, tc), lambda i, t: (0, nt + t)),
                  pl.BlockSpec((9, tc), lambda i, t: (0, t)),
                  pl.BlockSpec((9, tc), lambda i, t: (0, nt + t)),
                  pl.BlockSpec((1, tc), lambda i, t: (0, t)),
                  pl.BlockSpec((1, tc), lambda i, t: (0, nt + t)),
                  pl.BlockSpec((tc, d), lambda i, t: (t, 0))],
        out_specs=pl.BlockSpec((1, d // 128, l, 128), lambda i, t: (i, 0, 0, 0)),
        out_shape=jax.ShapeDtypeStruct((b, d // 128, l, 128), F32),
        scratch_shapes=[pltpu.VMEM((l, d), BF16),
                        pltpu.VMEM((l, tc), F32),
                        pltpu.VMEM((2, 10, 16, 128), BF16),
                        pltpu.VMEM((d // 128, FFN_GROUP * GRID_W, 128), F32)],
        compiler_params=_params("arbitrary", "arbitrary"),
        name="ffn",
    )(h2, up, up, conv_w, conv_w, conv_b, conv_b, down)


def _final_kernel(x1_ref, y_ref, mod_ref, g_ref, o_ref):
    n_slab = y_ref.shape[1]
    for r in range(y_ref.shape[2] // GRID_W):
        for a in range(8):
            tok = slice(r * GRID_W + 8 * a, r * GRID_W + 8 * a + 8)
            y = jnp.concatenate([y_ref[0, j, pl.ds(r * GRID_W + a, 8, stride=8), :]
                                 for j in range(n_slab)], axis=-1)
            x = x1_ref[0, tok, :] + mod_ref[0, 5:6, :] * y
            ms = jnp.mean(x * x, axis=-1, keepdims=True)
            o_ref[0, tok, :] = x * lax.rsqrt(ms + EPS) * g_ref[...]


def _final(x1, y, mod, g, *, tm):
    b, l, d = x1.shape
    tok = pl.BlockSpec((1, tm, d), lambda i, j: (i, j, 0))
    return pl.pallas_call(
        _final_kernel,
        grid=(b, l // tm),
        in_specs=[tok, pl.BlockSpec((1, d // 128, tm, 128), lambda i, j: (i, 0, j, 0)),
                  pl.BlockSpec((1, N_MOD, d), lambda i, j: (i, 0, 0)),
                  pl.BlockSpec((1, d), lambda i, j: (0, 0))],
        out_specs=tok,
        out_shape=jax.ShapeDtypeStruct((b, l, d), F32),
        compiler_params=_params("arbitrary", "arbitrary"),
        name="final",
    )(x1, y, mod, g)


def kernel(x, c, ctx, c_ctx, ada_w, ada_b, norm1_g, w_in, hg_lb, hg_onorm_g, w_a, w_b, w_out,
           norm2_g, ffn_up, ffn_conv_w, ffn_conv_b, ffn_down, final_g):
    b, seq, d = x.shape
    assert ada_w.shape[0] == 1, "single-layer kernel"
    assert (d, seq % GRID_W) == (D_MODEL, 0)

    lb = jnp.cumsum(jax.nn.softmax(hg_lb.astype(F32), axis=0), axis=0)[0].reshape(1, HG_DIM)

    n_rows = 16
    rows = jnp.concatenate([c, c_ctx[None, :], jnp.zeros((n_rows - b - 1, d), F32)], axis=0)
    mod = _ada(rows, ada_w[0], ada_b[0])
    mod_x = mod[:b].reshape(b, N_MOD, d)
    mod_c = mod[b:b + 1].reshape(1, N_MOD, d)

    w_in_b = w_in[0].astype(BF16)
    g1 = norm1_g[0].reshape(1, d)
    hg_x, u_f, gates = _inproj(x, mod_x, g1, lb, w_in_b, tm=256, full=True)
    w_ctx = w_in_b[:, COL_Q * HG_DIM:(COL_I + 1) * HG_DIM]
    (hg_c,) = _inproj(ctx, mod_c, g1, lb, w_ctx, tm=256, full=False)

    s_ctx = _hgrn(hg_c, None, emit_o=False)
    o_f, o_b = _hgrn(hg_x, s_ctx, emit_o=True)

    chan, pos = _dft_tables(seq)
    y_f = _fnet(u_f, jnp.asarray(chan).astype(BF16), jnp.asarray(pos).astype(BF16), tm=512)

    x1, h2 = _merge(x, o_f, o_b, gates, y_f, mod_x, hg_onorm_g[0].reshape(1, HG_DIM),
                    norm2_g[0].reshape(1, d), w_a[0].astype(BF16), w_b[0].astype(BF16),
                    w_out[0].astype(BF16), tm=512)

    y = _ffn(h2, ffn_up[0].astype(BF16), ffn_conv_w[0].reshape(9, 2 * D_FF),
             ffn_conv_b[0].reshape(1, 2 * D_FF), ffn_down[0].astype(BF16))
    return _final(x1, y, mod_x, final_g.reshape(1, d), tm=512)
```

```python
import functools

import numpy as np
import jax
import jax.numpy as jnp
from jax import lax
from jax.experimental import pallas as pl
from jax.experimental.pallas import tpu as pltpu

D_MODEL = 1024
GRID_W = 64
FNET_DIM = 512
FNET_GROUPS = 4
FNET_GROUP_DIM = FNET_DIM // FNET_GROUPS
HG_DIM = 512
HG_HEADS = 4
HG_HEAD_DIM = HG_DIM // HG_HEADS
D_FF = 2816
N_MOD = 6
EPS = 1e-6

COL_FNET, COL_Q, COL_FF, COL_FB, COL_I, COL_G = 0, 1, 2, 3, 4, 5

INPROJ_SUB = 256
MERGE_SUB = 256
HG_BLOCK = 256
HG_CHUNK = 64
HG_UNROLL = 2
HG_SUB = 16
FFN_TILE = 256

BF16 = jnp.bfloat16
F32 = jnp.float32
VMEM_LIMIT = 56 * 1024 * 1024


def _sigmoid(x):
    return 1.0 / (1.0 + jnp.exp(-x))


def _silu(x):
    return x * _sigmoid(x)


def _dot(a, b):
    return jnp.dot(a, b, preferred_element_type=F32)


def _dot_nt(a, b):
    return lax.dot_general(a, b, (((1,), (1,)), ((), ())), preferred_element_type=F32)


def _params(*sem, flags=None):
    return pltpu.CompilerParams(dimension_semantics=sem, vmem_limit_bytes=VMEM_LIMIT, flags=flags)


def _ada_kernel(c_ref, w_ref, b_ref, o_ref):
    a = _silu(c_ref[...])
    o_ref[...] = _dot(a.astype(BF16), w_ref[...].astype(BF16)) + b_ref[...]


def _ada(rows, ada_w, ada_b):
    m, d = rows.shape
    n = ada_w.shape[1]
    tn = 1536
    return pl.pallas_call(
        _ada_kernel,
        grid=(n // tn,),
        in_specs=[pl.BlockSpec((m, d), lambda j: (0, 0)),
                  pl.BlockSpec((d, tn), lambda j: (0, j)),
                  pl.BlockSpec((1, tn), lambda j: (0, j))],
        out_specs=pl.BlockSpec((m, tn), lambda j: (0, j)),
        out_shape=jax.ShapeDtypeStruct((m, n), F32),
        compiler_params=_params("arbitrary"),
        name="ada",
    )(rows, ada_w, ada_b.reshape(1, n))


def _modulated_norm(x, g, shift, scale):
    ms = jnp.mean(x * x, axis=-1, keepdims=True)
    return (x * lax.rsqrt(ms + EPS) * g) * (1.0 + scale) + shift


def _inproj_kernel(x_ref, mod_ref, g_ref, lb_ref, w_ref, *o_refs, hg_col0, full):
    lb = lb_ref[...]
    hg_ref = o_refs[0]
    n_sub = x_ref.shape[1] // INPROJ_SUB

    def normed(i):
        rows = slice(i * INPROJ_SUB, (i + 1) * INPROJ_SUB)
        return _modulated_norm(x_ref[0, rows, :], g_ref[...], mod_ref[0, 0:1, :],
                               mod_ref[0, 1:2, :]).astype(BF16)

    hb_next = normed(0)
    for i in range(n_sub):
        hb = hb_next
        rows = slice(i * INPROJ_SUB, (i + 1) * INPROJ_SUB)

        def proj(col, width=HG_DIM):
            return _dot(hb, w_ref[:, col * HG_DIM:col * HG_DIM + width])

        hg_ref[0, rows, 0:HG_DIM] = _silu(proj(hg_col0))
        if i + 1 < n_sub:
            hb_next = normed(i + 1)
        for n in (1, 2):
            f = lb + (1.0 - lb) * _sigmoid(proj(hg_col0 + n))
            hg_ref[0, rows, n * HG_DIM:(n + 1) * HG_DIM] = jnp.log2(f)
        hg_ref[0, rows, 3 * HG_DIM:4 * HG_DIM] = proj(hg_col0 + 3)
        if full:
            uf_ref, gate_ref = o_refs[1], o_refs[2]
            uf_ref[0, rows, :] = proj(COL_FNET).astype(BF16)
            gate_ref[0, rows, 0:2 * D_MODEL] = _sigmoid(proj(COL_G + 1, 2 * D_MODEL)).astype(BF16)
            gate_ref[0, rows, 2 * D_MODEL:2 * D_MODEL + HG_DIM] = _silu(proj(COL_G)).astype(BF16)


def _inproj(x, mod, g, lb, w, *, tm, full):
    b, l, d = x.shape
    n = w.shape[1]
    mod_map = (lambda i, j: (i, 0, 0)) if full else (lambda i, j: (0, 0, 0))
    tok = lambda width, dtype: (pl.BlockSpec((1, tm, width), lambda i, j: (i, j, 0)),
                                jax.ShapeDtypeStruct((b, l, width), dtype))
    outs = [tok(4 * HG_DIM, F32)]
    if full:
        outs += [tok(FNET_DIM, BF16), tok(HG_DIM + 2 * D_MODEL, BF16)]
    return pl.pallas_call(
        functools.partial(_inproj_kernel, hg_col0=COL_Q if full else 0, full=full),
        grid=(b, l // tm),
        in_specs=[pl.BlockSpec((1, tm, d), lambda i, j: (i, j, 0)),
                  pl.BlockSpec((1, N_MOD, d), mod_map),
                  pl.BlockSpec((1, d), lambda i, j: (0, 0)),
                  pl.BlockSpec((1, HG_DIM), lambda i, j: (0, 0)),
                  pl.BlockSpec((d, n), lambda i, j: (0, 0))],
        out_specs=tuple(o[0] for o in outs),
        out_shape=tuple(o[1] for o in outs),
        compiler_params=_params("arbitrary", "arbitrary"),
        name="inproj_x" if full else "inproj_ctx",
    )(x, mod, g, lb, w)


def _hgrn_step(dirs, st_ref):
    c = HG_CHUNK
    hd = HG_HEAD_DIM
    r_idx = lax.broadcasted_iota(jnp.int32, (c, c), 0)
    c_idx = lax.broadcasted_iota(jnp.int32, (c, c), 1)
    pairs = [slice(p * 2 * hd, (p + 1) * 2 * hd) for p in range(HG_HEADS // 2)]

    def block_diag(a0, a1):
        z0, z1 = jnp.zeros(a0.shape, a0.dtype), jnp.zeros(a1.shape, a1.dtype)
        return jnp.concatenate([jnp.concatenate([a0, z0], axis=1),
                                jnp.concatenate([z1, a1], axis=1)], axis=0)

    cums = []
    for q, l2f, v, reverse, o_ref, row0, base in dirs:
        tri = jnp.where((c_idx >= r_idx) if reverse else (c_idx <= r_idx), 1.0, 0.0).astype(BF16)
        hi = l2f.astype(BF16)
        lo = (l2f - hi.astype(F32)).astype(BF16)
        cums.append(_dot(tri, hi) + _dot(tri, lo))

    state_in, state_upd, intra = [], [], []
    for (q, l2f, v, reverse, o_ref, row0, base), cum in zip(dirs, cums):
        k = 1.0 - jnp.exp2(l2f)
        last = 0 if reverse else c - 1
        total = cum[last:last + 1, :]
        q_in = (q * jnp.exp2(cum)).astype(BF16)
        k_out = (k * jnp.exp2(total - cum)).astype(BF16)
        e_tot = jnp.exp2(total)
        v_b = v.astype(BF16)
        for p, ps in enumerate(pairs):
            h0, h1 = slice(ps.start, ps.start + hd), slice(ps.start + hd, ps.stop)
            v_t = jnp.concatenate([v[:, h0], v[:, h1]], axis=0).T.astype(BF16)
            state_in.append(q_in[:, ps])
            state_upd.append((base + p, e_tot[:, ps], v_t, block_diag(k_out[:, h0], k_out[:, h1])))
            if o_ref is None:
                continue
            for i in range(c // HG_SUB):
                r0 = i * HG_SUB
                rows = slice(r0, r0 + HG_SUB)
                cols = slice(r0, c) if reverse else slice(0, r0 + HG_SUB)
                mid = r0 + HG_SUB // 2 if reverse else r0 + HG_SUB // 2 - 1
                m = cum[mid:mid + 1, ps]
                q_t = (q[rows, ps] * jnp.exp2(cum[rows, ps] - m)).astype(BF16)
                k_t = (k[cols, ps] * jnp.exp2(m - cum[cols, ps])).astype(BF16)
                intra.append((o_ref, row0, rows, cols, ps, reverse, len(state_in) - 1, q_t,
                              block_diag(k_t[:, :hd], k_t[:, hd:]),
                              block_diag(v_b[cols, h0], v_b[cols, h1])))

    scores = [_dot_nt(item[7], item[8]) for item in intra]
    kv = [_dot(v_t, k_bd) for _, _, v_t, k_bd in state_upd]
    states = {}
    inter = []
    for n, (idx, e_tot_p, _, _) in enumerate(state_upd):
        st = states[idx] if idx in states else st_ref[idx]
        if intra:
            st_b = st.astype(BF16)
            inter.append(_dot_nt(state_in[n], block_diag(st_b[:, :hd], st_b[:, hd:])))
        states[idx] = st * e_tot_p + kv[n]
    for idx, st in states.items():
        st_ref[idx] = st

    for (o_ref, row0, rows, cols, ps, reverse, chain, _, _, v_bd), a in zip(intra, scores):
        n_cols = cols.stop - cols.start
        row_g = rows.start + lax.broadcasted_iota(jnp.int32, (HG_SUB, 2 * n_cols), 0)
        lane = lax.broadcasted_iota(jnp.int32, (HG_SUB, 2 * n_cols), 1)
        col_g = cols.start + jnp.where(lane >= n_cols, lane - n_cols, lane)
        a = jnp.where((col_g >= row_g) if reverse else (col_g <= row_g), a, 0.0)
        o_ref[0, pl.ds(row0 + rows.start, HG_SUB), ps] = _dot(a.astype(BF16), v_bd) + inter[chain][rows]


def _hgrn_kernel(*refs, has_s0, emit_o):
    it = iter(refs)
    fwd_refs = [next(it) for _ in range(3)]
    bwd_refs = [next(it) for _ in range(3)]
    qf_ref = fwd_refs[0]
    s0_ref = next(it) if has_s0 else None
    of_ref = next(it) if emit_o else None
    ob_ref = next(it) if emit_o else None
    sout_ref = None if emit_o else next(it)
    st_ref = next(it)
    j = pl.program_id(1)
    n_inner = qf_ref.shape[1] // HG_CHUNK

    @pl.when(j == 0)
    def _():
        if has_s0:
            st_ref[...] = s0_ref[0]
        else:
            st_ref[...] = jnp.zeros_like(st_ref)

    span = HG_UNROLL * HG_CHUNK

    def body(it, carry):
        dirs = []
        for jj in range(HG_UNROLL):
            rf = pl.multiple_of(it * span, span) + jj * HG_CHUNK
            rb = pl.multiple_of((n_inner // HG_UNROLL - 1 - it) * span, span) + (HG_UNROLL - 1 - jj) * HG_CHUNK
            sf, sb = pl.ds(rf, HG_CHUNK), pl.ds(rb, HG_CHUNK)
            dirs.append(tuple(r[0, sf, :] for r in fwd_refs) + (False, of_ref, rf, 0))
            dirs.append(tuple(r[0, sb, :] for r in bwd_refs) + (True, ob_ref, rb, HG_HEADS // 2))
        _hgrn_step(dirs, st_ref)
        return carry

    lax.fori_loop(0, n_inner // HG_UNROLL, body, 0)

    if not emit_o:
        @pl.when(j == pl.num_programs(1) - 1)
        def _():
            sout_ref[0] = st_ref[...]


def _hgrn(u, s0, *, emit_o):
    b, l, _ = u.shape
    c = min(HG_BLOCK, l)
    n = l // c
    blk = (1, c, HG_DIM)
    fwd = lambda col: pl.BlockSpec(blk, lambda i, j: (i, j, col))
    bwd = lambda col: pl.BlockSpec(blk, lambda i, j: (i, n - 1 - j, col))
    st_dims = (HG_HEADS, HG_HEAD_DIM, 2 * HG_HEAD_DIM)
    state_spec = pl.BlockSpec((1,) + st_dims, lambda i, j: (i, 0, 0, 0))
    state_shape = jax.ShapeDtypeStruct((b,) + st_dims, F32)
    in_specs = [fwd(0), fwd(1), fwd(3), bwd(0), bwd(2), bwd(3)]
    args = [u, u, u, u, u, u]
    if s0 is not None:
        in_specs.append(state_spec)
        args.append(s0)
    if emit_o:
        o_shape = jax.ShapeDtypeStruct((b, l, HG_DIM), F32)
        out_shape = (o_shape, o_shape)
        out_specs = (pl.BlockSpec(blk, lambda i, j: (i, j, 0)),
                     pl.BlockSpec(blk, lambda i, j: (i, n - 1 - j, 0)))
    else:
        out_shape = state_shape
        out_specs = state_spec
    return pl.pallas_call(
        functools.partial(_hgrn_kernel, has_s0=s0 is not None, emit_o=emit_o),
        grid=(b, n),
        in_specs=in_specs,
        out_specs=out_specs,
        out_shape=out_shape,
        scratch_shapes=[pltpu.VMEM(st_dims, F32)],
        compiler_params=_params("arbitrary", "arbitrary"),
        name="hgrn_x" if emit_o else "hgrn_ctx",
    )(*args)


def _dft_tables(seq):
    gd = FNET_GROUP_DIM
    kc = (np.arange(gd)[:, None] * np.arange(gd)[None, :]) % gd
    ang_c = 2.0 * np.pi * kc / gd
    scale = 1.0 / np.sqrt(float(seq) * gd)
    chan = np.concatenate([np.cos(ang_c), np.sin(ang_c)], axis=1) * scale
    kl = (np.arange(seq)[:, None] * np.arange(seq // 2)[None, :]) % seq
    ang_l = 2.0 * np.pi * kl / seq
    pos = np.concatenate([np.cos(ang_l), -np.sin(ang_l)], axis=1)
    return chan.astype(np.float32), pos.astype(np.float32)


def _fold_positions(v, sign):
    seq = v.shape[0]
    n_tiles = seq // 8
    sub = lax.broadcasted_iota(jnp.int32, (8, 128), 0)
    gather = jnp.where(sub == 0, 0, 8 - sub)
    tile = lambda k: v[8 * k:8 * k + 8, :]
    out = []
    for k in range(n_tiles // 2):
        tail = jnp.take_along_axis(tile(n_tiles - 1 - k), gather, axis=0)
        if k == 0:
            mirror = jnp.where(sub == 0, 0.0, tail)
        else:
            mirror = jnp.where(sub == 0, jnp.take_along_axis(tile(n_tiles - k), gather, axis=0), tail)
        out.append(tile(k) + mirror if sign > 0 else tile(k) - mirror)
    return jnp.concatenate(out, axis=0)


def _fnet_kernel(u_ref, chan_ref, pos_ref, y_ref, pq_ref, mid_ref):
    seq = u_ref.shape[1]
    half = seq // 2
    gd = FNET_GROUP_DIM

    @pl.when(pl.program_id(1) == 0)
    def _():
        for g in range(FNET_GROUPS):
            gs = slice(g * gd, (g + 1) * gd)
            pq = _dot(u_ref[0, :, gs], chan_ref[...])
            pq_ref[0:half, gs] = _fold_positions(pq[:, :gd], 1).astype(BF16)
            pq_ref[half:seq, gs] = _fold_positions(pq[:, gd:], -1).astype(BF16)
            mid_ref[:, gs] = pq[half:half + 8, :gd]

    y = _dot(pos_ref[...], pq_ref[...])
    row = lax.broadcasted_iota(jnp.int32, y.shape, 0)
    mid = mid_ref[0:1, :]
    y_ref[0] = (y + jnp.where(row % 2 == 0, mid, -mid)).astype(BF16)


def _fnet(u, chan, pos, *, tm):
    b, seq, _ = u.shape
    return pl.pallas_call(
        _fnet_kernel,
        grid=(b, seq // tm),
        in_specs=[pl.BlockSpec((1, seq, FNET_DIM), lambda i, j: (i, 0, 0)),
                  pl.BlockSpec((FNET_GROUP_DIM, 2 * FNET_GROUP_DIM), lambda i, j: (0, 0)),
                  pl.BlockSpec((tm, seq), lambda i, j: (j, 0))],
        out_specs=pl.BlockSpec((1, tm, FNET_DIM), lambda i, j: (i, j, 0)),
        out_shape=jax.ShapeDtypeStruct((b, seq, FNET_DIM), BF16),
        scratch_shapes=[pltpu.VMEM((seq, FNET_DIM), BF16), pltpu.VMEM((8, FNET_DIM), F32)],
        compiler_params=_params("arbitrary", "arbitrary"),
        name="fnet",
    )(u, chan, pos)


def _merge_kernel(x_ref, of_ref, ob_ref, sg_ref, ga_ref, gb_ref, yf_ref, mod_ref, og_ref, n2g_ref,
                  wa_ref, wb_ref, wo_ref, x1_ref, h2_ref):
    n_sub = x_ref.shape[1] // MERGE_SUB

    def mix(rows):
        o = of_ref[0, rows, :] + ob_ref[0, rows, :]
        parts = []
        for h in range(HG_HEADS):
            oh = o[:, h * HG_HEAD_DIM:(h + 1) * HG_HEAD_DIM]
            parts.append(oh * lax.rsqrt(jnp.mean(oh * oh, axis=-1, keepdims=True) + EPS))
        on = jnp.concatenate(parts, axis=-1) * og_ref[...] * sg_ref[0, rows, :].astype(F32)
        y_b = _dot(on.astype(BF16), wb_ref[...])
        y_a = _dot(yf_ref[0, rows, :], wa_ref[...])
        m = ga_ref[0, rows, :].astype(F32) * y_a + gb_ref[0, rows, :].astype(F32) * y_b
        return _dot(m.astype(BF16), wo_ref[...])

    def finish(rows, yx):
        x1 = x_ref[0, rows, :] + mod_ref[0, 2:3, :] * yx
        x1_ref[0, rows, :] = x1
        h2 = _modulated_norm(x1, n2g_ref[...], mod_ref[0, 3:4, :], mod_ref[0, 4:5, :])
        h2_ref[0, rows, :] = h2.astype(BF16)

    pending = None
    for i in range(n_sub):
        rows = slice(i * MERGE_SUB, (i + 1) * MERGE_SUB)
        yx = mix(rows)
        if pending is not None:
            finish(*pending)
        pending = (rows, yx)
    finish(*pending)


def _merge(x, o_f, o_b, gates, y_f, mod, og, n2g, w_a, w_b, w_out, *, tm):
    b, l, d = x.shape
    tok = lambda w, col: pl.BlockSpec((1, tm, w), lambda i, j: (i, j, col))
    const = lambda shape: pl.BlockSpec(shape, lambda i, j: tuple(0 for _ in shape))
    return pl.pallas_call(
        _merge_kernel,
        grid=(b, l // tm),
        in_specs=[tok(d, 0), tok(HG_DIM, 0), tok(HG_DIM, 0), tok(HG_DIM, 2 * d // HG_DIM),
                  tok(d, 0), tok(d, 1), tok(FNET_DIM, 0),
                  pl.BlockSpec((1, N_MOD, d), lambda i, j: (i, 0, 0)),
                  const((1, HG_DIM)), const((1, d)),
                  const((FNET_DIM, d)), const((HG_DIM, d)), const((d, d))],
        out_specs=(tok(d, 0), tok(d, 0)),
        out_shape=(jax.ShapeDtypeStruct((b, l, d), F32), jax.ShapeDtypeStruct((b, l, d), BF16)),
        compiler_params=_params("arbitrary", "arbitrary"),
        name="merge",
    )(x, o_f, o_b, gates, gates, gates, y_f, mod, og, n2g, w_a, w_b, w_out)


FFN_GROUP = 8


def _ffn_permute_tokens(h_ref, hp_ref, tmp_ref):
    n_tok = tmp_ref.shape[1]

    def body(g, carry):
        tok = pl.ds(pl.multiple_of(g * n_tok, n_tok), n_tok)
        hf = h_ref[0, tok, :].astype(F32)
        for lt in range(tmp_ref.shape[0]):
            for r in range(n_tok // GRID_W):
                for a in range(8):
                    r0 = r * GRID_W + 8 * a
                    tmp_ref[lt, pl.ds(r * GRID_W + a, 8, stride=8), :] = (
                        hf[r0:r0 + 8, lt * 128:(lt + 1) * 128])
        for lt in range(tmp_ref.shape[0]):
            hp_ref[tok, lt * 128:(lt + 1) * 128] = tmp_ref[lt].astype(BF16)
        return carry

    lax.fori_loop(0, h_ref.shape[1] // n_tok, body, 0)


def _ffn_up(hp_ref, up_refs, row0, n_rows):
    h = hp_ref[row0 * GRID_W:(row0 + n_rows) * GRID_W, :]
    return [_dot(h, up_ref[...]) for up_ref in up_refs]


def _pair(lo, hi):
    return jnp.concatenate([lo, hi], axis=0).astype(BF16)


def _ffn_row(z, r, taps, bias, first, last, sub):
    lo = [z[r * GRID_W + 8 * b:r * GRID_W + 8 * b + 8, 0:128] for b in range(8)]
    hi = [z[r * GRID_W + 8 * b:r * GRID_W + 8 * b + 8, 128:256] for b in range(8)]
    mid = [_pair(lo[b], hi[b]) for b in range(8)]
    down = lambda v: jnp.where(sub == 0, 0.0, pltpu.roll(v, 1, axis=0))
    left = [_pair(down(lo[7]), down(hi[7]))] + mid[0:7]
    up = lambda v: jnp.where(sub == 7, 0.0, pltpu.roll(v, 7, axis=0))
    right = mid[1:8] + [_pair(up(lo[0]), up(hi[0]))]
    term = lambda kh: [left[b] * taps[3 * kh] + mid[b] * taps[3 * kh + 1] + right[b] * taps[3 * kh + 2]
                       for b in range(8)]
    above = None if first else term(2)
    here = term(1)
    below = None if last else [t + bias for t in term(0)]
    return above, here, below


def _ffn_down(a_ref, dn_ref, y_ref, row0, n_rows):
    tok = slice(row0 * GRID_W, (row0 + n_rows) * GRID_W)
    part = _dot(a_ref[tok, :].astype(BF16), dn_ref[...])
    for j in range(y_ref.shape[1]):
        y_ref[0, j, tok, :] += part[:, j * 128:(j + 1) * 128]


def _ffn_kernel(h_ref, up1_ref, up2_ref, cw1_ref, cw2_ref, cb1_ref, cb2_ref, dn_ref, y_ref,
                hp_ref, a_ref, wp_ref, tmp_ref):
    t = pl.program_id(1)
    rows = hp_ref.shape[0] // GRID_W
    grp = FFN_GROUP

    @pl.when(t == 0)
    def _():
        _ffn_permute_tokens(h_ref, hp_ref, tmp_ref)
        y_ref[...] = jnp.zeros_like(y_ref)

    for half, (cw_ref, cb_ref) in enumerate(((cw1_ref, cb1_ref), (cw2_ref, cb2_ref))):
        for k in range(10):
            row = cw_ref[k:k + 1, :] if k < 9 else cb_ref[...]
            wp_ref[half, k] = _pair(jnp.broadcast_to(row[:, 0:128], (8, 128)),
                                    jnp.broadcast_to(row[:, 128:256], (8, 128)))
    taps = [[wp_ref[half, k] for k in range(9)] for half in range(2)]
    bias = [wp_ref[half, 9] for half in range(2)]
    sub = lax.broadcasted_iota(jnp.int32, (8, 128), 0)

    def gate(c, out_row):
        for b in range(8):
            act = (_silu(c[0][b]) * c[1][b]).astype(F32)
            tok_b = slice(out_row * GRID_W + 8 * b, out_row * GRID_W + 8 * b + 8)
            a_ref[tok_b, 0:128] = act[0:8]
            a_ref[tok_b, 128:256] = act[8:16]

    prev = [None, None]
    cur = [[bias[half]] * 8 for half in range(2)]
    z_next = _ffn_up(hp_ref, (up1_ref, up2_ref), 0, grp)
    for row0 in range(0, rows, grp):
        z_cur = z_next
        if row0 + grp < rows:
            z_next = _ffn_up(hp_ref, (up1_ref, up2_ref), row0 + grp, grp)
        for rr in range(grp):
            r = row0 + rr
            done = []
            for half in range(2):
                above, here, below = _ffn_row(z_cur[half], rr, taps[half], bias[half],
                                              r == 0, r == rows - 1, sub)
                if above is not None:
                    done.append([prev[half][b] + above[b] for b in range(8)])
                prev[half] = [cur[half][b] + here[b] for b in range(8)]
                cur[half] = below
            if done:
                gate(done, r - 1)
            if rr == 0 and row0 > 0:
                _ffn_down(a_ref, dn_ref, y_ref, row0 - grp, grp)
    gate(prev, rows - 1)
    _ffn_down(a_ref, dn_ref, y_ref, rows - grp, grp)


def _ffn(h2, up, conv_w, conv_b, down):
    b, l, d = h2.shape
    rows = l // GRID_W
    tc = FFN_TILE
    nt = D_FF // tc
    assert rows % FFN_GROUP == 0 and tc == 256, "the conv pairs the two lane tiles of a channel tile"
    return pl.pallas_call(
        _ffn_kernel,
        grid=(b, nt),
        in_specs=[pl.BlockSpec((1, l, d), lambda i, t: (i, 0, 0)),
                  pl.BlockSpec((d, tc), lambda i, t: (0, t)),
                  pl.BlockSpec((d, tc), lambda i, t: (0, nt + t)),
                  pl.BlockSpec((9, tc), lambda i, t: (0, t)),
                  pl.BlockSpec((9, tc), lambda i, t: (0, nt + t)),
                  pl.BlockSpec((1, tc), lambda i, t: (0, t)),
                  pl.BlockSpec((1, tc), lambda i, t: (0, nt + t)),
                  pl.BlockSpec((tc, d), lambda i, t: (t, 0))],
        out_specs=pl.BlockSpec((1, d // 128, l, 128), lambda i, t: (i, 0, 0, 0)),
        out_shape=jax.ShapeDtypeStruct((b, d // 128, l, 128), F32),
        scratch_shapes=[pltpu.VMEM((l, d), BF16),
                        pltpu.VMEM((l, tc), F32),
                        pltpu.VMEM((2, 10, 16, 128), BF16),
                        pltpu.VMEM((d // 128, FFN_GROUP * GRID_W, 128), F32)],
        compiler_params=_params("arbitrary", "arbitrary"),
        name="ffn",
    )(h2, up, up, conv_w, conv_w, conv_b, conv_b, down)


def _final_kernel(x1_ref, y_ref, mod_ref, g_ref, o_ref):
    n_slab = y_ref.shape[1]
    for r in range(y_ref.shape[2] // GRID_W):
        for a in range(8):
            tok = slice(r * GRID_W + 8 * a, r * GRID_W + 8 * a + 8)
            y = jnp.concatenate([y_ref[0, j, pl.ds(r * GRID_W + a, 8, stride=8), :]
                                 for j in range(n_slab)], axis=-1)
            x = x1_ref[0, tok, :] + mod_ref[0, 5:6, :] * y
            ms = jnp.mean(x * x, axis=-1, keepdims=True)
            o_ref[0, tok, :] = x * lax.rsqrt(ms + EPS) * g_ref[...]


def _final(x1, y, mod, g, *, tm):
    b, l, d = x1.shape
    tok = pl.BlockSpec((1, tm, d), lambda i, j: (i, j, 0))
    return pl.pallas_call(
        _final_kernel,
        grid=(b, l // tm),
        in_specs=[tok, pl.BlockSpec((1, d // 128, tm, 128), lambda i, j: (i, 0, j, 0)),
                  pl.BlockSpec((1, N_MOD, d), lambda i, j: (i, 0, 0)),
                  pl.BlockSpec((1, d), lambda i, j: (0, 0))],
        out_specs=tok,
        out_shape=jax.ShapeDtypeStruct((b, l, d), F32),
        compiler_params=_params("arbitrary", "arbitrary"),
        name="final",
    )(x1, y, mod, g)


def kernel(x, c, ctx, c_ctx, ada_w, ada_b, norm1_g, w_in, hg_lb, hg_onorm_g, w_a, w_b, w_out,
           norm2_g, ffn_up, ffn_conv_w, ffn_conv_b, ffn_down, final_g):
    b, seq, d = x.shape
    assert ada_w.shape[0] == 1, "single-layer kernel"
    assert (d, seq % GRID_W) == (D_MODEL, 0)

    lb = jnp.cumsum(jax.nn.softmax(hg_lb.astype(F32), axis=0), axis=0)[0].reshape(1, HG_DIM)

    n_rows = 16
    rows = jnp.concatenate([c, c_ctx[None, :], jnp.zeros((n_rows - b - 1, d), F32)], axis=0)
    mod = _ada(rows, ada_w[0], ada_b[0])
    mod_x = mod[:b].reshape(b, N_MOD, d)
    mod_c = mod[b:b + 1].reshape(1, N_MOD, d)

    w_in_b = w_in[0].astype(BF16)
    g1 = norm1_g[0].reshape(1, d)
    hg_x, u_f, gates = _inproj(x, mod_x, g1, lb, w_in_b, tm=1024, full=True)
    w_ctx = w_in_b[:, COL_Q * HG_DIM:(COL_I + 1) * HG_DIM]
    (hg_c,) = _inproj(ctx, mod_c, g1, lb, w_ctx, tm=256, full=False)

    s_ctx = _hgrn(hg_c, None, emit_o=False)
    o_f, o_b = _hgrn(hg_x, s_ctx, emit_o=True)

    chan, pos = _dft_tables(seq)
    y_f = _fnet(u_f, jnp.asarray(chan).astype(BF16), jnp.asarray(pos).astype(BF16), tm=512)

    x1, h2 = _merge(x, o_f, o_b, gates, y_f, mod_x, hg_onorm_g[0].reshape(1, HG_DIM),
                    norm2_g[0].reshape(1, d), w_a[0].astype(BF16), w_b[0].astype(BF16),
                    w_out[0].astype(BF16), tm=1024)

    y = _ffn(h2, ffn_up[0].astype(BF16), ffn_conv_w[0].reshape(9, 2 * D_FF),
             ffn_conv_b[0].reshape(1, 2 * D_FF), ffn_down[0].astype(BF16))
    return _final(x1, y, mod_x, final_g.reshape(1, d), tm=512)
```

```python
import functools

import numpy as np
import jax
import jax.numpy as jnp
from jax import lax
from jax.experimental import pallas as pl
from jax.experimental.pallas import tpu as pltpu

D_MODEL = 1024
GRID_W = 64
FNET_DIM = 512
FNET_GROUPS = 4
FNET_GROUP_DIM = FNET_DIM // FNET_GROUPS
HG_DIM = 512
HG_HEADS = 4
HG_HEAD_DIM = HG_DIM // HG_HEADS
D_FF = 2816
N_MOD = 6
EPS = 1e-6

COL_FNET, COL_Q, COL_FF, COL_FB, COL_I, COL_G = 0, 1, 2, 3, 4, 5

INPROJ_SUB = 256
MERGE_SUB = 256
HG_BLOCK = 1024
HG_CHUNK = 64
HG_UNROLL = 2
HG_SUB = 16
FFN_TILE = 256

BF16 = jnp.bfloat16
F32 = jnp.float32
VMEM_LIMIT = 56 * 1024 * 1024


def _sigmoid(x):
    return 1.0 / (1.0 + jnp.exp(-x))


def _silu(x):
    return x * _sigmoid(x)


def _dot(a, b):
    return jnp.dot(a, b, preferred_element_type=F32)


def _dot_nt(a, b):
    return lax.dot_general(a, b, (((1,), (1,)), ((), ())), preferred_element_type=F32)


def _params(*sem, flags=None):
    return pltpu.CompilerParams(dimension_semantics=sem, vmem_limit_bytes=VMEM_LIMIT, flags=flags)


def _ada_kernel(c_ref, w_ref, b_ref, o_ref):
    a = _silu(c_ref[...])
    o_ref[...] = _dot(a.astype(BF16), w_ref[...].astype(BF16)) + b_ref[...]


def _ada(rows, ada_w, ada_b):
    m, d = rows.shape
    n = ada_w.shape[1]
    tn = 1536
    return pl.pallas_call(
        _ada_kernel,
        grid=(n // tn,),
        in_specs=[pl.BlockSpec((m, d), lambda j: (0, 0)),
                  pl.BlockSpec((d, tn), lambda j: (0, j)),
                  pl.BlockSpec((1, tn), lambda j: (0, j))],
        out_specs=pl.BlockSpec((m, tn), lambda j: (0, j)),
        out_shape=jax.ShapeDtypeStruct((m, n), F32),
        compiler_params=_params("arbitrary"),
        name="ada",
    )(rows, ada_w, ada_b.reshape(1, n))


def _modulated_norm(x, g, shift, scale):
    ms = jnp.mean(x * x, axis=-1, keepdims=True)
    return (x * lax.rsqrt(ms + EPS) * g) * (1.0 + scale) + shift


def _inproj_kernel(x_ref, mod_ref, g_ref, lb_ref, w_ref, *o_refs, hg_col0, full):
    lb = lb_ref[...]
    hg_ref = o_refs[0]
    n_sub = x_ref.shape[1] // INPROJ_SUB

    def normed(i):
        rows = slice(i * INPROJ_SUB, (i + 1) * INPROJ_SUB)
        return _modulated_norm(x_ref[0, rows, :], g_ref[...], mod_ref[0, 0:1, :],
                               mod_ref[0, 1:2, :]).astype(BF16)

    hb_next = normed(0)
    for i in range(n_sub):
        hb = hb_next
        rows = slice(i * INPROJ_SUB, (i + 1) * INPROJ_SUB)

        def proj(col, width=HG_DIM):
            return _dot(hb, w_ref[:, col * HG_DIM:col * HG_DIM + width])

        hg_ref[0, rows, 0:HG_DIM] = _silu(proj(hg_col0))
        if i + 1 < n_sub:
            hb_next = normed(i + 1)
        for n in (1, 2):
            f = lb + (1.0 - lb) * _sigmoid(proj(hg_col0 + n))
            hg_ref[0, rows, n * HG_DIM:(n + 1) * HG_DIM] = jnp.log2(f)
        hg_ref[0, rows, 3 * HG_DIM:4 * HG_DIM] = proj(hg_col0 + 3)
        if full:
            uf_ref, gate_ref = o_refs[1], o_refs[2]
            uf_ref[0, rows, :] = proj(COL_FNET).astype(BF16)
            gate_ref[0, rows, 0:2 * D_MODEL] = _sigmoid(proj(COL_G + 1, 2 * D_MODEL)).astype(BF16)
            gate_ref[0, rows, 2 * D_MODEL:2 * D_MODEL + HG_DIM] = _silu(proj(COL_G)).astype(BF16)


def _inproj(x, mod, g, lb, w, *, tm, full):
    b, l, d = x.shape
    n = w.shape[1]
    mod_map = (lambda i, j: (i, 0, 0)) if full else (lambda i, j: (0, 0, 0))
    tok = lambda width, dtype: (pl.BlockSpec((1, tm, width), lambda i, j: (i, j, 0)),
                                jax.ShapeDtypeStruct((b, l, width), dtype))
    outs = [tok(4 * HG_DIM, F32)]
    if full:
        outs += [tok(FNET_DIM, BF16), tok(HG_DIM + 2 * D_MODEL, BF16)]
    return pl.pallas_call(
        functools.partial(_inproj_kernel, hg_col0=COL_Q if full else 0, full=full),
        grid=(b, l // tm),
        in_specs=[pl.BlockSpec((1, tm, d), lambda i, j: (i, j, 0)),
                  pl.BlockSpec((1, N_MOD, d), mod_map),
                  pl.BlockSpec((1, d), lambda i, j: (0, 0)),
                  pl.BlockSpec((1, HG_DIM), lambda i, j: (0, 0)),
                  pl.BlockSpec((d, n), lambda i, j: (0, 0))],
        out_specs=tuple(o[0] for o in outs),
        out_shape=tuple(o[1] for o in outs),
        compiler_params=_params("arbitrary", "arbitrary"),
        name="inproj_x" if full else "inproj_ctx",
    )(x, mod, g, lb, w)


def _hgrn_step(dirs, st_ref):
    c = HG_CHUNK
    hd = HG_HEAD_DIM
    r_idx = lax.broadcasted_iota(jnp.int32, (c, c), 0)
    c_idx = lax.broadcasted_iota(jnp.int32, (c, c), 1)
    pairs = [slice(p * 2 * hd, (p + 1) * 2 * hd) for p in range(HG_HEADS // 2)]

    def block_diag(a0, a1):
        z0, z1 = jnp.zeros(a0.shape, a0.dtype), jnp.zeros(a1.shape, a1.dtype)
        return jnp.concatenate([jnp.concatenate([a0, z0], axis=1),
                                jnp.concatenate([z1, a1], axis=1)], axis=0)

    cums = []
    for q, l2f, v, reverse, o_ref, row0, base in dirs:
        tri = jnp.where((c_idx >= r_idx) if reverse else (c_idx <= r_idx), 1.0, 0.0).astype(BF16)
        hi = l2f.astype(BF16)
        lo = (l2f - hi.astype(F32)).astype(BF16)
        cums.append(_dot(tri, hi) + _dot(tri, lo))

    state_in, state_upd, intra = [], [], []
    for (q, l2f, v, reverse, o_ref, row0, base), cum in zip(dirs, cums):
        k = 1.0 - jnp.exp2(l2f)
        last = 0 if reverse else c - 1
        total = cum[last:last + 1, :]
        q_in = (q * jnp.exp2(cum)).astype(BF16)
        k_out = (k * jnp.exp2(total - cum)).astype(BF16)
        e_tot = jnp.exp2(total)
        v_b = v.astype(BF16)
        for p, ps in enumerate(pairs):
            h0, h1 = slice(ps.start, ps.start + hd), slice(ps.start + hd, ps.stop)
            v_t = jnp.concatenate([v[:, h0], v[:, h1]], axis=0).T.astype(BF16)
            state_in.append(q_in[:, ps])
            state_upd.append((base + p, e_tot[:, ps], v_t, block_diag(k_out[:, h0], k_out[:, h1])))
            if o_ref is None:
                continue
            for i in range(c // HG_SUB):
                r0 = i * HG_SUB
                rows = slice(r0, r0 + HG_SUB)
                cols = slice(r0, c) if reverse else slice(0, r0 + HG_SUB)
                mid = r0 + HG_SUB // 2 if reverse else r0 + HG_SUB // 2 - 1
                m = cum[mid:mid + 1, ps]
                q_t = (q[rows, ps] * jnp.exp2(cum[rows, ps] - m)).astype(BF16)
                k_t = (k[cols, ps] * jnp.exp2(m - cum[cols, ps])).astype(BF16)
                intra.append((o_ref, row0, rows, cols, ps, reverse, len(state_in) - 1, q_t,
                              block_diag(k_t[:, :hd], k_t[:, hd:]),
                              block_diag(v_b[cols, h0], v_b[cols, h1])))

    scores = [_dot_nt(item[7], item[8]) for item in intra]
    kv = [_dot(v_t, k_bd) for _, _, v_t, k_bd in state_upd]
    states = {}
    inter = []
    for n, (idx, e_tot_p, _, _) in enumerate(state_upd):
        st = states[idx] if idx in states else st_ref[idx]
        if intra:
            st_b = st.astype(BF16)
            inter.append(_dot_nt(state_in[n], block_diag(st_b[:, :hd], st_b[:, hd:])))
        states[idx] = st * e_tot_p + kv[n]
    for idx, st in states.items():
        st_ref[idx] = st

    for (o_ref, row0, rows, cols, ps, reverse, chain, _, _, v_bd), a in zip(intra, scores):
        n_cols = cols.stop - cols.start
        row_g = rows.start + lax.broadcasted_iota(jnp.int32, (HG_SUB, 2 * n_cols), 0)
        lane = lax.broadcasted_iota(jnp.int32, (HG_SUB, 2 * n_cols), 1)
        col_g = cols.start + jnp.where(lane >= n_cols, lane - n_cols, lane)
        a = jnp.where((col_g >= row_g) if reverse else (col_g <= row_g), a, 0.0)
        o_ref[0, pl.ds(row0 + rows.start, HG_SUB), ps] = _dot(a.astype(BF16), v_bd) + inter[chain][rows]


def _hgrn_kernel(*refs, has_s0, emit_o):
    it = iter(refs)
    fwd_refs = [next(it) for _ in range(3)]
    bwd_refs = [next(it) for _ in range(3)]
    qf_ref = fwd_refs[0]
    s0_ref = next(it) if has_s0 else None
    of_ref = next(it) if emit_o else None
    ob_ref = next(it) if emit_o else None
    sout_ref = None if emit_o else next(it)
    st_ref = next(it)
    j = pl.program_id(1)
    n_inner = qf_ref.shape[1] // HG_CHUNK

    @pl.when(j == 0)
    def _():
        if has_s0:
            st_ref[...] = s0_ref[0]
        else:
            st_ref[...] = jnp.zeros_like(st_ref)

    span = HG_UNROLL * HG_CHUNK

    def body(it, carry):
        dirs = []
        for jj in range(HG_UNROLL):
            rf = pl.multiple_of(it * span, span) + jj * HG_CHUNK
            rb = pl.multiple_of((n_inner // HG_UNROLL - 1 - it) * span, span) + (HG_UNROLL - 1 - jj) * HG_CHUNK
            sf, sb = pl.ds(rf, HG_CHUNK), pl.ds(rb, HG_CHUNK)
            dirs.append(tuple(r[0, sf, :] for r in fwd_refs) + (False, of_ref, rf, 0))
            dirs.append(tuple(r[0, sb, :] for r in bwd_refs) + (True, ob_ref, rb, HG_HEADS // 2))
        _hgrn_step(dirs, st_ref)
        return carry

    lax.fori_loop(0, n_inner // HG_UNROLL, body, 0)

    if not emit_o:
        @pl.when(j == pl.num_programs(1) - 1)
        def _():
            sout_ref[0] = st_ref[...]


def _hgrn(u, s0, *, emit_o):
    b, l, _ = u.shape
    c = min(HG_BLOCK, l)
    n = l // c
    blk = (1, c, HG_DIM)
    fwd = lambda col: pl.BlockSpec(blk, lambda i, j: (i, j, col))
    bwd = lambda col: pl.BlockSpec(blk, lambda i, j: (i, n - 1 - j, col))
    st_dims = (HG_HEADS, HG_HEAD_DIM, 2 * HG_HEAD_DIM)
    state_spec = pl.BlockSpec((1,) + st_dims, lambda i, j: (i, 0, 0, 0))
    state_shape = jax.ShapeDtypeStruct((b,) + st_dims, F32)
    in_specs = [fwd(0), fwd(1), fwd(3), bwd(0), bwd(2), bwd(3)]
    args = [u, u, u, u, u, u]
    if s0 is not None:
        in_specs.append(state_spec)
        args.append(s0)
    if emit_o:
        o_shape = jax.ShapeDtypeStruct((b, l, HG_DIM), F32)
        out_shape = (o_shape, o_shape)
        out_specs = (pl.BlockSpec(blk, lambda i, j: (i, j, 0)),
                     pl.BlockSpec(blk, lambda i, j: (i, n - 1 - j, 0)))
    else:
        out_shape = state_shape
        out_specs = state_spec
    return pl.pallas_call(
        functools.partial(_hgrn_kernel, has_s0=s0 is not None, emit_o=emit_o),
        grid=(b, n),
        in_specs=in_specs,
        out_specs=out_specs,
        out_shape=out_shape,
        scratch_shapes=[pltpu.VMEM(st_dims, F32)],
        compiler_params=_params("arbitrary", "arbitrary"),
        name="hgrn_x" if emit_o else "hgrn_ctx",
    )(*args)


FNET_PAD = 16


def _dft_tables(seq):
    gd = FNET_GROUP_DIM
    kc = (np.arange(gd)[:, None] * np.arange(gd)[None, :]) % gd
    ang_c = 2.0 * np.pi * kc / gd
    scale = 1.0 / np.sqrt(float(seq) * gd)
    chan = np.concatenate([np.cos(ang_c), np.sin(ang_c)], axis=1) * scale
    half = seq // 2
    kl = (np.arange(half + FNET_PAD)[:, None] * np.arange(half)[None, :]) % seq
    ang_l = 2.0 * np.pi * kl / seq
    keep = (np.arange(half + FNET_PAD) <= half)[:, None]
    pos_c = np.where(keep, np.cos(ang_l), 0.0)
    pos_s = np.where(keep, -np.sin(ang_l), 0.0)
    return chan.astype(np.float32), pos_c.astype(np.float32), pos_s.astype(np.float32)


def _mirror_gather(width):
    sub = lax.broadcasted_iota(jnp.int32, (8, width), 0)
    return jnp.where(sub == 0, 0, 8 - sub), sub


def _fold_positions(v, sign):
    seq = v.shape[0]
    n_tiles = seq // 8
    gather, sub = _mirror_gather(v.shape[1])
    tile = lambda k: v[8 * k:8 * k + 8, :]
    out = []
    for k in range(n_tiles // 2):
        tail = jnp.take_along_axis(tile(n_tiles - 1 - k), gather, axis=0)
        if k == 0:
            mirror = jnp.where(sub == 0, 0.0, tail)
        else:
            mirror = jnp.where(sub == 0, jnp.take_along_axis(tile(n_tiles - k), gather, axis=0), tail)
        out.append(tile(k) + mirror if sign > 0 else tile(k) - mirror)
    return jnp.concatenate(out, axis=0)


def _fnet_kernel(u_ref, chan_ref, posc_ref, poss_ref, y_ref, pq_ref):
    seq = u_ref.shape[1]
    half = seq // 2
    gd = FNET_GROUP_DIM
    mids = []
    for g in range(FNET_GROUPS):
        gs = slice(g * gd, (g + 1) * gd)
        pq = _dot(u_ref[0, :, gs], chan_ref[...])
        pq_ref[0:half, gs] = _fold_positions(pq[:, :gd], 1).astype(BF16)
        pq_ref[half:seq, gs] = _fold_positions(pq[:, gd:], -1).astype(BF16)
        mids.append(pq[half:half + 1, :gd])
    gather, sub = _mirror_gather(FNET_DIM)
    mid = jnp.concatenate(mids, axis=1)
    mid_alt = jnp.where(sub % 2 == 0, mid, -mid)

    a = _dot(posc_ref[...], pq_ref[0:half, :])
    bm = _dot(poss_ref[...], pq_ref[half:seq, :])
    tile = lambda v, k: v[8 * k:8 * k + 8, :]
    n_t = half // 8
    for k in range(0, n_t, 2):
        pair = [tile(a, k + i) + tile(bm, k + i) + mid_alt for i in range(2)]
        y_ref[0, 8 * k:8 * k + 16, :] = jnp.concatenate(pair, axis=0).astype(BF16)
    flipped = [jnp.take_along_axis(tile(a, k) - tile(bm, k) + mid_alt, gather, axis=0) for k in range(n_t)]
    flipped.append(jnp.take_along_axis(tile(a, n_t) + tile(bm, n_t) + mid_alt, gather, axis=0))
    n_all = seq // 8
    for m in range(n_t, n_all, 2):
        pair = [jnp.where(sub == 0, flipped[n_all - mm], flipped[n_all - 1 - mm]) for mm in (m, m + 1)]
        y_ref[0, 8 * m:8 * m + 16, :] = jnp.concatenate(pair, axis=0).astype(BF16)


def _fnet(u, chan, pos_c, pos_s):
    b, seq, _ = u.shape
    table = pl.BlockSpec(pos_c.shape, lambda i: (0, 0))
    return pl.pallas_call(
        _fnet_kernel,
        grid=(b,),
        in_specs=[pl.BlockSpec((1, seq, FNET_DIM), lambda i: (i, 0, 0)),
                  pl.BlockSpec((FNET_GROUP_DIM, 2 * FNET_GROUP_DIM), lambda i: (0, 0)),
                  table, table],
        out_specs=pl.BlockSpec((1, seq, FNET_DIM), lambda i: (i, 0, 0)),
        out_shape=jax.ShapeDtypeStruct((b, seq, FNET_DIM), BF16),
        scratch_shapes=[pltpu.VMEM((seq, FNET_DIM), BF16)],
        compiler_params=_params("arbitrary"),
        name="fnet",
    )(u, chan, pos_c, pos_s)


def _merge_kernel(x_ref, of_ref, ob_ref, sg_ref, ga_ref, gb_ref, yf_ref, mod_ref, og_ref, n2g_ref,
                  wa_ref, wb_ref, wo_ref, x1_ref, h2_ref):
    n_sub = x_ref.shape[1] // MERGE_SUB

    def mix(rows):
        o = of_ref[0, rows, :] + ob_ref[0, rows, :]
        parts = []
        for h in range(HG_HEADS):
            oh = o[:, h * HG_HEAD_DIM:(h + 1) * HG_HEAD_DIM]
            parts.append(oh * lax.rsqrt(jnp.mean(oh * oh, axis=-1, keepdims=True) + EPS))
        on = jnp.concatenate(parts, axis=-1) * og_ref[...] * sg_ref[0, rows, :].astype(F32)
        y_b = _dot(on.astype(BF16), wb_ref[...])
        y_a = _dot(yf_ref[0, rows, :], wa_ref[...])
        m = ga_ref[0, rows, :].astype(F32) * y_a + gb_ref[0, rows, :].astype(F32) * y_b
        return _dot(m.astype(BF16), wo_ref[...])

    def finish(rows, yx):
        x1 = x_ref[0, rows, :] + mod_ref[0, 2:3, :] * yx
        x1_ref[0, rows, :] = x1
        h2 = _modulated_norm(x1, n2g_ref[...], mod_ref[0, 3:4, :], mod_ref[0, 4:5, :])
        h2_ref[0, rows, :] = h2.astype(BF16)

    pending = None
    for i in range(n_sub):
        rows = slice(i * MERGE_SUB, (i + 1) * MERGE_SUB)
        yx = mix(rows)
        if pending is not None:
            finish(*pending)
        pending = (rows, yx)
    finish(*pending)


def _merge(x, o_f, o_b, gates, y_f, mod, og, n2g, w_a, w_b, w_out, *, tm):
    b, l, d = x.shape
    tok = lambda w, col: pl.BlockSpec((1, tm, w), lambda i, j: (i, j, col))
    const = lambda shape: pl.BlockSpec(shape, lambda i, j: tuple(0 for _ in shape))
    return pl.pallas_call(
        _merge_kernel,
        grid=(b, l // tm),
        in_specs=[tok(d, 0), tok(HG_DIM, 0), tok(HG_DIM, 0), tok(HG_DIM, 2 * d // HG_DIM),
                  tok(d, 0), tok(d, 1), tok(FNET_DIM, 0),
                  pl.BlockSpec((1, N_MOD, d), lambda i, j: (i, 0, 0)),
                  const((1, HG_DIM)), const((1, d)),
                  const((FNET_DIM, d)), const((HG_DIM, d)), const((d, d))],
        out_specs=(tok(d, 0), tok(d, 0)),
        out_shape=(jax.ShapeDtypeStruct((b, l, d), F32), jax.ShapeDtypeStruct((b, l, d), BF16)),
        compiler_params=_params("arbitrary", "arbitrary"),
        name="merge",
    )(x, o_f, o_b, gates, gates, gates, y_f, mod, og, n2g, w_a, w_b, w_out)


FFN_GROUP = 8


def _ffn_permute_tokens(h_ref, hp_ref, tmp_ref):
    n_tok = tmp_ref.shape[1]

    def body(g, carry):
        tok = pl.ds(pl.multiple_of(g * n_tok, n_tok), n_tok)
        hf = h_ref[0, tok, :].astype(F32)
        for lt in range(tmp_ref.shape[0]):
            for r in range(n_tok // GRID_W):
                for a in range(8):
                    r0 = r * GRID_W + 8 * a
                    tmp_ref[lt, pl.ds(r * GRID_W + a, 8, stride=8), :] = (
                        hf[r0:r0 + 8, lt * 128:(lt + 1) * 128])
        for lt in range(tmp_ref.shape[0]):
            hp_ref[tok, lt * 128:(lt + 1) * 128] = tmp_ref[lt].astype(BF16)
        return carry

    lax.fori_loop(0, h_ref.shape[1] // n_tok, body, 0)


def _ffn_up(hp_ref, up_refs, row0, n_rows):
    h = hp_ref[row0 * GRID_W:(row0 + n_rows) * GRID_W, :]
    return [_dot(h, up_ref[...]) for up_ref in up_refs]


def _pair(lo, hi):
    return jnp.concatenate([lo, hi], axis=0).astype(BF16)


def _ffn_row(z, r, taps, bias, first, last, sub):
    lo = [z[r * GRID_W + 8 * b:r * GRID_W + 8 * b + 8, 0:128] for b in range(8)]
    hi = [z[r * GRID_W + 8 * b:r * GRID_W + 8 * b + 8, 128:256] for b in range(8)]
    mid = [_pair(lo[b], hi[b]) for b in range(8)]
    down = lambda v: jnp.where(sub == 0, 0.0, pltpu.roll(v, 1, axis=0))
    left = [_pair(down(lo[7]), down(hi[7]))] + mid[0:7]
    up = lambda v: jnp.where(sub == 7, 0.0, pltpu.roll(v, 7, axis=0))
    right = mid[1:8] + [_pair(up(lo[0]), up(hi[0]))]
    term = lambda kh: [left[b] * taps[3 * kh] + mid[b] * taps[3 * kh + 1] + right[b] * taps[3 * kh + 2]
                       for b in range(8)]
    above = None if first else term(2)
    here = term(1)
    below = None if last else [t + bias for t in term(0)]
    return above, here, below


def _ffn_down(a_ref, dn_ref, y_ref, row0, n_rows):
    tok = slice(row0 * GRID_W, (row0 + n_rows) * GRID_W)
    part = _dot(a_ref[tok, :].astype(BF16), dn_ref[...])
    for j in range(y_ref.shape[1]):
        y_ref[0, j, tok, :] += part[:, j * 128:(j + 1) * 128]


def _ffn_kernel(h_ref, up1_ref, up2_ref, cw1_ref, cw2_ref, cb1_ref, cb2_ref, dn_ref, y_ref,
                hp_ref, a_ref, wp_ref, tmp_ref):
    t = pl.program_id(1)
    rows = hp_ref.shape[0] // GRID_W
    grp = FFN_GROUP

    @pl.when(t == 0)
    def _():
        _ffn_permute_tokens(h_ref, hp_ref, tmp_ref)
        y_ref[...] = jnp.zeros_like(y_ref)

    for half, (cw_ref, cb_ref) in enumerate(((cw1_ref, cb1_ref), (cw2_ref, cb2_ref))):
        for k in range(10):
            row = cw_ref[k:k + 1, :] if k < 9 else cb_ref[...]
            wp_ref[half, k] = _pair(jnp.broadcast_to(row[:, 0:128], (8, 128)),
                                    jnp.broadcast_to(row[:, 128:256], (8, 128)))
    taps = [[wp_ref[half, k] for k in range(9)] for half in range(2)]
    bias = [wp_ref[half, 9] for half in range(2)]
    sub = lax.broadcasted_iota(jnp.int32, (8, 128), 0)

    def gate(c, out_row):
        for b in range(8):
            act = (_silu(c[0][b]) * c[1][b]).astype(F32)
            tok_b = slice(out_row * GRID_W + 8 * b, out_row * GRID_W + 8 * b + 8)
            a_ref[tok_b, 0:128] = act[0:8]
            a_ref[tok_b, 128:256] = act[8:16]

    prev = [None, None]
    cur = [[bias[half]] * 8 for half in range(2)]
    z_next = _ffn_up(hp_ref, (up1_ref, up2_ref), 0, grp)
    for row0 in range(0, rows, grp):
        z_cur = z_next
        if row0 + grp < rows:
            z_next = _ffn_up(hp_ref, (up1_ref, up2_ref), row0 + grp, grp)
        for rr in range(grp):
            r = row0 + rr
            done = []
            for half in range(2):
                above, here, below = _ffn_row(z_cur[half], rr, taps[half], bias[half],
                                              r == 0, r == rows - 1, sub)
                if above is not None:
                    done.append([prev[half][b] + above[b] for b in range(8)])
                prev[half] = [cur[half][b] + here[b] for b in range(8)]
                cur[half] = below
            if done:
                gate(done, r - 1)
            if rr == 0 and row0 > 0:
                _ffn_down(a_ref, dn_ref, y_ref, row0 - grp, grp)
    gate(prev, rows - 1)
    _ffn_down(a_ref, dn_ref, y_ref, rows - grp, grp)


def _ffn(h2, up, conv_w, conv_b, down):
    b, l, d = h2.shape
    rows = l // GRID_W
    tc = FFN_TILE
    nt = D_FF // tc
    assert rows % FFN_GROUP == 0 and tc == 256, "the conv pairs the two lane tiles of a channel tile"
    return pl.pallas_call(
        _ffn_kernel,
        grid=(b, nt),
        in_specs=[pl.BlockSpec((1, l, d), lambda i, t: (i, 0, 0)),
                  pl.BlockSpec((d, tc), lambda i, t: (0, t)),
                  pl.BlockSpec((d, tc), lambda i, t: (0, nt + t)),
                  pl.BlockSpec((9, tc), lambda i, t: (0, t)),
                  pl.BlockSpec((9, tc), lambda i, t: (0, nt + t)),
                  pl.BlockSpec((1, tc), lambda i, t: (0, t)),
                  pl.BlockSpec((1, tc), lambda i, t: (0, nt + t)),
                  pl.BlockSpec((tc, d), lambda i, t: (t, 0))],
        out_specs=pl.BlockSpec((1, d // 128, l, 128), lambda i, t: (i, 0, 0, 0)),
        out_shape=jax.ShapeDtypeStruct((b, d // 128, l, 128), F32),
        scratch_shapes=[pltpu.VMEM((l, d), BF16),
                        pltpu.VMEM((l, tc), F32),
                        pltpu.VMEM((2, 10, 16, 128), BF16),
                        pltpu.VMEM((d // 128, FFN_GROUP * GRID_W, 128), F32)],
        compiler_params=_params("arbitrary", "arbitrary"),
        name="ffn",
    )(h2, up, up, conv_w, conv_w, conv_b, conv_b, down)


def _final_kernel(x1_ref, y_ref, mod_ref, g_ref, o_ref):
    n_slab = y_ref.shape[1]
    for r in range(y_ref.shape[2] // GRID_W):
        for a in range(8):
            tok = slice(r * GRID_W + 8 * a, r * GRID_W + 8 * a + 8)
            y = jnp.concatenate([y_ref[0, j, pl.ds(r * GRID_W + a, 8, stride=8), :]
                                 for j in range(n_slab)], axis=-1)
            x = x1_ref[0, tok, :] + mod_ref[0, 5:6, :] * y
            ms = jnp.mean(x * x, axis=-1, keepdims=True)
            o_ref[0, tok, :] = x * lax.rsqrt(ms + EPS) * g_ref[...]


def _final(x1, y, mod, g, *, tm):
    b, l, d = x1.shape
    tok = pl.BlockSpec((1, tm, d), lambda i, j: (i, j, 0))
    return pl.pallas_call(
        _final_kernel,
        grid=(b, l // tm),
        in_specs=[tok, pl.BlockSpec((1, d // 128, tm, 128), lambda i, j: (i, 0, j, 0)),
                  pl.BlockSpec((1, N_MOD, d), lambda i, j: (i, 0, 0)),
                  pl.BlockSpec((1, d), lambda i, j: (0, 0))],
        out_specs=tok,
        out_shape=jax.ShapeDtypeStruct((b, l, d), F32),
        compiler_params=_params("arbitrary", "arbitrary"),
        name="final",
    )(x1, y, mod, g)


def kernel(x, c, ctx, c_ctx, ada_w, ada_b, norm1_g, w_in, hg_lb, hg_onorm_g, w_a, w_b, w_out,
           norm2_g, ffn_up, ffn_conv_w, ffn_conv_b, ffn_down, final_g):
    b, seq, d = x.shape
    assert ada_w.shape[0] == 1, "single-layer kernel"
    assert (d, seq % GRID_W) == (D_MODEL, 0)

    lb = jnp.cumsum(jax.nn.softmax(hg_lb.astype(F32), axis=0), axis=0)[0].reshape(1, HG_DIM)

    n_rows = 16
    rows = jnp.concatenate([c, c_ctx[None, :], jnp.zeros((n_rows - b - 1, d), F32)], axis=0)
    mod = _ada(rows, ada_w[0], ada_b[0])
    mod_x = mod[:b].reshape(b, N_MOD, d)
    mod_c = mod[b:b + 1].reshape(1, N_MOD, d)

    w_in_b = w_in[0].astype(BF16)
    g1 = norm1_g[0].reshape(1, d)
    hg_x, u_f, gates = _inproj(x, mod_x, g1, lb, w_in_b, tm=1024, full=True)
    w_ctx = w_in_b[:, COL_Q * HG_DIM:(COL_I + 1) * HG_DIM]
    (hg_c,) = _inproj(ctx, mod_c, g1, lb, w_ctx, tm=256, full=False)

    s_ctx = _hgrn(hg_c, None, emit_o=False)
    o_f, o_b = _hgrn(hg_x, s_ctx, emit_o=True)

    chan, pos_c, pos_s = (jnp.asarray(t).astype(BF16) for t in _dft_tables(seq))
    y_f = _fnet(u_f, chan, pos_c, pos_s)

    x1, h2 = _merge(x, o_f, o_b, gates, y_f, mod_x, hg_onorm_g[0].reshape(1, HG_DIM),
                    norm2_g[0].reshape(1, d), w_a[0].astype(BF16), w_b[0].astype(BF16),
                    w_out[0].astype(BF16), tm=1024)

    y = _ffn(h2, ffn_up[0].astype(BF16), ffn_conv_w[0].reshape(9, 2 * D_FF),
             ffn_conv_b[0].reshape(1, 2 * D_FF), ffn_down[0].astype(BF16))
    return _final(x1, y, mod_x, final_g.reshape(1, d), tm=512)
```

```python
import functools

import numpy as np
import jax
import jax.numpy as jnp
from jax import lax
from jax.experimental import pallas as pl
from jax.experimental.pallas import tpu as pltpu

D_MODEL = 1024
GRID_W = 64
FNET_DIM = 512
FNET_GROUPS = 4
FNET_GROUP_DIM = FNET_DIM // FNET_GROUPS
HG_DIM = 512
HG_HEADS = 4
HG_HEAD_DIM = HG_DIM // HG_HEADS
D_FF = 2816
N_MOD = 6
EPS = 1e-6

COL_FNET, COL_Q, COL_FF, COL_FB, COL_I, COL_G = 0, 1, 2, 3, 4, 5

INPROJ_SUB = 256
MERGE_SUB = 256
HG_BLOCK = 1024
HG_CHUNK = 64
HG_UNROLL = 2
HG_SUB = 16
FFN_TILE = 256

BF16 = jnp.bfloat16
F32 = jnp.float32
VMEM_LIMIT = 56 * 1024 * 1024


def _sigmoid(x):
    return 1.0 / (1.0 + jnp.exp(-x))


def _silu(x):
    return x * _sigmoid(x)


def _dot(a, b):
    return jnp.dot(a, b, preferred_element_type=F32)


def _dot_nt(a, b):
    return lax.dot_general(a, b, (((1,), (1,)), ((), ())), preferred_element_type=F32)


def _params(*sem, flags=None):
    return pltpu.CompilerParams(dimension_semantics=sem, vmem_limit_bytes=VMEM_LIMIT, flags=flags)


def _ada_kernel(c_ref, w_ref, b_ref, o_ref):
    a = _silu(c_ref[...])
    o_ref[...] = _dot(a.astype(BF16), w_ref[...].astype(BF16)) + b_ref[...]


def _ada(rows, ada_w, ada_b):
    m, d = rows.shape
    n = ada_w.shape[1]
    tn = 1536
    return pl.pallas_call(
        _ada_kernel,
        grid=(n // tn,),
        in_specs=[pl.BlockSpec((m, d), lambda j: (0, 0)),
                  pl.BlockSpec((d, tn), lambda j: (0, j)),
                  pl.BlockSpec((1, tn), lambda j: (0, j))],
        out_specs=pl.BlockSpec((m, tn), lambda j: (0, j)),
        out_shape=jax.ShapeDtypeStruct((m, n), F32),
        compiler_params=_params("arbitrary"),
        name="ada",
    )(rows, ada_w, ada_b.reshape(1, n))


def _modulated_norm(x, g, shift, scale):
    ms = jnp.mean(x * x, axis=-1, keepdims=True)
    return (x * lax.rsqrt(ms + EPS) * g) * (1.0 + scale) + shift


def _inproj_kernel(x_ref, mod_ref, g_ref, lb_ref, w_ref, *o_refs, hg_col0, full):
    lb = lb_ref[...]
    hg_ref = o_refs[0]
    n_sub = x_ref.shape[1] // INPROJ_SUB

    def normed(i):
        rows = slice(i * INPROJ_SUB, (i + 1) * INPROJ_SUB)
        return _modulated_norm(x_ref[0, rows, :], g_ref[...], mod_ref[0, 0:1, :],
                               mod_ref[0, 1:2, :]).astype(BF16)

    hb_next = normed(0)
    for i in range(n_sub):
        hb = hb_next
        rows = slice(i * INPROJ_SUB, (i + 1) * INPROJ_SUB)

        def proj(col, width=HG_DIM):
            return _dot(hb, w_ref[:, col * HG_DIM:col * HG_DIM + width])

        hg_ref[0, rows, 0:HG_DIM] = _silu(proj(hg_col0))
        if i + 1 < n_sub:
            hb_next = normed(i + 1)
        for n in (1, 2):
            f = lb + (1.0 - lb) * _sigmoid(proj(hg_col0 + n))
            hg_ref[0, rows, n * HG_DIM:(n + 1) * HG_DIM] = jnp.log2(f)
        hg_ref[0, rows, 3 * HG_DIM:4 * HG_DIM] = proj(hg_col0 + 3)
        if full:
            uf_ref, gate_ref = o_refs[1], o_refs[2]
            uf_ref[0, rows, :] = proj(COL_FNET).astype(BF16)
            gate_ref[0, rows, 0:2 * D_MODEL] = _sigmoid(proj(COL_G + 1, 2 * D_MODEL)).astype(BF16)
            gate_ref[0, rows, 2 * D_MODEL:2 * D_MODEL + HG_DIM] = _silu(proj(COL_G)).astype(BF16)


def _inproj(x, mod, g, lb, w, *, tm, full):
    b, l, d = x.shape
    n = w.shape[1]
    mod_map = (lambda i, j: (i, 0, 0)) if full else (lambda i, j: (0, 0, 0))
    tok = lambda width, dtype: (pl.BlockSpec((1, tm, width), lambda i, j: (i, j, 0)),
                                jax.ShapeDtypeStruct((b, l, width), dtype))
    outs = [tok(4 * HG_DIM, F32)]
    if full:
        outs += [tok(FNET_DIM, BF16), tok(HG_DIM + 2 * D_MODEL, BF16)]
    return pl.pallas_call(
        functools.partial(_inproj_kernel, hg_col0=COL_Q if full else 0, full=full),
        grid=(b, l // tm),
        in_specs=[pl.BlockSpec((1, tm, d), lambda i, j: (i, j, 0)),
                  pl.BlockSpec((1, N_MOD, d), mod_map),
                  pl.BlockSpec((1, d), lambda i, j: (0, 0)),
                  pl.BlockSpec((1, HG_DIM), lambda i, j: (0, 0)),
                  pl.BlockSpec((d, n), lambda i, j: (0, 0))],
        out_specs=tuple(o[0] for o in outs),
        out_shape=tuple(o[1] for o in outs),
        compiler_params=_params("arbitrary", "arbitrary"),
        name="inproj_x" if full else "inproj_ctx",
    )(x, mod, g, lb, w)


def _hgrn_step(dirs, st_ref):
    c = HG_CHUNK
    hd = HG_HEAD_DIM
    r_idx = lax.broadcasted_iota(jnp.int32, (c, c), 0)
    c_idx = lax.broadcasted_iota(jnp.int32, (c, c), 1)
    pairs = [slice(p * 2 * hd, (p + 1) * 2 * hd) for p in range(HG_HEADS // 2)]

    def block_diag(a0, a1):
        z0, z1 = jnp.zeros(a0.shape, a0.dtype), jnp.zeros(a1.shape, a1.dtype)
        return jnp.concatenate([jnp.concatenate([a0, z0], axis=1),
                                jnp.concatenate([z1, a1], axis=1)], axis=0)

    cums = []
    for q, l2f, v, reverse, o_ref, row0, base in dirs:
        tri = jnp.where((c_idx >= r_idx) if reverse else (c_idx <= r_idx), 1.0, 0.0).astype(BF16)
        hi = l2f.astype(BF16)
        lo = (l2f - hi.astype(F32)).astype(BF16)
        cums.append(_dot(tri, hi) + _dot(tri, lo))

    state_in, state_upd, intra = [], [], []
    for (q, l2f, v, reverse, o_ref, row0, base), cum in zip(dirs, cums):
        k = 1.0 - jnp.exp2(l2f)
        last = 0 if reverse else c - 1
        total = cum[last:last + 1, :]
        q_in = (q * jnp.exp2(cum)).astype(BF16)
        k_out = (k * jnp.exp2(total - cum)).astype(BF16)
        e_tot = jnp.exp2(total)
        v_b = v.astype(BF16)
        for p, ps in enumerate(pairs):
            h0, h1 = slice(ps.start, ps.start + hd), slice(ps.start + hd, ps.stop)
            v_t = jnp.concatenate([v[:, h0], v[:, h1]], axis=0).T.astype(BF16)
            state_in.append(q_in[:, ps])
            state_upd.append((base + p, e_tot[:, ps], v_t, block_diag(k_out[:, h0], k_out[:, h1])))
            if o_ref is None:
                continue
            for i in range(c // HG_SUB):
                r0 = i * HG_SUB
                rows = slice(r0, r0 + HG_SUB)
                cols = slice(r0, c) if reverse else slice(0, r0 + HG_SUB)
                mid = r0 + HG_SUB // 2 if reverse else r0 + HG_SUB // 2 - 1
                m = cum[mid:mid + 1, ps]
                q_t = (q[rows, ps] * jnp.exp2(cum[rows, ps] - m)).astype(BF16)
                k_t = (k[cols, ps] * jnp.exp2(m - cum[cols, ps])).astype(BF16)
                intra.append((o_ref, row0, rows, cols, ps, reverse, len(state_in) - 1, q_t,
                              block_diag(k_t[:, :hd], k_t[:, hd:]),
                              block_diag(v_b[cols, h0], v_b[cols, h1])))

    scores = [_dot_nt(item[7], item[8]) for item in intra]
    kv = [_dot(v_t, k_bd) for _, _, v_t, k_bd in state_upd]
    states = {}
    inter = []
    for n, (idx, e_tot_p, _, _) in enumerate(state_upd):
        st = states[idx] if idx in states else st_ref[idx]
        if intra:
            st_b = st.astype(BF16)
            inter.append(_dot_nt(state_in[n], block_diag(st_b[:, :hd], st_b[:, hd:])))
        states[idx] = st * e_tot_p + kv[n]
    for idx, st in states.items():
        st_ref[idx] = st

    for (o_ref, row0, rows, cols, ps, reverse, chain, _, _, v_bd), a in zip(intra, scores):
        n_cols = cols.stop - cols.start
        row_g = rows.start + lax.broadcasted_iota(jnp.int32, (HG_SUB, 2 * n_cols), 0)
        lane = lax.broadcasted_iota(jnp.int32, (HG_SUB, 2 * n_cols), 1)
        col_g = cols.start + jnp.where(lane >= n_cols, lane - n_cols, lane)
        a = jnp.where((col_g >= row_g) if reverse else (col_g <= row_g), a, 0.0)
        o_ref[0, pl.ds(row0 + rows.start, HG_SUB), ps] = _dot(a.astype(BF16), v_bd) + inter[chain][rows]


def _hgrn_kernel(*refs, has_s0, emit_o):
    it = iter(refs)
    fwd_refs = [next(it) for _ in range(3)]
    bwd_refs = [next(it) for _ in range(3)]
    qf_ref = fwd_refs[0]
    s0_ref = next(it) if has_s0 else None
    of_ref = next(it) if emit_o else None
    ob_ref = next(it) if emit_o else None
    sout_ref = None if emit_o else next(it)
    st_ref = next(it)
    j = pl.program_id(1)
    n_inner = qf_ref.shape[1] // HG_CHUNK

    @pl.when(j == 0)
    def _():
        if has_s0:
            st_ref[...] = s0_ref[0]
        else:
            st_ref[...] = jnp.zeros_like(st_ref)

    span = HG_UNROLL * HG_CHUNK

    def body(it, carry):
        dirs = []
        for jj in range(HG_UNROLL):
            rf = pl.multiple_of(it * span, span) + jj * HG_CHUNK
            rb = pl.multiple_of((n_inner // HG_UNROLL - 1 - it) * span, span) + (HG_UNROLL - 1 - jj) * HG_CHUNK
            sf, sb = pl.ds(rf, HG_CHUNK), pl.ds(rb, HG_CHUNK)
            dirs.append(tuple(r[0, sf, :] for r in fwd_refs) + (False, of_ref, rf, 0))
            dirs.append(tuple(r[0, sb, :] for r in bwd_refs) + (True, ob_ref, rb, HG_HEADS // 2))
        _hgrn_step(dirs, st_ref)
        return carry

    lax.fori_loop(0, n_inner // HG_UNROLL, body, 0)

    if not emit_o:
        @pl.when(j == pl.num_programs(1) - 1)
        def _():
            sout_ref[0] = st_ref[...]


def _hgrn(u, s0, *, emit_o):
    b, l, _ = u.shape
    c = min(HG_BLOCK, l)
    n = l // c
    blk = (1, c, HG_DIM)
    fwd = lambda col: pl.BlockSpec(blk, lambda i, j: (i, j, col))
    bwd = lambda col: pl.BlockSpec(blk, lambda i, j: (i, n - 1 - j, col))
    st_dims = (HG_HEADS, HG_HEAD_DIM, 2 * HG_HEAD_DIM)
    state_spec = pl.BlockSpec((1,) + st_dims, lambda i, j: (i, 0, 0, 0))
    state_shape = jax.ShapeDtypeStruct((b,) + st_dims, F32)
    in_specs = [fwd(0), fwd(1), fwd(3), bwd(0), bwd(2), bwd(3)]
    args = [u, u, u, u, u, u]
    if s0 is not None:
        in_specs.append(state_spec)
        args.append(s0)
    if emit_o:
        o_shape = jax.ShapeDtypeStruct((b, l, HG_DIM), F32)
        out_shape = (o_shape, o_shape)
        out_specs = (pl.BlockSpec(blk, lambda i, j: (i, j, 0)),
                     pl.BlockSpec(blk, lambda i, j: (i, n - 1 - j, 0)))
    else:
        out_shape = state_shape
        out_specs = state_spec
    return pl.pallas_call(
        functools.partial(_hgrn_kernel, has_s0=s0 is not None, emit_o=emit_o),
        grid=(b, n),
        in_specs=in_specs,
        out_specs=out_specs,
        out_shape=out_shape,
        scratch_shapes=[pltpu.VMEM(st_dims, F32)],
        compiler_params=_params("arbitrary", "arbitrary"),
        name="hgrn_x" if emit_o else "hgrn_ctx",
    )(*args)


FNET_PAD = 16


def _dft_tables(seq):
    gd = FNET_GROUP_DIM
    kc = (np.arange(gd)[:, None] * np.arange(gd)[None, :]) % gd
    ang_c = 2.0 * np.pi * kc / gd
    scale = 1.0 / np.sqrt(float(seq) * gd)
    chan = np.concatenate([np.cos(ang_c), np.sin(ang_c)], axis=1) * scale
    half = seq // 2
    kl = (np.arange(half + FNET_PAD)[:, None] * np.arange(half)[None, :]) % seq
    ang_l = 2.0 * np.pi * kl / seq
    keep = (np.arange(half + FNET_PAD) <= half)[:, None]
    pos_c = np.where(keep, np.cos(ang_l), 0.0)
    pos_s = np.where(keep, -np.sin(ang_l), 0.0)
    return chan.astype(np.float32), pos_c.astype(np.float32), pos_s.astype(np.float32)


def _mirror_gather(width):
    sub = lax.broadcasted_iota(jnp.int32, (8, width), 0)
    return jnp.where(sub == 0, 0, 8 - sub), sub


def _fold_positions(v, sign):
    seq = v.shape[0]
    n_tiles = seq // 8
    gather, sub = _mirror_gather(v.shape[1])
    tile = lambda k: v[8 * k:8 * k + 8, :]
    out = []
    for k in range(n_tiles // 2):
        tail = jnp.take_along_axis(tile(n_tiles - 1 - k), gather, axis=0)
        if k == 0:
            mirror = jnp.where(sub == 0, 0.0, tail)
        else:
            mirror = jnp.where(sub == 0, jnp.take_along_axis(tile(n_tiles - k), gather, axis=0), tail)
        out.append(tile(k) + mirror if sign > 0 else tile(k) - mirror)
    return jnp.concatenate(out, axis=0)


def _fnet_kernel(u_ref, chan_ref, posc_ref, poss_ref, y_ref, pq_ref):
    seq = u_ref.shape[1]
    half = seq // 2
    gd = FNET_GROUP_DIM
    mids = []
    for g in range(FNET_GROUPS):
        gs = slice(g * gd, (g + 1) * gd)
        pq = _dot(u_ref[0, :, gs], chan_ref[...])
        pq_ref[0:half, gs] = _fold_positions(pq[:, :gd], 1).astype(BF16)
        pq_ref[half:seq, gs] = _fold_positions(pq[:, gd:], -1).astype(BF16)
        mids.append(pq[half:half + 1, :gd])
    gather, sub = _mirror_gather(FNET_DIM)
    mid = jnp.concatenate(mids, axis=1)
    mid_alt = jnp.where(sub % 2 == 0, mid, -mid)

    a = _dot(posc_ref[...], pq_ref[0:half, :])
    bm = _dot(poss_ref[...], pq_ref[half:seq, :])
    tile = lambda v, k: v[8 * k:8 * k + 8, :]
    n_t = half // 8
    for k in range(0, n_t, 2):
        pair = [tile(a, k + i) + tile(bm, k + i) + mid_alt for i in range(2)]
        y_ref[0, 8 * k:8 * k + 16, :] = jnp.concatenate(pair, axis=0).astype(BF16)
    flipped = [jnp.take_along_axis(tile(a, k) - tile(bm, k) + mid_alt, gather, axis=0) for k in range(n_t)]
    flipped.append(jnp.take_along_axis(tile(a, n_t) + tile(bm, n_t) + mid_alt, gather, axis=0))
    n_all = seq // 8
    for m in range(n_t, n_all, 2):
        pair = [jnp.where(sub == 0, flipped[n_all - mm], flipped[n_all - 1 - mm]) for mm in (m, m + 1)]
        y_ref[0, 8 * m:8 * m + 16, :] = jnp.concatenate(pair, axis=0).astype(BF16)


def _fnet(u, chan, pos_c, pos_s):
    b, seq, _ = u.shape
    table = pl.BlockSpec(pos_c.shape, lambda i: (0, 0))
    return pl.pallas_call(
        _fnet_kernel,
        grid=(b,),
        in_specs=[pl.BlockSpec((1, seq, FNET_DIM), lambda i: (i, 0, 0)),
                  pl.BlockSpec((FNET_GROUP_DIM, 2 * FNET_GROUP_DIM), lambda i: (0, 0)),
                  table, table],
        out_specs=pl.BlockSpec((1, seq, FNET_DIM), lambda i: (i, 0, 0)),
        out_shape=jax.ShapeDtypeStruct((b, seq, FNET_DIM), BF16),
        scratch_shapes=[pltpu.VMEM((seq, FNET_DIM), BF16)],
        compiler_params=_params("arbitrary"),
        name="fnet",
    )(u, chan, pos_c, pos_s)


def _merge_kernel(x_ref, of_ref, ob_ref, sg_ref, ga_ref, gb_ref, yf_ref, mod_ref, og_ref, n2g_ref,
                  wa_ref, wb_ref, wo_ref, x1_ref, h2_ref):
    n_sub = x_ref.shape[1] // MERGE_SUB

    def mix(rows):
        o = of_ref[0, rows, :] + ob_ref[0, rows, :]
        parts = []
        for h in range(HG_HEADS):
            oh = o[:, h * HG_HEAD_DIM:(h + 1) * HG_HEAD_DIM]
            parts.append(oh * lax.rsqrt(jnp.mean(oh * oh, axis=-1, keepdims=True) + EPS))
        on = jnp.concatenate(parts, axis=-1) * og_ref[...] * sg_ref[0, rows, :].astype(F32)
        y_b = _dot(on.astype(BF16), wb_ref[...])
        y_a = _dot(yf_ref[0, rows, :], wa_ref[...])
        m = ga_ref[0, rows, :].astype(F32) * y_a + gb_ref[0, rows, :].astype(F32) * y_b
        return _dot(m.astype(BF16), wo_ref[...])

    def finish(rows, yx):
        x1 = x_ref[0, rows, :] + mod_ref[0, 2:3, :] * yx
        x1_ref[0, rows, :] = x1
        h2 = _modulated_norm(x1, n2g_ref[...], mod_ref[0, 3:4, :], mod_ref[0, 4:5, :])
        h2_ref[0, rows, :] = h2.astype(BF16)

    pending = None
    for i in range(n_sub):
        rows = slice(i * MERGE_SUB, (i + 1) * MERGE_SUB)
        yx = mix(rows)
        if pending is not None:
            finish(*pending)
        pending = (rows, yx)
    finish(*pending)


def _merge(x, o_f, o_b, gates, y_f, mod, og, n2g, w_a, w_b, w_out, *, tm):
    b, l, d = x.shape
    tok = lambda w, col: pl.BlockSpec((1, tm, w), lambda i, j: (i, j, col))
    const = lambda shape: pl.BlockSpec(shape, lambda i, j: tuple(0 for _ in shape))
    return pl.pallas_call(
        _merge_kernel,
        grid=(b, l // tm),
        in_specs=[tok(d, 0), tok(HG_DIM, 0), tok(HG_DIM, 0), tok(HG_DIM, 2 * d // HG_DIM),
                  tok(d, 0), tok(d, 1), tok(FNET_DIM, 0),
                  pl.BlockSpec((1, N_MOD, d), lambda i, j: (i, 0, 0)),
                  const((1, HG_DIM)), const((1, d)),
                  const((FNET_DIM, d)), const((HG_DIM, d)), const((d, d))],
        out_specs=(tok(d, 0), tok(d, 0)),
        out_shape=(jax.ShapeDtypeStruct((b, l, d), F32), jax.ShapeDtypeStruct((b, l, d), BF16)),
        compiler_params=_params("arbitrary", "arbitrary"),
        name="merge",
    )(x, o_f, o_b, gates, gates, gates, y_f, mod, og, n2g, w_a, w_b, w_out)


FFN_GROUP = 8


def _ffn_permute_tokens(h_ref, hp_ref, tmp_ref):
    n_tok = tmp_ref.shape[1]

    def body(g, carry):
        tok = pl.ds(pl.multiple_of(g * n_tok, n_tok), n_tok)
        hf = h_ref[0, tok, :].astype(F32)
        for lt in range(tmp_ref.shape[0]):
            for r in range(n_tok // GRID_W):
                for a in range(8):
                    r0 = r * GRID_W + 8 * a
                    tmp_ref[lt, pl.ds(r * GRID_W + a, 8, stride=8), :] = (
                        hf[r0:r0 + 8, lt * 128:(lt + 1) * 128])
        for lt in range(tmp_ref.shape[0]):
            hp_ref[tok, lt * 128:(lt + 1) * 128] = tmp_ref[lt].astype(BF16)
        return carry

    lax.fori_loop(0, h_ref.shape[1] // n_tok, body, 0)


def _ffn_up(hp_ref, up_refs, row0, n_rows):
    h = hp_ref[row0 * GRID_W:(row0 + n_rows) * GRID_W, :]
    return [_dot(h, up_ref[...]) for up_ref in up_refs]


def _pair(lo, hi):
    return jnp.concatenate([lo, hi], axis=0).astype(BF16)


def _ffn_row(z, r, taps, bias, first, last, sub):
    lo = [z[r * GRID_W + 8 * b:r * GRID_W + 8 * b + 8, 0:128] for b in range(8)]
    hi = [z[r * GRID_W + 8 * b:r * GRID_W + 8 * b + 8, 128:256] for b in range(8)]
    mid = [_pair(lo[b], hi[b]) for b in range(8)]
    down = lambda v: jnp.where(sub == 0, 0.0, pltpu.roll(v, 1, axis=0))
    left = [_pair(down(lo[7]), down(hi[7]))] + mid[0:7]
    up = lambda v: jnp.where(sub == 7, 0.0, pltpu.roll(v, 7, axis=0))
    right = mid[1:8] + [_pair(up(lo[0]), up(hi[0]))]
    term = lambda kh: [left[b] * taps[3 * kh] + mid[b] * taps[3 * kh + 1] + right[b] * taps[3 * kh + 2]
                       for b in range(8)]
    above = None if first else term(2)
    here = term(1)
    below = None if last else [t + bias for t in term(0)]
    return above, here, below


def _ffn_down(a_ref, dn_ref, y_ref, row0, n_rows):
    tok = slice(row0 * GRID_W, (row0 + n_rows) * GRID_W)
    part = _dot(a_ref[tok, :].astype(BF16), dn_ref[...])
    for j in range(y_ref.shape[1]):
        y_ref[0, j, tok, :] += part[:, j * 128:(j + 1) * 128]


def _ffn_kernel(h_ref, up1_ref, up2_ref, cw1_ref, cw2_ref, cb1_ref, cb2_ref, dn_ref, y_ref,
                hp_ref, a_ref, wp_ref, tmp_ref, wup_ref, wdn_ref):
    t = pl.program_id(1)
    rows = hp_ref.shape[0] // GRID_W
    grp = FFN_GROUP

    @pl.when(t == 0)
    def _():
        _ffn_permute_tokens(h_ref, hp_ref, tmp_ref)
        y_ref[...] = jnp.zeros_like(y_ref)

    wup_ref[0] = up1_ref[...].astype(BF16)
    wup_ref[1] = up2_ref[...].astype(BF16)
    wdn_ref[...] = dn_ref[...].astype(BF16)
    up_refs = (wup_ref.at[0], wup_ref.at[1])

    for half, (cw_ref, cb_ref) in enumerate(((cw1_ref, cb1_ref), (cw2_ref, cb2_ref))):
        for k in range(10):
            row = cw_ref[k:k + 1, :] if k < 9 else cb_ref[...]
            wp_ref[half, k] = _pair(jnp.broadcast_to(row[:, 0:128], (8, 128)),
                                    jnp.broadcast_to(row[:, 128:256], (8, 128)))
    taps = [[wp_ref[half, k] for k in range(9)] for half in range(2)]
    bias = [wp_ref[half, 9] for half in range(2)]
    sub = lax.broadcasted_iota(jnp.int32, (8, 128), 0)

    def gate(c, out_row):
        for b in range(8):
            act = (_silu(c[0][b]) * c[1][b]).astype(F32)
            tok_b = slice(out_row * GRID_W + 8 * b, out_row * GRID_W + 8 * b + 8)
            a_ref[tok_b, 0:128] = act[0:8]
            a_ref[tok_b, 128:256] = act[8:16]

    prev = [None, None]
    cur = [[bias[half]] * 8 for half in range(2)]
    z_next = _ffn_up(hp_ref, up_refs, 0, grp)
    for row0 in range(0, rows, grp):
        z_cur = z_next
        if row0 + grp < rows:
            z_next = _ffn_up(hp_ref, up_refs, row0 + grp, grp)
        for rr in range(grp):
            r = row0 + rr
            done = []
            for half in range(2):
                above, here, below = _ffn_row(z_cur[half], rr, taps[half], bias[half],
                                              r == 0, r == rows - 1, sub)
                if above is not None:
                    done.append([prev[half][b] + above[b] for b in range(8)])
                prev[half] = [cur[half][b] + here[b] for b in range(8)]
                cur[half] = below
            if done:
                gate(done, r - 1)
            if rr == 0 and row0 > 0:
                _ffn_down(a_ref, wdn_ref, y_ref, row0 - grp, grp)
    gate(prev, rows - 1)
    _ffn_down(a_ref, wdn_ref, y_ref, rows - grp, grp)


def _ffn(h2, up, conv_w, conv_b, down):
    b, l, d = h2.shape
    rows = l // GRID_W
    tc = FFN_TILE
    nt = D_FF // tc
    assert rows % FFN_GROUP == 0 and tc == 256, "the conv pairs the two lane tiles of a channel tile"
    return pl.pallas_call(
        _ffn_kernel,
        grid=(b, nt),
        in_specs=[pl.BlockSpec((1, l, d), lambda i, t: (i, 0, 0)),
                  pl.BlockSpec((d, tc), lambda i, t: (0, t)),
                  pl.BlockSpec((d, tc), lambda i, t: (0, nt + t)),
                  pl.BlockSpec((9, tc), lambda i, t: (0, t)),
                  pl.BlockSpec((9, tc), lambda i, t: (0, nt + t)),
                  pl.BlockSpec((1, tc), lambda i, t: (0, t)),
                  pl.BlockSpec((1, tc), lambda i, t: (0, nt + t)),
                  pl.BlockSpec((tc, d), lambda i, t: (t, 0))],
        out_specs=pl.BlockSpec((1, d // 128, l, 128), lambda i, t: (i, 0, 0, 0)),
        out_shape=jax.ShapeDtypeStruct((b, d // 128, l, 128), F32),
        scratch_shapes=[pltpu.VMEM((l, d), BF16),
                        pltpu.VMEM((l, tc), F32),
                        pltpu.VMEM((2, 10, 16, 128), BF16),
                        pltpu.VMEM((d // 128, FFN_GROUP * GRID_W, 128), F32),
                        pltpu.VMEM((2, d, tc), BF16),
                        pltpu.VMEM((tc, d), BF16)],
        compiler_params=_params("arbitrary", "arbitrary"),
        name="ffn",
    )(h2, up, up, conv_w, conv_w, conv_b, conv_b, down)


def _final_kernel(x1_ref, y_ref, mod_ref, g_ref, o_ref):
    n_slab = y_ref.shape[1]
    for r in range(y_ref.shape[2] // GRID_W):
        for a in range(8):
            tok = slice(r * GRID_W + 8 * a, r * GRID_W + 8 * a + 8)
            y = jnp.concatenate([y_ref[0, j, pl.ds(r * GRID_W + a, 8, stride=8), :]
                                 for j in range(n_slab)], axis=-1)
            x = x1_ref[0, tok, :] + mod_ref[0, 5:6, :] * y
            ms = jnp.mean(x * x, axis=-1, keepdims=True)
            o_ref[0, tok, :] = x * lax.rsqrt(ms + EPS) * g_ref[...]


def _final(x1, y, mod, g, *, tm):
    b, l, d = x1.shape
    tok = pl.BlockSpec((1, tm, d), lambda i, j: (i, j, 0))
    return pl.pallas_call(
        _final_kernel,
        grid=(b, l // tm),
        in_specs=[tok, pl.BlockSpec((1, d // 128, tm, 128), lambda i, j: (i, 0, j, 0)),
                  pl.BlockSpec((1, N_MOD, d), lambda i, j: (i, 0, 0)),
                  pl.BlockSpec((1, d), lambda i, j: (0, 0))],
        out_specs=tok,
        out_shape=jax.ShapeDtypeStruct((b, l, d), F32),
        compiler_params=_params("arbitrary", "arbitrary"),
        name="final",
    )(x1, y, mod, g)


def kernel(x, c, ctx, c_ctx, ada_w, ada_b, norm1_g, w_in, hg_lb, hg_onorm_g, w_a, w_b, w_out,
           norm2_g, ffn_up, ffn_conv_w, ffn_conv_b, ffn_down, final_g):
    b, seq, d = x.shape
    assert ada_w.shape[0] == 1, "single-layer kernel"
    assert (d, seq % GRID_W) == (D_MODEL, 0)

    lb = jnp.cumsum(jax.nn.softmax(hg_lb.astype(F32), axis=0), axis=0)[0].reshape(1, HG_DIM)

    n_rows = 16
    rows = jnp.concatenate([c, c_ctx[None, :], jnp.zeros((n_rows - b - 1, d), F32)], axis=0)
    mod = _ada(rows, ada_w[0], ada_b[0])
    mod_x = mod[:b].reshape(b, N_MOD, d)
    mod_c = mod[b:b + 1].reshape(1, N_MOD, d)

    w_in_b = w_in[0].astype(BF16)
    g1 = norm1_g[0].reshape(1, d)
    hg_x, u_f, gates = _inproj(x, mod_x, g1, lb, w_in_b, tm=1024, full=True)
    w_ctx = w_in_b[:, COL_Q * HG_DIM:(COL_I + 1) * HG_DIM]
    (hg_c,) = _inproj(ctx, mod_c, g1, lb, w_ctx, tm=256, full=False)

    s_ctx = _hgrn(hg_c, None, emit_o=False)
    o_f, o_b = _hgrn(hg_x, s_ctx, emit_o=True)

    chan, pos_c, pos_s = (jnp.asarray(t).astype(BF16) for t in _dft_tables(seq))
    y_f = _fnet(u_f, chan, pos_c, pos_s)

    x1, h2 = _merge(x, o_f, o_b, gates, y_f, mod_x, hg_onorm_g[0].reshape(1, HG_DIM),
                    norm2_g[0].reshape(1, d), w_a[0].astype(BF16), w_b[0].astype(BF16),
                    w_out[0].astype(BF16), tm=1024)

    y = _ffn(h2, ffn_up[0], ffn_conv_w[0].reshape(9, 2 * D_FF),
             ffn_conv_b[0].reshape(1, 2 * D_FF), ffn_down[0])
    return _final(x1, y, mod_x, final_g.reshape(1, d), tm=512)
```

```python
import functools

import numpy as np
import jax
import jax.numpy as jnp
from jax import lax
from jax.experimental import pallas as pl
from jax.experimental.pallas import tpu as pltpu

D_MODEL = 1024
GRID_W = 64
FNET_DIM = 512
FNET_GROUPS = 4
FNET_GROUP_DIM = FNET_DIM // FNET_GROUPS
HG_DIM = 512
HG_HEADS = 4
HG_HEAD_DIM = HG_DIM // HG_HEADS
D_FF = 2816
N_MOD = 6
EPS = 1e-6

COL_FNET, COL_Q, COL_FF, COL_FB, COL_I, COL_G = 0, 1, 2, 3, 4, 5

INPROJ_SUB = 256
MERGE_SUB = 256
HG_BLOCK = 1024
HG_CHUNK = 64
HG_UNROLL = 2
HG_SUB = 16
FFN_TILE = 256

BF16 = jnp.bfloat16
F32 = jnp.float32
VMEM_LIMIT = 56 * 1024 * 1024


def _sigmoid(x):
    return 1.0 / (1.0 + jnp.exp(-x))


def _silu(x):
    return x * _sigmoid(x)


def _dot(a, b):
    return jnp.dot(a, b, preferred_element_type=F32)


def _dot_nt(a, b):
    return lax.dot_general(a, b, (((1,), (1,)), ((), ())), preferred_element_type=F32)


def _params(*sem, flags=None):
    return pltpu.CompilerParams(dimension_semantics=sem, vmem_limit_bytes=VMEM_LIMIT, flags=flags)


def _ada_kernel(c_ref, w_ref, b_ref, o_ref):
    a = _silu(c_ref[...])
    o_ref[...] = _dot(a.astype(BF16), w_ref[...].astype(BF16)) + b_ref[...]


def _ada(rows, ada_w, ada_b):
    m, d = rows.shape
    n = ada_w.shape[1]
    tn = 1536
    return pl.pallas_call(
        _ada_kernel,
        grid=(n // tn,),
        in_specs=[pl.BlockSpec((m, d), lambda j: (0, 0)),
                  pl.BlockSpec((d, tn), lambda j: (0, j)),
                  pl.BlockSpec((1, tn), lambda j: (0, j))],
        out_specs=pl.BlockSpec((m, tn), lambda j: (0, j)),
        out_shape=jax.ShapeDtypeStruct((m, n), F32),
        compiler_params=_params("arbitrary"),
        name="ada",
    )(rows, ada_w, ada_b.reshape(1, n))


def _modulated_norm(x, g, shift, scale):
    ms = jnp.mean(x * x, axis=-1, keepdims=True)
    return (x * lax.rsqrt(ms + EPS) * g) * (1.0 + scale) + shift


def _inproj_kernel(x_ref, mod_ref, g_ref, lb_ref, w_ref, *o_refs, hg_col0, full):
    lb = lb_ref[...]
    hg_ref = o_refs[0]
    n_sub = x_ref.shape[1] // INPROJ_SUB

    def normed(i):
        rows = slice(i * INPROJ_SUB, (i + 1) * INPROJ_SUB)
        return _modulated_norm(x_ref[0, rows, :], g_ref[...], mod_ref[0, 0:1, :],
                               mod_ref[0, 1:2, :]).astype(BF16)

    hb_next = normed(0)
    for i in range(n_sub):
        hb = hb_next
        rows = slice(i * INPROJ_SUB, (i + 1) * INPROJ_SUB)

        def proj(col, width=HG_DIM):
            return _dot(hb, w_ref[:, col * HG_DIM:col * HG_DIM + width])

        hg_ref[0, rows, 0:HG_DIM] = _silu(proj(hg_col0))
        if i + 1 < n_sub:
            hb_next = normed(i + 1)
        for n in (1, 2):
            f = lb + (1.0 - lb) * _sigmoid(proj(hg_col0 + n))
            hg_ref[0, rows, n * HG_DIM:(n + 1) * HG_DIM] = jnp.log2(f)
        hg_ref[0, rows, 3 * HG_DIM:4 * HG_DIM] = proj(hg_col0 + 3)
        if full:
            uf_ref, gate_ref = o_refs[1], o_refs[2]
            uf_ref[0, rows, :] = proj(COL_FNET).astype(BF16)
            gate_ref[0, rows, 0:2 * D_MODEL] = _sigmoid(proj(COL_G + 1, 2 * D_MODEL)).astype(BF16)
            gate_ref[0, rows, 2 * D_MODEL:2 * D_MODEL + HG_DIM] = _silu(proj(COL_G)).astype(BF16)


def _inproj(x, mod, g, lb, w, *, tm, full):
    b, l, d = x.shape
    n = w.shape[1] if full else (COL_I + 1) * HG_DIM
    mod_map = (lambda i, j: (i, 0, 0)) if full else (lambda i, j: (0, 0, 0))
    tok = lambda width, dtype: (pl.BlockSpec((1, tm, width), lambda i, j: (i, j, 0)),
                                jax.ShapeDtypeStruct((b, l, width), dtype))
    outs = [tok(4 * HG_DIM, F32)]
    if full:
        outs += [tok(FNET_DIM, BF16), tok(HG_DIM + 2 * D_MODEL, BF16)]
    return pl.pallas_call(
        functools.partial(_inproj_kernel, hg_col0=COL_Q, full=full),
        grid=(b, l // tm),
        in_specs=[pl.BlockSpec((1, tm, d), lambda i, j: (i, j, 0)),
                  pl.BlockSpec((1, N_MOD, d), mod_map),
                  pl.BlockSpec((1, d), lambda i, j: (0, 0)),
                  pl.BlockSpec((1, HG_DIM), lambda i, j: (0, 0)),
                  pl.BlockSpec((d, n), lambda i, j: (0, 0))],
        out_specs=tuple(o[0] for o in outs),
        out_shape=tuple(o[1] for o in outs),
        compiler_params=_params("arbitrary", "arbitrary"),
        name="inproj_x" if full else "inproj_ctx",
    )(x, mod, g, lb, w)


def _hgrn_step(dirs, st_ref):
    c = HG_CHUNK
    hd = HG_HEAD_DIM
    r_idx = lax.broadcasted_iota(jnp.int32, (c, c), 0)
    c_idx = lax.broadcasted_iota(jnp.int32, (c, c), 1)
    pairs = [slice(p * 2 * hd, (p + 1) * 2 * hd) for p in range(HG_HEADS // 2)]

    def block_diag(a0, a1):
        z0, z1 = jnp.zeros(a0.shape, a0.dtype), jnp.zeros(a1.shape, a1.dtype)
        return jnp.concatenate([jnp.concatenate([a0, z0], axis=1),
                                jnp.concatenate([z1, a1], axis=1)], axis=0)

    cums = []
    for q, l2f, v, reverse, o_ref, row0, base in dirs:
        tri = jnp.where((c_idx >= r_idx) if reverse else (c_idx <= r_idx), 1.0, 0.0).astype(BF16)
        hi = l2f.astype(BF16)
        lo = (l2f - hi.astype(F32)).astype(BF16)
        cums.append(_dot(tri, hi) + _dot(tri, lo))

    state_in, state_upd, intra = [], [], []
    for (q, l2f, v, reverse, o_ref, row0, base), cum in zip(dirs, cums):
        k = 1.0 - jnp.exp2(l2f)
        last = 0 if reverse else c - 1
        total = cum[last:last + 1, :]
        q_in = (q * jnp.exp2(cum)).astype(BF16)
        k_out = (k * jnp.exp2(total - cum)).astype(BF16)
        e_tot = jnp.exp2(total)
        v_b = v.astype(BF16)
        for p, ps in enumerate(pairs):
            h0, h1 = slice(ps.start, ps.start + hd), slice(ps.start + hd, ps.stop)
            v_t = jnp.concatenate([v[:, h0], v[:, h1]], axis=0).T.astype(BF16)
            state_in.append(q_in[:, ps])
            state_upd.append((base + p, e_tot[:, ps], v_t, block_diag(k_out[:, h0], k_out[:, h1])))
            if o_ref is None:
                continue
            for i in range(c // HG_SUB):
                r0 = i * HG_SUB
                rows = slice(r0, r0 + HG_SUB)
                cols = slice(r0, c) if reverse else slice(0, r0 + HG_SUB)
                mid = r0 + HG_SUB // 2 if reverse else r0 + HG_SUB // 2 - 1
                m = cum[mid:mid + 1, ps]
                q_t = (q[rows, ps] * jnp.exp2(cum[rows, ps] - m)).astype(BF16)
                k_t = (k[cols, ps] * jnp.exp2(m - cum[cols, ps])).astype(BF16)
                intra.append((o_ref, row0, rows, cols, ps, reverse, len(state_in) - 1, q_t,
                              block_diag(k_t[:, :hd], k_t[:, hd:]),
                              block_diag(v_b[cols, h0], v_b[cols, h1])))

    scores = [_dot_nt(item[7], item[8]) for item in intra]
    kv = [_dot(v_t, k_bd) for _, _, v_t, k_bd in state_upd]
    states = {}
    inter = []
    for n, (idx, e_tot_p, _, _) in enumerate(state_upd):
        st = states[idx] if idx in states else st_ref[idx]
        if intra:
            st_b = st.astype(BF16)
            inter.append(_dot_nt(state_in[n], block_diag(st_b[:, :hd], st_b[:, hd:])))
        states[idx] = st * e_tot_p + kv[n]
    for idx, st in states.items():
        st_ref[idx] = st

    for (o_ref, row0, rows, cols, ps, reverse, chain, _, _, v_bd), a in zip(intra, scores):
        n_cols = cols.stop - cols.start
        row_g = rows.start + lax.broadcasted_iota(jnp.int32, (HG_SUB, 2 * n_cols), 0)
        lane = lax.broadcasted_iota(jnp.int32, (HG_SUB, 2 * n_cols), 1)
        col_g = cols.start + jnp.where(lane >= n_cols, lane - n_cols, lane)
        a = jnp.where((col_g >= row_g) if reverse else (col_g <= row_g), a, 0.0)
        o_ref[0, pl.ds(row0 + rows.start, HG_SUB), ps] = _dot(a.astype(BF16), v_bd) + inter[chain][rows]


def _hgrn_kernel(*refs, has_s0, emit_o):
    it = iter(refs)
    fwd_refs = [next(it) for _ in range(3)]
    bwd_refs = [next(it) for _ in range(3)]
    qf_ref = fwd_refs[0]
    s0_ref = next(it) if has_s0 else None
    of_ref = next(it) if emit_o else None
    ob_ref = next(it) if emit_o else None
    sout_ref = None if emit_o else next(it)
    st_ref = next(it)
    j = pl.program_id(1)
    n_inner = qf_ref.shape[1] // HG_CHUNK

    @pl.when(j == 0)
    def _():
        if has_s0:
            st_ref[...] = s0_ref[0]
        else:
            st_ref[...] = jnp.zeros_like(st_ref)

    span = HG_UNROLL * HG_CHUNK

    def body(it, carry):
        dirs = []
        for jj in range(HG_UNROLL):
            rf = pl.multiple_of(it * span, span) + jj * HG_CHUNK
            rb = pl.multiple_of((n_inner // HG_UNROLL - 1 - it) * span, span) + (HG_UNROLL - 1 - jj) * HG_CHUNK
            sf, sb = pl.ds(rf, HG_CHUNK), pl.ds(rb, HG_CHUNK)
            dirs.append(tuple(r[0, sf, :] for r in fwd_refs) + (False, of_ref, rf, 0))
            dirs.append(tuple(r[0, sb, :] for r in bwd_refs) + (True, ob_ref, rb, HG_HEADS // 2))
        _hgrn_step(dirs, st_ref)
        return carry

    lax.fori_loop(0, n_inner // HG_UNROLL, body, 0)

    if not emit_o:
        @pl.when(j == pl.num_programs(1) - 1)
        def _():
            sout_ref[0] = st_ref[...]


def _hgrn(u, s0, *, emit_o):
    b, l, _ = u.shape
    c = min(HG_BLOCK, l)
    n = l // c
    blk = (1, c, HG_DIM)
    fwd = lambda col: pl.BlockSpec(blk, lambda i, j: (i, j, col))
    bwd = lambda col: pl.BlockSpec(blk, lambda i, j: (i, n - 1 - j, col))
    st_dims = (HG_HEADS, HG_HEAD_DIM, 2 * HG_HEAD_DIM)
    state_spec = pl.BlockSpec((1,) + st_dims, lambda i, j: (i, 0, 0, 0))
    state_shape = jax.ShapeDtypeStruct((b,) + st_dims, F32)
    in_specs = [fwd(0), fwd(1), fwd(3), bwd(0), bwd(2), bwd(3)]
    args = [u, u, u, u, u, u]
    if s0 is not None:
        in_specs.append(state_spec)
        args.append(s0)
    if emit_o:
        o_shape = jax.ShapeDtypeStruct((b, l, HG_DIM), F32)
        out_shape = (o_shape, o_shape)
        out_specs = (pl.BlockSpec(blk, lambda i, j: (i, j, 0)),
                     pl.BlockSpec(blk, lambda i, j: (i, n - 1 - j, 0)))
    else:
        out_shape = state_shape
        out_specs = state_spec
    return pl.pallas_call(
        functools.partial(_hgrn_kernel, has_s0=s0 is not None, emit_o=emit_o),
        grid=(b, n),
        in_specs=in_specs,
        out_specs=out_specs,
        out_shape=out_shape,
        scratch_shapes=[pltpu.VMEM(st_dims, F32)],
        compiler_params=_params("arbitrary", "arbitrary"),
        name="hgrn_x" if emit_o else "hgrn_ctx",
    )(*args)


FNET_PAD = 16


def _dft_tables(seq):
    gd = FNET_GROUP_DIM
    kc = (np.arange(gd)[:, None] * np.arange(gd)[None, :]) % gd
    ang_c = 2.0 * np.pi * kc / gd
    scale = 1.0 / np.sqrt(float(seq) * gd)
    chan = np.concatenate([np.cos(ang_c), np.sin(ang_c)], axis=1) * scale
    half = seq // 2
    kl = (np.arange(half + FNET_PAD)[:, None] * np.arange(half)[None, :]) % seq
    ang_l = 2.0 * np.pi * kl / seq
    keep = (np.arange(half + FNET_PAD) <= half)[:, None]
    pos_c = np.where(keep, np.cos(ang_l), 0.0)
    pos_s = np.where(keep, -np.sin(ang_l), 0.0)
    return chan.astype(np.float32), pos_c.astype(np.float32), pos_s.astype(np.float32)


def _mirror_gather(width):
    sub = lax.broadcasted_iota(jnp.int32, (8, width), 0)
    return jnp.where(sub == 0, 0, 8 - sub), sub


def _fold_positions(v, sign):
    seq = v.shape[0]
    n_tiles = seq // 8
    gather, sub = _mirror_gather(v.shape[1])
    tile = lambda k: v[8 * k:8 * k + 8, :]
    out = []
    for k in range(n_tiles // 2):
        tail = jnp.take_along_axis(tile(n_tiles - 1 - k), gather, axis=0)
        if k == 0:
            mirror = jnp.where(sub == 0, 0.0, tail)
        else:
            mirror = jnp.where(sub == 0, jnp.take_along_axis(tile(n_tiles - k), gather, axis=0), tail)
        out.append(tile(k) + mirror if sign > 0 else tile(k) - mirror)
    return jnp.concatenate(out, axis=0)


def _fnet_kernel(u_ref, chan_ref, posc_ref, poss_ref, y_ref, pq_ref):
    seq = u_ref.shape[1]
    half = seq // 2
    gd = FNET_GROUP_DIM
    mids = []
    for g in range(FNET_GROUPS):
        gs = slice(g * gd, (g + 1) * gd)
        pq = _dot(u_ref[0, :, gs], chan_ref[...])
        pq_ref[0:half, gs] = _fold_positions(pq[:, :gd], 1).astype(BF16)
        pq_ref[half:seq, gs] = _fold_positions(pq[:, gd:], -1).astype(BF16)
        mids.append(pq[half:half + 1, :gd])
    gather, sub = _mirror_gather(FNET_DIM)
    mid = jnp.concatenate(mids, axis=1)
    mid_alt = jnp.where(sub % 2 == 0, mid, -mid)

    a = _dot(posc_ref[...], pq_ref[0:half, :])
    bm = _dot(poss_ref[...], pq_ref[half:seq, :])
    tile = lambda v, k: v[8 * k:8 * k + 8, :]
    n_t = half // 8
    for k in range(0, n_t, 2):
        pair = [tile(a, k + i) + tile(bm, k + i) + mid_alt for i in range(2)]
        y_ref[0, 8 * k:8 * k + 16, :] = jnp.concatenate(pair, axis=0).astype(BF16)
    flipped = [jnp.take_along_axis(tile(a, k) - tile(bm, k) + mid_alt, gather, axis=0) for k in range(n_t)]
    flipped.append(jnp.take_along_axis(tile(a, n_t) + tile(bm, n_t) + mid_alt, gather, axis=0))
    n_all = seq // 8
    for m in range(n_t, n_all, 2):
        pair = [jnp.where(sub == 0, flipped[n_all - mm], flipped[n_all - 1 - mm]) for mm in (m, m + 1)]
        y_ref[0, 8 * m:8 * m + 16, :] = jnp.concatenate(pair, axis=0).astype(BF16)


def _fnet(u, chan, pos_c, pos_s):
    b, seq, _ = u.shape
    table = pl.BlockSpec(pos_c.shape, lambda i: (0, 0))
    return pl.pallas_call(
        _fnet_kernel,
        grid=(b,),
        in_specs=[pl.BlockSpec((1, seq, FNET_DIM), lambda i: (i, 0, 0)),
                  pl.BlockSpec((FNET_GROUP_DIM, 2 * FNET_GROUP_DIM), lambda i: (0, 0)),
                  table, table],
        out_specs=pl.BlockSpec((1, seq, FNET_DIM), lambda i: (i, 0, 0)),
        out_shape=jax.ShapeDtypeStruct((b, seq, FNET_DIM), BF16),
        scratch_shapes=[pltpu.VMEM((seq, FNET_DIM), BF16)],
        compiler_params=_params("arbitrary"),
        name="fnet",
    )(u, chan, pos_c, pos_s)


def _merge_kernel(x_ref, of_ref, ob_ref, sg_ref, ga_ref, gb_ref, yf_ref, mod_ref, og_ref, n2g_ref,
                  wa_ref, wb_ref, wo_ref, x1_ref, h2_ref):
    n_sub = x_ref.shape[1] // MERGE_SUB

    def mix(rows):
        o = of_ref[0, rows, :] + ob_ref[0, rows, :]
        parts = []
        for h in range(HG_HEADS):
            oh = o[:, h * HG_HEAD_DIM:(h + 1) * HG_HEAD_DIM]
            parts.append(oh * lax.rsqrt(jnp.mean(oh * oh, axis=-1, keepdims=True) + EPS))
        on = jnp.concatenate(parts, axis=-1) * og_ref[...] * sg_ref[0, rows, :].astype(F32)
        y_b = _dot(on.astype(BF16), wb_ref[...]).astype(BF16)
        y_a = _dot(yf_ref[0, rows, :], wa_ref[...]).astype(BF16)
        m = ga_ref[0, rows, :] * y_a + gb_ref[0, rows, :] * y_b
        return _dot(m, wo_ref[...])

    def finish(rows, yx):
        x1 = x_ref[0, rows, :] + mod_ref[0, 2:3, :] * yx
        x1_ref[0, rows, :] = x1
        h2 = _modulated_norm(x1, n2g_ref[...], mod_ref[0, 3:4, :], mod_ref[0, 4:5, :])
        h2_ref[0, rows, :] = h2.astype(BF16)

    pending = None
    for i in range(n_sub):
        rows = slice(i * MERGE_SUB, (i + 1) * MERGE_SUB)
        yx = mix(rows)
        if pending is not None:
            finish(*pending)
        pending = (rows, yx)
    finish(*pending)


def _merge(x, o_f, o_b, gates, y_f, mod, og, n2g, w_a, w_b, w_out, *, tm):
    b, l, d = x.shape
    tok = lambda w, col: pl.BlockSpec((1, tm, w), lambda i, j: (i, j, col))
    const = lambda shape: pl.BlockSpec(shape, lambda i, j: tuple(0 for _ in shape))
    return pl.pallas_call(
        _merge_kernel,
        grid=(b, l // tm),
        in_specs=[tok(d, 0), tok(HG_DIM, 0), tok(HG_DIM, 0), tok(HG_DIM, 2 * d // HG_DIM),
                  tok(d, 0), tok(d, 1), tok(FNET_DIM, 0),
                  pl.BlockSpec((1, N_MOD, d), lambda i, j: (i, 0, 0)),
                  const((1, HG_DIM)), const((1, d)),
                  const((FNET_DIM, d)), const((HG_DIM, d)), const((d, d))],
        out_specs=(tok(d, 0), tok(d, 0)),
        out_shape=(jax.ShapeDtypeStruct((b, l, d), F32), jax.ShapeDtypeStruct((b, l, d), BF16)),
        compiler_params=_params("arbitrary", "arbitrary"),
        name="merge",
    )(x, o_f, o_b, gates, gates, gates, y_f, mod, og, n2g, w_a, w_b, w_out)


FFN_GROUP = 8


def _ffn_permute_tokens(h_ref, hp_ref, tmp_ref):
    n_tok = tmp_ref.shape[1]

    def body(g, carry):
        tok = pl.ds(pl.multiple_of(g * n_tok, n_tok), n_tok)
        hf = h_ref[0, tok, :].astype(F32)
        for lt in range(tmp_ref.shape[0]):
            for r in range(n_tok // GRID_W):
                for a in range(8):
                    r0 = r * GRID_W + 8 * a
                    tmp_ref[lt, pl.ds(r * GRID_W + a, 8, stride=8), :] = (
                        hf[r0:r0 + 8, lt * 128:(lt + 1) * 128])
        for lt in range(tmp_ref.shape[0]):
            hp_ref[tok, lt * 128:(lt + 1) * 128] = tmp_ref[lt].astype(BF16)
        return carry

    lax.fori_loop(0, h_ref.shape[1] // n_tok, body, 0)


def _ffn_up(hp_ref, up_refs, row0, n_rows):
    h = hp_ref[row0 * GRID_W:(row0 + n_rows) * GRID_W, :]
    return [_dot(h, up_ref[...]) for up_ref in up_refs]


def _pair(lo, hi):
    return jnp.concatenate([lo, hi], axis=0).astype(BF16)


def _ffn_row(z, r, taps, bias, first, last, sub):
    lo = [z[r * GRID_W + 8 * b:r * GRID_W + 8 * b + 8, 0:128] for b in range(8)]
    hi = [z[r * GRID_W + 8 * b:r * GRID_W + 8 * b + 8, 128:256] for b in range(8)]
    mid = [_pair(lo[b], hi[b]) for b in range(8)]
    down = lambda v: jnp.where(sub == 0, 0.0, pltpu.roll(v, 1, axis=0))
    left = [_pair(down(lo[7]), down(hi[7]))] + mid[0:7]
    up = lambda v: jnp.where(sub == 7, 0.0, pltpu.roll(v, 7, axis=0))
    right = mid[1:8] + [_pair(up(lo[0]), up(hi[0]))]
    term = lambda kh: [left[b] * taps[3 * kh] + mid[b] * taps[3 * kh + 1] + right[b] * taps[3 * kh + 2]
                       for b in range(8)]
    above = None if first else term(2)
    here = term(1)
    below = None if last else [t + bias for t in term(0)]
    return above, here, below


def _ffn_down(a_ref, dn_ref, y_ref, row0, n_rows):
    tok = slice(row0 * GRID_W, (row0 + n_rows) * GRID_W)
    part = _dot(a_ref[tok, :].astype(BF16), dn_ref[...])
    for j in range(y_ref.shape[1]):
        y_ref[0, j, tok, :] += part[:, j * 128:(j + 1) * 128]


def _ffn_kernel(h_ref, up1_ref, up2_ref, cw1_ref, cw2_ref, cb1_ref, cb2_ref, dn_ref, y_ref,
                hp_ref, a_ref, wp_ref, tmp_ref, wup_ref, wdn_ref):
    t = pl.program_id(1)
    rows = hp_ref.shape[0] // GRID_W
    grp = FFN_GROUP

    @pl.when(t == 0)
    def _():
        _ffn_permute_tokens(h_ref, hp_ref, tmp_ref)
        y_ref[...] = jnp.zeros_like(y_ref)

    wup_ref[0] = up1_ref[...].astype(BF16)
    wup_ref[1] = up2_ref[...].astype(BF16)
    wdn_ref[...] = dn_ref[...].astype(BF16)
    up_refs = (wup_ref.at[0], wup_ref.at[1])

    for half, (cw_ref, cb_ref) in enumerate(((cw1_ref, cb1_ref), (cw2_ref, cb2_ref))):
        for k in range(10):
            row = cw_ref[k:k + 1, :] if k < 9 else cb_ref[...]
            wp_ref[half, k] = _pair(jnp.broadcast_to(row[:, 0:128], (8, 128)),
                                    jnp.broadcast_to(row[:, 128:256], (8, 128)))
    taps = [[wp_ref[half, k] for k in range(9)] for half in range(2)]
    bias = [wp_ref[half, 9] for half in range(2)]
    sub = lax.broadcasted_iota(jnp.int32, (8, 128), 0)

    def gate(c, out_row):
        for b in range(8):
            act = (_silu(c[0][b]) * c[1][b]).astype(F32)
            tok_b = slice(out_row * GRID_W + 8 * b, out_row * GRID_W + 8 * b + 8)
            a_ref[tok_b, 0:128] = act[0:8]
            a_ref[tok_b, 128:256] = act[8:16]

    prev = [None, None]
    cur = [[bias[half]] * 8 for half in range(2)]
    z_next = _ffn_up(hp_ref, up_refs, 0, grp)
    for row0 in range(0, rows, grp):
        z_cur = z_next
        if row0 + grp < rows:
            z_next = _ffn_up(hp_ref, up_refs, row0 + grp, grp)
        for rr in range(grp):
            r = row0 + rr
            done = []
            for half in range(2):
                above, here, below = _ffn_row(z_cur[half], rr, taps[half], bias[half],
                                              r == 0, r == rows - 1, sub)
                if above is not None:
                    done.append([prev[half][b] + above[b] for b in range(8)])
                prev[half] = [cur[half][b] + here[b] for b in range(8)]
                cur[half] = below
            if done:
                gate(done, r - 1)
            if rr == 0 and row0 > 0:
                _ffn_down(a_ref, wdn_ref, y_ref, row0 - grp, grp)
    gate(prev, rows - 1)
    _ffn_down(a_ref, wdn_ref, y_ref, rows - grp, grp)


def _ffn(h2, up, conv_w, conv_b, down):
    b, l, d = h2.shape
    rows = l // GRID_W
    tc = FFN_TILE
    nt = D_FF // tc
    assert rows % FFN_GROUP == 0 and tc == 256, "the conv pairs the two lane tiles of a channel tile"
    return pl.pallas_call(
        _ffn_kernel,
        grid=(b, nt),
        in_specs=[pl.BlockSpec((1, l, d), lambda i, t: (i, 0, 0)),
                  pl.BlockSpec((d, tc), lambda i, t: (0, t)),
                  pl.BlockSpec((d, tc), lambda i, t: (0, nt + t)),
                  pl.BlockSpec((9, tc), lambda i, t: (0, t)),
                  pl.BlockSpec((9, tc), lambda i, t: (0, nt + t)),
                  pl.BlockSpec((1, tc), lambda i, t: (0, t)),
                  pl.BlockSpec((1, tc), lambda i, t: (0, nt + t)),
                  pl.BlockSpec((tc, d), lambda i, t: (t, 0))],
        out_specs=pl.BlockSpec((1, d // 128, l, 128), lambda i, t: (i, 0, 0, 0)),
        out_shape=jax.ShapeDtypeStruct((b, d // 128, l, 128), F32),
        scratch_shapes=[pltpu.VMEM((l, d), BF16),
                        pltpu.VMEM((l, tc), F32),
                        pltpu.VMEM((2, 10, 16, 128), BF16),
                        pltpu.VMEM((d // 128, FFN_GROUP * GRID_W, 128), F32),
                        pltpu.VMEM((2, d, tc), BF16),
                        pltpu.VMEM((tc, d), BF16)],
        compiler_params=_params("arbitrary", "arbitrary"),
        name="ffn",
    )(h2, up, up, conv_w, conv_w, conv_b, conv_b, down)


def _final_kernel(x1_ref, y_ref, mod_ref, g_ref, o_ref):
    n_slab = y_ref.shape[1]
    for r in range(y_ref.shape[2] // GRID_W):
        for a in range(8):
            tok = slice(r * GRID_W + 8 * a, r * GRID_W + 8 * a + 8)
            y = jnp.concatenate([y_ref[0, j, pl.ds(r * GRID_W + a, 8, stride=8), :]
                                 for j in range(n_slab)], axis=-1)
            x = x1_ref[0, tok, :] + mod_ref[0, 5:6, :] * y
            ms = jnp.mean(x * x, axis=-1, keepdims=True)
            o_ref[0, tok, :] = x * lax.rsqrt(ms + EPS) * g_ref[...]


def _final(x1, y, mod, g, *, tm):
    b, l, d = x1.shape
    tok = pl.BlockSpec((1, tm, d), lambda i, j: (i, j, 0))
    return pl.pallas_call(
        _final_kernel,
        grid=(b, l // tm),
        in_specs=[tok, pl.BlockSpec((1, d // 128, tm, 128), lambda i, j: (i, 0, j, 0)),
                  pl.BlockSpec((1, N_MOD, d), lambda i, j: (i, 0, 0)),
                  pl.BlockSpec((1, d), lambda i, j: (0, 0))],
        out_specs=tok,
        out_shape=jax.ShapeDtypeStruct((b, l, d), F32),
        compiler_params=_params("arbitrary", "arbitrary"),
        name="final",
    )(x1, y, mod, g)


def kernel(x, c, ctx, c_ctx, ada_w, ada_b, norm1_g, w_in, hg_lb, hg_onorm_g, w_a, w_b, w_out,
           norm2_g, ffn_up, ffn_conv_w, ffn_conv_b, ffn_down, final_g):
    b, seq, d = x.shape
    assert ada_w.shape[0] == 1, "single-layer kernel"
    assert (d, seq % GRID_W) == (D_MODEL, 0)

    lb = jnp.cumsum(jax.nn.softmax(hg_lb.astype(F32), axis=0), axis=0)[0].reshape(1, HG_DIM)

    n_rows = 16
    rows = jnp.concatenate([c, c_ctx[None, :], jnp.zeros((n_rows - b - 1, d), F32)], axis=0)
    mod = _ada(rows, ada_w[0], ada_b[0])
    mod_x = mod[:b].reshape(b, N_MOD, d)
    mod_c = mod[b:b + 1].reshape(1, N_MOD, d)

    w_in_b = w_in[0].astype(BF16)
    g1 = norm1_g[0].reshape(1, d)
    hg_x, u_f, gates = _inproj(x, mod_x, g1, lb, w_in_b, tm=1024, full=True)
    (hg_c,) = _inproj(ctx, mod_c, g1, lb, w_in_b, tm=256, full=False)

    s_ctx = _hgrn(hg_c, None, emit_o=False)
    o_f, o_b = _hgrn(hg_x, s_ctx, emit_o=True)

    chan, pos_c, pos_s = (jnp.asarray(t).astype(BF16) for t in _dft_tables(seq))
    y_f = _fnet(u_f, chan, pos_c, pos_s)

    x1, h2 = _merge(x, o_f, o_b, gates, y_f, mod_x, hg_onorm_g[0].reshape(1, HG_DIM),
                    norm2_g[0].reshape(1, d), w_a[0].astype(BF16), w_b[0].astype(BF16),
                    w_out[0].astype(BF16), tm=1024)

    y = _ffn(h2, ffn_up[0], ffn_conv_w[0].reshape(9, 2 * D_FF),
             ffn_conv_b[0].reshape(1, 2 * D_FF), ffn_down[0])
    return _final(x1, y, mod_x, final_g.reshape(1, d), tm=512)
```

```python
import functools

import numpy as np
import jax
import jax.numpy as jnp
from jax import lax
from jax.experimental import pallas as pl
from jax.experimental.pallas import tpu as pltpu

D_MODEL = 1024
GRID_W = 64
FNET_DIM = 512
FNET_GROUPS = 4
FNET_GROUP_DIM = FNET_DIM // FNET_GROUPS
HG_DIM = 512
HG_HEADS = 4
HG_HEAD_DIM = HG_DIM // HG_HEADS
D_FF = 2816
N_MOD = 6
EPS = 1e-6

COL_FNET, COL_Q, COL_FF, COL_FB, COL_I, COL_G = 0, 1, 2, 3, 4, 5

INPROJ_SUB = 256
MERGE_SUB = 256
HG_BLOCK = 1024
HG_CHUNK = 64
HG_UNROLL = 2
HG_SUB = 16
FFN_TILE = 256

BF16 = jnp.bfloat16
F32 = jnp.float32
VMEM_LIMIT = 56 * 1024 * 1024


def _sigmoid(x):
    return 1.0 / (1.0 + jnp.exp(-x))


def _silu(x):
    return x * _sigmoid(x)


def _dot(a, b):
    return jnp.dot(a, b, preferred_element_type=F32)


def _dot_nt(a, b):
    return lax.dot_general(a, b, (((1,), (1,)), ((), ())), preferred_element_type=F32)


def _params(*sem, flags=None):
    return pltpu.CompilerParams(dimension_semantics=sem, vmem_limit_bytes=VMEM_LIMIT, flags=flags)


def _ada_kernel(c_ref, w_ref, b_ref, o_ref):
    a = _silu(c_ref[...])
    o_ref[...] = _dot(a.astype(BF16), w_ref[...].astype(BF16)) + b_ref[...]


def _ada(rows, ada_w, ada_b):
    m, d = rows.shape
    n = ada_w.shape[1]
    tn = 1536
    return pl.pallas_call(
        _ada_kernel,
        grid=(n // tn,),
        in_specs=[pl.BlockSpec((m, d), lambda j: (0, 0)),
                  pl.BlockSpec((d, tn), lambda j: (0, j)),
                  pl.BlockSpec((1, tn), lambda j: (0, j))],
        out_specs=pl.BlockSpec((m, tn), lambda j: (0, j)),
        out_shape=jax.ShapeDtypeStruct((m, n), F32),
        compiler_params=_params("arbitrary"),
        name="ada",
    )(rows, ada_w, ada_b.reshape(1, n))


def _modulated_norm(x, g, shift, scale):
    ms = jnp.mean(x * x, axis=-1, keepdims=True)
    return (x * lax.rsqrt(ms + EPS) * g) * (1.0 + scale) + shift


def _inproj_kernel(x_ref, mod_ref, g_ref, lb_ref, w_ref, *o_refs, hg_col0, full):
    lb = lb_ref[...]
    hg_ref = o_refs[0]
    n_sub = x_ref.shape[1] // INPROJ_SUB

    def normed(i):
        rows = slice(i * INPROJ_SUB, (i + 1) * INPROJ_SUB)
        return _modulated_norm(x_ref[0, rows, :], g_ref[...], mod_ref[0, 0:1, :],
                               mod_ref[0, 1:2, :]).astype(BF16)

    hb_next = normed(0)
    for i in range(n_sub):
        hb = hb_next
        rows = slice(i * INPROJ_SUB, (i + 1) * INPROJ_SUB)

        def proj(col, width=HG_DIM):
            return _dot(hb, w_ref[:, col * HG_DIM:col * HG_DIM + width])

        hg_ref[0, rows, 0:HG_DIM] = _silu(proj(hg_col0))
        if i + 1 < n_sub:
            hb_next = normed(i + 1)
        for n in (1, 2):
            f = lb + (1.0 - lb) * _sigmoid(proj(hg_col0 + n))
            hg_ref[0, rows, n * HG_DIM:(n + 1) * HG_DIM] = jnp.log2(f)
        hg_ref[0, rows, 3 * HG_DIM:4 * HG_DIM] = proj(hg_col0 + 3)
        if full:
            uf_ref, gate_ref = o_refs[1], o_refs[2]
            uf_ref[0, rows, :] = proj(COL_FNET).astype(BF16)
            gate_ref[0, rows, 0:2 * D_MODEL] = _sigmoid(proj(COL_G + 1, 2 * D_MODEL)).astype(BF16)
            gate_ref[0, rows, 2 * D_MODEL:2 * D_MODEL + HG_DIM] = _silu(proj(COL_G)).astype(BF16)


def _inproj(x, mod, g, lb, w, *, tm, full):
    b, l, d = x.shape
    n = w.shape[1] if full else (COL_I + 1) * HG_DIM
    mod_map = (lambda i, j: (i, 0, 0)) if full else (lambda i, j: (0, 0, 0))
    tok = lambda width, dtype: (pl.BlockSpec((1, tm, width), lambda i, j: (i, j, 0)),
                                jax.ShapeDtypeStruct((b, l, width), dtype))
    outs = [tok(4 * HG_DIM, F32)]
    if full:
        outs += [tok(FNET_DIM, BF16), tok(HG_DIM + 2 * D_MODEL, BF16)]
    return pl.pallas_call(
        functools.partial(_inproj_kernel, hg_col0=COL_Q, full=full),
        grid=(b, l // tm),
        in_specs=[pl.BlockSpec((1, tm, d), lambda i, j: (i, j, 0)),
                  pl.BlockSpec((1, N_MOD, d), mod_map),
                  pl.BlockSpec((1, d), lambda i, j: (0, 0)),
                  pl.BlockSpec((1, HG_DIM), lambda i, j: (0, 0)),
                  pl.BlockSpec((d, n), lambda i, j: (0, 0))],
        out_specs=tuple(o[0] for o in outs),
        out_shape=tuple(o[1] for o in outs),
        compiler_params=_params("arbitrary", "arbitrary"),
        name="inproj_x" if full else "inproj_ctx",
    )(x, mod, g, lb, w)


def _hgrn_step(dirs, st_ref):
    c = HG_CHUNK
    hd = HG_HEAD_DIM
    r_idx = lax.broadcasted_iota(jnp.int32, (c, c), 0)
    c_idx = lax.broadcasted_iota(jnp.int32, (c, c), 1)
    pairs = [slice(p * 2 * hd, (p + 1) * 2 * hd) for p in range(HG_HEADS // 2)]

    def block_diag(a0, a1):
        z0, z1 = jnp.zeros(a0.shape, a0.dtype), jnp.zeros(a1.shape, a1.dtype)
        return jnp.concatenate([jnp.concatenate([a0, z0], axis=1),
                                jnp.concatenate([z1, a1], axis=1)], axis=0)

    cums = []
    for q, l2f, v, reverse, o_ref, row0, base in dirs:
        tri = jnp.where((c_idx >= r_idx) if reverse else (c_idx <= r_idx), 1.0, 0.0).astype(BF16)
        hi = l2f.astype(BF16)
        lo = (l2f - hi.astype(F32)).astype(BF16)
        cums.append(_dot(tri, hi) + _dot(tri, lo))

    state_in, state_upd, intra = [], [], []
    for (q, l2f, v, reverse, o_ref, row0, base), cum in zip(dirs, cums):
        k = 1.0 - jnp.exp2(l2f)
        last = 0 if reverse else c - 1
        total = cum[last:last + 1, :]
        q_in = (q * jnp.exp2(cum)).astype(BF16)
        k_out = (k * jnp.exp2(total - cum)).astype(BF16)
        e_tot = jnp.exp2(total)
        v_b = v.astype(BF16)
        for p, ps in enumerate(pairs):
            h0, h1 = slice(ps.start, ps.start + hd), slice(ps.start + hd, ps.stop)
            v_t = jnp.concatenate([v[:, h0], v[:, h1]], axis=0).T.astype(BF16)
            state_in.append(q_in[:, ps])
            state_upd.append((base + p, e_tot[:, ps], v_t, block_diag(k_out[:, h0], k_out[:, h1])))
            if o_ref is None:
                continue
            for i in range(c // HG_SUB):
                r0 = i * HG_SUB
                rows = slice(r0, r0 + HG_SUB)
                cols = slice(r0, c) if reverse else slice(0, r0 + HG_SUB)
                mid = r0 + HG_SUB // 2 if reverse else r0 + HG_SUB // 2 - 1
                m = cum[mid:mid + 1, ps]
                q_t = (q[rows, ps] * jnp.exp2(cum[rows, ps] - m)).astype(BF16)
                k_t = (k[cols, ps] * jnp.exp2(m - cum[cols, ps])).astype(BF16)
                intra.append((o_ref, row0, rows, cols, ps, reverse, len(state_in) - 1, q_t,
                              block_diag(k_t[:, :hd], k_t[:, hd:]),
                              block_diag(v_b[cols, h0], v_b[cols, h1])))

    scores = [_dot_nt(item[7], item[8]) for item in intra]
    kv = [_dot(v_t, k_bd) for _, _, v_t, k_bd in state_upd]
    states = {}
    inter = []
    for n, (idx, e_tot_p, _, _) in enumerate(state_upd):
        st = states[idx] if idx in states else st_ref[idx]
        if intra:
            st_b = st.astype(BF16)
            inter.append(_dot_nt(state_in[n], block_diag(st_b[:, :hd], st_b[:, hd:])))
        states[idx] = st * e_tot_p + kv[n]
    for idx, st in states.items():
        st_ref[idx] = st

    for (o_ref, row0, rows, cols, ps, reverse, chain, _, _, v_bd), a in zip(intra, scores):
        n_cols = cols.stop - cols.start
        row_g = rows.start + lax.broadcasted_iota(jnp.int32, (HG_SUB, 2 * n_cols), 0)
        lane = lax.broadcasted_iota(jnp.int32, (HG_SUB, 2 * n_cols), 1)
        col_g = cols.start + jnp.where(lane >= n_cols, lane - n_cols, lane)
        a = jnp.where((col_g >= row_g) if reverse else (col_g <= row_g), a, 0.0)
        o_ref[0, pl.ds(row0 + rows.start, HG_SUB), ps] = _dot(a.astype(BF16), v_bd) + inter[chain][rows]


def _hgrn_kernel(*refs, has_s0, emit_o):
    it = iter(refs)
    fwd_refs = [next(it) for _ in range(3)]
    bwd_refs = [next(it) for _ in range(3)]
    qf_ref = fwd_refs[0]
    s0_ref = next(it) if has_s0 else None
    of_ref = next(it) if emit_o else None
    ob_ref = next(it) if emit_o else None
    sout_ref = None if emit_o else next(it)
    st_ref = next(it)
    j = pl.program_id(1)
    n_inner = qf_ref.shape[1] // HG_CHUNK

    @pl.when(j == 0)
    def _():
        if has_s0:
            st_ref[...] = s0_ref[0]
        else:
            st_ref[...] = jnp.zeros_like(st_ref)

    span = HG_UNROLL * HG_CHUNK

    def body(it, carry):
        dirs = []
        for jj in range(HG_UNROLL):
            rf = pl.multiple_of(it * span, span) + jj * HG_CHUNK
            rb = pl.multiple_of((n_inner // HG_UNROLL - 1 - it) * span, span) + (HG_UNROLL - 1 - jj) * HG_CHUNK
            sf, sb = pl.ds(rf, HG_CHUNK), pl.ds(rb, HG_CHUNK)
            dirs.append(tuple(r[0, sf, :] for r in fwd_refs) + (False, of_ref, rf, 0))
            dirs.append(tuple(r[0, sb, :] for r in bwd_refs) + (True, ob_ref, rb, HG_HEADS // 2))
        _hgrn_step(dirs, st_ref)
        return carry

    lax.fori_loop(0, n_inner // HG_UNROLL, body, 0)

    if not emit_o:
        @pl.when(j == pl.num_programs(1) - 1)
        def _():
            sout_ref[0] = st_ref[...]


def _hgrn(u, s0, *, emit_o):
    b, l, _ = u.shape
    c = min(HG_BLOCK, l)
    n = l // c
    blk = (1, c, HG_DIM)
    fwd = lambda col: pl.BlockSpec(blk, lambda i, j: (i, j, col))
    bwd = lambda col: pl.BlockSpec(blk, lambda i, j: (i, n - 1 - j, col))
    st_dims = (HG_HEADS, HG_HEAD_DIM, 2 * HG_HEAD_DIM)
    state_spec = pl.BlockSpec((1,) + st_dims, lambda i, j: (i, 0, 0, 0))
    state_shape = jax.ShapeDtypeStruct((b,) + st_dims, F32)
    in_specs = [fwd(0), fwd(1), fwd(3), bwd(0), bwd(2), bwd(3)]
    args = [u, u, u, u, u, u]
    if s0 is not None:
        in_specs.append(state_spec)
        args.append(s0)
    if emit_o:
        o_shape = jax.ShapeDtypeStruct((b, l, HG_DIM), F32)
        out_shape = (o_shape, o_shape)
        out_specs = (pl.BlockSpec(blk, lambda i, j: (i, j, 0)),
                     pl.BlockSpec(blk, lambda i, j: (i, n - 1 - j, 0)))
    else:
        out_shape = state_shape
        out_specs = state_spec
    return pl.pallas_call(
        functools.partial(_hgrn_kernel, has_s0=s0 is not None, emit_o=emit_o),
        grid=(b, n),
        in_specs=in_specs,
        out_specs=out_specs,
        out_shape=out_shape,
        scratch_shapes=[pltpu.VMEM(st_dims, F32)],
        compiler_params=_params("arbitrary", "arbitrary"),
        name="hgrn_x" if emit_o else "hgrn_ctx",
    )(*args)


FNET_PAD = 16


def _dft_tables(seq):
    gd = FNET_GROUP_DIM
    kc = (np.arange(gd)[:, None] * np.arange(gd)[None, :]) % gd
    ang_c = 2.0 * np.pi * kc / gd
    scale = 1.0 / np.sqrt(float(seq) * gd)
    chan = np.concatenate([np.cos(ang_c), np.sin(ang_c)], axis=1) * scale
    half = seq // 2
    kl = (np.arange(half + FNET_PAD)[:, None] * np.arange(half)[None, :]) % seq
    ang_l = 2.0 * np.pi * kl / seq
    keep = (np.arange(half + FNET_PAD) <= half)[:, None]
    pos_c = np.where(keep, np.cos(ang_l), 0.0)
    pos_s = np.where(keep, -np.sin(ang_l), 0.0)
    return chan.astype(np.float32), pos_c.astype(np.float32), pos_s.astype(np.float32)


def _mirror_gather(width):
    sub = lax.broadcasted_iota(jnp.int32, (8, width), 0)
    return jnp.where(sub == 0, 0, 8 - sub), sub


def _fold_positions(v, sign):
    seq = v.shape[0]
    n_tiles = seq // 8
    gather, sub = _mirror_gather(v.shape[1])
    tile = lambda k: v[8 * k:8 * k + 8, :]
    out = []
    for k in range(n_tiles // 2):
        tail = jnp.take_along_axis(tile(n_tiles - 1 - k), gather, axis=0)
        if k == 0:
            mirror = jnp.where(sub == 0, 0.0, tail)
        else:
            mirror = jnp.where(sub == 0, jnp.take_along_axis(tile(n_tiles - k), gather, axis=0), tail)
        out.append(tile(k) + mirror if sign > 0 else tile(k) - mirror)
    return jnp.concatenate(out, axis=0)


def _fnet_kernel(u_ref, chan_ref, posc_ref, poss_ref, y_ref, pq_ref):
    seq = u_ref.shape[1]
    half = seq // 2
    gd = FNET_GROUP_DIM
    mids = []
    for g in range(FNET_GROUPS):
        gs = slice(g * gd, (g + 1) * gd)
        pq = _dot(u_ref[0, :, gs], chan_ref[...])
        pq_ref[0:half, gs] = _fold_positions(pq[:, :gd], 1).astype(BF16)
        pq_ref[half:seq, gs] = _fold_positions(pq[:, gd:], -1).astype(BF16)
        mids.append(pq[half:half + 1, :gd])
    gather, sub = _mirror_gather(FNET_DIM)
    mid = jnp.concatenate(mids, axis=1)
    mid_alt = jnp.where(sub % 2 == 0, mid, -mid)

    a = _dot(posc_ref[...], pq_ref[0:half, :])
    bm = _dot(poss_ref[...], pq_ref[half:seq, :])
    tile = lambda v, k: v[8 * k:8 * k + 8, :]
    n_t = half // 8
    for k in range(0, n_t, 2):
        pair = [tile(a, k + i) + tile(bm, k + i) + mid_alt for i in range(2)]
        y_ref[0, 8 * k:8 * k + 16, :] = jnp.concatenate(pair, axis=0).astype(BF16)
    flipped = [jnp.take_along_axis(tile(a, k) - tile(bm, k) + mid_alt, gather, axis=0) for k in range(n_t)]
    flipped.append(jnp.take_along_axis(tile(a, n_t) + tile(bm, n_t) + mid_alt, gather, axis=0))
    n_all = seq // 8
    for m in range(n_t, n_all, 2):
        pair = [jnp.where(sub == 0, flipped[n_all - mm], flipped[n_all - 1 - mm]) for mm in (m, m + 1)]
        y_ref[0, 8 * m:8 * m + 16, :] = jnp.concatenate(pair, axis=0).astype(BF16)


def _fnet(u, chan, pos_c, pos_s):
    b, seq, _ = u.shape
    table = pl.BlockSpec(pos_c.shape, lambda i: (0, 0))
    return pl.pallas_call(
        _fnet_kernel,
        grid=(b,),
        in_specs=[pl.BlockSpec((1, seq, FNET_DIM), lambda i: (i, 0, 0)),
                  pl.BlockSpec((FNET_GROUP_DIM, 2 * FNET_GROUP_DIM), lambda i: (0, 0)),
                  table, table],
        out_specs=pl.BlockSpec((1, seq, FNET_DIM), lambda i: (i, 0, 0)),
        out_shape=jax.ShapeDtypeStruct((b, seq, FNET_DIM), BF16),
        scratch_shapes=[pltpu.VMEM((seq, FNET_DIM), BF16)],
        compiler_params=_params("arbitrary"),
        name="fnet",
    )(u, chan, pos_c, pos_s)


def _merge_kernel(x_ref, of_ref, ob_ref, sg_ref, ga_ref, gb_ref, yf_ref, mod_ref, og_ref, n2g_ref,
                  wa_ref, wb_ref, wo_ref, x1_ref, h2_ref):
    n_sub = x_ref.shape[1] // MERGE_SUB

    def mix(rows):
        o = of_ref[0, rows, :] + ob_ref[0, rows, :]
        parts = []
        for h in range(HG_HEADS):
            oh = o[:, h * HG_HEAD_DIM:(h + 1) * HG_HEAD_DIM]
            parts.append(oh * lax.rsqrt(jnp.mean(oh * oh, axis=-1, keepdims=True) + EPS))
        on = jnp.concatenate(parts, axis=-1) * og_ref[...] * sg_ref[0, rows, :].astype(F32)
        y_b = _dot(on.astype(BF16), wb_ref[...])
        y_a = _dot(yf_ref[0, rows, :], wa_ref[...])
        m = ga_ref[0, rows, :].astype(F32) * y_a + gb_ref[0, rows, :].astype(F32) * y_b
        return _dot(m.astype(BF16), wo_ref[...])

    def finish(rows, yx):
        x1 = x_ref[0, rows, :] + mod_ref[0, 2:3, :] * yx
        x1_ref[0, rows, :] = x1
        h2 = _modulated_norm(x1, n2g_ref[...], mod_ref[0, 3:4, :], mod_ref[0, 4:5, :])
        h2_ref[0, rows, :] = h2.astype(BF16)

    pending = None
    for i in range(n_sub):
        rows = slice(i * MERGE_SUB, (i + 1) * MERGE_SUB)
        yx = mix(rows)
        if pending is not None:
            finish(*pending)
        pending = (rows, yx)
    finish(*pending)


def _merge(x, o_f, o_b, gates, y_f, mod, og, n2g, w_a, w_b, w_out, *, tm):
    b, l, d = x.shape
    tok = lambda w, col: pl.BlockSpec((1, tm, w), lambda i, j: (i, j, col))
    const = lambda shape: pl.BlockSpec(shape, lambda i, j: tuple(0 for _ in shape))
    return pl.pallas_call(
        _merge_kernel,
        grid=(b, l // tm),
        in_specs=[tok(d, 0), tok(HG_DIM, 0), tok(HG_DIM, 0), tok(HG_DIM, 2 * d // HG_DIM),
                  tok(d, 0), tok(d, 1), tok(FNET_DIM, 0),
                  pl.BlockSpec((1, N_MOD, d), lambda i, j: (i, 0, 0)),
                  const((1, HG_DIM)), const((1, d)),
                  const((FNET_DIM, d)), const((HG_DIM, d)), const((d, d))],
        out_specs=(tok(d, 0), tok(d, 0)),
        out_shape=(jax.ShapeDtypeStruct((b, l, d), F32), jax.ShapeDtypeStruct((b, l, d), BF16)),
        compiler_params=_params("arbitrary", "arbitrary"),
        name="merge",
    )(x, o_f, o_b, gates, gates, gates, y_f, mod, og, n2g, w_a, w_b, w_out)


FFN_GROUP = 8


def _ffn_permute_tokens(h_ref, hp_ref, tmp_ref):
    n_tok = tmp_ref.shape[1]

    def body(g, carry):
        tok = pl.ds(pl.multiple_of(g * n_tok, n_tok), n_tok)
        hf = h_ref[0, tok, :].astype(F32)
        for lt in range(tmp_ref.shape[0]):
            for r in range(n_tok // GRID_W):
                for a in range(8):
                    r0 = r * GRID_W + 8 * a
                    tmp_ref[lt, pl.ds(r * GRID_W + a, 8, stride=8), :] = (
                        hf[r0:r0 + 8, lt * 128:(lt + 1) * 128])
        for lt in range(tmp_ref.shape[0]):
            hp_ref[tok, lt * 128:(lt + 1) * 128] = tmp_ref[lt].astype(BF16)
        return carry

    lax.fori_loop(0, h_ref.shape[1] // n_tok, body, 0)


def _ffn_up(hp_ref, up_refs, row0, n_rows):
    h = hp_ref[row0 * GRID_W:(row0 + n_rows) * GRID_W, :]
    return [_dot(h, up_ref[...]) for up_ref in up_refs]


def _pair(lo, hi):
    return jnp.concatenate([lo, hi], axis=0).astype(BF16)


def _ffn_row(z, r, taps, bias, first, last, sub):
    lo = [z[r * GRID_W + 8 * b:r * GRID_W + 8 * b + 8, 0:128] for b in range(8)]
    hi = [z[r * GRID_W + 8 * b:r * GRID_W + 8 * b + 8, 128:256] for b in range(8)]
    mid = [_pair(lo[b], hi[b]) for b in range(8)]
    down = lambda v: jnp.where(sub == 0, 0.0, pltpu.roll(v, 1, axis=0))
    left = [_pair(down(lo[7]), down(hi[7]))] + mid[0:7]
    up = lambda v: jnp.where(sub == 7, 0.0, pltpu.roll(v, 7, axis=0))
    right = mid[1:8] + [_pair(up(lo[0]), up(hi[0]))]
    term = lambda kh: [left[b] * taps[3 * kh] + mid[b] * taps[3 * kh + 1] + right[b] * taps[3 * kh + 2]
                       for b in range(8)]
    above = None if first else term(2)
    here = term(1)
    below = None if last else [t + bias for t in term(0)]
    return above, here, below


def _ffn_down(a_ref, dn_ref, y_ref, row0, n_rows):
    tok = slice(row0 * GRID_W, (row0 + n_rows) * GRID_W)
    part = _dot(a_ref[tok, :].astype(BF16), dn_ref[...])
    for j in range(y_ref.shape[1]):
        y_ref[0, j, tok, :] += part[:, j * 128:(j + 1) * 128]


def _ffn_kernel(h_ref, up1_ref, up2_ref, cw1_ref, cw2_ref, cb1_ref, cb2_ref, dn_ref, y_ref,
                hp_ref, a_ref, wp_ref, tmp_ref, wup_ref, wdn_ref):
    t = pl.program_id(1)
    rows = hp_ref.shape[0] // GRID_W
    grp = FFN_GROUP

    @pl.when(t == 0)
    def _():
        _ffn_permute_tokens(h_ref, hp_ref, tmp_ref)
        y_ref[...] = jnp.zeros_like(y_ref)

    wup_ref[0] = up1_ref[...].astype(BF16)
    wup_ref[1] = up2_ref[...].astype(BF16)
    wdn_ref[...] = dn_ref[...].astype(BF16)
    up_refs = (wup_ref.at[0], wup_ref.at[1])

    for half, (cw_ref, cb_ref) in enumerate(((cw1_ref, cb1_ref), (cw2_ref, cb2_ref))):
        for k in range(10):
            row = cw_ref[k:k + 1, :] if k < 9 else cb_ref[...]
            wp_ref[half, k] = _pair(jnp.broadcast_to(row[:, 0:128], (8, 128)),
                                    jnp.broadcast_to(row[:, 128:256], (8, 128)))
    taps = [[wp_ref[half, k] for k in range(9)] for half in range(2)]
    bias = [wp_ref[half, 9] for half in range(2)]
    sub = lax.broadcasted_iota(jnp.int32, (8, 128), 0)

    def gate(c, out_row):
        for b in range(8):
            act = (_silu(c[0][b]) * c[1][b]).astype(F32)
            tok_b = slice(out_row * GRID_W + 8 * b, out_row * GRID_W + 8 * b + 8)
            a_ref[tok_b, 0:128] = act[0:8]
            a_ref[tok_b, 128:256] = act[8:16]

    prev = [None, None]
    cur = [[bias[half]] * 8 for half in range(2)]
    z_next = _ffn_up(hp_ref, up_refs, 0, grp)
    for row0 in range(0, rows, grp):
        z_cur = z_next
        if row0 + grp < rows:
            z_next = _ffn_up(hp_ref, up_refs, row0 + grp, grp)
        for rr in range(grp):
            r = row0 + rr
            done = []
            for half in range(2):
                above, here, below = _ffn_row(z_cur[half], rr, taps[half], bias[half],
                                              r == 0, r == rows - 1, sub)
                if above is not None:
                    done.append([prev[half][b] + above[b] for b in range(8)])
                prev[half] = [cur[half][b] + here[b] for b in range(8)]
                cur[half] = below
            if done:
                gate(done, r - 1)
            if rr == 0 and row0 > 0:
                _ffn_down(a_ref, wdn_ref, y_ref, row0 - grp, grp)
    gate(prev, rows - 1)
    _ffn_down(a_ref, wdn_ref, y_ref, rows - grp, grp)


def _ffn(h2, up, conv_w, conv_b, down):
    b, l, d = h2.shape
    rows = l // GRID_W
    tc = FFN_TILE
    nt = D_FF // tc
    assert rows % FFN_GROUP == 0 and tc == 256, "the conv pairs the two lane tiles of a channel tile"
    return pl.pallas_call(
        _ffn_kernel,
        grid=(b, nt),
        in_specs=[pl.BlockSpec((1, l, d), lambda i, t: (i, 0, 0)),
                  pl.BlockSpec((d, tc), lambda i, t: (0, t)),
                  pl.BlockSpec((d, tc), lambda i, t: (0, nt + t)),
                  pl.BlockSpec((9, tc), lambda i, t: (0, t)),
                  pl.BlockSpec((9, tc), lambda i, t: (0, nt + t)),
                  pl.BlockSpec((1, tc), lambda i, t: (0, t)),
                  pl.BlockSpec((1, tc), lambda i, t: (0, nt + t)),
                  pl.BlockSpec((tc, d), lambda i, t: (t, 0))],
        out_specs=pl.BlockSpec((1, d // 128, l, 128), lambda i, t: (i, 0, 0, 0)),
        out_shape=jax.ShapeDtypeStruct((b, d // 128, l, 128), F32),
        scratch_shapes=[pltpu.VMEM((l, d), BF16),
                        pltpu.VMEM((l, tc), F32),
                        pltpu.VMEM((2, 10, 16, 128), BF16),
                        pltpu.VMEM((d // 128, FFN_GROUP * GRID_W, 128), F32),
                        pltpu.VMEM((2, d, tc), BF16),
                        pltpu.VMEM((tc, d), BF16)],
        compiler_params=_params("arbitrary", "arbitrary"),
        name="ffn",
    )(h2, up, up, conv_w, conv_w, conv_b, conv_b, down)


def _final_kernel(x1_ref, y_ref, mod_ref, g_ref, o_ref):
    n_slab = y_ref.shape[1]
    for r in range(y_ref.shape[2] // GRID_W):
        for a in range(8):
            tok = slice(r * GRID_W + 8 * a, r * GRID_W + 8 * a + 8)
            y = jnp.concatenate([y_ref[0, j, pl.ds(r * GRID_W + a, 8, stride=8), :]
                                 for j in range(n_slab)], axis=-1)
            x = x1_ref[0, tok, :] + mod_ref[0, 5:6, :] * y
            ms = jnp.mean(x * x, axis=-1, keepdims=True)
            o_ref[0, tok, :] = x * lax.rsqrt(ms + EPS) * g_ref[...]


def _final(x1, y, mod, g, *, tm):
    b, l, d = x1.shape
    tok = pl.BlockSpec((1, tm, d), lambda i, j: (i, j, 0))
    return pl.pallas_call(
        _final_kernel,
        grid=(b, l // tm),
        in_specs=[tok, pl.BlockSpec((1, d // 128, tm, 128), lambda i, j: (i, 0, j, 0)),
                  pl.BlockSpec((1, N_MOD, d), lambda i, j: (i, 0, 0)),
                  pl.BlockSpec((1, d), lambda i, j: (0, 0))],
        out_specs=tok,
        out_shape=jax.ShapeDtypeStruct((b, l, d), F32),
        compiler_params=_params("arbitrary", "arbitrary"),
        name="final",
    )(x1, y, mod, g)


def kernel(x, c, ctx, c_ctx, ada_w, ada_b, norm1_g, w_in, hg_lb, hg_onorm_g, w_a, w_b, w_out,
           norm2_g, ffn_up, ffn_conv_w, ffn_conv_b, ffn_down, final_g):
    b, seq, d = x.shape
    assert ada_w.shape[0] == 1, "single-layer kernel"
    assert (d, seq % GRID_W) == (D_MODEL, 0)

    lb = jnp.cumsum(jax.nn.softmax(hg_lb.astype(F32), axis=0), axis=0)[0].reshape(1, HG_DIM)

    n_rows = 16
    rows = jnp.concatenate([c, c_ctx[None, :], jnp.zeros((n_rows - b - 1, d), F32)], axis=0)
    mod = _ada(rows, ada_w[0], ada_b[0])
    mod_x = mod[:b].reshape(b, N_MOD, d)
    mod_c = mod[b:b + 1].reshape(1, N_MOD, d)

    w_in_b = w_in[0].astype(BF16)
    g1 = norm1_g[0].reshape(1, d)
    hg_x, u_f, gates = _inproj(x, mod_x, g1, lb, w_in_b, tm=1024, full=True)
    (hg_c,) = _inproj(ctx.reshape(1, -1, d), mod_c, g1, lb, w_in_b, tm=1024, full=False)
    hg_c = hg_c.reshape(b, ctx.shape[1], 4 * HG_DIM)

    s_ctx = _hgrn(hg_c, None, emit_o=False)
    o_f, o_b = _hgrn(hg_x, s_ctx, emit_o=True)

    chan, pos_c, pos_s = (jnp.asarray(t).astype(BF16) for t in _dft_tables(seq))
    y_f = _fnet(u_f, chan, pos_c, pos_s)

    x1, h2 = _merge(x, o_f, o_b, gates, y_f, mod_x, hg_onorm_g[0].reshape(1, HG_DIM),
                    norm2_g[0].reshape(1, d), w_a[0].astype(BF16), w_b[0].astype(BF16),
                    w_out[0].astype(BF16), tm=1024)

    y = _ffn(h2, ffn_up[0], ffn_conv_w[0].reshape(9, 2 * D_FF),
             ffn_conv_b[0].reshape(1, 2 * D_FF), ffn_down[0])
    return _final(x1, y, mod_x, final_g.reshape(1, d), tm=1024)
```

```python
import functools

import numpy as np
import jax
import jax.numpy as jnp
from jax import lax
from jax.experimental import pallas as pl
from jax.experimental.pallas import tpu as pltpu

D_MODEL = 1024
GRID_W = 64
FNET_DIM = 512
FNET_GROUPS = 4
FNET_GROUP_DIM = FNET_DIM // FNET_GROUPS
HG_DIM = 512
HG_HEADS = 4
HG_HEAD_DIM = HG_DIM // HG_HEADS
D_FF = 2816
N_MOD = 6
EPS = 1e-6

COL_FNET, COL_Q, COL_FF, COL_FB, COL_I, COL_G = 0, 1, 2, 3, 4, 5

INPROJ_SUB = 256
MERGE_SUB = 256
HG_BLOCK = 1024
HG_CHUNK = 64
HG_UNROLL = 2
HG_SUB = 16
FFN_TILE = 256

BF16 = jnp.bfloat16
F32 = jnp.float32
VMEM_LIMIT = 56 * 1024 * 1024


def _sigmoid(x):
    return 1.0 / (1.0 + jnp.exp(-x))


def _silu(x):
    return x * _sigmoid(x)


def _dot(a, b):
    return jnp.dot(a, b, preferred_element_type=F32)


def _dot_nt(a, b):
    return lax.dot_general(a, b, (((1,), (1,)), ((), ())), preferred_element_type=F32)


def _params(*sem, flags=None):
    return pltpu.CompilerParams(dimension_semantics=sem, vmem_limit_bytes=VMEM_LIMIT, flags=flags)


def _ada_kernel(c_ref, w_ref, b_ref, o_ref):
    a = _silu(c_ref[...])
    o_ref[...] = _dot(a.astype(BF16), w_ref[...].astype(BF16)) + b_ref[...]


def _ada(rows, ada_w, ada_b):
    m, d = rows.shape
    n = ada_w.shape[1]
    tn = 1536
    return pl.pallas_call(
        _ada_kernel,
        grid=(n // tn,),
        in_specs=[pl.BlockSpec((m, d), lambda j: (0, 0)),
                  pl.BlockSpec((d, tn), lambda j: (0, j)),
                  pl.BlockSpec((1, tn), lambda j: (0, j))],
        out_specs=pl.BlockSpec((m, tn), lambda j: (0, j)),
        out_shape=jax.ShapeDtypeStruct((m, n), F32),
        compiler_params=_params("arbitrary"),
        name="ada",
    )(rows, ada_w, ada_b.reshape(1, n))


def _modulated_norm(x, g, shift, scale):
    ms = jnp.mean(x * x, axis=-1, keepdims=True)
    return (x * lax.rsqrt(ms + EPS) * g) * (1.0 + scale) + shift


def _inproj_kernel(x_ref, mod_ref, g_ref, lb_ref, w_ref, *o_refs, hg_col0, full):
    lb = lb_ref[...]
    hg_ref = o_refs[0]
    n_sub = x_ref.shape[1] // INPROJ_SUB

    def normed(i):
        rows = slice(i * INPROJ_SUB, (i + 1) * INPROJ_SUB)
        return _modulated_norm(x_ref[0, rows, :], g_ref[...], mod_ref[0, 0:1, :],
                               mod_ref[0, 1:2, :]).astype(BF16)

    hb_next = normed(0)
    for i in range(n_sub):
        hb = hb_next
        rows = slice(i * INPROJ_SUB, (i + 1) * INPROJ_SUB)

        def proj(col, width=HG_DIM):
            return _dot(hb, w_ref[:, col * HG_DIM:col * HG_DIM + width])

        hg_ref[0, rows, 0:HG_DIM] = _silu(proj(hg_col0))
        if i + 1 < n_sub:
            hb_next = normed(i + 1)
        for n in (1, 2):
            f = lb + (1.0 - lb) * _sigmoid(proj(hg_col0 + n))
            hg_ref[0, rows, n * HG_DIM:(n + 1) * HG_DIM] = jnp.log2(f)
        hg_ref[0, rows, 3 * HG_DIM:4 * HG_DIM] = proj(hg_col0 + 3)
        if full:
            uf_ref, gate_ref = o_refs[1], o_refs[2]
            uf_ref[0, rows, :] = proj(COL_FNET).astype(BF16)
            gate_ref[0, rows, 0:2 * D_MODEL] = _sigmoid(proj(COL_G + 1, 2 * D_MODEL)).astype(BF16)
            gate_ref[0, rows, 2 * D_MODEL:2 * D_MODEL + HG_DIM] = _silu(proj(COL_G)).astype(BF16)


def _inproj(x, mod, g, lb, w, *, tm, full):
    b, l, d = x.shape
    n = w.shape[1] if full else (COL_I + 1) * HG_DIM
    mod_map = (lambda i, j: (i, 0, 0)) if full else (lambda i, j: (0, 0, 0))
    tok = lambda width, dtype: (pl.BlockSpec((1, tm, width), lambda i, j: (i, j, 0)),
                                jax.ShapeDtypeStruct((b, l, width), dtype))
    outs = [tok(4 * HG_DIM, F32)]
    if full:
        outs += [tok(FNET_DIM, BF16), tok(HG_DIM + 2 * D_MODEL, BF16)]
    return pl.pallas_call(
        functools.partial(_inproj_kernel, hg_col0=COL_Q, full=full),
        grid=(b, l // tm),
        in_specs=[pl.BlockSpec((1, tm, d), lambda i, j: (i, j, 0)),
                  pl.BlockSpec((1, N_MOD, d), mod_map),
                  pl.BlockSpec((1, d), lambda i, j: (0, 0)),
                  pl.BlockSpec((1, HG_DIM), lambda i, j: (0, 0)),
                  pl.BlockSpec((d, n), lambda i, j: (0, 0))],
        out_specs=tuple(o[0] for o in outs),
        out_shape=tuple(o[1] for o in outs),
        compiler_params=_params("arbitrary", "arbitrary"),
        name="inproj_x" if full else "inproj_ctx",
    )(x, mod, g, lb, w)


def _hgrn_step(dirs, st_ref):
    c = HG_CHUNK
    hd = HG_HEAD_DIM
    r_idx = lax.broadcasted_iota(jnp.int32, (c, c), 0)
    c_idx = lax.broadcasted_iota(jnp.int32, (c, c), 1)
    pairs = [slice(p * 2 * hd, (p + 1) * 2 * hd) for p in range(HG_HEADS // 2)]

    def block_diag(a0, a1):
        z0, z1 = jnp.zeros(a0.shape, a0.dtype), jnp.zeros(a1.shape, a1.dtype)
        return jnp.concatenate([jnp.concatenate([a0, z0], axis=1),
                                jnp.concatenate([z1, a1], axis=1)], axis=0)

    cums = []
    for q, l2f, v, reverse, o_ref, row0, base in dirs:
        tri = jnp.where((c_idx >= r_idx) if reverse else (c_idx <= r_idx), 1.0, 0.0).astype(BF16)
        hi = l2f.astype(BF16)
        lo = (l2f - hi.astype(F32)).astype(BF16)
        cums.append(_dot(tri, hi) + _dot(tri, lo))

    state_in, state_upd, intra = [], [], []
    for (q, l2f, v, reverse, o_ref, row0, base), cum in zip(dirs, cums):
        k = 1.0 - jnp.exp2(l2f)
        last = 0 if reverse else c - 1
        total = cum[last:last + 1, :]
        q_in = (q * jnp.exp2(cum)).astype(BF16)
        k_out = (k * jnp.exp2(total - cum)).astype(BF16)
        e_tot = jnp.exp2(total)
        v_b = v.astype(BF16)
        for p, ps in enumerate(pairs):
            h0, h1 = slice(ps.start, ps.start + hd), slice(ps.start + hd, ps.stop)
            v_t = jnp.concatenate([v[:, h0], v[:, h1]], axis=0).T.astype(BF16)
            state_in.append(q_in[:, ps])
            state_upd.append((base + p, e_tot[:, ps], v_t, block_diag(k_out[:, h0], k_out[:, h1])))
            if o_ref is None:
                continue
            for i in range(c // HG_SUB):
                r0 = i * HG_SUB
                rows = slice(r0, r0 + HG_SUB)
                cols = slice(r0, c) if reverse else slice(0, r0 + HG_SUB)
                mid = r0 + HG_SUB // 2 if reverse else r0 + HG_SUB // 2 - 1
                m = cum[mid:mid + 1, ps]
                q_t = (q[rows, ps] * jnp.exp2(cum[rows, ps] - m)).astype(BF16)
                k_t = (k[cols, ps] * jnp.exp2(m - cum[cols, ps])).astype(BF16)
                intra.append((o_ref, row0, rows, cols, ps, reverse, len(state_in) - 1, q_t,
                              block_diag(k_t[:, :hd], k_t[:, hd:]),
                              block_diag(v_b[cols, h0], v_b[cols, h1])))

    scores = [_dot_nt(item[7], item[8]) for item in intra]
    kv = [_dot(v_t, k_bd) for _, _, v_t, k_bd in state_upd]
    states = {}
    inter = []
    for n, (idx, e_tot_p, _, _) in enumerate(state_upd):
        st = states[idx] if idx in states else st_ref[idx]
        if intra:
            st_b = st.astype(BF16)
            inter.append(_dot_nt(state_in[n], block_diag(st_b[:, :hd], st_b[:, hd:])))
        states[idx] = st * e_tot_p + kv[n]
    for idx, st in states.items():
        st_ref[idx] = st

    for (o_ref, row0, rows, cols, ps, reverse, chain, _, _, v_bd), a in zip(intra, scores):
        n_cols = cols.stop - cols.start
        row_g = rows.start + lax.broadcasted_iota(jnp.int32, (HG_SUB, 2 * n_cols), 0)
        lane = lax.broadcasted_iota(jnp.int32, (HG_SUB, 2 * n_cols), 1)
        col_g = cols.start + jnp.where(lane >= n_cols, lane - n_cols, lane)
        a = jnp.where((col_g >= row_g) if reverse else (col_g <= row_g), a, 0.0)
        o_ref[0, pl.ds(row0 + rows.start, HG_SUB), ps] = _dot(a.astype(BF16), v_bd) + inter[chain][rows]


def _hgrn_kernel(*refs, has_s0, emit_o):
    it = iter(refs)
    fwd_refs = [next(it) for _ in range(3)]
    bwd_refs = [next(it) for _ in range(3)]
    qf_ref = fwd_refs[0]
    s0_ref = next(it) if has_s0 else None
    of_ref = next(it) if emit_o else None
    ob_ref = next(it) if emit_o else None
    sout_ref = None if emit_o else next(it)
    st_ref = next(it)
    j = pl.program_id(1)
    n_inner = qf_ref.shape[1] // HG_CHUNK

    @pl.when(j == 0)
    def _():
        if has_s0:
            st_ref[...] = s0_ref[0]
        else:
            st_ref[...] = jnp.zeros_like(st_ref)

    span = HG_UNROLL * HG_CHUNK

    def body(it, carry):
        dirs = []
        for jj in range(HG_UNROLL):
            rf = pl.multiple_of(it * span, span) + jj * HG_CHUNK
            rb = pl.multiple_of((n_inner // HG_UNROLL - 1 - it) * span, span) + (HG_UNROLL - 1 - jj) * HG_CHUNK
            sf, sb = pl.ds(rf, HG_CHUNK), pl.ds(rb, HG_CHUNK)
            dirs.append(tuple(r[0, sf, :] for r in fwd_refs) + (False, of_ref, rf, 0))
            dirs.append(tuple(r[0, sb, :] for r in bwd_refs) + (True, ob_ref, rb, HG_HEADS // 2))
        _hgrn_step(dirs, st_ref)
        return carry

    lax.fori_loop(0, n_inner // HG_UNROLL, body, 0)

    if not emit_o:
        @pl.when(j == pl.num_programs(1) - 1)
        def _():
            sout_ref[0] = st_ref[...]


def _hgrn(u, s0, *, emit_o):
    b, l, _ = u.shape
    c = min(HG_BLOCK, l)
    n = l // c
    blk = (1, c, HG_DIM)
    fwd = lambda col: pl.BlockSpec(blk, lambda i, j: (i, j, col))
    bwd = lambda col: pl.BlockSpec(blk, lambda i, j: (i, n - 1 - j, col))
    st_dims = (HG_HEADS, HG_HEAD_DIM, 2 * HG_HEAD_DIM)
    state_spec = pl.BlockSpec((1,) + st_dims, lambda i, j: (i, 0, 0, 0))
    state_shape = jax.ShapeDtypeStruct((b,) + st_dims, F32)
    in_specs = [fwd(0), fwd(1), fwd(3), bwd(0), bwd(2), bwd(3)]
    args = [u, u, u, u, u, u]
    if s0 is not None:
        in_specs.append(state_spec)
        args.append(s0)
    if emit_o:
        o_shape = jax.ShapeDtypeStruct((b, l, HG_DIM), F32)
        out_shape = (o_shape, o_shape)
        out_specs = (pl.BlockSpec(blk, lambda i, j: (i, j, 0)),
                     pl.BlockSpec(blk, lambda i, j: (i, n - 1 - j, 0)))
    else:
        out_shape = state_shape
        out_specs = state_spec
    return pl.pallas_call(
        functools.partial(_hgrn_kernel, has_s0=s0 is not None, emit_o=emit_o),
        grid=(b, n),
        in_specs=in_specs,
        out_specs=out_specs,
        out_shape=out_shape,
        scratch_shapes=[pltpu.VMEM(st_dims, F32)],
        compiler_params=_params("arbitrary", "arbitrary"),
        name="hgrn_x" if emit_o else "hgrn_ctx",
    )(*args)


FNET_PAD = 16


def _dft_tables(seq):
    gd = FNET_GROUP_DIM
    kc = (np.arange(gd)[:, None] * np.arange(gd)[None, :]) % gd
    ang_c = 2.0 * np.pi * kc / gd
    scale = 1.0 / np.sqrt(float(seq) * gd)
    chan = np.concatenate([np.cos(ang_c), np.sin(ang_c)], axis=1) * scale
    half = seq // 2
    kl = (np.arange(half + FNET_PAD)[:, None] * np.arange(half)[None, :]) % seq
    ang_l = 2.0 * np.pi * kl / seq
    keep = (np.arange(half + FNET_PAD) <= half)[:, None]
    pos_c = np.where(keep, np.cos(ang_l), 0.0)
    pos_s = np.where(keep, -np.sin(ang_l), 0.0)
    return chan.astype(np.float32), pos_c.astype(np.float32), pos_s.astype(np.float32)


def _mirror_gather(width):
    sub = lax.broadcasted_iota(jnp.int32, (8, width), 0)
    return jnp.where(sub == 0, 0, 8 - sub), sub


def _fold_positions(v, sign):
    seq = v.shape[0]
    n_tiles = seq // 8
    gather, sub = _mirror_gather(v.shape[1])
    tile = lambda k: v[8 * k:8 * k + 8, :]
    out = []
    for k in range(n_tiles // 2):
        tail = jnp.take_along_axis(tile(n_tiles - 1 - k), gather, axis=0)
        if k == 0:
            mirror = jnp.where(sub == 0, 0.0, tail)
        else:
            mirror = jnp.where(sub == 0, jnp.take_along_axis(tile(n_tiles - k), gather, axis=0), tail)
        out.append(tile(k) + mirror if sign > 0 else tile(k) - mirror)
    return jnp.concatenate(out, axis=0)


def _fnet_kernel(u_ref, chan_ref, posc_ref, poss_ref, y_ref, pq_ref):
    seq = u_ref.shape[1]
    half = seq // 2
    gd = FNET_GROUP_DIM
    mids = []
    for g in range(FNET_GROUPS):
        gs = slice(g * gd, (g + 1) * gd)
        pq = _dot(u_ref[0, :, gs], chan_ref[...])
        pq_ref[0:half, gs] = _fold_positions(pq[:, :gd], 1).astype(BF16)
        pq_ref[half:seq, gs] = _fold_positions(pq[:, gd:], -1).astype(BF16)
        mids.append(pq[half:half + 1, :gd])
    gather, sub = _mirror_gather(FNET_DIM)
    mid = jnp.concatenate(mids, axis=1)
    mid_alt = jnp.where(sub % 2 == 0, mid, -mid)

    a = _dot(posc_ref[...], pq_ref[0:half, :])
    bm = _dot(poss_ref[...], pq_ref[half:seq, :])
    tile = lambda v, k: v[8 * k:8 * k + 8, :]
    n_t = half // 8
    for k in range(0, n_t, 2):
        pair = [tile(a, k + i) + tile(bm, k + i) + mid_alt for i in range(2)]
        y_ref[0, 8 * k:8 * k + 16, :] = jnp.concatenate(pair, axis=0).astype(BF16)
    flipped = [jnp.take_along_axis(tile(a, k) - tile(bm, k) + mid_alt, gather, axis=0) for k in range(n_t)]
    flipped.append(jnp.take_along_axis(tile(a, n_t) + tile(bm, n_t) + mid_alt, gather, axis=0))
    n_all = seq // 8
    for m in range(n_t, n_all, 2):
        pair = [jnp.where(sub == 0, flipped[n_all - mm], flipped[n_all - 1 - mm]) for mm in (m, m + 1)]
        y_ref[0, 8 * m:8 * m + 16, :] = jnp.concatenate(pair, axis=0).astype(BF16)


def _fnet(u, chan, pos_c, pos_s):
    b, seq, _ = u.shape
    table = pl.BlockSpec(pos_c.shape, lambda i: (0, 0))
    return pl.pallas_call(
        _fnet_kernel,
        grid=(b,),
        in_specs=[pl.BlockSpec((1, seq, FNET_DIM), lambda i: (i, 0, 0)),
                  pl.BlockSpec((FNET_GROUP_DIM, 2 * FNET_GROUP_DIM), lambda i: (0, 0)),
                  table, table],
        out_specs=pl.BlockSpec((1, seq, FNET_DIM), lambda i: (i, 0, 0)),
        out_shape=jax.ShapeDtypeStruct((b, seq, FNET_DIM), BF16),
        scratch_shapes=[pltpu.VMEM((seq, FNET_DIM), BF16)],
        compiler_params=_params("arbitrary"),
        name="fnet",
    )(u, chan, pos_c, pos_s)


def _merge_kernel(x_ref, of_ref, ob_ref, sg_ref, ga_ref, gb_ref, yf_ref, mod_ref, og_ref, n2g_ref,
                  wa_ref, wb_ref, wo_ref, x1_ref, h2_ref):
    n_sub = x_ref.shape[1] // MERGE_SUB

    def mix(rows):
        o = of_ref[0, rows, :] + ob_ref[0, rows, :]
        parts = []
        for h in range(HG_HEADS):
            oh = o[:, h * HG_HEAD_DIM:(h + 1) * HG_HEAD_DIM]
            parts.append(oh * lax.rsqrt(jnp.mean(oh * oh, axis=-1, keepdims=True) + EPS))
        on = jnp.concatenate(parts, axis=-1) * og_ref[...] * sg_ref[0, rows, :].astype(F32)
        y_b = _dot(on.astype(BF16), wb_ref[...])
        y_a = _dot(yf_ref[0, rows, :], wa_ref[...])
        m = ga_ref[0, rows, :].astype(F32) * y_a + gb_ref[0, rows, :].astype(F32) * y_b
        return _dot(m.astype(BF16), wo_ref[...])

    def finish(rows, yx):
        x1 = x_ref[0, rows, :] + mod_ref[0, 2:3, :] * yx
        x1_ref[0, rows, :] = x1
        h2 = _modulated_norm(x1, n2g_ref[...], mod_ref[0, 3:4, :], mod_ref[0, 4:5, :])
        h2_ref[0, rows, :] = h2.astype(BF16)

    pending = None
    for i in range(n_sub):
        rows = slice(i * MERGE_SUB, (i + 1) * MERGE_SUB)
        yx = mix(rows)
        if pending is not None:
            finish(*pending)
        pending = (rows, yx)
    finish(*pending)


def _merge(x, o_f, o_b, gates, y_f, mod, og, n2g, w_a, w_b, w_out, *, tm):
    b, l, d = x.shape
    tok = lambda w, col: pl.BlockSpec((1, tm, w), lambda i, j: (i, j, col))
    const = lambda shape: pl.BlockSpec(shape, lambda i, j: tuple(0 for _ in shape))
    return pl.pallas_call(
        _merge_kernel,
        grid=(b, l // tm),
        in_specs=[tok(d, 0), tok(HG_DIM, 0), tok(HG_DIM, 0), tok(HG_DIM, 2 * d // HG_DIM),
                  tok(d, 0), tok(d, 1), tok(FNET_DIM, 0),
                  pl.BlockSpec((1, N_MOD, d), lambda i, j: (i, 0, 0)),
                  const((1, HG_DIM)), const((1, d)),
                  const((FNET_DIM, d)), const((HG_DIM, d)), const((d, d))],
        out_specs=(tok(d, 0), tok(d, 0)),
        out_shape=(jax.ShapeDtypeStruct((b, l, d), F32), jax.ShapeDtypeStruct((b, l, d), BF16)),
        compiler_params=_params("arbitrary", "arbitrary"),
        name="merge",
    )(x, o_f, o_b, gates, gates, gates, y_f, mod, og, n2g, w_a, w_b, w_out)


FFN_GROUP = 8


def _ffn_permute_tokens(h_ref, hp_ref, tmp_ref):
    n_tok = tmp_ref.shape[1]

    def body(g, carry):
        tok = pl.ds(pl.multiple_of(g * n_tok, n_tok), n_tok)
        hf = h_ref[0, tok, :].astype(F32)
        for lt in range(tmp_ref.shape[0]):
            for r in range(n_tok // GRID_W):
                for a in range(8):
                    r0 = r * GRID_W + 8 * a
                    tmp_ref[lt, pl.ds(r * GRID_W + a, 8, stride=8), :] = (
                        hf[r0:r0 + 8, lt * 128:(lt + 1) * 128])
        for lt in range(tmp_ref.shape[0]):
            hp_ref[tok, lt * 128:(lt + 1) * 128] = tmp_ref[lt].astype(BF16)
        return carry

    lax.fori_loop(0, h_ref.shape[1] // n_tok, body, 0)


def _ffn_up(hp_ref, up_refs, row0, n_rows):
    h = hp_ref[row0 * GRID_W:(row0 + n_rows) * GRID_W, :]
    return [_dot(h, up_ref[...]) for up_ref in up_refs]


def _pair(lo, hi):
    return jnp.concatenate([lo, hi], axis=0).astype(BF16)


def _ffn_row(z, r, taps, bias, first, last, sub):
    lo = [z[r * GRID_W + 8 * b:r * GRID_W + 8 * b + 8, 0:128] for b in range(8)]
    hi = [z[r * GRID_W + 8 * b:r * GRID_W + 8 * b + 8, 128:256] for b in range(8)]
    mid = [_pair(lo[b], hi[b]) for b in range(8)]
    down = lambda v: jnp.where(sub == 0, 0.0, pltpu.roll(v, 1, axis=0))
    left = [_pair(down(lo[7]), down(hi[7]))] + mid[0:7]
    up = lambda v: jnp.where(sub == 7, 0.0, pltpu.roll(v, 7, axis=0))
    right = mid[1:8] + [_pair(up(lo[0]), up(hi[0]))]
    term = lambda kh: [left[b] * taps[3 * kh] + mid[b] * taps[3 * kh + 1] + right[b] * taps[3 * kh + 2]
                       for b in range(8)]
    above = None if first else term(2)
    here = term(1)
    below = None if last else [t + bias for t in term(0)]
    return above, here, below


def _ffn_down(a_ref, dn_ref, y_ref, row0, n_rows):
    tok = slice(row0 * GRID_W, (row0 + n_rows) * GRID_W)
    part = _dot(a_ref[tok, :], dn_ref[...])
    for j in range(y_ref.shape[1]):
        y_ref[0, j, tok, :] += part[:, j * 128:(j + 1) * 128]


def _ffn_kernel(h_ref, up1_ref, up2_ref, cw1_ref, cw2_ref, cb1_ref, cb2_ref, dn_ref, y_ref,
                hp_ref, a_ref, wp_ref, tmp_ref, wup_ref, wdn_ref):
    t = pl.program_id(1)
    rows = hp_ref.shape[0] // GRID_W
    grp = FFN_GROUP

    @pl.when(t == 0)
    def _():
        _ffn_permute_tokens(h_ref, hp_ref, tmp_ref)
        y_ref[...] = jnp.zeros_like(y_ref)

    wup_ref[0] = up1_ref[...].astype(BF16)
    wup_ref[1] = up2_ref[...].astype(BF16)
    wdn_ref[...] = dn_ref[...].astype(BF16)
    up_refs = (wup_ref.at[0], wup_ref.at[1])

    for half, (cw_ref, cb_ref) in enumerate(((cw1_ref, cb1_ref), (cw2_ref, cb2_ref))):
        for k in range(10):
            row = cw_ref[k:k + 1, :] if k < 9 else cb_ref[...]
            wp_ref[half, k] = _pair(jnp.broadcast_to(row[:, 0:128], (8, 128)),
                                    jnp.broadcast_to(row[:, 128:256], (8, 128)))
    taps = [[wp_ref[half, k] for k in range(9)] for half in range(2)]
    bias = [wp_ref[half, 9] for half in range(2)]
    sub = lax.broadcasted_iota(jnp.int32, (8, 128), 0)

    def gate(c, out_row):
        act = [_silu(c[0][b]) * c[1][b] for b in range(8)]
        for b in range(0, 8, 2):
            tok_b = slice(out_row * GRID_W + 8 * b, out_row * GRID_W + 8 * b + 16)
            a_ref[tok_b, 0:128] = jnp.concatenate([act[b][0:8], act[b + 1][0:8]], axis=0)
            a_ref[tok_b, 128:256] = jnp.concatenate([act[b][8:16], act[b + 1][8:16]], axis=0)

    prev = [None, None]
    cur = [[bias[half]] * 8 for half in range(2)]
    z_next = _ffn_up(hp_ref, up_refs, 0, grp)
    for row0 in range(0, rows, grp):
        z_cur = z_next
        if row0 + grp < rows:
            z_next = _ffn_up(hp_ref, up_refs, row0 + grp, grp)
        for rr in range(grp):
            r = row0 + rr
            done = []
            for half in range(2):
                above, here, below = _ffn_row(z_cur[half], rr, taps[half], bias[half],
                                              r == 0, r == rows - 1, sub)
                if above is not None:
                    done.append([prev[half][b] + above[b] for b in range(8)])
                prev[half] = [cur[half][b] + here[b] for b in range(8)]
                cur[half] = below
            if done:
                gate(done, r - 1)
            if rr == 0 and row0 > 0:
                _ffn_down(a_ref, wdn_ref, y_ref, row0 - grp, grp)
    gate(prev, rows - 1)
    _ffn_down(a_ref, wdn_ref, y_ref, rows - grp, grp)


def _ffn(h2, up, conv_w, conv_b, down):
    b, l, d = h2.shape
    rows = l // GRID_W
    tc = FFN_TILE
    nt = D_FF // tc
    assert rows % FFN_GROUP == 0 and tc == 256, "the conv pairs the two lane tiles of a channel tile"
    return pl.pallas_call(
        _ffn_kernel,
        grid=(b, nt),
        in_specs=[pl.BlockSpec((1, l, d), lambda i, t: (i, 0, 0)),
                  pl.BlockSpec((d, tc), lambda i, t: (0, t)),
                  pl.BlockSpec((d, tc), lambda i, t: (0, nt + t)),
                  pl.BlockSpec((9, tc), lambda i, t: (0, t)),
                  pl.BlockSpec((9, tc), lambda i, t: (0, nt + t)),
                  pl.BlockSpec((1, tc), lambda i, t: (0, t)),
                  pl.BlockSpec((1, tc), lambda i, t: (0, nt + t)),
                  pl.BlockSpec((tc, d), lambda i, t: (t, 0))],
        out_specs=pl.BlockSpec((1, d // 128, l, 128), lambda i, t: (i, 0, 0, 0)),
        out_shape=jax.ShapeDtypeStruct((b, d // 128, l, 128), F32),
        scratch_shapes=[pltpu.VMEM((l, d), BF16),
                        pltpu.VMEM((l, tc), BF16),
                        pltpu.VMEM((2, 10, 16, 128), BF16),
                        pltpu.VMEM((d // 128, FFN_GROUP * GRID_W, 128), F32),
                        pltpu.VMEM((2, d, tc), BF16),
                        pltpu.VMEM((tc, d), BF16)],
        compiler_params=_params("arbitrary", "arbitrary"),
        name="ffn",
    )(h2, up, up, conv_w, conv_w, conv_b, conv_b, down)


def _final_kernel(x1_ref, y_ref, mod_ref, g_ref, o_ref):
    n_slab = y_ref.shape[1]
    for r in range(y_ref.shape[2] // GRID_W):
        for a in range(8):
            tok = slice(r * GRID_W + 8 * a, r * GRID_W + 8 * a + 8)
            y = jnp.concatenate([y_ref[0, j, pl.ds(r * GRID_W + a, 8, stride=8), :]
                                 for j in range(n_slab)], axis=-1)
            x = x1_ref[0, tok, :] + mod_ref[0, 5:6, :] * y
            ms = jnp.mean(x * x, axis=-1, keepdims=True)
            o_ref[0, tok, :] = x * lax.rsqrt(ms + EPS) * g_ref[...]


def _final(x1, y, mod, g, *, tm):
    b, l, d = x1.shape
    tok = pl.BlockSpec((1, tm, d), lambda i, j: (i, j, 0))
    return pl.pallas_call(
        _final_kernel,
        grid=(b, l // tm),
        in_specs=[tok, pl.BlockSpec((1, d // 128, tm, 128), lambda i, j: (i, 0, j, 0)),
                  pl.BlockSpec((1, N_MOD, d), lambda i, j: (i, 0, 0)),
                  pl.BlockSpec((1, d), lambda i, j: (0, 0))],
        out_specs=tok,
        out_shape=jax.ShapeDtypeStruct((b, l, d), F32),
        compiler_params=_params("arbitrary", "arbitrary"),
        name="final",
    )(x1, y, mod, g)


def kernel(x, c, ctx, c_ctx, ada_w, ada_b, norm1_g, w_in, hg_lb, hg_onorm_g, w_a, w_b, w_out,
           norm2_g, ffn_up, ffn_conv_w, ffn_conv_b, ffn_down, final_g):
    b, seq, d = x.shape
    assert ada_w.shape[0] == 1, "single-layer kernel"
    assert (d, seq % GRID_W) == (D_MODEL, 0)

    lb = jnp.cumsum(jax.nn.softmax(hg_lb.astype(F32), axis=0), axis=0)[0].reshape(1, HG_DIM)

    n_rows = 16
    rows = jnp.concatenate([c, c_ctx[None, :], jnp.zeros((n_rows - b - 1, d), F32)], axis=0)
    mod = _ada(rows, ada_w[0], ada_b[0])
    mod_x = mod[:b].reshape(b, N_MOD, d)
    mod_c = mod[b:b + 1].reshape(1, N_MOD, d)

    w_in_b = w_in[0].astype(BF16)
    g1 = norm1_g[0].reshape(1, d)
    hg_x, u_f, gates = _inproj(x, mod_x, g1, lb, w_in_b, tm=1024, full=True)
    (hg_c,) = _inproj(ctx.reshape(1, -1, d), mod_c, g1, lb, w_in_b, tm=1024, full=False)
    hg_c = hg_c.reshape(b, ctx.shape[1], 4 * HG_DIM)

    s_ctx = _hgrn(hg_c, None, emit_o=False)
    o_f, o_b = _hgrn(hg_x, s_ctx, emit_o=True)

    chan, pos_c, pos_s = (jnp.asarray(t).astype(BF16) for t in _dft_tables(seq))
    y_f = _fnet(u_f, chan, pos_c, pos_s)

    x1, h2 = _merge(x, o_f, o_b, gates, y_f, mod_x, hg_onorm_g[0].reshape(1, HG_DIM),
                    norm2_g[0].reshape(1, d), w_a[0].astype(BF16), w_b[0].astype(BF16),
                    w_out[0].astype(BF16), tm=1024)

    y = _ffn(h2, ffn_up[0], ffn_conv_w[0].reshape(9, 2 * D_FF),
             ffn_conv_b[0].reshape(1, 2 * D_FF), ffn_down[0])
    return _final(x1, y, mod_x, final_g.reshape(1, d), tm=1024)
```

```python
import functools

import numpy as np
import jax
import jax.numpy as jnp
from jax import lax
from jax.experimental import pallas as pl
from jax.experimental.pallas import tpu as pltpu

D_MODEL = 1024
GRID_W = 64
FNET_DIM = 512
FNET_GROUPS = 4
FNET_GROUP_DIM = FNET_DIM // FNET_GROUPS
HG_DIM = 512
HG_HEADS = 4
HG_HEAD_DIM = HG_DIM // HG_HEADS
D_FF = 2816
N_MOD = 6
EPS = 1e-6

COL_FNET, COL_Q, COL_FF, COL_FB, COL_I, COL_G = 0, 1, 2, 3, 4, 5

INPROJ_SUB = 256
MERGE_SUB = 256
HG_BLOCK = 1024
HG_CHUNK = 64
HG_UNROLL = 2
HG_SUB = 16
FFN_TILE = 256

BF16 = jnp.bfloat16
F32 = jnp.float32
VMEM_LIMIT = 56 * 1024 * 1024


def _sigmoid(x):
    return 1.0 / (1.0 + jnp.exp(-x))


def _silu(x):
    return x * _sigmoid(x)


def _dot(a, b):
    return jnp.dot(a, b, preferred_element_type=F32)


def _dot_nt(a, b):
    return lax.dot_general(a, b, (((1,), (1,)), ((), ())), preferred_element_type=F32)


def _params(*sem, flags=None):
    return pltpu.CompilerParams(dimension_semantics=sem, vmem_limit_bytes=VMEM_LIMIT, flags=flags)


def _ada_kernel(c_ref, w_ref, b_ref, o_ref):
    a = _silu(c_ref[...])
    o_ref[...] = _dot(a.astype(BF16), w_ref[...].astype(BF16)) + b_ref[...]


def _ada(rows, ada_w, ada_b):
    m, d = rows.shape
    n = ada_w.shape[1]
    tn = 1536
    return pl.pallas_call(
        _ada_kernel,
        grid=(n // tn,),
        in_specs=[pl.BlockSpec((m, d), lambda j: (0, 0)),
                  pl.BlockSpec((d, tn), lambda j: (0, j)),
                  pl.BlockSpec((1, tn), lambda j: (0, j))],
        out_specs=pl.BlockSpec((m, tn), lambda j: (0, j)),
        out_shape=jax.ShapeDtypeStruct((m, n), F32),
        compiler_params=_params("arbitrary"),
        name="ada",
    )(rows, ada_w, ada_b.reshape(1, n))


def _modulated_norm(x, g, shift, scale):
    ms = jnp.mean(x * x, axis=-1, keepdims=True)
    return (x * lax.rsqrt(ms + EPS) * g) * (1.0 + scale) + shift


def _inproj_kernel(x_ref, mod_ref, g_ref, lb_ref, w_ref, *o_refs, hg_col0, full):
    lb = lb_ref[...]
    hg_ref = o_refs[0]
    n_sub = x_ref.shape[1] // INPROJ_SUB

    def normed(i):
        rows = slice(i * INPROJ_SUB, (i + 1) * INPROJ_SUB)
        return _modulated_norm(x_ref[0, rows, :], g_ref[...], mod_ref[0, 0:1, :],
                               mod_ref[0, 1:2, :]).astype(BF16)

    hb_next = normed(0)
    for i in range(n_sub):
        hb = hb_next
        rows = slice(i * INPROJ_SUB, (i + 1) * INPROJ_SUB)

        def proj(col, width=HG_DIM):
            return _dot(hb, w_ref[:, col * HG_DIM:col * HG_DIM + width])

        hg_ref[0, rows, 0:HG_DIM] = _silu(proj(hg_col0))
        if i + 1 < n_sub:
            hb_next = normed(i + 1)
        for n in (1, 2):
            f = lb + (1.0 - lb) * _sigmoid(proj(hg_col0 + n))
            hg_ref[0, rows, n * HG_DIM:(n + 1) * HG_DIM] = jnp.log2(f)
        hg_ref[0, rows, 3 * HG_DIM:4 * HG_DIM] = proj(hg_col0 + 3)
        if full:
            uf_ref, gate_ref = o_refs[1], o_refs[2]
            uf_ref[0, rows, :] = proj(COL_FNET).astype(BF16)
            gate_ref[0, rows, 0:2 * D_MODEL] = _sigmoid(proj(COL_G + 1, 2 * D_MODEL)).astype(BF16)
            gate_ref[0, rows, 2 * D_MODEL:2 * D_MODEL + HG_DIM] = _silu(proj(COL_G)).astype(BF16)


def _inproj(x, mod, g, lb, w, *, tm, full):
    b, l, d = x.shape
    n = w.shape[1] if full else (COL_I + 1) * HG_DIM
    mod_map = (lambda i, j: (i, 0, 0)) if full else (lambda i, j: (0, 0, 0))
    tok = lambda width, dtype: (pl.BlockSpec((1, tm, width), lambda i, j: (i, j, 0)),
                                jax.ShapeDtypeStruct((b, l, width), dtype))
    outs = [tok(4 * HG_DIM, F32)]
    if full:
        outs += [tok(FNET_DIM, BF16), tok(HG_DIM + 2 * D_MODEL, BF16)]
    return pl.pallas_call(
        functools.partial(_inproj_kernel, hg_col0=COL_Q, full=full),
        grid=(b, l // tm),
        in_specs=[pl.BlockSpec((1, tm, d), lambda i, j: (i, j, 0)),
                  pl.BlockSpec((1, N_MOD, d), mod_map),
                  pl.BlockSpec((1, d), lambda i, j: (0, 0)),
                  pl.BlockSpec((1, HG_DIM), lambda i, j: (0, 0)),
                  pl.BlockSpec((d, n), lambda i, j: (0, 0))],
        out_specs=tuple(o[0] for o in outs),
        out_shape=tuple(o[1] for o in outs),
        compiler_params=_params("arbitrary", "arbitrary"),
        name="inproj_x" if full else "inproj_ctx",
    )(x, mod, g, lb, w)


def _hgrn_step(dirs, st_ref):
    c = HG_CHUNK
    hd = HG_HEAD_DIM
    r_idx = lax.broadcasted_iota(jnp.int32, (c, c), 0)
    c_idx = lax.broadcasted_iota(jnp.int32, (c, c), 1)
    pairs = [slice(p * 2 * hd, (p + 1) * 2 * hd) for p in range(HG_HEADS // 2)]

    def block_diag(a0, a1):
        z0, z1 = jnp.zeros(a0.shape, a0.dtype), jnp.zeros(a1.shape, a1.dtype)
        return jnp.concatenate([jnp.concatenate([a0, z0], axis=1),
                                jnp.concatenate([z1, a1], axis=1)], axis=0)

    cums = []
    for q, l2f, v, reverse, o_ref, row0, base in dirs:
        tri = jnp.where((c_idx >= r_idx) if reverse else (c_idx <= r_idx), 1.0, 0.0).astype(BF16)
        hi = l2f.astype(BF16)
        lo = (l2f - hi.astype(F32)).astype(BF16)
        cums.append(_dot(tri, hi) + _dot(tri, lo))

    state_in, state_upd, intra = [], [], []
    for (q, l2f, v, reverse, o_ref, row0, base), cum in zip(dirs, cums):
        k = 1.0 - jnp.exp2(l2f)
        last = 0 if reverse else c - 1
        total = cum[last:last + 1, :]
        q_in = (q * jnp.exp2(cum)).astype(BF16)
        k_out = (k * jnp.exp2(total - cum)).astype(BF16)
        e_tot = jnp.exp2(total)
        v_b = v.astype(BF16)
        for p, ps in enumerate(pairs):
            h0, h1 = slice(ps.start, ps.start + hd), slice(ps.start + hd, ps.stop)
            v_t = jnp.concatenate([v[:, h0], v[:, h1]], axis=0).T.astype(BF16)
            state_in.append(q_in[:, ps])
            state_upd.append((base + p, e_tot[:, ps], v_t, block_diag(k_out[:, h0], k_out[:, h1])))
            if o_ref is None:
                continue
            for i in range(c // HG_SUB):
                r0 = i * HG_SUB
                rows = slice(r0, r0 + HG_SUB)
                cols = slice(r0, c) if reverse else slice(0, r0 + HG_SUB)
                mid = r0 + HG_SUB // 2 if reverse else r0 + HG_SUB // 2 - 1
                m = cum[mid:mid + 1, ps]
                q_t = (q[rows, ps] * jnp.exp2(cum[rows, ps] - m)).astype(BF16)
                k_t = (k[cols, ps] * jnp.exp2(m - cum[cols, ps])).astype(BF16)
                intra.append((o_ref, row0, rows, cols, ps, reverse, len(state_in) - 1, q_t,
                              block_diag(k_t[:, :hd], k_t[:, hd:]),
                              block_diag(v_b[cols, h0], v_b[cols, h1])))

    scores = [_dot_nt(item[7], item[8]) for item in intra]
    kv = [_dot(v_t, k_bd) for _, _, v_t, k_bd in state_upd]
    states = {}
    inter = []
    for n, (idx, e_tot_p, _, _) in enumerate(state_upd):
        st = states[idx] if idx in states else st_ref[idx]
        if intra:
            st_b = st.astype(BF16)
            inter.append(_dot_nt(state_in[n], block_diag(st_b[:, :hd], st_b[:, hd:])))
        states[idx] = st * e_tot_p + kv[n]
    for idx, st in states.items():
        st_ref[idx] = st

    for (o_ref, row0, rows, cols, ps, reverse, chain, _, _, v_bd), a in zip(intra, scores):
        n_cols = cols.stop - cols.start
        row_g = rows.start + lax.broadcasted_iota(jnp.int32, (HG_SUB, 2 * n_cols), 0)
        lane = lax.broadcasted_iota(jnp.int32, (HG_SUB, 2 * n_cols), 1)
        col_g = cols.start + jnp.where(lane >= n_cols, lane - n_cols, lane)
        a = jnp.where((col_g >= row_g) if reverse else (col_g <= row_g), a, 0.0)
        o_ref[0, pl.ds(row0 + rows.start, HG_SUB), ps] = _dot(a.astype(BF16), v_bd) + inter[chain][rows]


def _hgrn_kernel(*refs, has_s0, emit_o):
    it = iter(refs)
    fwd_refs = [next(it) for _ in range(3)]
    bwd_refs = [next(it) for _ in range(3)]
    qf_ref = fwd_refs[0]
    s0_ref = next(it) if has_s0 else None
    of_ref = next(it) if emit_o else None
    ob_ref = next(it) if emit_o else None
    sout_ref = None if emit_o else next(it)
    st_ref = next(it)
    j = pl.program_id(1)
    n_inner = qf_ref.shape[1] // HG_CHUNK

    @pl.when(j == 0)
    def _():
        if has_s0:
            st_ref[...] = s0_ref[0]
        else:
            st_ref[...] = jnp.zeros_like(st_ref)

    span = HG_UNROLL * HG_CHUNK

    def body(it, carry):
        dirs = []
        for jj in range(HG_UNROLL):
            rf = pl.multiple_of(it * span, span) + jj * HG_CHUNK
            rb = pl.multiple_of((n_inner // HG_UNROLL - 1 - it) * span, span) + (HG_UNROLL - 1 - jj) * HG_CHUNK
            sf, sb = pl.ds(rf, HG_CHUNK), pl.ds(rb, HG_CHUNK)
            dirs.append(tuple(r[0, sf, :] for r in fwd_refs) + (False, of_ref, rf, 0))
            dirs.append(tuple(r[0, sb, :] for r in bwd_refs) + (True, ob_ref, rb, HG_HEADS // 2))
        _hgrn_step(dirs, st_ref)
        return carry

    lax.fori_loop(0, n_inner // HG_UNROLL, body, 0)

    if not emit_o:
        @pl.when(j == pl.num_programs(1) - 1)
        def _():
            sout_ref[0] = st_ref[...]


def _hgrn(u, s0, *, emit_o):
    b, l, _ = u.shape
    c = min(HG_BLOCK, l)
    n = l // c
    blk = (1, c, HG_DIM)
    fwd = lambda col: pl.BlockSpec(blk, lambda i, j: (i, j, col))
    bwd = lambda col: pl.BlockSpec(blk, lambda i, j: (i, n - 1 - j, col))
    st_dims = (HG_HEADS, HG_HEAD_DIM, 2 * HG_HEAD_DIM)
    state_spec = pl.BlockSpec((1,) + st_dims, lambda i, j: (i, 0, 0, 0))
    state_shape = jax.ShapeDtypeStruct((b,) + st_dims, F32)
    in_specs = [fwd(0), fwd(1), fwd(3), bwd(0), bwd(2), bwd(3)]
    args = [u, u, u, u, u, u]
    if s0 is not None:
        in_specs.append(state_spec)
        args.append(s0)
    if emit_o:
        o_shape = jax.ShapeDtypeStruct((b, l, HG_DIM), F32)
        out_shape = (o_shape, o_shape)
        out_specs = (pl.BlockSpec(blk, lambda i, j: (i, j, 0)),
                     pl.BlockSpec(blk, lambda i, j: (i, n - 1 - j, 0)))
    else:
        out_shape = state_shape
        out_specs = state_spec
    return pl.pallas_call(
        functools.partial(_hgrn_kernel, has_s0=s0 is not None, emit_o=emit_o),
        grid=(b, n),
        in_specs=in_specs,
        out_specs=out_specs,
        out_shape=out_shape,
        scratch_shapes=[pltpu.VMEM(st_dims, F32)],
        compiler_params=_params("arbitrary", "arbitrary"),
        name="hgrn_x" if emit_o else "hgrn_ctx",
    )(*args)


FNET_PAD = 16


def _dft_tables(seq):
    gd = FNET_GROUP_DIM
    kc = (np.arange(gd)[:, None] * np.arange(gd)[None, :]) % gd
    ang_c = 2.0 * np.pi * kc / gd
    scale = 1.0 / np.sqrt(float(seq) * gd)
    chan = np.concatenate([np.cos(ang_c), np.sin(ang_c)], axis=1) * scale
    half = seq // 2
    kl = (np.arange(half + FNET_PAD)[:, None] * np.arange(half)[None, :]) % seq
    ang_l = 2.0 * np.pi * kl / seq
    keep = (np.arange(half + FNET_PAD) <= half)[:, None]
    pos_c = np.where(keep, np.cos(ang_l), 0.0)
    pos_s = np.where(keep, -np.sin(ang_l), 0.0)
    return chan.astype(np.float32), pos_c.astype(np.float32), pos_s.astype(np.float32)


def _mirror_gather(width):
    sub = lax.broadcasted_iota(jnp.int32, (8, width), 0)
    return jnp.where(sub == 0, 0, 8 - sub), sub


def _fold_positions(v, sign):
    seq = v.shape[0]
    n_tiles = seq // 8
    gather, sub = _mirror_gather(v.shape[1])
    tile = lambda k: v[8 * k:8 * k + 8, :]
    out = []
    for k in range(n_tiles // 2):
        tail = jnp.take_along_axis(tile(n_tiles - 1 - k), gather, axis=0)
        if k == 0:
            mirror = jnp.where(sub == 0, 0.0, tail)
        else:
            mirror = jnp.where(sub == 0, jnp.take_along_axis(tile(n_tiles - k), gather, axis=0), tail)
        out.append(tile(k) + mirror if sign > 0 else tile(k) - mirror)
    return jnp.concatenate(out, axis=0)


def _fnet_kernel(u_ref, chan_ref, posc_ref, poss_ref, y_ref, pq_ref):
    seq = u_ref.shape[1]
    half = seq // 2
    gd = FNET_GROUP_DIM
    mids = []
    for g in range(FNET_GROUPS):
        gs = slice(g * gd, (g + 1) * gd)
        pq = _dot(u_ref[0, :, gs], chan_ref[...])
        pq_ref[0:half, gs] = _fold_positions(pq[:, :gd], 1).astype(BF16)
        pq_ref[half:seq, gs] = _fold_positions(pq[:, gd:], -1).astype(BF16)
        mids.append(pq[half:half + 1, :gd])
    gather, sub = _mirror_gather(FNET_DIM)
    mid = jnp.concatenate(mids, axis=1)
    mid_alt = jnp.where(sub % 2 == 0, mid, -mid)

    a = _dot(posc_ref[...], pq_ref[0:half, :])
    bm = _dot(poss_ref[...], pq_ref[half:seq, :])
    tile = lambda v, k: v[8 * k:8 * k + 8, :]
    n_t = half // 8
    for k in range(0, n_t, 2):
        pair = [tile(a, k + i) + tile(bm, k + i) + mid_alt for i in range(2)]
        y_ref[0, 8 * k:8 * k + 16, :] = jnp.concatenate(pair, axis=0).astype(BF16)
    flipped = [jnp.take_along_axis(tile(a, k) - tile(bm, k) + mid_alt, gather, axis=0) for k in range(n_t)]
    flipped.append(jnp.take_along_axis(tile(a, n_t) + tile(bm, n_t) + mid_alt, gather, axis=0))
    n_all = seq // 8
    for m in range(n_t, n_all, 2):
        pair = [jnp.where(sub == 0, flipped[n_all - mm], flipped[n_all - 1 - mm]) for mm in (m, m + 1)]
        y_ref[0, 8 * m:8 * m + 16, :] = jnp.concatenate(pair, axis=0).astype(BF16)


def _fnet(u, chan, pos_c, pos_s):
    b, seq, _ = u.shape
    table = pl.BlockSpec(pos_c.shape, lambda i: (0, 0))
    return pl.pallas_call(
        _fnet_kernel,
        grid=(b,),
        in_specs=[pl.BlockSpec((1, seq, FNET_DIM), lambda i: (i, 0, 0)),
                  pl.BlockSpec((FNET_GROUP_DIM, 2 * FNET_GROUP_DIM), lambda i: (0, 0)),
                  table, table],
        out_specs=pl.BlockSpec((1, seq, FNET_DIM), lambda i: (i, 0, 0)),
        out_shape=jax.ShapeDtypeStruct((b, seq, FNET_DIM), BF16),
        scratch_shapes=[pltpu.VMEM((seq, FNET_DIM), BF16)],
        compiler_params=_params("arbitrary"),
        name="fnet",
    )(u, chan, pos_c, pos_s)


def _merge_kernel(x_ref, of_ref, ob_ref, sg_ref, ga_ref, gb_ref, yf_ref, mod_ref, og_ref, n2g_ref,
                  wa_ref, wb_ref, wo_ref, x1_ref, h2_ref):
    n_sub = x_ref.shape[1] // MERGE_SUB

    def mix(rows):
        o = of_ref[0, rows, :] + ob_ref[0, rows, :]
        parts = []
        for h in range(HG_HEADS):
            oh = o[:, h * HG_HEAD_DIM:(h + 1) * HG_HEAD_DIM]
            parts.append(oh * lax.rsqrt(jnp.mean(oh * oh, axis=-1, keepdims=True) + EPS))
        on = jnp.concatenate(parts, axis=-1) * og_ref[...] * sg_ref[0, rows, :].astype(F32)
        y_b = _dot(on.astype(BF16), wb_ref[...])
        y_a = _dot(yf_ref[0, rows, :], wa_ref[...])
        m = ga_ref[0, rows, :].astype(F32) * y_a + gb_ref[0, rows, :].astype(F32) * y_b
        return _dot(m.astype(BF16), wo_ref[...])

    def finish(rows, yx):
        x1 = x_ref[0, rows, :] + mod_ref[0, 2:3, :] * yx
        x1_ref[0, rows, :] = x1
        h2 = _modulated_norm(x1, n2g_ref[...], mod_ref[0, 3:4, :], mod_ref[0, 4:5, :])
        h2_ref[0, rows, :] = h2.astype(BF16)

    pending = None
    for i in range(n_sub):
        rows = slice(i * MERGE_SUB, (i + 1) * MERGE_SUB)
        yx = mix(rows)
        if pending is not None:
            finish(*pending)
        pending = (rows, yx)
    finish(*pending)


def _merge(x, o_f, o_b, gates, y_f, mod, og, n2g, w_a, w_b, w_out, *, tm):
    b, l, d = x.shape
    tok = lambda w, col: pl.BlockSpec((1, tm, w), lambda i, j: (i, j, col))
    const = lambda shape: pl.BlockSpec(shape, lambda i, j: tuple(0 for _ in shape))
    return pl.pallas_call(
        _merge_kernel,
        grid=(b, l // tm),
        in_specs=[tok(d, 0), tok(HG_DIM, 0), tok(HG_DIM, 0), tok(HG_DIM, 2 * d // HG_DIM),
                  tok(d, 0), tok(d, 1), tok(FNET_DIM, 0),
                  pl.BlockSpec((1, N_MOD, d), lambda i, j: (i, 0, 0)),
                  const((1, HG_DIM)), const((1, d)),
                  const((FNET_DIM, d)), const((HG_DIM, d)), const((d, d))],
        out_specs=(tok(d, 0), tok(d, 0)),
        out_shape=(jax.ShapeDtypeStruct((b, l, d), F32), jax.ShapeDtypeStruct((b, l, d), BF16)),
        compiler_params=_params("arbitrary", "arbitrary"),
        name="merge",
    )(x, o_f, o_b, gates, gates, gates, y_f, mod, og, n2g, w_a, w_b, w_out)


FFN_GROUP = 8


def _ffn_permute_tokens(h_ref, hp_ref, tmp_ref):
    n_tok = tmp_ref.shape[1]

    def body(g, carry):
        tok = pl.ds(pl.multiple_of(g * n_tok, n_tok), n_tok)
        hf = h_ref[0, tok, :].astype(F32)
        for lt in range(tmp_ref.shape[0]):
            for r in range(n_tok // GRID_W):
                for a in range(8):
                    r0 = r * GRID_W + 8 * a
                    tmp_ref[lt, pl.ds(r * GRID_W + a, 8, stride=8), :] = (
                        hf[r0:r0 + 8, lt * 128:(lt + 1) * 128])
        for lt in range(tmp_ref.shape[0]):
            hp_ref[tok, lt * 128:(lt + 1) * 128] = tmp_ref[lt].astype(BF16)
        return carry

    lax.fori_loop(0, h_ref.shape[1] // n_tok, body, 0)


def _ffn_up(hp_ref, up_refs, row0, n_rows):
    h = hp_ref[row0 * GRID_W:(row0 + n_rows) * GRID_W, :]
    return [_dot(h, up_ref[...]) for up_ref in up_refs]


def _pair(lo, hi):
    return jnp.concatenate([lo, hi], axis=0).astype(BF16)


def _ffn_row(z, r, taps, bias, first, last, sub):
    lo = [z[r * GRID_W + 8 * b:r * GRID_W + 8 * b + 8, 0:128] for b in range(8)]
    hi = [z[r * GRID_W + 8 * b:r * GRID_W + 8 * b + 8, 128:256] for b in range(8)]
    mid = [_pair(lo[b], hi[b]) for b in range(8)]
    down = lambda v: jnp.where(sub == 0, 0.0, pltpu.roll(v, 1, axis=0))
    left = [_pair(down(lo[7]), down(hi[7]))] + mid[0:7]
    up = lambda v: jnp.where(sub == 7, 0.0, pltpu.roll(v, 7, axis=0))
    right = mid[1:8] + [_pair(up(lo[0]), up(hi[0]))]
    term = lambda kh: [left[b] * taps[3 * kh] + mid[b] * taps[3 * kh + 1] + right[b] * taps[3 * kh + 2]
                       for b in range(8)]
    above = None if first else term(2)
    here = term(1)
    below = None if last else [t + bias for t in term(0)]
    return above, here, below


def _ffn_down(a_ref, dn_ref, y_ref, row0, n_rows):
    tok = slice(row0 * GRID_W, (row0 + n_rows) * GRID_W)
    part = _dot(a_ref[tok, :].astype(BF16), dn_ref[...])
    for j in range(y_ref.shape[1]):
        y_ref[0, j, tok, :] += part[:, j * 128:(j + 1) * 128]


def _ffn_kernel(h_ref, up1_ref, up2_ref, cw1_ref, cw2_ref, cb1_ref, cb2_ref, dn_ref, y_ref,
                hp_ref, a_ref, wp_ref, tmp_ref, wup_ref, wdn_ref):
    t = pl.program_id(1)
    rows = hp_ref.shape[0] // GRID_W
    grp = FFN_GROUP

    @pl.when(t == 0)
    def _():
        _ffn_permute_tokens(h_ref, hp_ref, tmp_ref)
        y_ref[...] = jnp.zeros_like(y_ref)

    wup_ref[0] = up1_ref[...].astype(BF16)
    wup_ref[1] = up2_ref[...].astype(BF16)
    wdn_ref[...] = dn_ref[...].astype(BF16)
    up_refs = (wup_ref.at[0], wup_ref.at[1])

    for half, (cw_ref, cb_ref) in enumerate(((cw1_ref, cb1_ref), (cw2_ref, cb2_ref))):
        for k in range(10):
            row = cw_ref[k:k + 1, :] if k < 9 else cb_ref[...]
            wp_ref[half, k] = _pair(jnp.broadcast_to(row[:, 0:128], (8, 128)),
                                    jnp.broadcast_to(row[:, 128:256], (8, 128)))
    taps = [[wp_ref[half, k] for k in range(9)] for half in range(2)]
    bias = [wp_ref[half, 9] for half in range(2)]
    sub = lax.broadcasted_iota(jnp.int32, (8, 128), 0)

    def gate(c, out_row):
        for b in range(8):
            act = (_silu(c[0][b]) * c[1][b]).astype(F32)
            tok_b = slice(out_row * GRID_W + 8 * b, out_row * GRID_W + 8 * b + 8)
            a_ref[tok_b, 0:128] = act[0:8]
            a_ref[tok_b, 128:256] = act[8:16]

    prev = [None, None]
    cur = [[bias[half]] * 8 for half in range(2)]
    z_next = _ffn_up(hp_ref, up_refs, 0, grp)
    for row0 in range(0, rows, grp):
        z_cur = z_next
        if row0 + grp < rows:
            z_next = _ffn_up(hp_ref, up_refs, row0 + grp, grp)
        for rr in range(grp):
            r = row0 + rr
            done = []
            for half in range(2):
                above, here, below = _ffn_row(z_cur[half], rr, taps[half], bias[half],
                                              r == 0, r == rows - 1, sub)
                if above is not None:
                    done.append([prev[half][b] + above[b] for b in range(8)])
                prev[half] = [cur[half][b] + here[b] for b in range(8)]
                cur[half] = below
            if done:
                gate(done, r - 1)
            if rr == 0 and row0 > 0:
                _ffn_down(a_ref, wdn_ref, y_ref, row0 - grp, grp)
    gate(prev, rows - 1)
    _ffn_down(a_ref, wdn_ref, y_ref, rows - grp, grp)


def _ffn(h2, up, conv_w, conv_b, down):
    b, l, d = h2.shape
    rows = l // GRID_W
    tc = FFN_TILE
    nt = D_FF // tc
    assert rows % FFN_GROUP == 0 and tc == 256, "the conv pairs the two lane tiles of a channel tile"
    return pl.pallas_call(
        _ffn_kernel,
        grid=(b, nt),
        in_specs=[pl.BlockSpec((1, l, d), lambda i, t: (i, 0, 0)),
                  pl.BlockSpec((d, tc), lambda i, t: (0, t)),
                  pl.BlockSpec((d, tc), lambda i, t: (0, nt + t)),
                  pl.BlockSpec((9, tc), lambda i, t: (0, t)),
                  pl.BlockSpec((9, tc), lambda i, t: (0, nt + t)),
                  pl.BlockSpec((1, tc), lambda i, t: (0, t)),
                  pl.BlockSpec((1, tc), lambda i, t: (0, nt + t)),
                  pl.BlockSpec((tc, d), lambda i, t: (t, 0))],
        out_specs=pl.BlockSpec((1, d // 128, l, 128), lambda i, t: (i, 0, 0, 0)),
        out_shape=jax.ShapeDtypeStruct((b, d // 128, l, 128), F32),
        scratch_shapes=[pltpu.VMEM((l, d), BF16),
                        pltpu.VMEM((l, tc), F32),
                        pltpu.VMEM((2, 10, 16, 128), BF16),
                        pltpu.VMEM((d // 128, FFN_GROUP * GRID_W, 128), F32),
                        pltpu.VMEM((2, d, tc), BF16),
                        pltpu.VMEM((tc, d), BF16)],
        compiler_params=_params("arbitrary", "arbitrary"),
        name="ffn",
    )(h2, up, up, conv_w, conv_w, conv_b, conv_b, down)


def _final_kernel(x1_ref, y_ref, mod_ref, g_ref, o_ref):
    n_slab = y_ref.shape[1]
    for r in range(y_ref.shape[2] // GRID_W):
        for a in range(8):
            tok = slice(r * GRID_W + 8 * a, r * GRID_W + 8 * a + 8)
            y = jnp.concatenate([y_ref[0, j, pl.ds(r * GRID_W + a, 8, stride=8), :]
                                 for j in range(n_slab)], axis=-1)
            x = x1_ref[0, tok, :] + mod_ref[0, 5:6, :] * y
            ms = jnp.mean(x * x, axis=-1, keepdims=True)
            o_ref[0, tok, :] = x * lax.rsqrt(ms + EPS) * g_ref[...]


def _final(x1, y, mod, g, *, tm):
    b, l, d = x1.shape
    tok = pl.BlockSpec((1, tm, d), lambda i, j: (i, j, 0))
    return pl.pallas_call(
        _final_kernel,
        grid=(b, l // tm),
        in_specs=[tok, pl.BlockSpec((1, d // 128, tm, 128), lambda i, j: (i, 0, j, 0)),
                  pl.BlockSpec((1, N_MOD, d), lambda i, j: (i, 0, 0)),
                  pl.BlockSpec((1, d), lambda i, j: (0, 0))],
        out_specs=tok,
        out_shape=jax.ShapeDtypeStruct((b, l, d), F32),
        compiler_params=_params("arbitrary", "arbitrary"),
        name="final",
    )(x1, y, mod, g)


def kernel(x, c, ctx, c_ctx, ada_w, ada_b, norm1_g, w_in, hg_lb, hg_onorm_g, w_a, w_b, w_out,
           norm2_g, ffn_up, ffn_conv_w, ffn_conv_b, ffn_down, final_g):
    b, seq, d = x.shape
    assert ada_w.shape[0] == 1, "single-layer kernel"
    assert (d, seq % GRID_W) == (D_MODEL, 0)

    lb = jnp.cumsum(jax.nn.softmax(hg_lb.astype(F32), axis=0), axis=0)[0].reshape(1, HG_DIM)

    n_rows = 16
    rows = jnp.concatenate([c, c_ctx[None, :], jnp.zeros((n_rows - b - 1, d), F32)], axis=0)
    mod = _ada(rows, ada_w[0], ada_b[0])
    mod_x = mod[:b].reshape(b, N_MOD, d)
    mod_c = mod[b:b + 1].reshape(1, N_MOD, d)

    w_in_b = w_in[0].astype(BF16)
    g1 = norm1_g[0].reshape(1, d)
    hg_x, u_f, gates = _inproj(x, mod_x, g1, lb, w_in_b, tm=1024, full=True)
    (hg_c,) = _inproj(ctx.reshape(1, -1, d), mod_c, g1, lb, w_in_b, tm=512, full=False)
    hg_c = hg_c.reshape(b, ctx.shape[1], 4 * HG_DIM)

    s_ctx = _hgrn(hg_c, None, emit_o=False)
    o_f, o_b = _hgrn(hg_x, s_ctx, emit_o=True)

    chan, pos_c, pos_s = (jnp.asarray(t).astype(BF16) for t in _dft_tables(seq))
    y_f = _fnet(u_f, chan, pos_c, pos_s)

    x1, h2 = _merge(x, o_f, o_b, gates, y_f, mod_x, hg_onorm_g[0].reshape(1, HG_DIM),
                    norm2_g[0].reshape(1, d), w_a[0].astype(BF16), w_b[0].astype(BF16),
                    w_out[0].astype(BF16), tm=1024)

    y = _ffn(h2, ffn_up[0], ffn_conv_w[0].reshape(9, 2 * D_FF),
             ffn_conv_b[0].reshape(1, 2 * D_FF), ffn_down[0])
    return _final(x1, y, mod_x, final_g.reshape(1, d), tm=1024)
```

```python
import functools

import numpy as np
import jax
import jax.numpy as jnp
from jax import lax
from jax.experimental import pallas as pl
from jax.experimental.pallas import tpu as pltpu

D_MODEL = 1024
GRID_W = 64
FNET_DIM = 512
FNET_GROUPS = 4
FNET_GROUP_DIM = FNET_DIM // FNET_GROUPS
HG_DIM = 512
HG_HEADS = 4
HG_HEAD_DIM = HG_DIM // HG_HEADS
D_FF = 2816
N_MOD = 6
EPS = 1e-6

COL_FNET, COL_Q, COL_FF, COL_FB, COL_I, COL_G = 0, 1, 2, 3, 4, 5

INPROJ_SUB = 256
MERGE_SUB = 256
HG_BLOCK = 1024
HG_CHUNK = 64
HG_UNROLL = 4
HG_SUB = 16
FFN_TILE = 256

BF16 = jnp.bfloat16
F32 = jnp.float32
VMEM_LIMIT = 56 * 1024 * 1024


def _sigmoid(x):
    return 1.0 / (1.0 + jnp.exp(-x))


def _silu(x):
    return x * _sigmoid(x)


def _dot(a, b):
    return jnp.dot(a, b, preferred_element_type=F32)


def _dot_nt(a, b):
    return lax.dot_general(a, b, (((1,), (1,)), ((), ())), preferred_element_type=F32)


def _params(*sem, flags=None):
    return pltpu.CompilerParams(dimension_semantics=sem, vmem_limit_bytes=VMEM_LIMIT, flags=flags)


def _ada_kernel(c_ref, w_ref, b_ref, o_ref):
    a = _silu(c_ref[...])
    o_ref[...] = _dot(a.astype(BF16), w_ref[...].astype(BF16)) + b_ref[...]


def _ada(rows, ada_w, ada_b):
    m, d = rows.shape
    n = ada_w.shape[1]
    tn = 1536
    return pl.pallas_call(
        _ada_kernel,
        grid=(n // tn,),
        in_specs=[pl.BlockSpec((m, d), lambda j: (0, 0)),
                  pl.BlockSpec((d, tn), lambda j: (0, j)),
                  pl.BlockSpec((1, tn), lambda j: (0, j))],
        out_specs=pl.BlockSpec((m, tn), lambda j: (0, j)),
        out_shape=jax.ShapeDtypeStruct((m, n), F32),
        compiler_params=_params("arbitrary"),
        name="ada",
    )(rows, ada_w, ada_b.reshape(1, n))


def _modulated_norm(x, g, shift, scale):
    ms = jnp.mean(x * x, axis=-1, keepdims=True)
    return (x * lax.rsqrt(ms + EPS) * g) * (1.0 + scale) + shift


def _inproj_kernel(x_ref, mod_ref, g_ref, lb_ref, w_ref, *o_refs, hg_col0, full):
    lb = lb_ref[...]
    hg_ref = o_refs[0]
    n_sub = x_ref.shape[1] // INPROJ_SUB

    def normed(i):
        rows = slice(i * INPROJ_SUB, (i + 1) * INPROJ_SUB)
        return _modulated_norm(x_ref[0, rows, :], g_ref[...], mod_ref[0, 0:1, :],
                               mod_ref[0, 1:2, :]).astype(BF16)

    hb_next = normed(0)
    for i in range(n_sub):
        hb = hb_next
        rows = slice(i * INPROJ_SUB, (i + 1) * INPROJ_SUB)

        def proj(col, width=HG_DIM):
            return _dot(hb, w_ref[:, col * HG_DIM:col * HG_DIM + width])

        hg_ref[0, rows, 0:HG_DIM] = _silu(proj(hg_col0))
        if i + 1 < n_sub:
            hb_next = normed(i + 1)
        for n in (1, 2):
            f = lb + (1.0 - lb) * _sigmoid(proj(hg_col0 + n))
            hg_ref[0, rows, n * HG_DIM:(n + 1) * HG_DIM] = jnp.log2(f)
        hg_ref[0, rows, 3 * HG_DIM:4 * HG_DIM] = proj(hg_col0 + 3)
        if full:
            uf_ref, gate_ref = o_refs[1], o_refs[2]
            uf_ref[0, rows, :] = proj(COL_FNET).astype(BF16)
            gate_ref[0, rows, 0:2 * D_MODEL] = _sigmoid(proj(COL_G + 1, 2 * D_MODEL)).astype(BF16)
            gate_ref[0, rows, 2 * D_MODEL:2 * D_MODEL + HG_DIM] = _silu(proj(COL_G)).astype(BF16)


def _inproj(x, mod, g, lb, w, *, tm, full):
    b, l, d = x.shape
    n = w.shape[1] if full else (COL_I + 1) * HG_DIM
    mod_map = (lambda i, j: (i, 0, 0)) if full else (lambda i, j: (0, 0, 0))
    tok = lambda width, dtype: (pl.BlockSpec((1, tm, width), lambda i, j: (i, j, 0)),
                                jax.ShapeDtypeStruct((b, l, width), dtype))
    outs = [tok(4 * HG_DIM, F32)]
    if full:
        outs += [tok(FNET_DIM, BF16), tok(HG_DIM + 2 * D_MODEL, BF16)]
    return pl.pallas_call(
        functools.partial(_inproj_kernel, hg_col0=COL_Q, full=full),
        grid=(b, l // tm),
        in_specs=[pl.BlockSpec((1, tm, d), lambda i, j: (i, j, 0)),
                  pl.BlockSpec((1, N_MOD, d), mod_map),
                  pl.BlockSpec((1, d), lambda i, j: (0, 0)),
                  pl.BlockSpec((1, HG_DIM), lambda i, j: (0, 0)),
                  pl.BlockSpec((d, n), lambda i, j: (0, 0))],
        out_specs=tuple(o[0] for o in outs),
        out_shape=tuple(o[1] for o in outs),
        compiler_params=_params("arbitrary", "arbitrary"),
        name="inproj_x" if full else "inproj_ctx",
    )(x, mod, g, lb, w)


def _hgrn_step(dirs, st_ref):
    c = HG_CHUNK
    hd = HG_HEAD_DIM
    r_idx = lax.broadcasted_iota(jnp.int32, (c, c), 0)
    c_idx = lax.broadcasted_iota(jnp.int32, (c, c), 1)
    pairs = [slice(p * 2 * hd, (p + 1) * 2 * hd) for p in range(HG_HEADS // 2)]

    def block_diag(a0, a1):
        z0, z1 = jnp.zeros(a0.shape, a0.dtype), jnp.zeros(a1.shape, a1.dtype)
        return jnp.concatenate([jnp.concatenate([a0, z0], axis=1),
                                jnp.concatenate([z1, a1], axis=1)], axis=0)

    cums = []
    for q, l2f, v, reverse, o_ref, row0, base in dirs:
        tri = jnp.where((c_idx >= r_idx) if reverse else (c_idx <= r_idx), 1.0, 0.0).astype(BF16)
        hi = l2f.astype(BF16)
        lo = (l2f - hi.astype(F32)).astype(BF16)
        cums.append(_dot(tri, hi) + _dot(tri, lo))

    state_in, state_upd, intra = [], [], []
    for (q, l2f, v, reverse, o_ref, row0, base), cum in zip(dirs, cums):
        k = 1.0 - jnp.exp2(l2f)
        last = 0 if reverse else c - 1
        total = cum[last:last + 1, :]
        q_in = (q * jnp.exp2(cum)).astype(BF16)
        k_out = (k * jnp.exp2(total - cum)).astype(BF16)
        e_tot = jnp.exp2(total)
        v_b = v.astype(BF16)
        for p, ps in enumerate(pairs):
            h0, h1 = slice(ps.start, ps.start + hd), slice(ps.start + hd, ps.stop)
            v_t = jnp.concatenate([v[:, h0], v[:, h1]], axis=0).T.astype(BF16)
            state_in.append(q_in[:, ps])
            state_upd.append((base + p, e_tot[:, ps], v_t, block_diag(k_out[:, h0], k_out[:, h1])))
            if o_ref is None:
                continue
            for i in range(c // HG_SUB):
                r0 = i * HG_SUB
                rows = slice(r0, r0 + HG_SUB)
                cols = slice(r0, c) if reverse else slice(0, r0 + HG_SUB)
                mid = r0 + HG_SUB // 2 if reverse else r0 + HG_SUB // 2 - 1
                m = cum[mid:mid + 1, ps]
                q_t = (q[rows, ps] * jnp.exp2(cum[rows, ps] - m)).astype(BF16)
                k_t = (k[cols, ps] * jnp.exp2(m - cum[cols, ps])).astype(BF16)
                intra.append((o_ref, row0, rows, cols, ps, reverse, len(state_in) - 1, q_t,
                              block_diag(k_t[:, :hd], k_t[:, hd:]),
                              block_diag(v_b[cols, h0], v_b[cols, h1])))

    scores = [_dot_nt(item[7], item[8]) for item in intra]
    kv = [_dot(v_t, k_bd) for _, _, v_t, k_bd in state_upd]
    states = {}
    inter = []
    for n, (idx, e_tot_p, _, _) in enumerate(state_upd):
        st = states[idx] if idx in states else st_ref[idx]
        if intra:
            st_b = st.astype(BF16)
            inter.append(_dot_nt(state_in[n], block_diag(st_b[:, :hd], st_b[:, hd:])))
        states[idx] = st * e_tot_p + kv[n]
    for idx, st in states.items():
        st_ref[idx] = st

    for (o_ref, row0, rows, cols, ps, reverse, chain, _, _, v_bd), a in zip(intra, scores):
        n_cols = cols.stop - cols.start
        row_g = rows.start + lax.broadcasted_iota(jnp.int32, (HG_SUB, 2 * n_cols), 0)
        lane = lax.broadcasted_iota(jnp.int32, (HG_SUB, 2 * n_cols), 1)
        col_g = cols.start + jnp.where(lane >= n_cols, lane - n_cols, lane)
        a = jnp.where((col_g >= row_g) if reverse else (col_g <= row_g), a, 0.0)
        o_ref[0, pl.ds(row0 + rows.start, HG_SUB), ps] = _dot(a.astype(BF16), v_bd) + inter[chain][rows]


def _hgrn_kernel(*refs, has_s0, emit_o):
    it = iter(refs)
    fwd_refs = [next(it) for _ in range(3)]
    bwd_refs = [next(it) for _ in range(3)]
    qf_ref = fwd_refs[0]
    s0_ref = next(it) if has_s0 else None
    of_ref = next(it) if emit_o else None
    ob_ref = next(it) if emit_o else None
    sout_ref = None if emit_o else next(it)
    st_ref = next(it)
    j = pl.program_id(1)
    n_inner = qf_ref.shape[1] // HG_CHUNK

    @pl.when(j == 0)
    def _():
        if has_s0:
            st_ref[...] = s0_ref[0]
        else:
            st_ref[...] = jnp.zeros_like(st_ref)

    span = HG_UNROLL * HG_CHUNK

    def body(it, carry):
        dirs = []
        for jj in range(HG_UNROLL):
            rf = pl.multiple_of(it * span, span) + jj * HG_CHUNK
            rb = pl.multiple_of((n_inner // HG_UNROLL - 1 - it) * span, span) + (HG_UNROLL - 1 - jj) * HG_CHUNK
            sf, sb = pl.ds(rf, HG_CHUNK), pl.ds(rb, HG_CHUNK)
            dirs.append(tuple(r[0, sf, :] for r in fwd_refs) + (False, of_ref, rf, 0))
            dirs.append(tuple(r[0, sb, :] for r in bwd_refs) + (True, ob_ref, rb, HG_HEADS // 2))
        _hgrn_step(dirs, st_ref)
        return carry

    lax.fori_loop(0, n_inner // HG_UNROLL, body, 0)

    if not emit_o:
        @pl.when(j == pl.num_programs(1) - 1)
        def _():
            sout_ref[0] = st_ref[...]


def _hgrn(u, s0, *, emit_o):
    b, l, _ = u.shape
    c = min(HG_BLOCK, l)
    n = l // c
    blk = (1, c, HG_DIM)
    fwd = lambda col: pl.BlockSpec(blk, lambda i, j: (i, j, col))
    bwd = lambda col: pl.BlockSpec(blk, lambda i, j: (i, n - 1 - j, col))
    st_dims = (HG_HEADS, HG_HEAD_DIM, 2 * HG_HEAD_DIM)
    state_spec = pl.BlockSpec((1,) + st_dims, lambda i, j: (i, 0, 0, 0))
    state_shape = jax.ShapeDtypeStruct((b,) + st_dims, F32)
    in_specs = [fwd(0), fwd(1), fwd(3), bwd(0), bwd(2), bwd(3)]
    args = [u, u, u, u, u, u]
    if s0 is not None:
        in_specs.append(state_spec)
        args.append(s0)
    if emit_o:
        o_shape = jax.ShapeDtypeStruct((b, l, HG_DIM), F32)
        out_shape = (o_shape, o_shape)
        out_specs = (pl.BlockSpec(blk, lambda i, j: (i, j, 0)),
                     pl.BlockSpec(blk, lambda i, j: (i, n - 1 - j, 0)))
    else:
        out_shape = state_shape
        out_specs = state_spec
    return pl.pallas_call(
        functools.partial(_hgrn_kernel, has_s0=s0 is not None, emit_o=emit_o),
        grid=(b, n),
        in_specs=in_specs,
        out_specs=out_specs,
        out_shape=out_shape,
        scratch_shapes=[pltpu.VMEM(st_dims, F32)],
        compiler_params=_params("arbitrary", "arbitrary"),
        name="hgrn_x" if emit_o else "hgrn_ctx",
    )(*args)


FNET_PAD = 16


def _dft_tables(seq):
    gd = FNET_GROUP_DIM
    kc = (np.arange(gd)[:, None] * np.arange(gd)[None, :]) % gd
    ang_c = 2.0 * np.pi * kc / gd
    scale = 1.0 / np.sqrt(float(seq) * gd)
    chan = np.concatenate([np.cos(ang_c), np.sin(ang_c)], axis=1) * scale
    half = seq // 2
    kl = (np.arange(half + FNET_PAD)[:, None] * np.arange(half)[None, :]) % seq
    ang_l = 2.0 * np.pi * kl / seq
    keep = (np.arange(half + FNET_PAD) <= half)[:, None]
    pos_c = np.where(keep, np.cos(ang_l), 0.0)
    pos_s = np.where(keep, -np.sin(ang_l), 0.0)
    return chan.astype(np.float32), pos_c.astype(np.float32), pos_s.astype(np.float32)


def _mirror_gather(width):
    sub = lax.broadcasted_iota(jnp.int32, (8, width), 0)
    return jnp.where(sub == 0, 0, 8 - sub), sub


def _fold_positions(v, sign):
    seq = v.shape[0]
    n_tiles = seq // 8
    gather, sub = _mirror_gather(v.shape[1])
    tile = lambda k: v[8 * k:8 * k + 8, :]
    out = []
    for k in range(n_tiles // 2):
        tail = jnp.take_along_axis(tile(n_tiles - 1 - k), gather, axis=0)
        if k == 0:
            mirror = jnp.where(sub == 0, 0.0, tail)
        else:
            mirror = jnp.where(sub == 0, jnp.take_along_axis(tile(n_tiles - k), gather, axis=0), tail)
        out.append(tile(k) + mirror if sign > 0 else tile(k) - mirror)
    return jnp.concatenate(out, axis=0)


def _fnet_kernel(u_ref, chan_ref, posc_ref, poss_ref, y_ref, pq_ref):
    seq = u_ref.shape[1]
    half = seq // 2
    gd = FNET_GROUP_DIM
    mids = []
    for g in range(FNET_GROUPS):
        gs = slice(g * gd, (g + 1) * gd)
        pq = _dot(u_ref[0, :, gs], chan_ref[...])
        pq_ref[0:half, gs] = _fold_positions(pq[:, :gd], 1).astype(BF16)
        pq_ref[half:seq, gs] = _fold_positions(pq[:, gd:], -1).astype(BF16)
        mids.append(pq[half:half + 1, :gd])
    gather, sub = _mirror_gather(FNET_DIM)
    mid = jnp.concatenate(mids, axis=1)
    mid_alt = jnp.where(sub % 2 == 0, mid, -mid)

    a = _dot(posc_ref[...], pq_ref[0:half, :])
    bm = _dot(poss_ref[...], pq_ref[half:seq, :])
    tile = lambda v, k: v[8 * k:8 * k + 8, :]
    n_t = half // 8
    for k in range(0, n_t, 2):
        pair = [tile(a, k + i) + tile(bm, k + i) + mid_alt for i in range(2)]
        y_ref[0, 8 * k:8 * k + 16, :] = jnp.concatenate(pair, axis=0).astype(BF16)
    flipped = [jnp.take_along_axis(tile(a, k) - tile(bm, k) + mid_alt, gather, axis=0) for k in range(n_t)]
    flipped.append(jnp.take_along_axis(tile(a, n_t) + tile(bm, n_t) + mid_alt, gather, axis=0))
    n_all = seq // 8
    for m in range(n_t, n_all, 2):
        pair = [jnp.where(sub == 0, flipped[n_all - mm], flipped[n_all - 1 - mm]) for mm in (m, m + 1)]
        y_ref[0, 8 * m:8 * m + 16, :] = jnp.concatenate(pair, axis=0).astype(BF16)


def _fnet(u, chan, pos_c, pos_s):
    b, seq, _ = u.shape
    table = pl.BlockSpec(pos_c.shape, lambda i: (0, 0))
    return pl.pallas_call(
        _fnet_kernel,
        grid=(b,),
        in_specs=[pl.BlockSpec((1, seq, FNET_DIM), lambda i: (i, 0, 0)),
                  pl.BlockSpec((FNET_GROUP_DIM, 2 * FNET_GROUP_DIM), lambda i: (0, 0)),
                  table, table],
        out_specs=pl.BlockSpec((1, seq, FNET_DIM), lambda i: (i, 0, 0)),
        out_shape=jax.ShapeDtypeStruct((b, seq, FNET_DIM), BF16),
        scratch_shapes=[pltpu.VMEM((seq, FNET_DIM), BF16)],
        compiler_params=_params("arbitrary"),
        name="fnet",
    )(u, chan, pos_c, pos_s)


def _merge_kernel(x_ref, of_ref, ob_ref, sg_ref, ga_ref, gb_ref, yf_ref, mod_ref, og_ref, n2g_ref,
                  wa_ref, wb_ref, wo_ref, x1_ref, h2_ref):
    n_sub = x_ref.shape[1] // MERGE_SUB

    def mix(rows):
        o = of_ref[0, rows, :] + ob_ref[0, rows, :]
        parts = []
        for h in range(HG_HEADS):
            oh = o[:, h * HG_HEAD_DIM:(h + 1) * HG_HEAD_DIM]
            parts.append(oh * lax.rsqrt(jnp.mean(oh * oh, axis=-1, keepdims=True) + EPS))
        on = jnp.concatenate(parts, axis=-1) * og_ref[...] * sg_ref[0, rows, :].astype(F32)
        y_b = _dot(on.astype(BF16), wb_ref[...])
        y_a = _dot(yf_ref[0, rows, :], wa_ref[...])
        m = ga_ref[0, rows, :].astype(F32) * y_a + gb_ref[0, rows, :].astype(F32) * y_b
        return _dot(m.astype(BF16), wo_ref[...])

    def finish(rows, yx):
        x1 = x_ref[0, rows, :] + mod_ref[0, 2:3, :] * yx
        x1_ref[0, rows, :] = x1
        h2 = _modulated_norm(x1, n2g_ref[...], mod_ref[0, 3:4, :], mod_ref[0, 4:5, :])
        h2_ref[0, rows, :] = h2.astype(BF16)

    pending = None
    for i in range(n_sub):
        rows = slice(i * MERGE_SUB, (i + 1) * MERGE_SUB)
        yx = mix(rows)
        if pending is not None:
            finish(*pending)
        pending = (rows, yx)
    finish(*pending)


def _merge(x, o_f, o_b, gates, y_f, mod, og, n2g, w_a, w_b, w_out, *, tm):
    b, l, d = x.shape
    tok = lambda w, col: pl.BlockSpec((1, tm, w), lambda i, j: (i, j, col))
    const = lambda shape: pl.BlockSpec(shape, lambda i, j: tuple(0 for _ in shape))
    return pl.pallas_call(
        _merge_kernel,
        grid=(b, l // tm),
        in_specs=[tok(d, 0), tok(HG_DIM, 0), tok(HG_DIM, 0), tok(HG_DIM, 2 * d // HG_DIM),
                  tok(d, 0), tok(d, 1), tok(FNET_DIM, 0),
                  pl.BlockSpec((1, N_MOD, d), lambda i, j: (i, 0, 0)),
                  const((1, HG_DIM)), const((1, d)),
                  const((FNET_DIM, d)), const((HG_DIM, d)), const((d, d))],
        out_specs=(tok(d, 0), tok(d, 0)),
        out_shape=(jax.ShapeDtypeStruct((b, l, d), F32), jax.ShapeDtypeStruct((b, l, d), BF16)),
        compiler_params=_params("arbitrary", "arbitrary"),
        name="merge",
    )(x, o_f, o_b, gates, gates, gates, y_f, mod, og, n2g, w_a, w_b, w_out)


FFN_GROUP = 8


def _ffn_permute_tokens(h_ref, hp_ref, tmp_ref):
    n_tok = tmp_ref.shape[1]

    def body(g, carry):
        tok = pl.ds(pl.multiple_of(g * n_tok, n_tok), n_tok)
        hf = h_ref[0, tok, :].astype(F32)
        for lt in range(tmp_ref.shape[0]):
            for r in range(n_tok // GRID_W):
                for a in range(8):
                    r0 = r * GRID_W + 8 * a
                    tmp_ref[lt, pl.ds(r * GRID_W + a, 8, stride=8), :] = (
                        hf[r0:r0 + 8, lt * 128:(lt + 1) * 128])
        for lt in range(tmp_ref.shape[0]):
            hp_ref[tok, lt * 128:(lt + 1) * 128] = tmp_ref[lt].astype(BF16)
        return carry

    lax.fori_loop(0, h_ref.shape[1] // n_tok, body, 0)


def _ffn_up(hp_ref, up_refs, row0, n_rows):
    h = hp_ref[row0 * GRID_W:(row0 + n_rows) * GRID_W, :]
    return [_dot(h, up_ref[...]) for up_ref in up_refs]


def _pair(lo, hi):
    return jnp.concatenate([lo, hi], axis=0).astype(BF16)


def _ffn_row(z, r, taps, bias, first, last, sub):
    lo = [z[r * GRID_W + 8 * b:r * GRID_W + 8 * b + 8, 0:128] for b in range(8)]
    hi = [z[r * GRID_W + 8 * b:r * GRID_W + 8 * b + 8, 128:256] for b in range(8)]
    mid = [_pair(lo[b], hi[b]) for b in range(8)]
    down = lambda v: jnp.where(sub == 0, 0.0, pltpu.roll(v, 1, axis=0))
    left = [_pair(down(lo[7]), down(hi[7]))] + mid[0:7]
    up = lambda v: jnp.where(sub == 7, 0.0, pltpu.roll(v, 7, axis=0))
    right = mid[1:8] + [_pair(up(lo[0]), up(hi[0]))]
    term = lambda kh: [left[b] * taps[3 * kh] + mid[b] * taps[3 * kh + 1] + right[b] * taps[3 * kh + 2]
                       for b in range(8)]
    above = None if first else term(2)
    here = term(1)
    below = None if last else [t + bias for t in term(0)]
    return above, here, below


def _ffn_down(a_ref, dn_ref, y_ref, row0, n_rows):
    tok = slice(row0 * GRID_W, (row0 + n_rows) * GRID_W)
    part = _dot(a_ref[tok, :].astype(BF16), dn_ref[...])
    for j in range(y_ref.shape[1]):
        y_ref[0, j, tok, :] += part[:, j * 128:(j + 1) * 128]


def _ffn_kernel(h_ref, up1_ref, up2_ref, cw1_ref, cw2_ref, cb1_ref, cb2_ref, dn_ref, y_ref,
                hp_ref, a_ref, wp_ref, tmp_ref, wup_ref, wdn_ref):
    t = pl.program_id(1)
    rows = hp_ref.shape[0] // GRID_W
    grp = FFN_GROUP

    @pl.when(t == 0)
    def _():
        _ffn_permute_tokens(h_ref, hp_ref, tmp_ref)
        y_ref[...] = jnp.zeros_like(y_ref)

    wup_ref[0] = up1_ref[...].astype(BF16)
    wup_ref[1] = up2_ref[...].astype(BF16)
    wdn_ref[...] = dn_ref[...].astype(BF16)
    up_refs = (wup_ref.at[0], wup_ref.at[1])

    for half, (cw_ref, cb_ref) in enumerate(((cw1_ref, cb1_ref), (cw2_ref, cb2_ref))):
        for k in range(10):
            row = cw_ref[k:k + 1, :] if k < 9 else cb_ref[...]
            wp_ref[half, k] = _pair(jnp.broadcast_to(row[:, 0:128], (8, 128)),
                                    jnp.broadcast_to(row[:, 128:256], (8, 128)))
    taps = [[wp_ref[half, k] for k in range(9)] for half in range(2)]
    bias = [wp_ref[half, 9] for half in range(2)]
    sub = lax.broadcasted_iota(jnp.int32, (8, 128), 0)

    def gate(c, out_row):
        for b in range(8):
            act = (_silu(c[0][b]) * c[1][b]).astype(F32)
            tok_b = slice(out_row * GRID_W + 8 * b, out_row * GRID_W + 8 * b + 8)
            a_ref[tok_b, 0:128] = act[0:8]
            a_ref[tok_b, 128:256] = act[8:16]

    prev = [None, None]
    cur = [[bias[half]] * 8 for half in range(2)]
    z_next = _ffn_up(hp_ref, up_refs, 0, grp)
    for row0 in range(0, rows, grp):
        z_cur = z_next
        if row0 + grp < rows:
            z_next = _ffn_up(hp_ref, up_refs, row0 + grp, grp)
        for rr in range(grp):
            r = row0 + rr
            done = []
            for half in range(2):
                above, here, below = _ffn_row(z_cur[half], rr, taps[half], bias[half],
                                              r == 0, r == rows - 1, sub)
                if above is not None:
                    done.append([prev[half][b] + above[b] for b in range(8)])
                prev[half] = [cur[half][b] + here[b] for b in range(8)]
                cur[half] = below
            if done:
                gate(done, r - 1)
            if rr == 0 and row0 > 0:
                _ffn_down(a_ref, wdn_ref, y_ref, row0 - grp, grp)
    gate(prev, rows - 1)
    _ffn_down(a_ref, wdn_ref, y_ref, rows - grp, grp)


def _ffn(h2, up, conv_w, conv_b, down):
    b, l, d = h2.shape
    rows = l // GRID_W
    tc = FFN_TILE
    nt = D_FF // tc
    assert rows % FFN_GROUP == 0 and tc == 256, "the conv pairs the two lane tiles of a channel tile"
    return pl.pallas_call(
        _ffn_kernel,
        grid=(b, nt),
        in_specs=[pl.BlockSpec((1, l, d), lambda i, t: (i, 0, 0)),
                  pl.BlockSpec((d, tc), lambda i, t: (0, t)),
                  pl.BlockSpec((d, tc), lambda i, t: (0, nt + t)),
                  pl.BlockSpec((9, tc), lambda i, t: (0, t)),
                  pl.BlockSpec((9, tc), lambda i, t: (0, nt + t)),
                  pl.BlockSpec((1, tc), lambda i, t: (0, t)),
                  pl.BlockSpec((1, tc), lambda i, t: (0, nt + t)),
                  pl.BlockSpec((tc, d), lambda i, t: (t, 0))],
        out_specs=pl.BlockSpec((1, d // 128, l, 128), lambda i, t: (i, 0, 0, 0)),
        out_shape=jax.ShapeDtypeStruct((b, d // 128, l, 128), F32),
        scratch_shapes=[pltpu.VMEM((l, d), BF16),
                        pltpu.VMEM((l, tc), F32),
                        pltpu.VMEM((2, 10, 16, 128), BF16),
                        pltpu.VMEM((d // 128, FFN_GROUP * GRID_W, 128), F32),
                        pltpu.VMEM((2, d, tc), BF16),
                        pltpu.VMEM((tc, d), BF16)],
        compiler_params=_params("arbitrary", "arbitrary"),
        name="ffn",
    )(h2, up, up, conv_w, conv_w, conv_b, conv_b, down)


def _final_kernel(x1_ref, y_ref, mod_ref, g_ref, o_ref):
    n_slab = y_ref.shape[1]
    for r in range(y_ref.shape[2] // GRID_W):
        for a in range(8):
            tok = slice(r * GRID_W + 8 * a, r * GRID_W + 8 * a + 8)
            y = jnp.concatenate([y_ref[0, j, pl.ds(r * GRID_W + a, 8, stride=8), :]
                                 for j in range(n_slab)], axis=-1)
            x = x1_ref[0, tok, :] + mod_ref[0, 5:6, :] * y
            ms = jnp.mean(x * x, axis=-1, keepdims=True)
            o_ref[0, tok, :] = x * lax.rsqrt(ms + EPS) * g_ref[...]


def _final(x1, y, mod, g, *, tm):
    b, l, d = x1.shape
    tok = pl.BlockSpec((1, tm, d), lambda i, j: (i, j, 0))
    return pl.pallas_call(
        _final_kernel,
        grid=(b, l // tm),
        in_specs=[tok, pl.BlockSpec((1, d // 128, tm, 128), lambda i, j: (i, 0, j, 0)),
                  pl.BlockSpec((1, N_MOD, d), lambda i, j: (i, 0, 0)),
                  pl.BlockSpec((1, d), lambda i, j: (0, 0))],
        out_specs=tok,
        out_shape=jax.ShapeDtypeStruct((b, l, d), F32),
        compiler_params=_params("arbitrary", "arbitrary"),
        name="final",
    )(x1, y, mod, g)


def kernel(x, c, ctx, c_ctx, ada_w, ada_b, norm1_g, w_in, hg_lb, hg_onorm_g, w_a, w_b, w_out,
           norm2_g, ffn_up, ffn_conv_w, ffn_conv_b, ffn_down, final_g):
    b, seq, d = x.shape
    assert ada_w.shape[0] == 1, "single-layer kernel"
    assert (d, seq % GRID_W) == (D_MODEL, 0)

    lb = jnp.cumsum(jax.nn.softmax(hg_lb.astype(F32), axis=0), axis=0)[0].reshape(1, HG_DIM)

    n_rows = 16
    rows = jnp.concatenate([c, c_ctx[None, :], jnp.zeros((n_rows - b - 1, d), F32)], axis=0)
    mod = _ada(rows, ada_w[0], ada_b[0])
    mod_x = mod[:b].reshape(b, N_MOD, d)
    mod_c = mod[b:b + 1].reshape(1, N_MOD, d)

    w_in_b = w_in[0].astype(BF16)
    g1 = norm1_g[0].reshape(1, d)
    hg_x, u_f, gates = _inproj(x, mod_x, g1, lb, w_in_b, tm=1024, full=True)
    (hg_c,) = _inproj(ctx.reshape(1, -1, d), mod_c, g1, lb, w_in_b, tm=1024, full=False)
    hg_c = hg_c.reshape(b, ctx.shape[1], 4 * HG_DIM)

    s_ctx = _hgrn(hg_c, None, emit_o=False)
    o_f, o_b = _hgrn(hg_x, s_ctx, emit_o=True)

    chan, pos_c, pos_s = (jnp.asarray(t).astype(BF16) for t in _dft_tables(seq))
    y_f = _fnet(u_f, chan, pos_c, pos_s)

    x1, h2 = _merge(x, o_f, o_b, gates, y_f, mod_x, hg_onorm_g[0].reshape(1, HG_DIM),
                    norm2_g[0].reshape(1, d), w_a[0].astype(BF16), w_b[0].astype(BF16),
                    w_out[0].astype(BF16), tm=1024)

    y = _ffn(h2, ffn_up[0], ffn_conv_w[0].reshape(9, 2 * D_FF),
             ffn_conv_b[0].reshape(1, 2 * D_FF), ffn_down[0])
    return _final(x1, y, mod_x, final_g.reshape(1, d), tm=1024)
```

```python
import functools

import numpy as np
import jax
import jax.numpy as jnp
from jax import lax
from jax.experimental import pallas as pl
from jax.experimental.pallas import tpu as pltpu

D_MODEL = 1024
GRID_W = 64
FNET_DIM = 512
FNET_GROUPS = 4
FNET_GROUP_DIM = FNET_DIM // FNET_GROUPS
HG_DIM = 512
HG_HEADS = 4
HG_HEAD_DIM = HG_DIM // HG_HEADS
D_FF = 2816
N_MOD = 6
EPS = 1e-6

COL_FNET, COL_Q, COL_FF, COL_FB, COL_I, COL_G = 0, 1, 2, 3, 4, 5

INPROJ_SUB = 256
MERGE_SUB = 256
HG_BLOCK = 1024
HG_CHUNK = 64
HG_UNROLL = 4
HG_SUB = 16
FFN_TILE = 256

BF16 = jnp.bfloat16
F32 = jnp.float32
VMEM_LIMIT = 56 * 1024 * 1024


def _sigmoid(x):
    return 1.0 / (1.0 + jnp.exp(-x))


def _silu(x):
    return x * _sigmoid(x)


def _dot(a, b):
    return jnp.dot(a, b, preferred_element_type=F32)


def _dot_nt(a, b):
    return lax.dot_general(a, b, (((1,), (1,)), ((), ())), preferred_element_type=F32)


def _params(*sem, flags=None):
    return pltpu.CompilerParams(dimension_semantics=sem, vmem_limit_bytes=VMEM_LIMIT, flags=flags)


def _ada_kernel(c_ref, w_ref, b_ref, o_ref):
    a = _silu(c_ref[...])
    o_ref[...] = _dot(a.astype(BF16), w_ref[...].astype(BF16)) + b_ref[...]


def _ada(rows, ada_w, ada_b):
    m, d = rows.shape
    n = ada_w.shape[1]
    tn = 1536
    return pl.pallas_call(
        _ada_kernel,
        grid=(n // tn,),
        in_specs=[pl.BlockSpec((m, d), lambda j: (0, 0)),
                  pl.BlockSpec((d, tn), lambda j: (0, j)),
                  pl.BlockSpec((1, tn), lambda j: (0, j))],
        out_specs=pl.BlockSpec((m, tn), lambda j: (0, j)),
        out_shape=jax.ShapeDtypeStruct((m, n), F32),
        compiler_params=_params("arbitrary"),
        name="ada",
    )(rows, ada_w, ada_b.reshape(1, n))


def _modulated_norm(x, g, shift, scale):
    ms = jnp.mean(x * x, axis=-1, keepdims=True)
    return (x * lax.rsqrt(ms + EPS) * g) * (1.0 + scale) + shift


def _inproj_kernel(x_ref, mod_ref, g_ref, lb_ref, w_ref, *o_refs, hg_col0, full):
    lb = lb_ref[...]
    hg_ref = o_refs[0]
    n_sub = x_ref.shape[1] // INPROJ_SUB

    def normed(i):
        rows = slice(i * INPROJ_SUB, (i + 1) * INPROJ_SUB)
        return _modulated_norm(x_ref[0, rows, :], g_ref[...], mod_ref[0, 0:1, :],
                               mod_ref[0, 1:2, :]).astype(BF16)

    hb_next = normed(0)
    for i in range(n_sub):
        hb = hb_next
        rows = slice(i * INPROJ_SUB, (i + 1) * INPROJ_SUB)

        def proj(col, width=HG_DIM):
            return _dot(hb, w_ref[:, col * HG_DIM:col * HG_DIM + width])

        hg_ref[0, rows, 0:HG_DIM] = _silu(proj(hg_col0))
        if i + 1 < n_sub:
            hb_next = normed(i + 1)
        for n in (1, 2):
            f = lb + (1.0 - lb) * _sigmoid(proj(hg_col0 + n))
            hg_ref[0, rows, n * HG_DIM:(n + 1) * HG_DIM] = jnp.log2(f)
        hg_ref[0, rows, 3 * HG_DIM:4 * HG_DIM] = proj(hg_col0 + 3)
        if full:
            uf_ref, gate_ref = o_refs[1], o_refs[2]
            uf_ref[0, rows, :] = proj(COL_FNET).astype(BF16)
            gate_ref[0, rows, 0:2 * D_MODEL] = _sigmoid(proj(COL_G + 1, 2 * D_MODEL)).astype(BF16)
            gate_ref[0, rows, 2 * D_MODEL:2 * D_MODEL + HG_DIM] = _silu(proj(COL_G)).astype(BF16)


def _inproj(x, mod, g, lb, w, *, tm, full):
    b, l, d = x.shape
    n = w.shape[1] if full else (COL_I + 1) * HG_DIM
    mod_map = (lambda i, j: (i, 0, 0)) if full else (lambda i, j: (0, 0, 0))
    tok = lambda width, dtype: (pl.BlockSpec((1, tm, width), lambda i, j: (i, j, 0)),
                                jax.ShapeDtypeStruct((b, l, width), dtype))
    outs = [tok(4 * HG_DIM, F32)]
    if full:
        outs += [tok(FNET_DIM, BF16), tok(HG_DIM + 2 * D_MODEL, BF16)]
    return pl.pallas_call(
        functools.partial(_inproj_kernel, hg_col0=COL_Q, full=full),
        grid=(b, l // tm),
        in_specs=[pl.BlockSpec((1, tm, d), lambda i, j: (i, j, 0)),
                  pl.BlockSpec((1, N_MOD, d), mod_map),
                  pl.BlockSpec((1, d), lambda i, j: (0, 0)),
                  pl.BlockSpec((1, HG_DIM), lambda i, j: (0, 0)),
                  pl.BlockSpec((d, n), lambda i, j: (0, 0))],
        out_specs=tuple(o[0] for o in outs),
        out_shape=tuple(o[1] for o in outs),
        compiler_params=_params("arbitrary", "arbitrary"),
        name="inproj_x" if full else "inproj_ctx",
    )(x, mod, g, lb, w)


def _hgrn_step(dirs, st_ref):
    c = HG_CHUNK
    hd = HG_HEAD_DIM
    r_idx = lax.broadcasted_iota(jnp.int32, (c, c), 0)
    c_idx = lax.broadcasted_iota(jnp.int32, (c, c), 1)
    pairs = [slice(p * 2 * hd, (p + 1) * 2 * hd) for p in range(HG_HEADS // 2)]

    def block_diag(a0, a1):
        z0, z1 = jnp.zeros(a0.shape, a0.dtype), jnp.zeros(a1.shape, a1.dtype)
        return jnp.concatenate([jnp.concatenate([a0, z0], axis=1),
                                jnp.concatenate([z1, a1], axis=1)], axis=0)

    cums = []
    for q, l2f, v, reverse, o_ref, row0, base in dirs:
        tri = jnp.where((c_idx >= r_idx) if reverse else (c_idx <= r_idx), 1.0, 0.0).astype(BF16)
        hi = l2f.astype(BF16)
        lo = (l2f - hi.astype(F32)).astype(BF16)
        cums.append(_dot(tri, hi) + _dot(tri, lo))

    state_in, state_upd, intra = [], [], []
    for (q, l2f, v, reverse, o_ref, row0, base), cum in zip(dirs, cums):
        k = 1.0 - jnp.exp2(l2f)
        last = 0 if reverse else c - 1
        total = cum[last:last + 1, :]
        q_in = (q * jnp.exp2(cum)).astype(BF16)
        k_out = (k * jnp.exp2(total - cum)).astype(BF16)
        e_tot = jnp.exp2(total)
        v_b = v.astype(BF16)
        for p, ps in enumerate(pairs):
            h0, h1 = slice(ps.start, ps.start + hd), slice(ps.start + hd, ps.stop)
            v_t = jnp.concatenate([v[:, h0], v[:, h1]], axis=0).T.astype(BF16)
            state_in.append(q_in[:, ps])
            state_upd.append((base + p, e_tot[:, ps], v_t, block_diag(k_out[:, h0], k_out[:, h1])))
            if o_ref is None:
                continue
            for i in range(c // HG_SUB):
                r0 = i * HG_SUB
                rows = slice(r0, r0 + HG_SUB)
                cols = slice(r0, c) if reverse else slice(0, r0 + HG_SUB)
                mid = r0 + HG_SUB // 2 if reverse else r0 + HG_SUB // 2 - 1
                m = cum[mid:mid + 1, ps]
                q_t = (q[rows, ps] * jnp.exp2(cum[rows, ps] - m)).astype(BF16)
                k_t = (k[cols, ps] * jnp.exp2(m - cum[cols, ps])).astype(BF16)
                intra.append((o_ref, row0, rows, cols, ps, reverse, len(state_in) - 1, q_t,
                              block_diag(k_t[:, :hd], k_t[:, hd:]),
                              block_diag(v_b[cols, h0], v_b[cols, h1])))

    scores = [_dot_nt(item[7], item[8]) for item in intra]
    kv = [_dot(v_t, k_bd) for _, _, v_t, k_bd in state_upd]
    states = {}
    inter = []
    for n, (idx, e_tot_p, _, _) in enumerate(state_upd):
        st = states[idx] if idx in states else st_ref[idx]
        if intra:
            st_b = st.astype(BF16)
            inter.append(_dot_nt(state_in[n], block_diag(st_b[:, :hd], st_b[:, hd:])))
        states[idx] = st * e_tot_p + kv[n]
    for idx, st in states.items():
        st_ref[idx] = st

    for (o_ref, row0, rows, cols, ps, reverse, chain, _, _, v_bd), a in zip(intra, scores):
        n_cols = cols.stop - cols.start
        row_g = rows.start + lax.broadcasted_iota(jnp.int32, (HG_SUB, 2 * n_cols), 0)
        lane = lax.broadcasted_iota(jnp.int32, (HG_SUB, 2 * n_cols), 1)
        col_g = cols.start + jnp.where(lane >= n_cols, lane - n_cols, lane)
        a = jnp.where((col_g >= row_g) if reverse else (col_g <= row_g), a, 0.0)
        o_ref[0, pl.ds(row0 + rows.start, HG_SUB), ps] = _dot(a.astype(BF16), v_bd) + inter[chain][rows]


def _hgrn_kernel(*refs, has_s0, emit_o):
    it = iter(refs)
    fwd_refs = [next(it) for _ in range(3)]
    bwd_refs = [next(it) for _ in range(3)]
    qf_ref = fwd_refs[0]
    s0_ref = next(it) if has_s0 else None
    of_ref = next(it) if emit_o else None
    ob_ref = next(it) if emit_o else None
    sout_ref = None if emit_o else next(it)
    st_ref = next(it)
    j = pl.program_id(1)
    n_inner = qf_ref.shape[1] // HG_CHUNK

    @pl.when(j == 0)
    def _():
        if has_s0:
            st_ref[...] = s0_ref[0]
        else:
            st_ref[...] = jnp.zeros_like(st_ref)

    span = HG_UNROLL * HG_CHUNK

    def body(it, carry):
        dirs = []
        for jj in range(HG_UNROLL):
            rf = pl.multiple_of(it * span, span) + jj * HG_CHUNK
            rb = pl.multiple_of((n_inner // HG_UNROLL - 1 - it) * span, span) + (HG_UNROLL - 1 - jj) * HG_CHUNK
            sf, sb = pl.ds(rf, HG_CHUNK), pl.ds(rb, HG_CHUNK)
            dirs.append(tuple(r[0, sf, :] for r in fwd_refs) + (False, of_ref, rf, 0))
            dirs.append(tuple(r[0, sb, :] for r in bwd_refs) + (True, ob_ref, rb, HG_HEADS // 2))
        _hgrn_step(dirs, st_ref)
        return carry

    lax.fori_loop(0, n_inner // HG_UNROLL, body, 0)

    if not emit_o:
        @pl.when(j == pl.num_programs(1) - 1)
        def _():
            sout_ref[0] = st_ref[...]


def _hgrn(u, s0, *, emit_o):
    b, l, _ = u.shape
    c = min(HG_BLOCK, l)
    n = l // c
    blk = (1, c, HG_DIM)
    fwd = lambda col: pl.BlockSpec(blk, lambda i, j: (i, j, col))
    bwd = lambda col: pl.BlockSpec(blk, lambda i, j: (i, n - 1 - j, col))
    st_dims = (HG_HEADS, HG_HEAD_DIM, 2 * HG_HEAD_DIM)
    state_spec = pl.BlockSpec((1,) + st_dims, lambda i, j: (i, 0, 0, 0))
    state_shape = jax.ShapeDtypeStruct((b,) + st_dims, F32)
    in_specs = [fwd(0), fwd(1), fwd(3), bwd(0), bwd(2), bwd(3)]
    args = [u, u, u, u, u, u]
    if s0 is not None:
        in_specs.append(state_spec)
        args.append(s0)
    if emit_o:
        o_shape = jax.ShapeDtypeStruct((b, l, HG_DIM), F32)
        out_shape = (o_shape, o_shape)
        out_specs = (pl.BlockSpec(blk, lambda i, j: (i, j, 0)),
                     pl.BlockSpec(blk, lambda i, j: (i, n - 1 - j, 0)))
    else:
        out_shape = state_shape
        out_specs = state_spec
    return pl.pallas_call(
        functools.partial(_hgrn_kernel, has_s0=s0 is not None, emit_o=emit_o),
        grid=(b, n),
        in_specs=in_specs,
        out_specs=out_specs,
        out_shape=out_shape,
        scratch_shapes=[pltpu.VMEM(st_dims, F32)],
        compiler_params=_params("arbitrary", "arbitrary"),
        name="hgrn_x" if emit_o else "hgrn_ctx",
    )(*args)


FNET_PAD = 16


def _dft_tables(seq):
    gd = FNET_GROUP_DIM
    kc = (np.arange(gd)[:, None] * np.arange(gd)[None, :]) % gd
    ang_c = 2.0 * np.pi * kc / gd
    scale = 1.0 / np.sqrt(float(seq) * gd)
    chan = np.concatenate([np.cos(ang_c), np.sin(ang_c)], axis=1) * scale
    half = seq // 2
    kl = (np.arange(half + FNET_PAD)[:, None] * np.arange(half)[None, :]) % seq
    ang_l = 2.0 * np.pi * kl / seq
    keep = (np.arange(half + FNET_PAD) <= half)[:, None]
    pos_c = np.where(keep, np.cos(ang_l), 0.0)
    pos_s = np.where(keep, -np.sin(ang_l), 0.0)
    return chan.astype(np.float32), pos_c.astype(np.float32), pos_s.astype(np.float32)


def _mirror_gather(width):
    sub = lax.broadcasted_iota(jnp.int32, (8, width), 0)
    return jnp.where(sub == 0, 0, 8 - sub), sub


def _fold_positions(v, sign):
    seq = v.shape[0]
    n_tiles = seq // 8
    gather, sub = _mirror_gather(v.shape[1])
    tile = lambda k: v[8 * k:8 * k + 8, :]
    out = []
    for k in range(n_tiles // 2):
        tail = jnp.take_along_axis(tile(n_tiles - 1 - k), gather, axis=0)
        if k == 0:
            mirror = jnp.where(sub == 0, 0.0, tail)
        else:
            mirror = jnp.where(sub == 0, jnp.take_along_axis(tile(n_tiles - k), gather, axis=0), tail)
        out.append(tile(k) + mirror if sign > 0 else tile(k) - mirror)
    return jnp.concatenate(out, axis=0)


def _fnet_kernel(u_ref, chan_ref, posc_ref, poss_ref, y_ref, pq_ref):
    seq = u_ref.shape[1]
    half = seq // 2
    gd = FNET_GROUP_DIM
    mids = []
    for g in range(FNET_GROUPS):
        gs = slice(g * gd, (g + 1) * gd)
        pq = _dot(u_ref[0, :, gs], chan_ref[...])
        pq_ref[0:half, gs] = _fold_positions(pq[:, :gd], 1).astype(BF16)
        pq_ref[half:seq, gs] = _fold_positions(pq[:, gd:], -1).astype(BF16)
        mids.append(pq[half:half + 1, :gd])
    gather, sub = _mirror_gather(FNET_DIM)
    mid = jnp.concatenate(mids, axis=1)
    mid_alt = jnp.where(sub % 2 == 0, mid, -mid)

    a = _dot(posc_ref[...], pq_ref[0:half, :])
    bm = _dot(poss_ref[...], pq_ref[half:seq, :])
    tile = lambda v, k: v[8 * k:8 * k + 8, :]
    n_t = half // 8
    for k in range(0, n_t, 2):
        pair = [tile(a, k + i) + tile(bm, k + i) + mid_alt for i in range(2)]
        y_ref[0, 8 * k:8 * k + 16, :] = jnp.concatenate(pair, axis=0).astype(BF16)
    flipped = [jnp.take_along_axis(tile(a, k) - tile(bm, k) + mid_alt, gather, axis=0) for k in range(n_t)]
    flipped.append(jnp.take_along_axis(tile(a, n_t) + tile(bm, n_t) + mid_alt, gather, axis=0))
    n_all = seq // 8
    for m in range(n_t, n_all, 2):
        pair = [jnp.where(sub == 0, flipped[n_all - mm], flipped[n_all - 1 - mm]) for mm in (m, m + 1)]
        y_ref[0, 8 * m:8 * m + 16, :] = jnp.concatenate(pair, axis=0).astype(BF16)


def _fnet(u, chan, pos_c, pos_s):
    b, seq, _ = u.shape
    table = pl.BlockSpec(pos_c.shape, lambda i: (0, 0))
    return pl.pallas_call(
        _fnet_kernel,
        grid=(b,),
        in_specs=[pl.BlockSpec((1, seq, FNET_DIM), lambda i: (i, 0, 0)),
                  pl.BlockSpec((FNET_GROUP_DIM, 2 * FNET_GROUP_DIM), lambda i: (0, 0)),
                  table, table],
        out_specs=pl.BlockSpec((1, seq, FNET_DIM), lambda i: (i, 0, 0)),
        out_shape=jax.ShapeDtypeStruct((b, seq, FNET_DIM), BF16),
        scratch_shapes=[pltpu.VMEM((seq, FNET_DIM), BF16)],
        compiler_params=_params("arbitrary"),
        name="fnet",
    )(u, chan, pos_c, pos_s)


def _merge_kernel(x_ref, of_ref, ob_ref, sg_ref, ga_ref, gb_ref, yf_ref, mod_ref, og_ref, n2g_ref,
                  wa_ref, wb_ref, wo_ref, x1_ref, h2_ref):
    n_sub = x_ref.shape[1] // MERGE_SUB

    def mix(rows):
        o = of_ref[0, rows, :] + ob_ref[0, rows, :]
        parts = []
        for h in range(HG_HEADS):
            oh = o[:, h * HG_HEAD_DIM:(h + 1) * HG_HEAD_DIM]
            parts.append(oh * lax.rsqrt(jnp.mean(oh * oh, axis=-1, keepdims=True) + EPS))
        on = jnp.concatenate(parts, axis=-1) * og_ref[...] * sg_ref[0, rows, :].astype(F32)
        y_b = _dot(on.astype(BF16), wb_ref[...])
        y_a = _dot(yf_ref[0, rows, :], wa_ref[...])
        m = ga_ref[0, rows, :].astype(F32) * y_a + gb_ref[0, rows, :].astype(F32) * y_b
        return _dot(m.astype(BF16), wo_ref[...])

    def finish(rows, yx):
        x1 = x_ref[0, rows, :] + mod_ref[0, 2:3, :] * yx
        x1_ref[0, rows, :] = x1
        h2 = _modulated_norm(x1, n2g_ref[...], mod_ref[0, 3:4, :], mod_ref[0, 4:5, :])
        h2_ref[0, rows, :] = h2.astype(BF16)

    pending = None
    for i in range(n_sub):
        rows = slice(i * MERGE_SUB, (i + 1) * MERGE_SUB)
        yx = mix(rows)
        if pending is not None:
            finish(*pending)
        pending = (rows, yx)
    finish(*pending)


def _merge(x, o_f, o_b, gates, y_f, mod, og, n2g, w_a, w_b, w_out, *, tm):
    b, l, d = x.shape
    tok = lambda w, col: pl.BlockSpec((1, tm, w), lambda i, j: (i, j, col))
    const = lambda shape: pl.BlockSpec(shape, lambda i, j: tuple(0 for _ in shape))
    return pl.pallas_call(
        _merge_kernel,
        grid=(b, l // tm),
        in_specs=[tok(d, 0), tok(HG_DIM, 0), tok(HG_DIM, 0), tok(HG_DIM, 2 * d // HG_DIM),
                  tok(d, 0), tok(d, 1), tok(FNET_DIM, 0),
                  pl.BlockSpec((1, N_MOD, d), lambda i, j: (i, 0, 0)),
                  const((1, HG_DIM)), const((1, d)),
                  const((FNET_DIM, d)), const((HG_DIM, d)), const((d, d))],
        out_specs=(tok(d, 0), tok(d, 0)),
        out_shape=(jax.ShapeDtypeStruct((b, l, d), F32), jax.ShapeDtypeStruct((b, l, d), BF16)),
        compiler_params=_params("arbitrary", "arbitrary"),
        name="merge",
    )(x, o_f, o_b, gates, gates, gates, y_f, mod, og, n2g, w_a, w_b, w_out)


FFN_GROUP = 4


def _ffn_permute_tokens(h_ref, hp_ref, tmp_ref):
    n_tok = tmp_ref.shape[1]

    def body(g, carry):
        tok = pl.ds(pl.multiple_of(g * n_tok, n_tok), n_tok)
        hf = h_ref[0, tok, :].astype(F32)
        for lt in range(tmp_ref.shape[0]):
            for r in range(n_tok // GRID_W):
                for a in range(8):
                    r0 = r * GRID_W + 8 * a
                    tmp_ref[lt, pl.ds(r * GRID_W + a, 8, stride=8), :] = (
                        hf[r0:r0 + 8, lt * 128:(lt + 1) * 128])
        for lt in range(tmp_ref.shape[0]):
            hp_ref[tok, lt * 128:(lt + 1) * 128] = tmp_ref[lt].astype(BF16)
        return carry

    lax.fori_loop(0, h_ref.shape[1] // n_tok, body, 0)


def _ffn_up(hp_ref, up_refs, row0, n_rows):
    h = hp_ref[row0 * GRID_W:(row0 + n_rows) * GRID_W, :]
    return [_dot(h, up_ref[...]) for up_ref in up_refs]


def _pair(lo, hi):
    return jnp.concatenate([lo, hi], axis=0).astype(BF16)


def _ffn_row(z, r, taps, bias, first, last, sub):
    lo = [z[r * GRID_W + 8 * b:r * GRID_W + 8 * b + 8, 0:128] for b in range(8)]
    hi = [z[r * GRID_W + 8 * b:r * GRID_W + 8 * b + 8, 128:256] for b in range(8)]
    mid = [_pair(lo[b], hi[b]) for b in range(8)]
    down = lambda v: jnp.where(sub == 0, 0.0, pltpu.roll(v, 1, axis=0))
    left = [_pair(down(lo[7]), down(hi[7]))] + mid[0:7]
    up = lambda v: jnp.where(sub == 7, 0.0, pltpu.roll(v, 7, axis=0))
    right = mid[1:8] + [_pair(up(lo[0]), up(hi[0]))]
    term = lambda kh: [left[b] * taps[3 * kh] + mid[b] * taps[3 * kh + 1] + right[b] * taps[3 * kh + 2]
                       for b in range(8)]
    above = None if first else term(2)
    here = term(1)
    below = None if last else [t + bias for t in term(0)]
    return above, here, below


def _ffn_down(a_ref, dn_ref, y_ref, row0, n_rows):
    tok = slice(row0 * GRID_W, (row0 + n_rows) * GRID_W)
    part = _dot(a_ref[tok, :].astype(BF16), dn_ref[...])
    for j in range(y_ref.shape[1]):
        y_ref[0, j, tok, :] += part[:, j * 128:(j + 1) * 128]


def _ffn_kernel(h_ref, up1_ref, up2_ref, cw1_ref, cw2_ref, cb1_ref, cb2_ref, dn_ref, y_ref,
                hp_ref, a_ref, wp_ref, tmp_ref, wup_ref, wdn_ref):
    t = pl.program_id(1)
    rows = hp_ref.shape[0] // GRID_W
    grp = FFN_GROUP

    @pl.when(t == 0)
    def _():
        _ffn_permute_tokens(h_ref, hp_ref, tmp_ref)
        y_ref[...] = jnp.zeros_like(y_ref)

    wup_ref[0] = up1_ref[...].astype(BF16)
    wup_ref[1] = up2_ref[...].astype(BF16)
    wdn_ref[...] = dn_ref[...].astype(BF16)
    up_refs = (wup_ref.at[0], wup_ref.at[1])

    for half, (cw_ref, cb_ref) in enumerate(((cw1_ref, cb1_ref), (cw2_ref, cb2_ref))):
        for k in range(10):
            row = cw_ref[k:k + 1, :] if k < 9 else cb_ref[...]
            wp_ref[half, k] = _pair(jnp.broadcast_to(row[:, 0:128], (8, 128)),
                                    jnp.broadcast_to(row[:, 128:256], (8, 128)))
    taps = [[wp_ref[half, k] for k in range(9)] for half in range(2)]
    bias = [wp_ref[half, 9] for half in range(2)]
    sub = lax.broadcasted_iota(jnp.int32, (8, 128), 0)

    def gate(c, out_row):
        for b in range(8):
            act = (_silu(c[0][b]) * c[1][b]).astype(F32)
            tok_b = slice(out_row * GRID_W + 8 * b, out_row * GRID_W + 8 * b + 8)
            a_ref[tok_b, 0:128] = act[0:8]
            a_ref[tok_b, 128:256] = act[8:16]

    prev = [None, None]
    cur = [[bias[half]] * 8 for half in range(2)]
    z_next = _ffn_up(hp_ref, up_refs, 0, grp)
    for row0 in range(0, rows, grp):
        z_cur = z_next
        if row0 + grp < rows:
            z_next = _ffn_up(hp_ref, up_refs, row0 + grp, grp)
        for rr in range(grp):
            r = row0 + rr
            done = []
            for half in range(2):
                above, here, below = _ffn_row(z_cur[half], rr, taps[half], bias[half],
                                              r == 0, r == rows - 1, sub)
                if above is not None:
                    done.append([prev[half][b] + above[b] for b in range(8)])
                prev[half] = [cur[half][b] + here[b] for b in range(8)]
                cur[half] = below
            if done:
                gate(done, r - 1)
            if rr == 0 and row0 > 0:
                _ffn_down(a_ref, wdn_ref, y_ref, row0 - grp, grp)
    gate(prev, rows - 1)
    _ffn_down(a_ref, wdn_ref, y_ref, rows - grp, grp)


def _ffn(h2, up, conv_w, conv_b, down):
    b, l, d = h2.shape
    rows = l // GRID_W
    tc = FFN_TILE
    nt = D_FF // tc
    assert rows % FFN_GROUP == 0 and tc == 256, "the conv pairs the two lane tiles of a channel tile"
    return pl.pallas_call(
        _ffn_kernel,
        grid=(b, nt),
        in_specs=[pl.BlockSpec((1, l, d), lambda i, t: (i, 0, 0)),
                  pl.BlockSpec((d, tc), lambda i, t: (0, t)),
                  pl.BlockSpec((d, tc), lambda i, t: (0, nt + t)),
                  pl.BlockSpec((9, tc), lambda i, t: (0, t)),
                  pl.BlockSpec((9, tc), lambda i, t: (0, nt + t)),
                  pl.BlockSpec((1, tc), lambda i, t: (0, t)),
                  pl.BlockSpec((1, tc), lambda i, t: (0, nt + t)),
                  pl.BlockSpec((tc, d), lambda i, t: (t, 0))],
        out_specs=pl.BlockSpec((1, d // 128, l, 128), lambda i, t: (i, 0, 0, 0)),
        out_shape=jax.ShapeDtypeStruct((b, d // 128, l, 128), F32),
        scratch_shapes=[pltpu.VMEM((l, d), BF16),
                        pltpu.VMEM((l, tc), F32),
                        pltpu.VMEM((2, 10, 16, 128), BF16),
                        pltpu.VMEM((d // 128, FFN_GROUP * GRID_W, 128), F32),
                        pltpu.VMEM((2, d, tc), BF16),
                        pltpu.VMEM((tc, d), BF16)],
        compiler_params=_params("arbitrary", "arbitrary"),
        name="ffn",
    )(h2, up, up, conv_w, conv_w, conv_b, conv_b, down)


def _final_kernel(x1_ref, y_ref, mod_ref, g_ref, o_ref):
    n_slab = y_ref.shape[1]
    for r in range(y_ref.shape[2] // GRID_W):
        for a in range(8):
            tok = slice(r * GRID_W + 8 * a, r * GRID_W + 8 * a + 8)
            y = jnp.concatenate([y_ref[0, j, pl.ds(r * GRID_W + a, 8, stride=8), :]
                                 for j in range(n_slab)], axis=-1)
            x = x1_ref[0, tok, :] + mod_ref[0, 5:6, :] * y
            ms = jnp.mean(x * x, axis=-1, keepdims=True)
            o_ref[0, tok, :] = x * lax.rsqrt(ms + EPS) * g_ref[...]


def _final(x1, y, mod, g, *, tm):
    b, l, d = x1.shape
    tok = pl.BlockSpec((1, tm, d), lambda i, j: (i, j, 0))
    return pl.pallas_call(
        _final_kernel,
        grid=(b, l // tm),
        in_specs=[tok, pl.BlockSpec((1, d // 128, tm, 128), lambda i, j: (i, 0, j, 0)),
                  pl.BlockSpec((1, N_MOD, d), lambda i, j: (i, 0, 0)),
                  pl.BlockSpec((1, d), lambda i, j: (0, 0))],
        out_specs=tok,
        out_shape=jax.ShapeDtypeStruct((b, l, d), F32),
        compiler_params=_params("arbitrary", "arbitrary"),
        name="final",
    )(x1, y, mod, g)


def kernel(x, c, ctx, c_ctx, ada_w, ada_b, norm1_g, w_in, hg_lb, hg_onorm_g, w_a, w_b, w_out,
           norm2_g, ffn_up, ffn_conv_w, ffn_conv_b, ffn_down, final_g):
    b, seq, d = x.shape
    assert ada_w.shape[0] == 1, "single-layer kernel"
    assert (d, seq % GRID_W) == (D_MODEL, 0)

    lb = jnp.cumsum(jax.nn.softmax(hg_lb.astype(F32), axis=0), axis=0)[0].reshape(1, HG_DIM)

    n_rows = 16
    rows = jnp.concatenate([c, c_ctx[None, :], jnp.zeros((n_rows - b - 1, d), F32)], axis=0)
    mod = _ada(rows, ada_w[0], ada_b[0])
    mod_x = mod[:b].reshape(b, N_MOD, d)
    mod_c = mod[b:b + 1].reshape(1, N_MOD, d)

    w_in_b = w_in[0].astype(BF16)
    g1 = norm1_g[0].reshape(1, d)
    hg_x, u_f, gates = _inproj(x, mod_x, g1, lb, w_in_b, tm=1024, full=True)
    (hg_c,) = _inproj(ctx.reshape(1, -1, d), mod_c, g1, lb, w_in_b, tm=1024, full=False)
    hg_c = hg_c.reshape(b, ctx.shape[1], 4 * HG_DIM)

    s_ctx = _hgrn(hg_c, None, emit_o=False)
    o_f, o_b = _hgrn(hg_x, s_ctx, emit_o=True)

    chan, pos_c, pos_s = (jnp.asarray(t).astype(BF16) for t in _dft_tables(seq))
    y_f = _fnet(u_f, chan, pos_c, pos_s)

    x1, h2 = _merge(x, o_f, o_b, gates, y_f, mod_x, hg_onorm_g[0].reshape(1, HG_DIM),
                    norm2_g[0].reshape(1, d), w_a[0].astype(BF16), w_b[0].astype(BF16),
                    w_out[0].astype(BF16), tm=1024)

    y = _ffn(h2, ffn_up[0], ffn_conv_w[0].reshape(9, 2 * D_FF),
             ffn_conv_b[0].reshape(1, 2 * D_FF), ffn_down[0])
    return _final(x1, y, mod_x, final_g.reshape(1, d), tm=1024)
```

```python
import functools

import numpy as np
import jax
import jax.numpy as jnp
from jax import lax
from jax.experimental import pallas as pl
from jax.experimental.pallas import tpu as pltpu

D_MODEL = 1024
GRID_W = 64
FNET_DIM = 512
FNET_GROUPS = 4
FNET_GROUP_DIM = FNET_DIM // FNET_GROUPS
HG_DIM = 512
HG_HEADS = 4
HG_HEAD_DIM = HG_DIM // HG_HEADS
D_FF = 2816
N_MOD = 6
EPS = 1e-6

COL_FNET, COL_Q, COL_FF, COL_FB, COL_I, COL_G = 0, 1, 2, 3, 4, 5

INPROJ_SUB = 256
MERGE_SUB = 256
HG_BLOCK = 1024
HG_CHUNK = 64
HG_UNROLL = 4
HG_SUB = 16
FFN_TILE = 256

BF16 = jnp.bfloat16
F32 = jnp.float32
LANES = 128
SUBLANES = 8
VMEM_LIMIT = 56 * 1024 * 1024


def _sigmoid(x):
    return 1.0 / (1.0 + jnp.exp(-x))


def _silu(x):
    return x * _sigmoid(x)


def _dot(a, b):
    return jnp.dot(a, b, preferred_element_type=F32)


def _dot_nt(a, b):
    return lax.dot_general(a, b, (((1,), (1,)), ((), ())), preferred_element_type=F32)


def _params(*sem, flags=None):
    return pltpu.CompilerParams(dimension_semantics=sem, vmem_limit_bytes=VMEM_LIMIT, flags=flags)


def _ada_kernel(c_ref, w_ref, b_ref, o_ref):
    a = _silu(c_ref[...])
    o_ref[...] = _dot(a.astype(BF16), w_ref[...].astype(BF16)) + b_ref[...]


def _ada(rows, ada_w, ada_b):
    m, d = rows.shape
    n = ada_w.shape[1]
    tn = 1536
    return pl.pallas_call(
        _ada_kernel,
        grid=(n // tn,),
        in_specs=[pl.BlockSpec((m, d), lambda j: (0, 0)),
                  pl.BlockSpec((d, tn), lambda j: (0, j)),
                  pl.BlockSpec((1, tn), lambda j: (0, j))],
        out_specs=pl.BlockSpec((m, tn), lambda j: (0, j)),
        out_shape=jax.ShapeDtypeStruct((m, n), F32),
        compiler_params=_params("arbitrary"),
        name="ada",
    )(rows, ada_w, ada_b.reshape(1, n))


def _modulated_norm(x, g, shift, scale):
    ms = jnp.mean(x * x, axis=-1, keepdims=True)
    return (x * lax.rsqrt(ms + EPS) * g) * (1.0 + scale) + shift


def _inproj_kernel(x_ref, mod_ref, g_ref, lb_ref, w_ref, *o_refs, hg_col0, full):
    lb = lb_ref[...]
    hg_ref = o_refs[0]
    n_sub = x_ref.shape[1] // INPROJ_SUB

    def normed(i):
        rows = slice(i * INPROJ_SUB, (i + 1) * INPROJ_SUB)
        return _modulated_norm(x_ref[0, rows, :], g_ref[...], mod_ref[0, 0:1, :],
                               mod_ref[0, 1:2, :]).astype(BF16)

    hb_next = normed(0)
    for i in range(n_sub):
        hb = hb_next
        rows = slice(i * INPROJ_SUB, (i + 1) * INPROJ_SUB)

        def proj(col, width=HG_DIM):
            return _dot(hb, w_ref[:, col * HG_DIM:col * HG_DIM + width])

        hg_ref[0, rows, 0:HG_DIM] = _silu(proj(hg_col0))
        if i + 1 < n_sub:
            hb_next = normed(i + 1)
        for n in (1, 2):
            f = lb + (1.0 - lb) * _sigmoid(proj(hg_col0 + n))
            hg_ref[0, rows, n * HG_DIM:(n + 1) * HG_DIM] = jnp.log2(f)
        hg_ref[0, rows, 3 * HG_DIM:4 * HG_DIM] = proj(hg_col0 + 3)
        if full:
            uf_ref, gate_ref = o_refs[1], o_refs[2]
            uf_ref[0, rows, :] = proj(COL_FNET).astype(BF16)
            gate_ref[0, rows, 0:2 * D_MODEL] = _sigmoid(proj(COL_G + 1, 2 * D_MODEL)).astype(BF16)
            gate_ref[0, rows, 2 * D_MODEL:2 * D_MODEL + HG_DIM] = _silu(proj(COL_G)).astype(BF16)


def _inproj(x, mod, g, lb, w, *, tm, full):
    b, l, d = x.shape
    n = w.shape[1] if full else (COL_I + 1) * HG_DIM
    mod_map = (lambda i, j: (i, 0, 0)) if full else (lambda i, j: (0, 0, 0))
    tok = lambda width, dtype: (pl.BlockSpec((1, tm, width), lambda i, j: (i, j, 0)),
                                jax.ShapeDtypeStruct((b, l, width), dtype))
    outs = [tok(4 * HG_DIM, F32)]
    if full:
        outs += [tok(FNET_DIM, BF16), tok(HG_DIM + 2 * D_MODEL, BF16)]
    return pl.pallas_call(
        functools.partial(_inproj_kernel, hg_col0=COL_Q, full=full),
        grid=(b, l // tm),
        in_specs=[pl.BlockSpec((1, tm, d), lambda i, j: (i, j, 0)),
                  pl.BlockSpec((1, N_MOD, d), mod_map),
                  pl.BlockSpec((1, d), lambda i, j: (0, 0)),
                  pl.BlockSpec((1, HG_DIM), lambda i, j: (0, 0)),
                  pl.BlockSpec((d, n), lambda i, j: (0, 0))],
        out_specs=tuple(o[0] for o in outs),
        out_shape=tuple(o[1] for o in outs),
        compiler_params=_params("arbitrary", "arbitrary"),
        name="inproj_x" if full else "inproj_ctx",
    )(x, mod, g, lb, w)


def _hgrn_step(dirs, st_ref):
    c = HG_CHUNK
    hd = HG_HEAD_DIM
    r_idx = lax.broadcasted_iota(jnp.int32, (c, c), 0)
    c_idx = lax.broadcasted_iota(jnp.int32, (c, c), 1)
    pairs = [slice(p * 2 * hd, (p + 1) * 2 * hd) for p in range(HG_HEADS // 2)]

    def block_diag(a0, a1):
        z0, z1 = jnp.zeros(a0.shape, a0.dtype), jnp.zeros(a1.shape, a1.dtype)
        return jnp.concatenate([jnp.concatenate([a0, z0], axis=1),
                                jnp.concatenate([z1, a1], axis=1)], axis=0)

    cums = []
    for q, l2f, v, reverse, o_ref, row0, base in dirs:
        tri = jnp.where((c_idx >= r_idx) if reverse else (c_idx <= r_idx), 1.0, 0.0).astype(BF16)
        hi = l2f.astype(BF16)
        lo = (l2f - hi.astype(F32)).astype(BF16)
        cums.append(_dot(tri, hi) + _dot(tri, lo))

    state_in, state_upd, intra = [], [], []
    for (q, l2f, v, reverse, o_ref, row0, base), cum in zip(dirs, cums):
        k = 1.0 - jnp.exp2(l2f)
        last = 0 if reverse else c - 1
        total = cum[last:last + 1, :]
        q_in = (q * jnp.exp2(cum)).astype(BF16)
        k_out = (k * jnp.exp2(total - cum)).astype(BF16)
        e_tot = jnp.exp2(total)
        v_b = v.astype(BF16)
        for p, ps in enumerate(pairs):
            h0, h1 = slice(ps.start, ps.start + hd), slice(ps.start + hd, ps.stop)
            v_t = jnp.concatenate([v[:, h0], v[:, h1]], axis=0).T.astype(BF16)
            state_in.append(q_in[:, ps])
            state_upd.append((base + p, e_tot[:, ps], v_t, block_diag(k_out[:, h0], k_out[:, h1])))
            if o_ref is None:
                continue
            for i in range(c // HG_SUB):
                r0 = i * HG_SUB
                rows = slice(r0, r0 + HG_SUB)
                cols = slice(r0, c) if reverse else slice(0, r0 + HG_SUB)
                mid = r0 + HG_SUB // 2 if reverse else r0 + HG_SUB // 2 - 1
                m = cum[mid:mid + 1, ps]
                q_t = (q[rows, ps] * jnp.exp2(cum[rows, ps] - m)).astype(BF16)
                k_t = (k[cols, ps] * jnp.exp2(m - cum[cols, ps])).astype(BF16)
                intra.append((o_ref, row0, rows, cols, ps, reverse, len(state_in) - 1, q_t,
                              block_diag(k_t[:, :hd], k_t[:, hd:]),
                              block_diag(v_b[cols, h0], v_b[cols, h1])))

    scores = [_dot_nt(item[7], item[8]) for item in intra]
    kv = [_dot(v_t, k_bd) for _, _, v_t, k_bd in state_upd]
    states = {}
    inter = []
    for n, (idx, e_tot_p, _, _) in enumerate(state_upd):
        st = states[idx] if idx in states else st_ref[idx]
        if intra:
            st_b = st.astype(BF16)
            inter.append(_dot_nt(state_in[n], block_diag(st_b[:, :hd], st_b[:, hd:])))
        states[idx] = st * e_tot_p + kv[n]
    for idx, st in states.items():
        st_ref[idx] = st

    for (o_ref, row0, rows, cols, ps, reverse, chain, _, _, v_bd), a in zip(intra, scores):
        n_cols = cols.stop - cols.start
        row_g = rows.start + lax.broadcasted_iota(jnp.int32, (HG_SUB, 2 * n_cols), 0)
        lane = lax.broadcasted_iota(jnp.int32, (HG_SUB, 2 * n_cols), 1)
        col_g = cols.start + jnp.where(lane >= n_cols, lane - n_cols, lane)
        a = jnp.where((col_g >= row_g) if reverse else (col_g <= row_g), a, 0.0)
        o_ref[0, pl.ds(row0 + rows.start, HG_SUB), ps] = _dot(a.astype(BF16), v_bd) + inter[chain][rows]


def _hgrn_kernel(*refs, has_s0, emit_o):
    it = iter(refs)
    fwd_refs = [next(it) for _ in range(3)]
    bwd_refs = [next(it) for _ in range(3)]
    qf_ref = fwd_refs[0]
    s0_ref = next(it) if has_s0 else None
    of_ref = next(it) if emit_o else None
    ob_ref = next(it) if emit_o else None
    sout_ref = None if emit_o else next(it)
    st_ref = next(it)
    j = pl.program_id(1)
    n_inner = qf_ref.shape[1] // HG_CHUNK

    @pl.when(j == 0)
    def _():
        if has_s0:
            st_ref[...] = s0_ref[0]
        else:
            st_ref[...] = jnp.zeros_like(st_ref)

    span = HG_UNROLL * HG_CHUNK

    def body(it, carry):
        dirs = []
        for jj in range(HG_UNROLL):
            rf = pl.multiple_of(it * span, span) + jj * HG_CHUNK
            rb = pl.multiple_of((n_inner // HG_UNROLL - 1 - it) * span, span) + (HG_UNROLL - 1 - jj) * HG_CHUNK
            sf, sb = pl.ds(rf, HG_CHUNK), pl.ds(rb, HG_CHUNK)
            dirs.append(tuple(r[0, sf, :] for r in fwd_refs) + (False, of_ref, rf, 0))
            dirs.append(tuple(r[0, sb, :] for r in bwd_refs) + (True, ob_ref, rb, HG_HEADS // 2))
        _hgrn_step(dirs, st_ref)
        return carry

    lax.fori_loop(0, n_inner // HG_UNROLL, body, 0)

    if not emit_o:
        @pl.when(j == pl.num_programs(1) - 1)
        def _():
            sout_ref[0] = st_ref[...]


def _hgrn(u, s0, *, emit_o):
    b, l, _ = u.shape
    c = min(HG_BLOCK, l)
    n = l // c
    blk = (1, c, HG_DIM)
    fwd = lambda col: pl.BlockSpec(blk, lambda i, j: (i, j, col))
    bwd = lambda col: pl.BlockSpec(blk, lambda i, j: (i, n - 1 - j, col))
    st_dims = (HG_HEADS, HG_HEAD_DIM, 2 * HG_HEAD_DIM)
    state_spec = pl.BlockSpec((1,) + st_dims, lambda i, j: (i, 0, 0, 0))
    state_shape = jax.ShapeDtypeStruct((b,) + st_dims, F32)
    in_specs = [fwd(0), fwd(1), fwd(3), bwd(0), bwd(2), bwd(3)]
    args = [u, u, u, u, u, u]
    if s0 is not None:
        in_specs.append(state_spec)
        args.append(s0)
    if emit_o:
        o_shape = jax.ShapeDtypeStruct((b, l, HG_DIM), F32)
        out_shape = (o_shape, o_shape)
        out_specs = (pl.BlockSpec(blk, lambda i, j: (i, j, 0)),
                     pl.BlockSpec(blk, lambda i, j: (i, n - 1 - j, 0)))
    else:
        out_shape = state_shape
        out_specs = state_spec
    return pl.pallas_call(
        functools.partial(_hgrn_kernel, has_s0=s0 is not None, emit_o=emit_o),
        grid=(b, n),
        in_specs=in_specs,
        out_specs=out_specs,
        out_shape=out_shape,
        scratch_shapes=[pltpu.VMEM(st_dims, F32)],
        compiler_params=_params("arbitrary", "arbitrary"),
        name="hgrn_x" if emit_o else "hgrn_ctx",
    )(*args)


FNET_PAD = 2 * SUBLANES


def _dft_tables(seq):
    gd = FNET_GROUP_DIM
    kc = (np.arange(gd)[:, None] * np.arange(gd)[None, :]) % gd
    ang_c = 2.0 * np.pi * kc / gd
    scale = 1.0 / np.sqrt(float(seq) * gd)
    chan = np.concatenate([np.cos(ang_c), np.sin(ang_c)], axis=1) * scale
    half = seq // 2
    kl = (np.arange(half + FNET_PAD)[:, None] * np.arange(half)[None, :]) % seq
    ang_l = 2.0 * np.pi * kl / seq
    keep = (np.arange(half + FNET_PAD) <= half)[:, None]
    pos_c = np.where(keep, np.cos(ang_l), 0.0)
    pos_s = np.where(keep, -np.sin(ang_l), 0.0)
    return chan.astype(np.float32), pos_c.astype(np.float32), pos_s.astype(np.float32)


def _mirror_gather(width):
    sub = lax.broadcasted_iota(jnp.int32, (SUBLANES, width), 0)
    return jnp.where(sub == 0, 0, SUBLANES - sub), sub


def _row_tile(v, k):
    return v[SUBLANES * k:SUBLANES * (k + 1), :]


def _fold_positions(v, sign):
    seq = v.shape[0]
    n_tiles = seq // SUBLANES
    gather, sub = _mirror_gather(v.shape[1])
    tile = lambda k: _row_tile(v, k)
    out = []
    for k in range(n_tiles // 2):
        tail = jnp.take_along_axis(tile(n_tiles - 1 - k), gather, axis=0)
        if k == 0:
            mirror = jnp.where(sub == 0, 0.0, tail)
        else:
            mirror = jnp.where(sub == 0, jnp.take_along_axis(tile(n_tiles - k), gather, axis=0), tail)
        out.append(tile(k) + mirror if sign > 0 else tile(k) - mirror)
    return jnp.concatenate(out, axis=0)


def _fnet_kernel(u_ref, chan_ref, posc_ref, poss_ref, y_ref, pq_ref):
    seq = u_ref.shape[1]
    half = seq // 2
    gd = FNET_GROUP_DIM
    mids = []
    for g in range(FNET_GROUPS):
        gs = slice(g * gd, (g + 1) * gd)
        pq = _dot(u_ref[0, :, gs], chan_ref[...])
        pq_ref[0:half, gs] = _fold_positions(pq[:, :gd], 1).astype(BF16)
        pq_ref[half:seq, gs] = _fold_positions(pq[:, gd:], -1).astype(BF16)
        mids.append(pq[half:half + 1, :gd])
    gather, sub = _mirror_gather(FNET_DIM)
    mid = jnp.concatenate(mids, axis=1)
    mid_alt = jnp.where(sub % 2 == 0, mid, -mid)

    a = _dot(posc_ref[...], pq_ref[0:half, :])
    bm = _dot(poss_ref[...], pq_ref[half:seq, :])
    tile = _row_tile
    s = SUBLANES
    n_t = half // s
    for k in range(0, n_t, 2):
        pair = [tile(a, k + i) + tile(bm, k + i) + mid_alt for i in range(2)]
        y_ref[0, s * k:s * (k + 2), :] = jnp.concatenate(pair, axis=0).astype(BF16)
    flipped = [jnp.take_along_axis(tile(a, k) - tile(bm, k) + mid_alt, gather, axis=0) for k in range(n_t)]
    flipped.append(jnp.take_along_axis(tile(a, n_t) + tile(bm, n_t) + mid_alt, gather, axis=0))
    n_all = seq // s
    for m in range(n_t, n_all, 2):
        pair = [jnp.where(sub == 0, flipped[n_all - mm], flipped[n_all - 1 - mm]) for mm in (m, m + 1)]
        y_ref[0, s * m:s * (m + 2), :] = jnp.concatenate(pair, axis=0).astype(BF16)


def _fnet(u, chan, pos_c, pos_s):
    b, seq, _ = u.shape
    table = pl.BlockSpec(pos_c.shape, lambda i: (0, 0))
    return pl.pallas_call(
        _fnet_kernel,
        grid=(b,),
        in_specs=[pl.BlockSpec((1, seq, FNET_DIM), lambda i: (i, 0, 0)),
                  pl.BlockSpec((FNET_GROUP_DIM, 2 * FNET_GROUP_DIM), lambda i: (0, 0)),
                  table, table],
        out_specs=pl.BlockSpec((1, seq, FNET_DIM), lambda i: (i, 0, 0)),
        out_shape=jax.ShapeDtypeStruct((b, seq, FNET_DIM), BF16),
        scratch_shapes=[pltpu.VMEM((seq, FNET_DIM), BF16)],
        compiler_params=_params("arbitrary"),
        name="fnet",
    )(u, chan, pos_c, pos_s)


def _merge_kernel(x_ref, of_ref, ob_ref, sg_ref, ga_ref, gb_ref, yf_ref, mod_ref, og_ref, n2g_ref,
                  wa_ref, wb_ref, wo_ref, x1_ref, h2_ref):
    n_sub = x_ref.shape[1] // MERGE_SUB

    def mix(rows):
        o = of_ref[0, rows, :] + ob_ref[0, rows, :]
        parts = []
        for h in range(HG_HEADS):
            oh = o[:, h * HG_HEAD_DIM:(h + 1) * HG_HEAD_DIM]
            parts.append(oh * lax.rsqrt(jnp.mean(oh * oh, axis=-1, keepdims=True) + EPS))
        on = jnp.concatenate(parts, axis=-1) * og_ref[...] * sg_ref[0, rows, :].astype(F32)
        y_b = _dot(on.astype(BF16), wb_ref[...])
        y_a = _dot(yf_ref[0, rows, :], wa_ref[...])
        m = ga_ref[0, rows, :].astype(F32) * y_a + gb_ref[0, rows, :].astype(F32) * y_b
        return _dot(m.astype(BF16), wo_ref[...])

    def finish(rows, yx):
        x1 = x_ref[0, rows, :] + mod_ref[0, 2:3, :] * yx
        x1_ref[0, rows, :] = x1
        h2 = _modulated_norm(x1, n2g_ref[...], mod_ref[0, 3:4, :], mod_ref[0, 4:5, :])
        h2_ref[0, rows, :] = h2.astype(BF16)

    pending = None
    for i in range(n_sub):
        rows = slice(i * MERGE_SUB, (i + 1) * MERGE_SUB)
        yx = mix(rows)
        if pending is not None:
            finish(*pending)
        pending = (rows, yx)
    finish(*pending)


def _merge(x, o_f, o_b, gates, y_f, mod, og, n2g, w_a, w_b, w_out, *, tm):
    b, l, d = x.shape
    tok = lambda w, col: pl.BlockSpec((1, tm, w), lambda i, j: (i, j, col))
    const = lambda shape: pl.BlockSpec(shape, lambda i, j: tuple(0 for _ in shape))
    return pl.pallas_call(
        _merge_kernel,
        grid=(b, l // tm),
        in_specs=[tok(d, 0), tok(HG_DIM, 0), tok(HG_DIM, 0), tok(HG_DIM, 2 * d // HG_DIM),
                  tok(d, 0), tok(d, 1), tok(FNET_DIM, 0),
                  pl.BlockSpec((1, N_MOD, d), lambda i, j: (i, 0, 0)),
                  const((1, HG_DIM)), const((1, d)),
                  const((FNET_DIM, d)), const((HG_DIM, d)), const((d, d))],
        out_specs=(tok(d, 0), tok(d, 0)),
        out_shape=(jax.ShapeDtypeStruct((b, l, d), F32), jax.ShapeDtypeStruct((b, l, d), BF16)),
        compiler_params=_params("arbitrary", "arbitrary"),
        name="merge",
    )(x, o_f, o_b, gates, gates, gates, y_f, mod, og, n2g, w_a, w_b, w_out)


assert GRID_W == SUBLANES * SUBLANES
FFN_GROUP = 8


def _ffn_permute_tokens(h_ref, hp_ref, tmp_ref):
    n_tok = tmp_ref.shape[1]
    s = SUBLANES

    def body(g, carry):
        tok = pl.ds(pl.multiple_of(g * n_tok, n_tok), n_tok)
        hf = h_ref[0, tok, :].astype(F32)
        for lt in range(tmp_ref.shape[0]):
            for r in range(n_tok // GRID_W):
                for a in range(s):
                    r0 = r * GRID_W + s * a
                    tmp_ref[lt, pl.ds(r * GRID_W + a, s, stride=s), :] = (
                        hf[r0:r0 + s, lt * LANES:(lt + 1) * LANES])
        for lt in range(tmp_ref.shape[0]):
            hp_ref[tok, lt * LANES:(lt + 1) * LANES] = tmp_ref[lt].astype(BF16)
        return carry

    lax.fori_loop(0, h_ref.shape[1] // n_tok, body, 0)


def _ffn_up(hp_ref, up_refs, row0, n_rows):
    h = hp_ref[row0 * GRID_W:(row0 + n_rows) * GRID_W, :]
    return [_dot(h, up_ref[...]) for up_ref in up_refs]


def _pair(lo, hi):
    return jnp.concatenate([lo, hi], axis=0).astype(BF16)


def _ffn_row(z, r, taps, bias, first, last, sub):
    s = SUBLANES
    rows_b = lambda b: slice(r * GRID_W + s * b, r * GRID_W + s * b + s)
    lo = [z[rows_b(b), 0:LANES] for b in range(s)]
    hi = [z[rows_b(b), LANES:2 * LANES] for b in range(s)]
    mid = [_pair(lo[b], hi[b]) for b in range(s)]
    down = lambda v: jnp.where(sub == 0, 0.0, pltpu.roll(v, 1, axis=0))
    left = [_pair(down(lo[s - 1]), down(hi[s - 1]))] + mid[0:s - 1]
    up = lambda v: jnp.where(sub == s - 1, 0.0, pltpu.roll(v, s - 1, axis=0))
    right = mid[1:s] + [_pair(up(lo[0]), up(hi[0]))]
    term = lambda kh: [left[b] * taps[3 * kh] + mid[b] * taps[3 * kh + 1] + right[b] * taps[3 * kh + 2]
                       for b in range(s)]
    above = None if first else term(2)
    here = term(1)
    below = None if last else [t + bias for t in term(0)]
    return above, here, below


def _ffn_down(a_ref, dn_ref, y_ref, row0, n_rows):
    tok = slice(row0 * GRID_W, (row0 + n_rows) * GRID_W)
    part = _dot(a_ref[tok, :].astype(BF16), dn_ref[...])
    for j in range(y_ref.shape[1]):
        y_ref[0, j, tok, :] += part[:, j * LANES:(j + 1) * LANES]


def _ffn_kernel(h_ref, up1_ref, up2_ref, cw1_ref, cw2_ref, cb1_ref, cb2_ref, dn_ref, y_ref,
                hp_ref, a_ref, wp_ref, tmp_ref, wup_ref, wdn_ref):
    t = pl.program_id(1)
    rows = hp_ref.shape[0] // GRID_W
    grp = FFN_GROUP
    s = SUBLANES

    @pl.when(t == 0)
    def _():
        _ffn_permute_tokens(h_ref, hp_ref, tmp_ref)
        y_ref[...] = jnp.zeros_like(y_ref)

    wup_ref[0] = up1_ref[...].astype(BF16)
    wup_ref[1] = up2_ref[...].astype(BF16)
    wdn_ref[...] = dn_ref[...].astype(BF16)
    up_refs = (wup_ref.at[0], wup_ref.at[1])

    for half, (cw_ref, cb_ref) in enumerate(((cw1_ref, cb1_ref), (cw2_ref, cb2_ref))):
        for k in range(10):
            row = cw_ref[k:k + 1, :] if k < 9 else cb_ref[...]
            wp_ref[half, k] = _pair(jnp.broadcast_to(row[:, 0:LANES], (s, LANES)),
                                    jnp.broadcast_to(row[:, LANES:2 * LANES], (s, LANES)))
    taps = [[wp_ref[half, k] for k in range(9)] for half in range(2)]
    bias = [wp_ref[half, 9] for half in range(2)]
    sub = lax.broadcasted_iota(jnp.int32, (s, LANES), 0)

    def gate(c, out_row):
        for b in range(s):
            act = (_silu(c[0][b]) * c[1][b]).astype(F32)
            tok_b = slice(out_row * GRID_W + s * b, out_row * GRID_W + s * b + s)
            a_ref[tok_b, 0:LANES] = act[0:s]
            a_ref[tok_b, LANES:2 * LANES] = act[s:2 * s]

    prev = [None, None]
    cur = [[bias[half]] * s for half in range(2)]
    z_next = _ffn_up(hp_ref, up_refs, 0, grp)
    for row0 in range(0, rows, grp):
        z_cur = z_next
        if row0 + grp < rows:
            z_next = _ffn_up(hp_ref, up_refs, row0 + grp, grp)
        for rr in range(grp):
            r = row0 + rr
            done = []
            for half in range(2):
                above, here, below = _ffn_row(z_cur[half], rr, taps[half], bias[half],
                                              r == 0, r == rows - 1, sub)
                if above is not None:
                    done.append([prev[half][b] + above[b] for b in range(s)])
                prev[half] = [cur[half][b] + here[b] for b in range(s)]
                cur[half] = below
            if done:
                gate(done, r - 1)
            if rr == 0 and row0 > 0:
                _ffn_down(a_ref, wdn_ref, y_ref, row0 - grp, grp)
    gate(prev, rows - 1)
    _ffn_down(a_ref, wdn_ref, y_ref, rows - grp, grp)


def _ffn(h2, up, conv_w, conv_b, down):
    b, l, d = h2.shape
    rows = l // GRID_W
    tc = FFN_TILE
    nt = D_FF // tc
    assert rows % FFN_GROUP == 0 and tc == 2 * LANES, "the conv pairs the two lane tiles of a channel tile"
    return pl.pallas_call(
        _ffn_kernel,
        grid=(b, nt),
        in_specs=[pl.BlockSpec((1, l, d), lambda i, t: (i, 0, 0)),
                  pl.BlockSpec((d, tc), lambda i, t: (0, t)),
                  pl.BlockSpec((d, tc), lambda i, t: (0, nt + t)),
                  pl.BlockSpec((9, tc), lambda i, t: (0, t)),
                  pl.BlockSpec((9, tc), lambda i, t: (0, nt + t)),
                  pl.BlockSpec((1, tc), lambda i, t: (0, t)),
                  pl.BlockSpec((1, tc), lambda i, t: (0, nt + t)),
                  pl.BlockSpec((tc, d), lambda i, t: (t, 0))],
        out_specs=pl.BlockSpec((1, d // LANES, l, LANES), lambda i, t: (i, 0, 0, 0)),
        out_shape=jax.ShapeDtypeStruct((b, d // LANES, l, LANES), F32),
        scratch_shapes=[pltpu.VMEM((l, d), BF16),
                        pltpu.VMEM((l, tc), F32),
                        pltpu.VMEM((2, 10, 2 * SUBLANES, LANES), BF16),
                        pltpu.VMEM((d // LANES, FFN_GROUP * GRID_W, LANES), F32),
                        pltpu.VMEM((2, d, tc), BF16),
                        pltpu.VMEM((tc, d), BF16)],
        compiler_params=_params("arbitrary", "arbitrary"),
        name="ffn",
    )(h2, up, up, conv_w, conv_w, conv_b, conv_b, down)


def _final_kernel(x1_ref, y_ref, mod_ref, g_ref, o_ref):
    n_slab = y_ref.shape[1]
    s = SUBLANES
    for r in range(y_ref.shape[2] // GRID_W):
        for a in range(s):
            tok = slice(r * GRID_W + s * a, r * GRID_W + s * a + s)
            y = jnp.concatenate([y_ref[0, j, pl.ds(r * GRID_W + a, s, stride=s), :]
                                 for j in range(n_slab)], axis=-1)
            x = x1_ref[0, tok, :] + mod_ref[0, 5:6, :] * y
            ms = jnp.mean(x * x, axis=-1, keepdims=True)
            o_ref[0, tok, :] = x * lax.rsqrt(ms + EPS) * g_ref[...]


def _final(x1, y, mod, g, *, tm):
    b, l, d = x1.shape
    tok = pl.BlockSpec((1, tm, d), lambda i, j: (i, j, 0))
    return pl.pallas_call(
        _final_kernel,
        grid=(b, l // tm),
        in_specs=[tok, pl.BlockSpec((1, d // LANES, tm, LANES), lambda i, j: (i, 0, j, 0)),
                  pl.BlockSpec((1, N_MOD, d), lambda i, j: (i, 0, 0)),
                  pl.BlockSpec((1, d), lambda i, j: (0, 0))],
        out_specs=tok,
        out_shape=jax.ShapeDtypeStruct((b, l, d), F32),
        compiler_params=_params("arbitrary", "arbitrary"),
        name="final",
    )(x1, y, mod, g)


def kernel(x, c, ctx, c_ctx, ada_w, ada_b, norm1_g, w_in, hg_lb, hg_onorm_g, w_a, w_b, w_out,
           norm2_g, ffn_up, ffn_conv_w, ffn_conv_b, ffn_down, final_g):
    b, seq, d = x.shape
    assert ada_w.shape[0] == 1, "single-layer kernel"
    assert (d, seq % GRID_W) == (D_MODEL, 0)

    lb = jnp.cumsum(jax.nn.softmax(hg_lb.astype(F32), axis=0), axis=0)[0].reshape(1, HG_DIM)

    n_rows = 16
    rows = jnp.concatenate([c, c_ctx[None, :], jnp.zeros((n_rows - b - 1, d), F32)], axis=0)
    mod = _ada(rows, ada_w[0], ada_b[0])
    mod_x = mod[:b].reshape(b, N_MOD, d)
    mod_c = mod[b:b + 1].reshape(1, N_MOD, d)

    w_in_b = w_in[0].astype(BF16)
    g1 = norm1_g[0].reshape(1, d)
    hg_x, u_f, gates = _inproj(x, mod_x, g1, lb, w_in_b, tm=1024, full=True)
    (hg_c,) = _inproj(ctx.reshape(1, -1, d), mod_c, g1, lb, w_in_b, tm=1024, full=False)
    hg_c = hg_c.reshape(b, ctx.shape[1], 4 * HG_DIM)

    s_ctx = _hgrn(hg_c, None, emit_o=False)
    o_f, o_b = _hgrn(hg_x, s_ctx, emit_o=True)

    chan, pos_c, pos_s = (jnp.asarray(t).astype(BF16) for t in _dft_tables(seq))
    y_f = _fnet(u_f, chan, pos_c, pos_s)

    x1, h2 = _merge(x, o_f, o_b, gates, y_f, mod_x, hg_onorm_g[0].reshape(1, HG_DIM),
                    norm2_g[0].reshape(1, d), w_a[0].astype(BF16), w_b[0].astype(BF16),
                    w_out[0].astype(BF16), tm=1024)

    y = _ffn(h2, ffn_up[0], ffn_conv_w[0].reshape(9, 2 * D_FF),
             ffn_conv_b[0].reshape(1, 2 * D_FF), ffn_down[0])
    return _final(x1, y, mod_x, final_g.reshape(1, d), tm=1024)
```

```python
import functools

import numpy as np
import jax
import jax.numpy as jnp
from jax import lax
from jax.experimental import pallas as pl
from jax.experimental.pallas import tpu as pltpu

D_MODEL = 1024
GRID_W = 64
FNET_DIM = 512
FNET_GROUPS = 4
FNET_GROUP_DIM = FNET_DIM // FNET_GROUPS
HG_DIM = 512
HG_HEADS = 4
HG_HEAD_DIM = HG_DIM // HG_HEADS
D_FF = 2816
N_MOD = 6
EPS = 1e-6

COL_FNET, COL_Q, COL_FF, COL_FB, COL_I, COL_G = 0, 1, 2, 3, 4, 5

INPROJ_SUB = 256
MERGE_SUB = 256
HG_BLOCK = 1024
HG_CHUNK = 64
HG_UNROLL = 4
HG_SUB = 16
FFN_TILE = 256

BF16 = jnp.bfloat16
F32 = jnp.float32
LANES = 128
SUBLANES = 8
VMEM_LIMIT = 56 * 1024 * 1024


def _sigmoid(x):
    return 1.0 / (1.0 + jnp.exp(-x))


def _silu(x):
    return x * _sigmoid(x)


def _dot(a, b):
    return jnp.dot(a, b, preferred_element_type=F32)


def _dot_nt(a, b):
    return lax.dot_general(a, b, (((1,), (1,)), ((), ())), preferred_element_type=F32)


def _params(*sem, flags=None):
    return pltpu.CompilerParams(dimension_semantics=sem, vmem_limit_bytes=VMEM_LIMIT, flags=flags)


def _ada_kernel(c_ref, w_ref, b_ref, o_ref):
    a = _silu(c_ref[...])
    o_ref[...] = _dot(a.astype(BF16), w_ref[...].astype(BF16)) + b_ref[...]


def _ada(rows, ada_w, ada_b):
    m, d = rows.shape
    n = ada_w.shape[1]
    tn = 1536
    return pl.pallas_call(
        _ada_kernel,
        grid=(n // tn,),
        in_specs=[pl.BlockSpec((m, d), lambda j: (0, 0)),
                  pl.BlockSpec((d, tn), lambda j: (0, j)),
                  pl.BlockSpec((1, tn), lambda j: (0, j))],
        out_specs=pl.BlockSpec((m, tn), lambda j: (0, j)),
        out_shape=jax.ShapeDtypeStruct((m, n), F32),
        compiler_params=_params("arbitrary"),
        name="ada",
    )(rows, ada_w, ada_b.reshape(1, n))


def _modulated_norm(x, g, shift, scale):
    ms = jnp.mean(x * x, axis=-1, keepdims=True)
    return (x * lax.rsqrt(ms + EPS) * g) * (1.0 + scale) + shift


def _inproj_kernel(x_ref, mod_ref, g_ref, lb_ref, w_ref, *o_refs, hg_col0, full):
    lb = lb_ref[...]
    hg_ref = o_refs[0]
    n_sub = x_ref.shape[1] // INPROJ_SUB

    def normed(i):
        rows = slice(i * INPROJ_SUB, (i + 1) * INPROJ_SUB)
        return _modulated_norm(x_ref[0, rows, :], g_ref[...], mod_ref[0, 0:1, :],
                               mod_ref[0, 1:2, :]).astype(BF16)

    hb_next = normed(0)
    for i in range(n_sub):
        hb = hb_next
        rows = slice(i * INPROJ_SUB, (i + 1) * INPROJ_SUB)

        def proj(col, width=HG_DIM):
            return _dot(hb, w_ref[:, col * HG_DIM:col * HG_DIM + width])

        hg_ref[0, rows, 0:HG_DIM] = _silu(proj(hg_col0))
        if i + 1 < n_sub:
            hb_next = normed(i + 1)
        for n in (1, 2):
            f = lb + (1.0 - lb) * _sigmoid(proj(hg_col0 + n))
            hg_ref[0, rows, n * HG_DIM:(n + 1) * HG_DIM] = jnp.log2(f)
        hg_ref[0, rows, 3 * HG_DIM:4 * HG_DIM] = proj(hg_col0 + 3)
        if full:
            uf_ref, gate_ref = o_refs[1], o_refs[2]
            uf_ref[0, rows, :] = proj(COL_FNET).astype(BF16)
            gate_ref[0, rows, 0:2 * D_MODEL] = _sigmoid(proj(COL_G + 1, 2 * D_MODEL)).astype(BF16)
            gate_ref[0, rows, 2 * D_MODEL:2 * D_MODEL + HG_DIM] = _silu(proj(COL_G)).astype(BF16)


def _inproj(x, mod, g, lb, w, *, tm, full):
    b, l, d = x.shape
    n = w.shape[1] if full else (COL_I + 1) * HG_DIM
    mod_map = (lambda i, j: (i, 0, 0)) if full else (lambda i, j: (0, 0, 0))
    tok = lambda width, dtype: (pl.BlockSpec((1, tm, width), lambda i, j: (i, j, 0)),
                                jax.ShapeDtypeStruct((b, l, width), dtype))
    outs = [tok(4 * HG_DIM, F32)]
    if full:
        outs += [tok(FNET_DIM, BF16), tok(HG_DIM + 2 * D_MODEL, BF16)]
    return pl.pallas_call(
        functools.partial(_inproj_kernel, hg_col0=COL_Q, full=full),
        grid=(b, l // tm),
        in_specs=[pl.BlockSpec((1, tm, d), lambda i, j: (i, j, 0)),
                  pl.BlockSpec((1, N_MOD, d), mod_map),
                  pl.BlockSpec((1, d), lambda i, j: (0, 0)),
                  pl.BlockSpec((1, HG_DIM), lambda i, j: (0, 0)),
                  pl.BlockSpec((d, n), lambda i, j: (0, 0))],
        out_specs=tuple(o[0] for o in outs),
        out_shape=tuple(o[1] for o in outs),
        compiler_params=_params("arbitrary", "arbitrary"),
        name="inproj_x" if full else "inproj_ctx",
    )(x, mod, g, lb, w)


def _hgrn_step(dirs, st_ref):
    c = HG_CHUNK
    hd = HG_HEAD_DIM
    r_idx = lax.broadcasted_iota(jnp.int32, (c, c), 0)
    c_idx = lax.broadcasted_iota(jnp.int32, (c, c), 1)
    pairs = [slice(p * 2 * hd, (p + 1) * 2 * hd) for p in range(HG_HEADS // 2)]

    def block_diag(a0, a1):
        z0, z1 = jnp.zeros(a0.shape, a0.dtype), jnp.zeros(a1.shape, a1.dtype)
        return jnp.concatenate([jnp.concatenate([a0, z0], axis=1),
                                jnp.concatenate([z1, a1], axis=1)], axis=0)

    cums = []
    for q, l2f, v, reverse, o_ref, row0, base in dirs:
        tri = jnp.where((c_idx >= r_idx) if reverse else (c_idx <= r_idx), 1.0, 0.0).astype(BF16)
        hi = l2f.astype(BF16)
        lo = (l2f - hi.astype(F32)).astype(BF16)
        cums.append(_dot(tri, hi) + _dot(tri, lo))

    state_in, state_upd, intra = [], [], []
    for (q, l2f, v, reverse, o_ref, row0, base), cum in zip(dirs, cums):
        k = 1.0 - jnp.exp2(l2f)
        last = 0 if reverse else c - 1
        total = cum[last:last + 1, :]
        q_in = (q * jnp.exp2(cum)).astype(BF16)
        k_out = (k * jnp.exp2(total - cum)).astype(BF16)
        e_tot = jnp.exp2(total)
        v_b = v.astype(BF16)
        for p, ps in enumerate(pairs):
            h0, h1 = slice(ps.start, ps.start + hd), slice(ps.start + hd, ps.stop)
            v_t = jnp.concatenate([v[:, h0], v[:, h1]], axis=0).T.astype(BF16)
            state_in.append(q_in[:, ps])
            state_upd.append((base + p, e_tot[:, ps], v_t, block_diag(k_out[:, h0], k_out[:, h1])))
            if o_ref is None:
                continue
            for i in range(c // HG_SUB):
                r0 = i * HG_SUB
                rows = slice(r0, r0 + HG_SUB)
                cols = slice(r0, c) if reverse else slice(0, r0 + HG_SUB)
                mid = r0 + HG_SUB // 2 if reverse else r0 + HG_SUB // 2 - 1
                m = cum[mid:mid + 1, ps]
                q_t = (q[rows, ps] * jnp.exp2(cum[rows, ps] - m)).astype(BF16)
                k_t = (k[cols, ps] * jnp.exp2(m - cum[cols, ps])).astype(BF16)
                intra.append((o_ref, row0, rows, cols, ps, reverse, len(state_in) - 1, q_t,
                              block_diag(k_t[:, :hd], k_t[:, hd:]),
                              block_diag(v_b[cols, h0], v_b[cols, h1])))

    scores = [_dot_nt(item[7], item[8]) for item in intra]
    kv = [_dot(v_t, k_bd) for _, _, v_t, k_bd in state_upd]
    states = {}
    inter = []
    for n, (idx, e_tot_p, _, _) in enumerate(state_upd):
        st = states[idx] if idx in states else st_ref[idx]
        if intra:
            st_b = st.astype(BF16)
            inter.append(_dot_nt(state_in[n], block_diag(st_b[:, :hd], st_b[:, hd:])))
        states[idx] = st * e_tot_p + kv[n]
    for idx, st in states.items():
        st_ref[idx] = st

    for (o_ref, row0, rows, cols, ps, reverse, chain, _, _, v_bd), a in zip(intra, scores):
        n_cols = cols.stop - cols.start
        row_g = rows.start + lax.broadcasted_iota(jnp.int32, (HG_SUB, 2 * n_cols), 0)
        lane = lax.broadcasted_iota(jnp.int32, (HG_SUB, 2 * n_cols), 1)
        col_g = cols.start + jnp.where(lane >= n_cols, lane - n_cols, lane)
        a = jnp.where((col_g >= row_g) if reverse else (col_g <= row_g), a, 0.0)
        o_ref[0, pl.ds(row0 + rows.start, HG_SUB), ps] = _dot(a.astype(BF16), v_bd) + inter[chain][rows]


def _hgrn_kernel(*refs, has_s0, emit_o):
    it = iter(refs)
    fwd_refs = [next(it) for _ in range(3)]
    bwd_refs = [next(it) for _ in range(3)]
    qf_ref = fwd_refs[0]
    s0_ref = next(it) if has_s0 else None
    of_ref = next(it) if emit_o else None
    ob_ref = next(it) if emit_o else None
    sout_ref = None if emit_o else next(it)
    st_ref = next(it)
    j = pl.program_id(1)
    n_inner = qf_ref.shape[1] // HG_CHUNK

    @pl.when(j == 0)
    def _():
        if has_s0:
            st_ref[...] = s0_ref[0]
        else:
            st_ref[...] = jnp.zeros_like(st_ref)

    span = HG_UNROLL * HG_CHUNK

    def body(it, carry):
        dirs = []
        for jj in range(HG_UNROLL):
            rf = pl.multiple_of(it * span, span) + jj * HG_CHUNK
            rb = pl.multiple_of((n_inner // HG_UNROLL - 1 - it) * span, span) + (HG_UNROLL - 1 - jj) * HG_CHUNK
            sf, sb = pl.ds(rf, HG_CHUNK), pl.ds(rb, HG_CHUNK)
            dirs.append(tuple(r[0, sf, :] for r in fwd_refs) + (False, of_ref, rf, 0))
            dirs.append(tuple(r[0, sb, :] for r in bwd_refs) + (True, ob_ref, rb, HG_HEADS // 2))
        _hgrn_step(dirs, st_ref)
        return carry

    lax.fori_loop(0, n_inner // HG_UNROLL, body, 0)

    if not emit_o:
        @pl.when(j == pl.num_programs(1) - 1)
        def _():
            sout_ref[0] = st_ref[...]


def _hgrn(u, s0, *, emit_o):
    b, l, _ = u.shape
    c = min(HG_BLOCK, l)
    n = l // c
    blk = (1, c, HG_DIM)
    fwd = lambda col: pl.BlockSpec(blk, lambda i, j: (i, j, col))
    bwd = lambda col: pl.BlockSpec(blk, lambda i, j: (i, n - 1 - j, col))
    st_dims = (HG_HEADS, HG_HEAD_DIM, 2 * HG_HEAD_DIM)
    state_spec = pl.BlockSpec((1,) + st_dims, lambda i, j: (i, 0, 0, 0))
    state_shape = jax.ShapeDtypeStruct((b,) + st_dims, F32)
    in_specs = [fwd(0), fwd(1), fwd(3), bwd(0), bwd(2), bwd(3)]
    args = [u, u, u, u, u, u]
    if s0 is not None:
        in_specs.append(state_spec)
        args.append(s0)
    if emit_o:
        o_shape = jax.ShapeDtypeStruct((b, l, HG_DIM), F32)
        out_shape = (o_shape, o_shape)
        out_specs = (pl.BlockSpec(blk, lambda i, j: (i, j, 0)),
                     pl.BlockSpec(blk, lambda i, j: (i, n - 1 - j, 0)))
    else:
        out_shape = state_shape
        out_specs = state_spec
    return pl.pallas_call(
        functools.partial(_hgrn_kernel, has_s0=s0 is not None, emit_o=emit_o),
        grid=(b, n),
        in_specs=in_specs,
        out_specs=out_specs,
        out_shape=out_shape,
        scratch_shapes=[pltpu.VMEM(st_dims, F32)],
        compiler_params=_params("arbitrary", "arbitrary"),
        name="hgrn_x" if emit_o else "hgrn_ctx",
    )(*args)


FNET_PAD = 2 * SUBLANES


def _dft_tables(seq):
    gd = FNET_GROUP_DIM
    kc = (np.arange(gd)[:, None] * np.arange(gd)[None, :]) % gd
    ang_c = 2.0 * np.pi * kc / gd
    scale = 1.0 / np.sqrt(float(seq) * gd)
    chan = np.concatenate([np.cos(ang_c), np.sin(ang_c)], axis=1) * scale
    half = seq // 2
    kl = (np.arange(half + FNET_PAD)[:, None] * np.arange(half)[None, :]) % seq
    ang_l = 2.0 * np.pi * kl / seq
    keep = (np.arange(half + FNET_PAD) <= half)[:, None]
    pos_c = np.where(keep, np.cos(ang_l), 0.0)
    pos_s = np.where(keep, -np.sin(ang_l), 0.0)
    return chan.astype(np.float32), pos_c.astype(np.float32), pos_s.astype(np.float32)


def _mirror_gather(width):
    sub = lax.broadcasted_iota(jnp.int32, (SUBLANES, width), 0)
    return jnp.where(sub == 0, 0, SUBLANES - sub), sub


def _row_tile(v, k):
    return v[SUBLANES * k:SUBLANES * (k + 1), :]


def _fold_positions(v, sign):
    seq = v.shape[0]
    n_tiles = seq // SUBLANES
    gather, sub = _mirror_gather(v.shape[1])
    tile = lambda k: _row_tile(v, k)
    out = []
    for k in range(n_tiles // 2):
        tail = jnp.take_along_axis(tile(n_tiles - 1 - k), gather, axis=0)
        if k == 0:
            mirror = jnp.where(sub == 0, 0.0, tail)
        else:
            mirror = jnp.where(sub == 0, jnp.take_along_axis(tile(n_tiles - k), gather, axis=0), tail)
        out.append(tile(k) + mirror if sign > 0 else tile(k) - mirror)
    return jnp.concatenate(out, axis=0)


def _fnet_kernel(u_ref, chan_ref, posc_ref, poss_ref, y_ref, pq_ref):
    seq = u_ref.shape[1]
    half = seq // 2
    gd = FNET_GROUP_DIM
    mids = []
    for g in range(FNET_GROUPS):
        gs = slice(g * gd, (g + 1) * gd)
        pq = _dot(u_ref[0, :, gs], chan_ref[...])
        pq_ref[0:half, gs] = _fold_positions(pq[:, :gd], 1).astype(BF16)
        pq_ref[half:seq, gs] = _fold_positions(pq[:, gd:], -1).astype(BF16)
        mids.append(pq[half:half + 1, :gd])
    gather, sub = _mirror_gather(FNET_DIM)
    mid = jnp.concatenate(mids, axis=1)
    mid_alt = jnp.where(sub % 2 == 0, mid, -mid)

    a = _dot(posc_ref[...], pq_ref[0:half, :])
    bm = _dot(poss_ref[...], pq_ref[half:seq, :])
    tile = _row_tile
    s = SUBLANES
    n_t = half // s
    for k in range(0, n_t, 2):
        pair = [tile(a, k + i) + tile(bm, k + i) + mid_alt for i in range(2)]
        y_ref[0, s * k:s * (k + 2), :] = jnp.concatenate(pair, axis=0).astype(BF16)
    flipped = [jnp.take_along_axis(tile(a, k) - tile(bm, k) + mid_alt, gather, axis=0) for k in range(n_t)]
    flipped.append(jnp.take_along_axis(tile(a, n_t) + tile(bm, n_t) + mid_alt, gather, axis=0))
    n_all = seq // s
    for m in range(n_t, n_all, 2):
        pair = [jnp.where(sub == 0, flipped[n_all - mm], flipped[n_all - 1 - mm]) for mm in (m, m + 1)]
        y_ref[0, s * m:s * (m + 2), :] = jnp.concatenate(pair, axis=0).astype(BF16)


def _fnet(u, chan, pos_c, pos_s):
    b, seq, _ = u.shape
    table = pl.BlockSpec(pos_c.shape, lambda i: (0, 0))
    return pl.pallas_call(
        _fnet_kernel,
        grid=(b,),
        in_specs=[pl.BlockSpec((1, seq, FNET_DIM), lambda i: (i, 0, 0)),
                  pl.BlockSpec((FNET_GROUP_DIM, 2 * FNET_GROUP_DIM), lambda i: (0, 0)),
                  table, table],
        out_specs=pl.BlockSpec((1, seq, FNET_DIM), lambda i: (i, 0, 0)),
        out_shape=jax.ShapeDtypeStruct((b, seq, FNET_DIM), BF16),
        scratch_shapes=[pltpu.VMEM((seq, FNET_DIM), BF16)],
        compiler_params=_params("arbitrary"),
        name="fnet",
    )(u, chan, pos_c, pos_s)


def _merge_kernel(x_ref, of_ref, ob_ref, sg_ref, ga_ref, gb_ref, yf_ref, mod_ref, og_ref, n2g_ref,
                  wa_ref, wb_ref, wo_ref, x1_ref, h2_ref):
    n_sub = x_ref.shape[1] // MERGE_SUB

    def mix(rows):
        o = of_ref[0, rows, :] + ob_ref[0, rows, :]
        parts = []
        for h in range(HG_HEADS):
            oh = o[:, h * HG_HEAD_DIM:(h + 1) * HG_HEAD_DIM]
            parts.append(oh * lax.rsqrt(jnp.mean(oh * oh, axis=-1, keepdims=True) + EPS))
        on = jnp.concatenate(parts, axis=-1) * og_ref[...] * sg_ref[0, rows, :].astype(F32)
        y_b = _dot(on.astype(BF16), wb_ref[...])
        y_a = _dot(yf_ref[0, rows, :], wa_ref[...])
        m = ga_ref[0, rows, :].astype(F32) * y_a + gb_ref[0, rows, :].astype(F32) * y_b
        return _dot(m.astype(BF16), wo_ref[...])

    def finish(rows, yx):
        x1 = x_ref[0, rows, :] + mod_ref[0, 2:3, :] * yx
        x1_ref[0, rows, :] = x1
        h2 = _modulated_norm(x1, n2g_ref[...], mod_ref[0, 3:4, :], mod_ref[0, 4:5, :])
        h2_ref[0, rows, :] = h2.astype(BF16)

    pending = None
    for i in range(n_sub):
        rows = slice(i * MERGE_SUB, (i + 1) * MERGE_SUB)
        yx = mix(rows)
        if pending is not None:
            finish(*pending)
        pending = (rows, yx)
    finish(*pending)


def _merge(x, o_f, o_b, gates, y_f, mod, og, n2g, w_a, w_b, w_out, *, tm):
    b, l, d = x.shape
    tok = lambda w, col: pl.BlockSpec((1, tm, w), lambda i, j: (i, j, col))
    const = lambda shape: pl.BlockSpec(shape, lambda i, j: tuple(0 for _ in shape))
    return pl.pallas_call(
        _merge_kernel,
        grid=(b, l // tm),
        in_specs=[tok(d, 0), tok(HG_DIM, 0), tok(HG_DIM, 0), tok(HG_DIM, 2 * d // HG_DIM),
                  tok(d, 0), tok(d, 1), tok(FNET_DIM, 0),
                  pl.BlockSpec((1, N_MOD, d), lambda i, j: (i, 0, 0)),
                  const((1, HG_DIM)), const((1, d)),
                  const((FNET_DIM, d)), const((HG_DIM, d)), const((d, d))],
        out_specs=(tok(d, 0), tok(d, 0)),
        out_shape=(jax.ShapeDtypeStruct((b, l, d), F32), jax.ShapeDtypeStruct((b, l, d), BF16)),
        compiler_params=_params("arbitrary", "arbitrary"),
        name="merge",
    )(x, o_f, o_b, gates, gates, gates, y_f, mod, og, n2g, w_a, w_b, w_out)


assert GRID_W == SUBLANES * SUBLANES
FFN_GROUP = 8


def _ffn_permute_tokens(h_ref, hp_ref, tmp_ref):
    n_tok = tmp_ref.shape[1]
    s = SUBLANES

    def body(g, carry):
        tok = pl.ds(pl.multiple_of(g * n_tok, n_tok), n_tok)
        hf = h_ref[0, tok, :].astype(F32)
        for lt in range(tmp_ref.shape[0]):
            for r in range(n_tok // GRID_W):
                for a in range(s):
                    r0 = r * GRID_W + s * a
                    tmp_ref[lt, pl.ds(r * GRID_W + a, s, stride=s), :] = (
                        hf[r0:r0 + s, lt * LANES:(lt + 1) * LANES])
        for lt in range(tmp_ref.shape[0]):
            hp_ref[tok, lt * LANES:(lt + 1) * LANES] = tmp_ref[lt].astype(BF16)
        return carry

    lax.fori_loop(0, h_ref.shape[1] // n_tok, body, 0)


def _ffn_up(hp_ref, up_refs, row0, n_rows):
    h = hp_ref[row0 * GRID_W:(row0 + n_rows) * GRID_W, :]
    return [_dot(h, up_ref[...]) for up_ref in up_refs]


def _pair(lo, hi):
    return jnp.concatenate([lo, hi], axis=0).astype(BF16)


def _ffn_row(z, r, taps, bias, first, last, sub):
    s = SUBLANES
    rows_b = lambda b: slice(r * GRID_W + s * b, r * GRID_W + s * b + s)
    lo = [z[rows_b(b), 0:LANES] for b in range(s)]
    hi = [z[rows_b(b), LANES:2 * LANES] for b in range(s)]
    mid = [_pair(lo[b], hi[b]) for b in range(s)]
    down = lambda v: jnp.where(sub == 0, 0.0, pltpu.roll(v, 1, axis=0))
    left = [_pair(down(lo[s - 1]), down(hi[s - 1]))] + mid[0:s - 1]
    up = lambda v: jnp.where(sub == s - 1, 0.0, pltpu.roll(v, s - 1, axis=0))
    right = mid[1:s] + [_pair(up(lo[0]), up(hi[0]))]
    term = lambda kh: [left[b] * taps[3 * kh] + mid[b] * taps[3 * kh + 1] + right[b] * taps[3 * kh + 2]
                       for b in range(s)]
    above = None if first else term(2)
    here = term(1)
    below = None if last else [t + bias for t in term(0)]
    return above, here, below


def _ffn_down(a_ref, dn_ref, y_ref, row0, n_rows):
    tok = slice(row0 * GRID_W, (row0 + n_rows) * GRID_W)
    part = _dot(a_ref[tok, :].astype(BF16), dn_ref[...])
    for j in range(y_ref.shape[1]):
        y_ref[0, j, tok, :] += part[:, j * LANES:(j + 1) * LANES]


def _ffn_kernel(h_ref, up1_ref, up2_ref, cw1_ref, cw2_ref, cb1_ref, cb2_ref, dn_ref, y_ref,
                hp_ref, a_ref, wp_ref, tmp_ref, wup_ref, wdn_ref):
    t = pl.program_id(1)
    rows = hp_ref.shape[0] // GRID_W
    grp = FFN_GROUP
    s = SUBLANES

    @pl.when(t == 0)
    def _():
        _ffn_permute_tokens(h_ref, hp_ref, tmp_ref)
        y_ref[...] = jnp.zeros_like(y_ref)

    wup_ref[0] = up1_ref[...].astype(BF16)
    wup_ref[1] = up2_ref[...].astype(BF16)
    wdn_ref[...] = dn_ref[...].astype(BF16)
    up_refs = (wup_ref.at[0], wup_ref.at[1])

    for half, (cw_ref, cb_ref) in enumerate(((cw1_ref, cb1_ref), (cw2_ref, cb2_ref))):
        for k in range(10):
            row = cw_ref[k:k + 1, :] if k < 9 else cb_ref[...]
            wp_ref[half, k] = _pair(jnp.broadcast_to(row[:, 0:LANES], (s, LANES)),
                                    jnp.broadcast_to(row[:, LANES:2 * LANES], (s, LANES)))
    taps = [[wp_ref[half, k] for k in range(9)] for half in range(2)]
    bias = [wp_ref[half, 9] for half in range(2)]
    sub = lax.broadcasted_iota(jnp.int32, (s, LANES), 0)

    def gate(c, out_row):
        for b in range(s):
            act = (_silu(c[0][b]) * c[1][b]).astype(F32)
            tok_b = slice(out_row * GRID_W + s * b, out_row * GRID_W + s * b + s)
            a_ref[tok_b, 0:LANES] = act[0:s]
            a_ref[tok_b, LANES:2 * LANES] = act[s:2 * s]

    prev = [None, None]
    cur = [[bias[half]] * s for half in range(2)]
    z_next = _ffn_up(hp_ref, up_refs, 0, grp)
    for row0 in range(0, rows, grp):
        z_cur = z_next
        if row0 + grp < rows:
            z_next = _ffn_up(hp_ref, up_refs, row0 + grp, grp)
        for rr in range(grp):
            r = row0 + rr
            done = []
            for half in range(2):
                above, here, below = _ffn_row(z_cur[half], rr, taps[half], bias[half],
                                              r == 0, r == rows - 1, sub)
                if above is not None:
                    done.append([prev[half][b] + above[b] for b in range(s)])
                prev[half] = [cur[half][b] + here[b] for b in range(s)]
                cur[half] = below
            if done:
                gate(done, r - 1)
            if rr == 0 and row0 > 0:
                _ffn_down(a_ref, wdn_ref, y_ref, row0 - grp, grp)
    gate(prev, rows - 1)
    _ffn_down(a_ref, wdn_ref, y_ref, rows - grp, grp)


def _ffn(h2, up, conv_w, conv_b, down):
    b, l, d = h2.shape
    rows = l // GRID_W
    tc = FFN_TILE
    nt = D_FF // tc
    assert rows % FFN_GROUP == 0 and tc == 2 * LANES, "the conv pairs the two lane tiles of a channel tile"
    return pl.pallas_call(
        _ffn_kernel,
        grid=(b, nt),
        in_specs=[pl.BlockSpec((1, l, d), lambda i, t: (i, 0, 0)),
                  pl.BlockSpec((d, tc), lambda i, t: (0, t)),
                  pl.BlockSpec((d, tc), lambda i, t: (0, nt + t)),
                  pl.BlockSpec((9, tc), lambda i, t: (0, t)),
                  pl.BlockSpec((9, tc), lambda i, t: (0, nt + t)),
                  pl.BlockSpec((1, tc), lambda i, t: (0, t)),
                  pl.BlockSpec((1, tc), lambda i, t: (0, nt + t)),
                  pl.BlockSpec((tc, d), lambda i, t: (t, 0))],
        out_specs=pl.BlockSpec((1, d // LANES, l, LANES), lambda i, t: (i, 0, 0, 0)),
        out_shape=jax.ShapeDtypeStruct((b, d // LANES, l, LANES), F32),
        scratch_shapes=[pltpu.VMEM((l, d), BF16),
                        pltpu.VMEM((l, tc), F32),
                        pltpu.VMEM((2, 10, 2 * SUBLANES, LANES), BF16),
                        pltpu.VMEM((d // LANES, FFN_GROUP * GRID_W, LANES), F32),
                        pltpu.VMEM((2, d, tc), BF16),
                        pltpu.VMEM((tc, d), BF16)],
        compiler_params=_params("arbitrary", "arbitrary"),
        name="ffn",
    )(h2, up, up, conv_w, conv_w, conv_b, conv_b, down)


def _final_kernel(x1_ref, y_ref, mod_ref, g_ref, o_ref):
    n_slab = y_ref.shape[1]
    s = SUBLANES
    for r in range(y_ref.shape[2] // GRID_W):
        for a in range(s):
            tok = slice(r * GRID_W + s * a, r * GRID_W + s * a + s)
            y = jnp.concatenate([y_ref[0, j, pl.ds(r * GRID_W + a, s, stride=s), :]
                                 for j in range(n_slab)], axis=-1)
            x = x1_ref[0, tok, :] + mod_ref[0, 5:6, :] * y
            ms = jnp.mean(x * x, axis=-1, keepdims=True)
            o_ref[0, tok, :] = x * lax.rsqrt(ms + EPS) * g_ref[...]


def _final(x1, y, mod, g, *, tm):
    b, l, d = x1.shape
    tok = pl.BlockSpec((1, tm, d), lambda i, j: (i, j, 0))
    return pl.pallas_call(
        _final_kernel,
        grid=(b, l // tm),
        in_specs=[tok, pl.BlockSpec((1, d // LANES, tm, LANES), lambda i, j: (i, 0, j, 0)),
                  pl.BlockSpec((1, N_MOD, d), lambda i, j: (i, 0, 0)),
                  pl.BlockSpec((1, d), lambda i, j: (0, 0))],
        out_specs=tok,
        out_shape=jax.ShapeDtypeStruct((b, l, d), F32),
        compiler_params=_params("arbitrary", "arbitrary"),
        name="final",
    )(x1, y, mod, g)


def kernel(x, c, ctx, c_ctx, ada_w, ada_b, norm1_g, w_in, hg_lb, hg_onorm_g, w_a, w_b, w_out,
           norm2_g, ffn_up, ffn_conv_w, ffn_conv_b, ffn_down, final_g):
    b, seq, d = x.shape
    assert ada_w.shape[0] == 1, "single-layer kernel"
    assert (d, seq % GRID_W) == (D_MODEL, 0)

    lb = jnp.cumsum(jax.nn.softmax(hg_lb.astype(F32), axis=0), axis=0)[0].reshape(1, HG_DIM)

    n_rows = 16
    rows = jnp.concatenate([c, c_ctx[None, :], jnp.zeros((n_rows - b - 1, d), F32)], axis=0)
    mod = _ada(rows, ada_w[0], ada_b[0])
    mod_x = mod[:b].reshape(b, N_MOD, d)
    mod_c = mod[b:b + 1].reshape(1, N_MOD, d)

    w_in_b = w_in[0].astype(BF16)
    g1 = norm1_g[0].reshape(1, d)
    hg_x, u_f, gates = _inproj(x, mod_x, g1, lb, w_in_b, tm=1024, full=True)
    (hg_c,) = _inproj(ctx.reshape(1, -1, d), mod_c, g1, lb, w_in_b, tm=1024, full=False)
    hg_c = hg_c.reshape(b, ctx.shape[1], 4 * HG_DIM)

    s_ctx = _hgrn(hg_c, None, emit_o=False)
    o_f, o_b = _hgrn(hg_x, s_ctx, emit_o=True)

    chan, pos_c, pos_s = (jnp.asarray(t).astype(BF16) for t in _dft_tables(seq))
    y_f = _fnet(u_f, chan, pos_c, pos_s)

    x1, h2 = _merge(x, o_f, o_b, gates, y_f, mod_x, hg_onorm_g[0].reshape(1, HG_DIM),
                    norm2_g[0].reshape(1, d), w_a[0].astype(BF16), w_b[0].astype(BF16),
                    w_out[0].astype(BF16), tm=1024)

    y = _ffn(h2, ffn_up[0], ffn_conv_w[0].reshape(9, 2 * D_FF),
             ffn_conv_b[0].reshape(1, 2 * D_FF), ffn_down[0])
    return _final(x1, y, mod_x, final_g.reshape(1, d), tm=2048)
```

```python
import functools

import numpy as np
import jax
import jax.numpy as jnp
from jax import lax
from jax.experimental import pallas as pl
from jax.experimental.pallas import tpu as pltpu

D_MODEL = 1024
GRID_W = 64
FNET_DIM = 512
FNET_GROUPS = 4
FNET_GROUP_DIM = FNET_DIM // FNET_GROUPS
HG_DIM = 512
HG_HEADS = 4
HG_HEAD_DIM = HG_DIM // HG_HEADS
D_FF = 2816
N_MOD = 6
EPS = 1e-6

COL_FNET, COL_Q, COL_FF, COL_FB, COL_I, COL_G = 0, 1, 2, 3, 4, 5

INPROJ_SUB = 256
MERGE_SUB = 256
HG_BLOCK = 1024
HG_CHUNK = 64
HG_UNROLL = 4
HG_SUB = 16
FFN_TILE = 256

BF16 = jnp.bfloat16
F32 = jnp.float32
LANES = 128
SUBLANES = 8
VMEM_LIMIT = 56 * 1024 * 1024


def _sigmoid(x):
    return 1.0 / (1.0 + jnp.exp(-x))


def _silu(x):
    return x * _sigmoid(x)


def _dot(a, b):
    return jnp.dot(a, b, preferred_element_type=F32)


def _dot_nt(a, b):
    return lax.dot_general(a, b, (((1,), (1,)), ((), ())), preferred_element_type=F32)


def _params(*sem, flags=None):
    return pltpu.CompilerParams(dimension_semantics=sem, vmem_limit_bytes=VMEM_LIMIT, flags=flags)


def _ada_kernel(c_ref, w_ref, b_ref, o_ref):
    a = _silu(c_ref[...])
    o_ref[...] = _dot(a.astype(BF16), w_ref[...].astype(BF16)) + b_ref[...]


def _ada(rows, ada_w, ada_b):
    m, d = rows.shape
    n = ada_w.shape[1]
    tn = 1536
    return pl.pallas_call(
        _ada_kernel,
        grid=(n // tn,),
        in_specs=[pl.BlockSpec((m, d), lambda j: (0, 0)),
                  pl.BlockSpec((d, tn), lambda j: (0, j)),
                  pl.BlockSpec((1, tn), lambda j: (0, j))],
        out_specs=pl.BlockSpec((m, tn), lambda j: (0, j)),
        out_shape=jax.ShapeDtypeStruct((m, n), F32),
        compiler_params=_params("arbitrary"),
        name="ada",
    )(rows, ada_w, ada_b.reshape(1, n))


def _modulated_norm(x, g, shift, scale):
    ms = jnp.mean(x * x, axis=-1, keepdims=True)
    return (x * lax.rsqrt(ms + EPS) * g) * (1.0 + scale) + shift


def _inproj_kernel(x_ref, mod_ref, g_ref, lb_ref, w_ref, *o_refs, hg_col0, full):
    lb = lb_ref[...]
    hg_ref = o_refs[0]
    n_sub = x_ref.shape[1] // INPROJ_SUB

    def normed(i):
        rows = slice(i * INPROJ_SUB, (i + 1) * INPROJ_SUB)
        return _modulated_norm(x_ref[0, rows, :], g_ref[...], mod_ref[0, 0:1, :],
                               mod_ref[0, 1:2, :]).astype(BF16)

    hb_next = normed(0)
    for i in range(n_sub):
        hb = hb_next
        rows = slice(i * INPROJ_SUB, (i + 1) * INPROJ_SUB)

        def proj(col, width=HG_DIM):
            return _dot(hb, w_ref[:, col * HG_DIM:col * HG_DIM + width])

        hg_ref[0, rows, 0:HG_DIM] = _silu(proj(hg_col0))
        if i + 1 < n_sub:
            hb_next = normed(i + 1)
        for n in (1, 2):
            f = lb + (1.0 - lb) * _sigmoid(proj(hg_col0 + n))
            hg_ref[0, rows, n * HG_DIM:(n + 1) * HG_DIM] = jnp.log2(f)
        hg_ref[0, rows, 3 * HG_DIM:4 * HG_DIM] = proj(hg_col0 + 3)
        if full:
            uf_ref, gate_ref = o_refs[1], o_refs[2]
            uf_ref[0, rows, :] = proj(COL_FNET).astype(BF16)
            gate_ref[0, rows, 0:2 * D_MODEL] = _sigmoid(proj(COL_G + 1, 2 * D_MODEL)).astype(BF16)
            gate_ref[0, rows, 2 * D_MODEL:2 * D_MODEL + HG_DIM] = _silu(proj(COL_G)).astype(BF16)


def _inproj(x, mod, g, lb, w, *, tm, full):
    b, l, d = x.shape
    n = w.shape[1] if full else (COL_I + 1) * HG_DIM
    mod_map = (lambda i, j: (i, 0, 0)) if full else (lambda i, j: (0, 0, 0))
    tok = lambda width, dtype: (pl.BlockSpec((1, tm, width), lambda i, j: (i, j, 0)),
                                jax.ShapeDtypeStruct((b, l, width), dtype))
    outs = [tok(4 * HG_DIM, F32)]
    if full:
        outs += [tok(FNET_DIM, BF16), tok(HG_DIM + 2 * D_MODEL, BF16)]
    return pl.pallas_call(
        functools.partial(_inproj_kernel, hg_col0=COL_Q, full=full),
        grid=(b, l // tm),
        in_specs=[pl.BlockSpec((1, tm, d), lambda i, j: (i, j, 0)),
                  pl.BlockSpec((1, N_MOD, d), mod_map),
                  pl.BlockSpec((1, d), lambda i, j: (0, 0)),
                  pl.BlockSpec((1, HG_DIM), lambda i, j: (0, 0)),
                  pl.BlockSpec((d, n), lambda i, j: (0, 0))],
        out_specs=tuple(o[0] for o in outs),
        out_shape=tuple(o[1] for o in outs),
        compiler_params=_params("arbitrary", "arbitrary"),
        name="inproj_x" if full else "inproj_ctx",
    )(x, mod, g, lb, w)


def _hgrn_step(dirs, st_ref):
    c = HG_CHUNK
    hd = HG_HEAD_DIM
    r_idx = lax.broadcasted_iota(jnp.int32, (c, c), 0)
    c_idx = lax.broadcasted_iota(jnp.int32, (c, c), 1)
    pairs = [slice(p * 2 * hd, (p + 1) * 2 * hd) for p in range(HG_HEADS // 2)]

    def block_diag(a0, a1):
        z0, z1 = jnp.zeros(a0.shape, a0.dtype), jnp.zeros(a1.shape, a1.dtype)
        return jnp.concatenate([jnp.concatenate([a0, z0], axis=1),
                                jnp.concatenate([z1, a1], axis=1)], axis=0)

    cums = []
    for q, l2f, v, reverse, o_ref, row0, base in dirs:
        tri = jnp.where((c_idx >= r_idx) if reverse else (c_idx <= r_idx), 1.0, 0.0).astype(BF16)
        hi = l2f.astype(BF16)
        lo = (l2f - hi.astype(F32)).astype(BF16)
        cums.append(_dot(tri, hi) + _dot(tri, lo))

    state_in, state_upd, intra = [], [], []
    for (q, l2f, v, reverse, o_ref, row0, base), cum in zip(dirs, cums):
        k = 1.0 - jnp.exp2(l2f)
        last = 0 if reverse else c - 1
        total = cum[last:last + 1, :]
        q_in = (q * jnp.exp2(cum)).astype(BF16)
        k_out = (k * jnp.exp2(total - cum)).astype(BF16)
        e_tot = jnp.exp2(total)
        v_b = v.astype(BF16)
        for p, ps in enumerate(pairs):
            h0, h1 = slice(ps.start, ps.start + hd), slice(ps.start + hd, ps.stop)
            v_t = jnp.concatenate([v[:, h0], v[:, h1]], axis=0).T.astype(BF16)
            state_in.append(q_in[:, ps])
            state_upd.append((base + p, e_tot[:, ps], v_t, block_diag(k_out[:, h0], k_out[:, h1])))
            if o_ref is None:
                continue
            for i in range(c // HG_SUB):
                r0 = i * HG_SUB
                rows = slice(r0, r0 + HG_SUB)
                cols = slice(r0, c) if reverse else slice(0, r0 + HG_SUB)
                mid = r0 + HG_SUB // 2 if reverse else r0 + HG_SUB // 2 - 1
                m = cum[mid:mid + 1, ps]
                q_t = (q[rows, ps] * jnp.exp2(cum[rows, ps] - m)).astype(BF16)
                k_t = (k[cols, ps] * jnp.exp2(m - cum[cols, ps])).astype(BF16)
                intra.append((o_ref, row0, rows, cols, ps, reverse, len(state_in) - 1, q_t,
                              block_diag(k_t[:, :hd], k_t[:, hd:]),
                              block_diag(v_b[cols, h0], v_b[cols, h1])))

    scores = [_dot_nt(item[7], item[8]) for item in intra]
    kv = [_dot(v_t, k_bd) for _, _, v_t, k_bd in state_upd]
    states = {}
    inter = []
    for n, (idx, e_tot_p, _, _) in enumerate(state_upd):
        st = states[idx] if idx in states else st_ref[idx]
        if intra:
            st_b = st.astype(BF16)
            inter.append(_dot_nt(state_in[n], block_diag(st_b[:, :hd], st_b[:, hd:])))
        states[idx] = st * e_tot_p + kv[n]
    for idx, st in states.items():
        st_ref[idx] = st

    for (o_ref, row0, rows, cols, ps, reverse, chain, _, _, v_bd), a in zip(intra, scores):
        n_cols = cols.stop - cols.start
        row_g = rows.start + lax.broadcasted_iota(jnp.int32, (HG_SUB, 2 * n_cols), 0)
        lane = lax.broadcasted_iota(jnp.int32, (HG_SUB, 2 * n_cols), 1)
        col_g = cols.start + jnp.where(lane >= n_cols, lane - n_cols, lane)
        a = jnp.where((col_g >= row_g) if reverse else (col_g <= row_g), a, 0.0)
        o_ref[0, pl.ds(row0 + rows.start, HG_SUB), ps] = _dot(a.astype(BF16), v_bd) + inter[chain][rows]


def _hgrn_kernel(*refs, has_s0, emit_o):
    it = iter(refs)
    fwd_refs = [next(it) for _ in range(3)]
    bwd_refs = [next(it) for _ in range(3)]
    qf_ref = fwd_refs[0]
    s0_ref = next(it) if has_s0 else None
    of_ref = next(it) if emit_o else None
    ob_ref = next(it) if emit_o else None
    sout_ref = None if emit_o else next(it)
    st_ref = next(it)
    j = pl.program_id(1)
    n_inner = qf_ref.shape[1] // HG_CHUNK

    @pl.when(j == 0)
    def _():
        if has_s0:
            st_ref[...] = s0_ref[0]
        else:
            st_ref[...] = jnp.zeros_like(st_ref)

    span = HG_UNROLL * HG_CHUNK

    def body(it, carry):
        dirs = []
        for jj in range(HG_UNROLL):
            rf = pl.multiple_of(it * span, span) + jj * HG_CHUNK
            rb = pl.multiple_of((n_inner // HG_UNROLL - 1 - it) * span, span) + (HG_UNROLL - 1 - jj) * HG_CHUNK
            sf, sb = pl.ds(rf, HG_CHUNK), pl.ds(rb, HG_CHUNK)
            dirs.append(tuple(r[0, sf, :] for r in fwd_refs) + (False, of_ref, rf, 0))
            dirs.append(tuple(r[0, sb, :] for r in bwd_refs) + (True, ob_ref, rb, HG_HEADS // 2))
        _hgrn_step(dirs, st_ref)
        return carry

    lax.fori_loop(0, n_inner // HG_UNROLL, body, 0)

    if not emit_o:
        @pl.when(j == pl.num_programs(1) - 1)
        def _():
            sout_ref[0] = st_ref[...]


def _hgrn(u, s0, *, emit_o):
    b, l, _ = u.shape
    c = min(HG_BLOCK, l)
    n = l // c
    blk = (1, c, HG_DIM)
    fwd = lambda col: pl.BlockSpec(blk, lambda i, j: (i, j, col))
    bwd = lambda col: pl.BlockSpec(blk, lambda i, j: (i, n - 1 - j, col))
    st_dims = (HG_HEADS, HG_HEAD_DIM, 2 * HG_HEAD_DIM)
    state_spec = pl.BlockSpec((1,) + st_dims, lambda i, j: (i, 0, 0, 0))
    state_shape = jax.ShapeDtypeStruct((b,) + st_dims, F32)
    in_specs = [fwd(0), fwd(1), fwd(3), bwd(0), bwd(2), bwd(3)]
    args = [u, u, u, u, u, u]
    if s0 is not None:
        in_specs.append(state_spec)
        args.append(s0)
    if emit_o:
        o_shape = jax.ShapeDtypeStruct((b, l, HG_DIM), F32)
        out_shape = (o_shape, o_shape)
        out_specs = (pl.BlockSpec(blk, lambda i, j: (i, j, 0)),
                     pl.BlockSpec(blk, lambda i, j: (i, n - 1 - j, 0)))
    else:
        out_shape = state_shape
        out_specs = state_spec
    return pl.pallas_call(
        functools.partial(_hgrn_kernel, has_s0=s0 is not None, emit_o=emit_o),
        grid=(b, n),
        in_specs=in_specs,
        out_specs=out_specs,
        out_shape=out_shape,
        scratch_shapes=[pltpu.VMEM(st_dims, F32)],
        compiler_params=_params("arbitrary", "arbitrary"),
        name="hgrn_x" if emit_o else "hgrn_ctx",
    )(*args)


FNET_PAD = 2 * SUBLANES


def _dft_tables(seq):
    gd = FNET_GROUP_DIM
    kc = (np.arange(gd)[:, None] * np.arange(gd)[None, :]) % gd
    ang_c = 2.0 * np.pi * kc / gd
    scale = 1.0 / np.sqrt(float(seq) * gd)
    chan = np.concatenate([np.cos(ang_c), np.sin(ang_c)], axis=1) * scale
    half = seq // 2
    kl = (np.arange(half + FNET_PAD)[:, None] * np.arange(half)[None, :]) % seq
    ang_l = 2.0 * np.pi * kl / seq
    keep = (np.arange(half + FNET_PAD) <= half)[:, None]
    pos_c = np.where(keep, np.cos(ang_l), 0.0)
    pos_s = np.where(keep, -np.sin(ang_l), 0.0)
    return chan.astype(np.float32), pos_c.astype(np.float32), pos_s.astype(np.float32)


def _mirror_gather(width):
    sub = lax.broadcasted_iota(jnp.int32, (SUBLANES, width), 0)
    return jnp.where(sub == 0, 0, SUBLANES - sub), sub


def _row_tile(v, k):
    return v[SUBLANES * k:SUBLANES * (k + 1), :]


def _fold_positions(v, sign):
    seq = v.shape[0]
    n_tiles = seq // SUBLANES
    gather, sub = _mirror_gather(v.shape[1])
    tile = lambda k: _row_tile(v, k)
    out = []
    for k in range(n_tiles // 2):
        tail = jnp.take_along_axis(tile(n_tiles - 1 - k), gather, axis=0)
        if k == 0:
            mirror = jnp.where(sub == 0, 0.0, tail)
        else:
            mirror = jnp.where(sub == 0, jnp.take_along_axis(tile(n_tiles - k), gather, axis=0), tail)
        out.append(tile(k) + mirror if sign > 0 else tile(k) - mirror)
    return jnp.concatenate(out, axis=0)


def _fnet_kernel(u_ref, chan_ref, posc_ref, poss_ref, y_ref, pq_ref):
    seq = u_ref.shape[1]
    half = seq // 2
    gd = FNET_GROUP_DIM
    mids = []
    for g in range(FNET_GROUPS):
        gs = slice(g * gd, (g + 1) * gd)
        pq = _dot(u_ref[0, :, gs], chan_ref[...])
        pq_ref[0:half, gs] = _fold_positions(pq[:, :gd], 1).astype(BF16)
        pq_ref[half:seq, gs] = _fold_positions(pq[:, gd:], -1).astype(BF16)
        mids.append(pq[half:half + 1, :gd])
    gather, sub = _mirror_gather(FNET_DIM)
    mid = jnp.concatenate(mids, axis=1)
    mid_alt = jnp.where(sub % 2 == 0, mid, -mid)

    a = _dot(posc_ref[...], pq_ref[0:half, :])
    bm = _dot(poss_ref[...], pq_ref[half:seq, :])
    tile = _row_tile
    s = SUBLANES
    n_t = half // s
    for k in range(0, n_t, 2):
        pair = [tile(a, k + i) + tile(bm, k + i) + mid_alt for i in range(2)]
        y_ref[0, s * k:s * (k + 2), :] = jnp.concatenate(pair, axis=0).astype(BF16)
    flipped = [jnp.take_along_axis(tile(a, k) - tile(bm, k) + mid_alt, gather, axis=0) for k in range(n_t)]
    flipped.append(jnp.take_along_axis(tile(a, n_t) + tile(bm, n_t) + mid_alt, gather, axis=0))
    n_all = seq // s
    for m in range(n_t, n_all, 2):
        pair = [jnp.where(sub == 0, flipped[n_all - mm], flipped[n_all - 1 - mm]) for mm in (m, m + 1)]
        y_ref[0, s * m:s * (m + 2), :] = jnp.concatenate(pair, axis=0).astype(BF16)


def _fnet(u, chan, pos_c, pos_s):
    b, seq, _ = u.shape
    table = pl.BlockSpec(pos_c.shape, lambda i: (0, 0))
    return pl.pallas_call(
        _fnet_kernel,
        grid=(b,),
        in_specs=[pl.BlockSpec((1, seq, FNET_DIM), lambda i: (i, 0, 0)),
                  pl.BlockSpec((FNET_GROUP_DIM, 2 * FNET_GROUP_DIM), lambda i: (0, 0)),
                  table, table],
        out_specs=pl.BlockSpec((1, seq, FNET_DIM), lambda i: (i, 0, 0)),
        out_shape=jax.ShapeDtypeStruct((b, seq, FNET_DIM), BF16),
        scratch_shapes=[pltpu.VMEM((seq, FNET_DIM), BF16)],
        compiler_params=_params("arbitrary"),
        name="fnet",
    )(u, chan, pos_c, pos_s)


def _merge_kernel(x_ref, of_ref, ob_ref, sg_ref, ga_ref, gb_ref, yf_ref, mod_ref, og_ref, n2g_ref,
                  wa_ref, wb_ref, wo_ref, x1_ref, h2_ref):
    n_sub = x_ref.shape[1] // MERGE_SUB

    def mix(rows):
        y_a = _dot(yf_ref[0, rows, :], wa_ref[...])
        o = of_ref[0, rows, :] + ob_ref[0, rows, :]
        parts = []
        for h in range(HG_HEADS):
            oh = o[:, h * HG_HEAD_DIM:(h + 1) * HG_HEAD_DIM]
            parts.append(oh * lax.rsqrt(jnp.mean(oh * oh, axis=-1, keepdims=True) + EPS))
        on = jnp.concatenate(parts, axis=-1) * og_ref[...] * sg_ref[0, rows, :].astype(F32)
        y_b = _dot(on.astype(BF16), wb_ref[...])
        m = ga_ref[0, rows, :].astype(F32) * y_a + gb_ref[0, rows, :].astype(F32) * y_b
        return _dot(m.astype(BF16), wo_ref[...])

    def finish(rows, yx):
        x1 = x_ref[0, rows, :] + mod_ref[0, 2:3, :] * yx
        x1_ref[0, rows, :] = x1
        h2 = _modulated_norm(x1, n2g_ref[...], mod_ref[0, 3:4, :], mod_ref[0, 4:5, :])
        h2_ref[0, rows, :] = h2.astype(BF16)

    pending = None
    for i in range(n_sub):
        rows = slice(i * MERGE_SUB, (i + 1) * MERGE_SUB)
        yx = mix(rows)
        if pending is not None:
            finish(*pending)
        pending = (rows, yx)
    finish(*pending)


def _merge(x, o_f, o_b, gates, y_f, mod, og, n2g, w_a, w_b, w_out, *, tm):
    b, l, d = x.shape
    tok = lambda w, col: pl.BlockSpec((1, tm, w), lambda i, j: (i, j, col))
    const = lambda shape: pl.BlockSpec(shape, lambda i, j: tuple(0 for _ in shape))
    return pl.pallas_call(
        _merge_kernel,
        grid=(b, l // tm),
        in_specs=[tok(d, 0), tok(HG_DIM, 0), tok(HG_DIM, 0), tok(HG_DIM, 2 * d // HG_DIM),
                  tok(d, 0), tok(d, 1), tok(FNET_DIM, 0),
                  pl.BlockSpec((1, N_MOD, d), lambda i, j: (i, 0, 0)),
                  const((1, HG_DIM)), const((1, d)),
                  const((FNET_DIM, d)), const((HG_DIM, d)), const((d, d))],
        out_specs=(tok(d, 0), tok(d, 0)),
        out_shape=(jax.ShapeDtypeStruct((b, l, d), F32), jax.ShapeDtypeStruct((b, l, d), BF16)),
        compiler_params=_params("arbitrary", "arbitrary"),
        name="merge",
    )(x, o_f, o_b, gates, gates, gates, y_f, mod, og, n2g, w_a, w_b, w_out)


assert GRID_W == SUBLANES * SUBLANES
FFN_GROUP = 8


def _ffn_permute_tokens(h_ref, hp_ref, tmp_ref):
    n_tok = tmp_ref.shape[1]
    s = SUBLANES

    def body(g, carry):
        tok = pl.ds(pl.multiple_of(g * n_tok, n_tok), n_tok)
        hf = h_ref[0, tok, :].astype(F32)
        for lt in range(tmp_ref.shape[0]):
            for r in range(n_tok // GRID_W):
                for a in range(s):
                    r0 = r * GRID_W + s * a
                    tmp_ref[lt, pl.ds(r * GRID_W + a, s, stride=s), :] = (
                        hf[r0:r0 + s, lt * LANES:(lt + 1) * LANES])
        for lt in range(tmp_ref.shape[0]):
            hp_ref[tok, lt * LANES:(lt + 1) * LANES] = tmp_ref[lt].astype(BF16)
        return carry

    lax.fori_loop(0, h_ref.shape[1] // n_tok, body, 0)


def _ffn_up(hp_ref, up_refs, row0, n_rows):
    h = hp_ref[row0 * GRID_W:(row0 + n_rows) * GRID_W, :]
    return [_dot(h, up_ref[...]) for up_ref in up_refs]


def _pair(lo, hi):
    return jnp.concatenate([lo, hi], axis=0).astype(BF16)


def _ffn_row(z, r, taps, bias, first, last, sub):
    s = SUBLANES
    rows_b = lambda b: slice(r * GRID_W + s * b, r * GRID_W + s * b + s)
    lo = [z[rows_b(b), 0:LANES] for b in range(s)]
    hi = [z[rows_b(b), LANES:2 * LANES] for b in range(s)]
    mid = [_pair(lo[b], hi[b]) for b in range(s)]
    down = lambda v: jnp.where(sub == 0, 0.0, pltpu.roll(v, 1, axis=0))
    left = [_pair(down(lo[s - 1]), down(hi[s - 1]))] + mid[0:s - 1]
    up = lambda v: jnp.where(sub == s - 1, 0.0, pltpu.roll(v, s - 1, axis=0))
    right = mid[1:s] + [_pair(up(lo[0]), up(hi[0]))]
    term = lambda kh: [left[b] * taps[3 * kh] + mid[b] * taps[3 * kh + 1] + right[b] * taps[3 * kh + 2]
                       for b in range(s)]
    above = None if first else term(2)
    here = term(1)
    below = None if last else [t + bias for t in term(0)]
    return above, here, below


def _ffn_down(a_ref, dn_ref, y_ref, row0, n_rows):
    tok = slice(row0 * GRID_W, (row0 + n_rows) * GRID_W)
    part = _dot(a_ref[tok, :].astype(BF16), dn_ref[...])
    for j in range(y_ref.shape[1]):
        y_ref[0, j, tok, :] += part[:, j * LANES:(j + 1) * LANES]


def _ffn_kernel(h_ref, up1_ref, up2_ref, cw1_ref, cw2_ref, cb1_ref, cb2_ref, dn_ref, y_ref,
                hp_ref, a_ref, wp_ref, tmp_ref, wup_ref, wdn_ref):
    t = pl.program_id(1)
    rows = hp_ref.shape[0] // GRID_W
    grp = FFN_GROUP
    s = SUBLANES

    @pl.when(t == 0)
    def _():
        _ffn_permute_tokens(h_ref, hp_ref, tmp_ref)
        y_ref[...] = jnp.zeros_like(y_ref)

    wup_ref[0] = up1_ref[...].astype(BF16)
    wup_ref[1] = up2_ref[...].astype(BF16)
    wdn_ref[...] = dn_ref[...].astype(BF16)
    up_refs = (wup_ref.at[0], wup_ref.at[1])

    for half, (cw_ref, cb_ref) in enumerate(((cw1_ref, cb1_ref), (cw2_ref, cb2_ref))):
        for k in range(10):
            row = cw_ref[k:k + 1, :] if k < 9 else cb_ref[...]
            wp_ref[half, k] = _pair(jnp.broadcast_to(row[:, 0:LANES], (s, LANES)),
                                    jnp.broadcast_to(row[:, LANES:2 * LANES], (s, LANES)))
    taps = [[wp_ref[half, k] for k in range(9)] for half in range(2)]
    bias = [wp_ref[half, 9] for half in range(2)]
    sub = lax.broadcasted_iota(jnp.int32, (s, LANES), 0)

    def gate(c, out_row):
        for b in range(s):
            act = (_silu(c[0][b]) * c[1][b]).astype(F32)
            tok_b = slice(out_row * GRID_W + s * b, out_row * GRID_W + s * b + s)
            a_ref[tok_b, 0:LANES] = act[0:s]
            a_ref[tok_b, LANES:2 * LANES] = act[s:2 * s]

    prev = [None, None]
    cur = [[bias[half]] * s for half in range(2)]
    z_next = _ffn_up(hp_ref, up_refs, 0, grp)
    for row0 in range(0, rows, grp):
        z_cur = z_next
        if row0 + grp < rows:
            z_next = _ffn_up(hp_ref, up_refs, row0 + grp, grp)
        for rr in range(grp):
            r = row0 + rr
            done = []
            for half in range(2):
                above, here, below = _ffn_row(z_cur[half], rr, taps[half], bias[half],
                                              r == 0, r == rows - 1, sub)
                if above is not None:
                    done.append([prev[half][b] + above[b] for b in range(s)])
                prev[half] = [cur[half][b] + here[b] for b in range(s)]
                cur[half] = below
            if done:
                gate(done, r - 1)
            if rr == 0 and row0 > 0:
                _ffn_down(a_ref, wdn_ref, y_ref, row0 - grp, grp)
    gate(prev, rows - 1)
    _ffn_down(a_ref, wdn_ref, y_ref, rows - grp, grp)


def _ffn(h2, up, conv_w, conv_b, down):
    b, l, d = h2.shape
    rows = l // GRID_W
    tc = FFN_TILE
    nt = D_FF // tc
    assert rows % FFN_GROUP == 0 and tc == 2 * LANES, "the conv pairs the two lane tiles of a channel tile"
    return pl.pallas_call(
        _ffn_kernel,
        grid=(b, nt),
        in_specs=[pl.BlockSpec((1, l, d), lambda i, t: (i, 0, 0)),
                  pl.BlockSpec((d, tc), lambda i, t: (0, t)),
                  pl.BlockSpec((d, tc), lambda i, t: (0, nt + t)),
                  pl.BlockSpec((9, tc), lambda i, t: (0, t)),
                  pl.BlockSpec((9, tc), lambda i, t: (0, nt + t)),
                  pl.BlockSpec((1, tc), lambda i, t: (0, t)),
                  pl.BlockSpec((1, tc), lambda i, t: (0, nt + t)),
                  pl.BlockSpec((tc, d), lambda i, t: (t, 0))],
        out_specs=pl.BlockSpec((1, d // LANES, l, LANES), lambda i, t: (i, 0, 0, 0)),
        out_shape=jax.ShapeDtypeStruct((b, d // LANES, l, LANES), F32),
        scratch_shapes=[pltpu.VMEM((l, d), BF16),
                        pltpu.VMEM((l, tc), F32),
                        pltpu.VMEM((2, 10, 2 * SUBLANES, LANES), BF16),
                        pltpu.VMEM((d // LANES, FFN_GROUP * GRID_W, LANES), F32),
                        pltpu.VMEM((2, d, tc), BF16),
                        pltpu.VMEM((tc, d), BF16)],
        compiler_params=_params("arbitrary", "arbitrary"),
        name="ffn",
    )(h2, up, up, conv_w, conv_w, conv_b, conv_b, down)


def _final_kernel(x1_ref, y_ref, mod_ref, g_ref, o_ref):
    n_slab = y_ref.shape[1]
    s = SUBLANES
    for r in range(y_ref.shape[2] // GRID_W):
        for a in range(s):
            tok = slice(r * GRID_W + s * a, r * GRID_W + s * a + s)
            y = jnp.concatenate([y_ref[0, j, pl.ds(r * GRID_W + a, s, stride=s), :]
                                 for j in range(n_slab)], axis=-1)
            x = x1_ref[0, tok, :] + mod_ref[0, 5:6, :] * y
            ms = jnp.mean(x * x, axis=-1, keepdims=True)
            o_ref[0, tok, :] = x * lax.rsqrt(ms + EPS) * g_ref[...]


def _final(x1, y, mod, g, *, tm):
    b, l, d = x1.shape
    tok = pl.BlockSpec((1, tm, d), lambda i, j: (i, j, 0))
    return pl.pallas_call(
        _final_kernel,
        grid=(b, l // tm),
        in_specs=[tok, pl.BlockSpec((1, d // LANES, tm, LANES), lambda i, j: (i, 0, j, 0)),
                  pl.BlockSpec((1, N_MOD, d), lambda i, j: (i, 0, 0)),
                  pl.BlockSpec((1, d), lambda i, j: (0, 0))],
        out_specs=tok,
        out_shape=jax.ShapeDtypeStruct((b, l, d), F32),
        compiler_params=_params("arbitrary", "arbitrary"),
        name="final",
    )(x1, y, mod, g)


def kernel(x, c, ctx, c_ctx, ada_w, ada_b, norm1_g, w_in, hg_lb, hg_onorm_g, w_a, w_b, w_out,
           norm2_g, ffn_up, ffn_conv_w, ffn_conv_b, ffn_down, final_g):
    b, seq, d = x.shape
    assert ada_w.shape[0] == 1, "single-layer kernel"
    assert (d, seq % GRID_W) == (D_MODEL, 0)

    lb = jnp.cumsum(jax.nn.softmax(hg_lb.astype(F32), axis=0), axis=0)[0].reshape(1, HG_DIM)

    n_rows = 16
    rows = jnp.concatenate([c, c_ctx[None, :], jnp.zeros((n_rows - b - 1, d), F32)], axis=0)
    mod = _ada(rows, ada_w[0], ada_b[0])
    mod_x = mod[:b].reshape(b, N_MOD, d)
    mod_c = mod[b:b + 1].reshape(1, N_MOD, d)

    w_in_b = w_in[0].astype(BF16)
    g1 = norm1_g[0].reshape(1, d)
    hg_x, u_f, gates = _inproj(x, mod_x, g1, lb, w_in_b, tm=1024, full=True)
    (hg_c,) = _inproj(ctx.reshape(1, -1, d), mod_c, g1, lb, w_in_b, tm=1024, full=False)
    hg_c = hg_c.reshape(b, ctx.shape[1], 4 * HG_DIM)

    s_ctx = _hgrn(hg_c, None, emit_o=False)
    o_f, o_b = _hgrn(hg_x, s_ctx, emit_o=True)

    chan, pos_c, pos_s = (jnp.asarray(t).astype(BF16) for t in _dft_tables(seq))
    y_f = _fnet(u_f, chan, pos_c, pos_s)

    x1, h2 = _merge(x, o_f, o_b, gates, y_f, mod_x, hg_onorm_g[0].reshape(1, HG_DIM),
                    norm2_g[0].reshape(1, d), w_a[0].astype(BF16), w_b[0].astype(BF16),
                    w_out[0].astype(BF16), tm=1024)

    y = _ffn(h2, ffn_up[0], ffn_conv_w[0].reshape(9, 2 * D_FF),
             ffn_conv_b[0].reshape(1, 2 * D_FF), ffn_down[0])
    return _final(x1, y, mod_x, final_g.reshape(1, d), tm=1024)
```

```python
import functools

import numpy as np
import jax
import jax.numpy as jnp
from jax import lax
from jax.experimental import pallas as pl
from jax.experimental.pallas import tpu as pltpu

D_MODEL = 1024
GRID_W = 64
FNET_DIM = 512
FNET_GROUPS = 4
FNET_GROUP_DIM = FNET_DIM // FNET_GROUPS
HG_DIM = 512
HG_HEADS = 4
HG_HEAD_DIM = HG_DIM // HG_HEADS
D_FF = 2816
N_MOD = 6
EPS = 1e-6

COL_FNET, COL_Q, COL_FF, COL_FB, COL_I, COL_G = 0, 1, 2, 3, 4, 5

INPROJ_SUB = 256
MERGE_SUB = 256
HG_BLOCK = 1024
HG_CHUNK = 64
HG_UNROLL = 4
HG_SUB = 16
FFN_TILE = 256

BF16 = jnp.bfloat16
F32 = jnp.float32
LANES = 128
SUBLANES = 8
VMEM_LIMIT = 56 * 1024 * 1024


def _sigmoid(x):
    return 1.0 / (1.0 + jnp.exp(-x))


def _silu(x):
    return x * _sigmoid(x)


def _dot(a, b):
    return jnp.dot(a, b, preferred_element_type=F32)


def _dot_nt(a, b):
    return lax.dot_general(a, b, (((1,), (1,)), ((), ())), preferred_element_type=F32)


def _params(*sem, flags=None):
    return pltpu.CompilerParams(dimension_semantics=sem, vmem_limit_bytes=VMEM_LIMIT, flags=flags)


def _ada_kernel(c_ref, w_ref, b_ref, o_ref):
    a = _silu(c_ref[...])
    o_ref[...] = _dot(a.astype(BF16), w_ref[...].astype(BF16)) + b_ref[...]


def _ada(rows, ada_w, ada_b):
    m, d = rows.shape
    n = ada_w.shape[1]
    tn = 1536
    return pl.pallas_call(
        _ada_kernel,
        grid=(n // tn,),
        in_specs=[pl.BlockSpec((m, d), lambda j: (0, 0)),
                  pl.BlockSpec((d, tn), lambda j: (0, j)),
                  pl.BlockSpec((1, tn), lambda j: (0, j))],
        out_specs=pl.BlockSpec((m, tn), lambda j: (0, j)),
        out_shape=jax.ShapeDtypeStruct((m, n), F32),
        compiler_params=_params("arbitrary"),
        name="ada",
    )(rows, ada_w, ada_b.reshape(1, n))


def _modulated_norm(x, g, shift, scale):
    ms = jnp.mean(x * x, axis=-1, keepdims=True)
    return (x * lax.rsqrt(ms + EPS) * g) * (1.0 + scale) + shift


def _inproj_kernel(x_ref, mod_ref, g_ref, lb_ref, w_ref, *o_refs, hg_col0, full):
    lb = lb_ref[...]
    hg_ref = o_refs[0]
    n_sub = x_ref.shape[1] // INPROJ_SUB

    def normed(i):
        rows = slice(i * INPROJ_SUB, (i + 1) * INPROJ_SUB)
        return _modulated_norm(x_ref[0, rows, :], g_ref[...], mod_ref[0, 0:1, :],
                               mod_ref[0, 1:2, :]).astype(BF16)

    hb_next = normed(0)
    for i in range(n_sub):
        hb = hb_next
        rows = slice(i * INPROJ_SUB, (i + 1) * INPROJ_SUB)

        def proj(col, width=HG_DIM):
            return _dot(hb, w_ref[:, col * HG_DIM:col * HG_DIM + width])

        hg_ref[0, rows, 0:HG_DIM] = _silu(proj(hg_col0))
        if i + 1 < n_sub:
            hb_next = normed(i + 1)
        for n in (1, 2):
            f = lb + (1.0 - lb) * _sigmoid(proj(hg_col0 + n))
            hg_ref[0, rows, n * HG_DIM:(n + 1) * HG_DIM] = jnp.log2(f)
        hg_ref[0, rows, 3 * HG_DIM:4 * HG_DIM] = proj(hg_col0 + 3)
        if full:
            uf_ref, gate_ref = o_refs[1], o_refs[2]
            uf_ref[0, rows, :] = proj(COL_FNET).astype(BF16)
            gate_ref[0, rows, 0:2 * D_MODEL] = _sigmoid(proj(COL_G + 1, 2 * D_MODEL)).astype(BF16)
            gate_ref[0, rows, 2 * D_MODEL:2 * D_MODEL + HG_DIM] = _silu(proj(COL_G)).astype(BF16)


def _inproj(x, mod, g, lb, w, *, tm, full):
    b, l, d = x.shape
    n = w.shape[1] if full else (COL_I + 1) * HG_DIM
    mod_map = (lambda i, j: (i, 0, 0)) if full else (lambda i, j: (0, 0, 0))
    tok = lambda width, dtype: (pl.BlockSpec((1, tm, width), lambda i, j: (i, j, 0)),
                                jax.ShapeDtypeStruct((b, l, width), dtype))
    outs = [tok(4 * HG_DIM, F32)]
    if full:
        outs += [tok(FNET_DIM, BF16), tok(HG_DIM + 2 * D_MODEL, BF16)]
    return pl.pallas_call(
        functools.partial(_inproj_kernel, hg_col0=COL_Q, full=full),
        grid=(b, l // tm),
        in_specs=[pl.BlockSpec((1, tm, d), lambda i, j: (i, j, 0)),
                  pl.BlockSpec((1, N_MOD, d), mod_map),
                  pl.BlockSpec((1, d), lambda i, j: (0, 0)),
                  pl.BlockSpec((1, HG_DIM), lambda i, j: (0, 0)),
                  pl.BlockSpec((d, n), lambda i, j: (0, 0))],
        out_specs=tuple(o[0] for o in outs),
        out_shape=tuple(o[1] for o in outs),
        compiler_params=_params("arbitrary", "arbitrary"),
        name="inproj_x" if full else "inproj_ctx",
    )(x, mod, g, lb, w)


def _hgrn_step(dirs, st_ref):
    c = HG_CHUNK
    hd = HG_HEAD_DIM
    r_idx = lax.broadcasted_iota(jnp.int32, (c, c), 0)
    c_idx = lax.broadcasted_iota(jnp.int32, (c, c), 1)
    pairs = [slice(p * 2 * hd, (p + 1) * 2 * hd) for p in range(HG_HEADS // 2)]

    def block_diag(a0, a1):
        z0, z1 = jnp.zeros(a0.shape, a0.dtype), jnp.zeros(a1.shape, a1.dtype)
        return jnp.concatenate([jnp.concatenate([a0, z0], axis=1),
                                jnp.concatenate([z1, a1], axis=1)], axis=0)

    cums = []
    for q, l2f, v, reverse, o_ref, row0, base in dirs:
        tri = jnp.where((c_idx >= r_idx) if reverse else (c_idx <= r_idx), 1.0, 0.0).astype(BF16)
        hi = l2f.astype(BF16)
        lo = (l2f - hi.astype(F32)).astype(BF16)
        cums.append(_dot(tri, hi) + _dot(tri, lo))

    state_in, state_upd, intra = [], [], []
    for (q, l2f, v, reverse, o_ref, row0, base), cum in zip(dirs, cums):
        k = 1.0 - jnp.exp2(l2f)
        last = 0 if reverse else c - 1
        total = cum[last:last + 1, :]
        q_in = (q * jnp.exp2(cum)).astype(BF16)
        k_out = (k * jnp.exp2(total - cum)).astype(BF16)
        e_tot = jnp.exp2(total)
        v_b = v.astype(BF16)
        for p, ps in enumerate(pairs):
            h0, h1 = slice(ps.start, ps.start + hd), slice(ps.start + hd, ps.stop)
            v_t = jnp.concatenate([v[:, h0], v[:, h1]], axis=0).T.astype(BF16)
            state_in.append(q_in[:, ps])
            state_upd.append((base + p, e_tot[:, ps], v_t, block_diag(k_out[:, h0], k_out[:, h1])))
            if o_ref is None:
                continue
            for i in range(c // HG_SUB):
                r0 = i * HG_SUB
                rows = slice(r0, r0 + HG_SUB)
                cols = slice(r0, c) if reverse else slice(0, r0 + HG_SUB)
                mid = r0 + HG_SUB // 2 if reverse else r0 + HG_SUB // 2 - 1
                m = cum[mid:mid + 1, ps]
                q_t = (q[rows, ps] * jnp.exp2(cum[rows, ps] - m)).astype(BF16)
                k_t = (k[cols, ps] * jnp.exp2(m - cum[cols, ps])).astype(BF16)
                intra.append((o_ref, row0, rows, cols, ps, reverse, len(state_in) - 1, q_t,
                              block_diag(k_t[:, :hd], k_t[:, hd:]),
                              block_diag(v_b[cols, h0], v_b[cols, h1])))

    scores = [_dot_nt(item[7], item[8]) for item in intra]
    kv = [_dot(v_t, k_bd) for _, _, v_t, k_bd in state_upd]
    states = {}
    inter = []
    for n, (idx, e_tot_p, _, _) in enumerate(state_upd):
        st = states[idx] if idx in states else st_ref[idx]
        if intra:
            st_b = st.astype(BF16)
            inter.append(_dot_nt(state_in[n], block_diag(st_b[:, :hd], st_b[:, hd:])))
        states[idx] = st * e_tot_p + kv[n]
    for idx, st in states.items():
        st_ref[idx] = st

    for (o_ref, row0, rows, cols, ps, reverse, chain, _, _, v_bd), a in zip(intra, scores):
        n_cols = cols.stop - cols.start
        row_g = rows.start + lax.broadcasted_iota(jnp.int32, (HG_SUB, 2 * n_cols), 0)
        lane = lax.broadcasted_iota(jnp.int32, (HG_SUB, 2 * n_cols), 1)
        col_g = cols.start + jnp.where(lane >= n_cols, lane - n_cols, lane)
        a = jnp.where((col_g >= row_g) if reverse else (col_g <= row_g), a, 0.0)
        o = _dot(a.astype(BF16), v_bd) + inter[chain][rows]
        o_ref[0, pl.ds(row0 + rows.start, HG_SUB), ps] = o.astype(o_ref.dtype)


def _hgrn_kernel(*refs, has_s0, emit_o):
    it = iter(refs)
    fwd_refs = [next(it) for _ in range(3)]
    bwd_refs = [next(it) for _ in range(3)]
    qf_ref = fwd_refs[0]
    s0_ref = next(it) if has_s0 else None
    of_ref = next(it) if emit_o else None
    ob_ref = next(it) if emit_o else None
    sout_ref = None if emit_o else next(it)
    st_ref = next(it)
    j = pl.program_id(1)
    n_inner = qf_ref.shape[1] // HG_CHUNK

    @pl.when(j == 0)
    def _():
        if has_s0:
            st_ref[...] = s0_ref[0]
        else:
            st_ref[...] = jnp.zeros_like(st_ref)

    span = HG_UNROLL * HG_CHUNK

    def body(it, carry):
        dirs = []
        for jj in range(HG_UNROLL):
            rf = pl.multiple_of(it * span, span) + jj * HG_CHUNK
            rb = pl.multiple_of((n_inner // HG_UNROLL - 1 - it) * span, span) + (HG_UNROLL - 1 - jj) * HG_CHUNK
            sf, sb = pl.ds(rf, HG_CHUNK), pl.ds(rb, HG_CHUNK)
            dirs.append(tuple(r[0, sf, :] for r in fwd_refs) + (False, of_ref, rf, 0))
            dirs.append(tuple(r[0, sb, :] for r in bwd_refs) + (True, ob_ref, rb, HG_HEADS // 2))
        _hgrn_step(dirs, st_ref)
        return carry

    lax.fori_loop(0, n_inner // HG_UNROLL, body, 0)

    if not emit_o:
        @pl.when(j == pl.num_programs(1) - 1)
        def _():
            sout_ref[0] = st_ref[...]


def _hgrn(u, s0, *, emit_o):
    b, l, _ = u.shape
    c = min(HG_BLOCK, l)
    n = l // c
    blk = (1, c, HG_DIM)
    fwd = lambda col: pl.BlockSpec(blk, lambda i, j: (i, j, col))
    bwd = lambda col: pl.BlockSpec(blk, lambda i, j: (i, n - 1 - j, col))
    st_dims = (HG_HEADS, HG_HEAD_DIM, 2 * HG_HEAD_DIM)
    state_spec = pl.BlockSpec((1,) + st_dims, lambda i, j: (i, 0, 0, 0))
    state_shape = jax.ShapeDtypeStruct((b,) + st_dims, F32)
    in_specs = [fwd(0), fwd(1), fwd(3), bwd(0), bwd(2), bwd(3)]
    args = [u, u, u, u, u, u]
    if s0 is not None:
        in_specs.append(state_spec)
        args.append(s0)
    if emit_o:
        o_shape = jax.ShapeDtypeStruct((b, l, HG_DIM), BF16)
        out_shape = (o_shape, o_shape)
        out_specs = (pl.BlockSpec(blk, lambda i, j: (i, j, 0)),
                     pl.BlockSpec(blk, lambda i, j: (i, n - 1 - j, 0)))
    else:
        out_shape = state_shape
        out_specs = state_spec
    return pl.pallas_call(
        functools.partial(_hgrn_kernel, has_s0=s0 is not None, emit_o=emit_o),
        grid=(b, n),
        in_specs=in_specs,
        out_specs=out_specs,
        out_shape=out_shape,
        scratch_shapes=[pltpu.VMEM(st_dims, F32)],
        compiler_params=_params("arbitrary", "arbitrary"),
        name="hgrn_x" if emit_o else "hgrn_ctx",
    )(*args)


FNET_PAD = 2 * SUBLANES


def _dft_tables(seq):
    gd = FNET_GROUP_DIM
    kc = (np.arange(gd)[:, None] * np.arange(gd)[None, :]) % gd
    ang_c = 2.0 * np.pi * kc / gd
    scale = 1.0 / np.sqrt(float(seq) * gd)
    chan = np.concatenate([np.cos(ang_c), np.sin(ang_c)], axis=1) * scale
    half = seq // 2
    kl = (np.arange(half + FNET_PAD)[:, None] * np.arange(half)[None, :]) % seq
    ang_l = 2.0 * np.pi * kl / seq
    keep = (np.arange(half + FNET_PAD) <= half)[:, None]
    pos_c = np.where(keep, np.cos(ang_l), 0.0)
    pos_s = np.where(keep, -np.sin(ang_l), 0.0)
    return chan.astype(np.float32), pos_c.astype(np.float32), pos_s.astype(np.float32)


def _mirror_gather(width):
    sub = lax.broadcasted_iota(jnp.int32, (SUBLANES, width), 0)
    return jnp.where(sub == 0, 0, SUBLANES - sub), sub


def _row_tile(v, k):
    return v[SUBLANES * k:SUBLANES * (k + 1), :]


def _fold_positions(v, sign):
    seq = v.shape[0]
    n_tiles = seq // SUBLANES
    gather, sub = _mirror_gather(v.shape[1])
    tile = lambda k: _row_tile(v, k)
    out = []
    for k in range(n_tiles // 2):
        tail = jnp.take_along_axis(tile(n_tiles - 1 - k), gather, axis=0)
        if k == 0:
            mirror = jnp.where(sub == 0, 0.0, tail)
        else:
            mirror = jnp.where(sub == 0, jnp.take_along_axis(tile(n_tiles - k), gather, axis=0), tail)
        out.append(tile(k) + mirror if sign > 0 else tile(k) - mirror)
    return jnp.concatenate(out, axis=0)


def _fnet_kernel(u_ref, chan_ref, posc_ref, poss_ref, y_ref, pq_ref):
    seq = u_ref.shape[1]
    half = seq // 2
    gd = FNET_GROUP_DIM
    mids = []
    for g in range(FNET_GROUPS):
        gs = slice(g * gd, (g + 1) * gd)
        pq = _dot(u_ref[0, :, gs], chan_ref[...])
        pq_ref[0:half, gs] = _fold_positions(pq[:, :gd], 1).astype(BF16)
        pq_ref[half:seq, gs] = _fold_positions(pq[:, gd:], -1).astype(BF16)
        mids.append(pq[half:half + 1, :gd])
    gather, sub = _mirror_gather(FNET_DIM)
    mid = jnp.concatenate(mids, axis=1)
    mid_alt = jnp.where(sub % 2 == 0, mid, -mid)

    a = _dot(posc_ref[...], pq_ref[0:half, :])
    bm = _dot(poss_ref[...], pq_ref[half:seq, :])
    tile = _row_tile
    s = SUBLANES
    n_t = half // s
    for k in range(0, n_t, 2):
        pair = [tile(a, k + i) + tile(bm, k + i) + mid_alt for i in range(2)]
        y_ref[0, s * k:s * (k + 2), :] = jnp.concatenate(pair, axis=0).astype(BF16)
    flipped = [jnp.take_along_axis(tile(a, k) - tile(bm, k) + mid_alt, gather, axis=0) for k in range(n_t)]
    flipped.append(jnp.take_along_axis(tile(a, n_t) + tile(bm, n_t) + mid_alt, gather, axis=0))
    n_all = seq // s
    for m in range(n_t, n_all, 2):
        pair = [jnp.where(sub == 0, flipped[n_all - mm], flipped[n_all - 1 - mm]) for mm in (m, m + 1)]
        y_ref[0, s * m:s * (m + 2), :] = jnp.concatenate(pair, axis=0).astype(BF16)


def _fnet(u, chan, pos_c, pos_s):
    b, seq, _ = u.shape
    table = pl.BlockSpec(pos_c.shape, lambda i: (0, 0))
    return pl.pallas_call(
        _fnet_kernel,
        grid=(b,),
        in_specs=[pl.BlockSpec((1, seq, FNET_DIM), lambda i: (i, 0, 0)),
                  pl.BlockSpec((FNET_GROUP_DIM, 2 * FNET_GROUP_DIM), lambda i: (0, 0)),
                  table, table],
        out_specs=pl.BlockSpec((1, seq, FNET_DIM), lambda i: (i, 0, 0)),
        out_shape=jax.ShapeDtypeStruct((b, seq, FNET_DIM), BF16),
        scratch_shapes=[pltpu.VMEM((seq, FNET_DIM), BF16)],
        compiler_params=_params("arbitrary"),
        name="fnet",
    )(u, chan, pos_c, pos_s)


def _merge_kernel(x_ref, of_ref, ob_ref, sg_ref, ga_ref, gb_ref, yf_ref, mod_ref, og_ref, n2g_ref,
                  wa_ref, wb_ref, wo_ref, x1_ref, h2_ref):
    n_sub = x_ref.shape[1] // MERGE_SUB

    def mix(rows):
        o = of_ref[0, rows, :].astype(F32) + ob_ref[0, rows, :].astype(F32)
        parts = []
        for h in range(HG_HEADS):
            oh = o[:, h * HG_HEAD_DIM:(h + 1) * HG_HEAD_DIM]
            parts.append(oh * lax.rsqrt(jnp.mean(oh * oh, axis=-1, keepdims=True) + EPS))
        on = jnp.concatenate(parts, axis=-1) * og_ref[...] * sg_ref[0, rows, :].astype(F32)
        y_b = _dot(on.astype(BF16), wb_ref[...])
        y_a = _dot(yf_ref[0, rows, :], wa_ref[...])
        m = ga_ref[0, rows, :].astype(F32) * y_a + gb_ref[0, rows, :].astype(F32) * y_b
        return _dot(m.astype(BF16), wo_ref[...])

    def finish(rows, yx):
        x1 = x_ref[0, rows, :] + mod_ref[0, 2:3, :] * yx
        x1_ref[0, rows, :] = x1
        h2 = _modulated_norm(x1, n2g_ref[...], mod_ref[0, 3:4, :], mod_ref[0, 4:5, :])
        h2_ref[0, rows, :] = h2.astype(BF16)

    pending = None
    for i in range(n_sub):
        rows = slice(i * MERGE_SUB, (i + 1) * MERGE_SUB)
        yx = mix(rows)
        if pending is not None:
            finish(*pending)
        pending = (rows, yx)
    finish(*pending)


def _merge(x, o_f, o_b, gates, y_f, mod, og, n2g, w_a, w_b, w_out, *, tm):
    b, l, d = x.shape
    tok = lambda w, col: pl.BlockSpec((1, tm, w), lambda i, j: (i, j, col))
    const = lambda shape: pl.BlockSpec(shape, lambda i, j: tuple(0 for _ in shape))
    return pl.pallas_call(
        _merge_kernel,
        grid=(b, l // tm),
        in_specs=[tok(d, 0), tok(HG_DIM, 0), tok(HG_DIM, 0), tok(HG_DIM, 2 * d // HG_DIM),
                  tok(d, 0), tok(d, 1), tok(FNET_DIM, 0),
                  pl.BlockSpec((1, N_MOD, d), lambda i, j: (i, 0, 0)),
                  const((1, HG_DIM)), const((1, d)),
                  const((FNET_DIM, d)), const((HG_DIM, d)), const((d, d))],
        out_specs=(tok(d, 0), tok(d, 0)),
        out_shape=(jax.ShapeDtypeStruct((b, l, d), F32), jax.ShapeDtypeStruct((b, l, d), BF16)),
        compiler_params=_params("arbitrary", "arbitrary"),
        name="merge",
    )(x, o_f, o_b, gates, gates, gates, y_f, mod, og, n2g, w_a, w_b, w_out)


assert GRID_W == SUBLANES * SUBLANES
FFN_GROUP = 8


def _ffn_permute_tokens(h_ref, hp_ref, tmp_ref):
    n_tok = tmp_ref.shape[1]
    s = SUBLANES

    def body(g, carry):
        tok = pl.ds(pl.multiple_of(g * n_tok, n_tok), n_tok)
        hf = h_ref[0, tok, :].astype(F32)
        for lt in range(tmp_ref.shape[0]):
            for r in range(n_tok // GRID_W):
                for a in range(s):
                    r0 = r * GRID_W + s * a
                    tmp_ref[lt, pl.ds(r * GRID_W + a, s, stride=s), :] = (
                        hf[r0:r0 + s, lt * LANES:(lt + 1) * LANES])
        for lt in range(tmp_ref.shape[0]):
            hp_ref[tok, lt * LANES:(lt + 1) * LANES] = tmp_ref[lt].astype(BF16)
        return carry

    lax.fori_loop(0, h_ref.shape[1] // n_tok, body, 0)


def _ffn_up(hp_ref, up_refs, row0, n_rows):
    h = hp_ref[row0 * GRID_W:(row0 + n_rows) * GRID_W, :]
    return [_dot(h, up_ref[...]) for up_ref in up_refs]


def _pair(lo, hi):
    return jnp.concatenate([lo, hi], axis=0).astype(BF16)


def _ffn_row(z, r, taps, bias, first, last, sub):
    s = SUBLANES
    rows_b = lambda b: slice(r * GRID_W + s * b, r * GRID_W + s * b + s)
    lo = [z[rows_b(b), 0:LANES] for b in range(s)]
    hi = [z[rows_b(b), LANES:2 * LANES] for b in range(s)]
    mid = [_pair(lo[b], hi[b]) for b in range(s)]
    down = lambda v: jnp.where(sub == 0, 0.0, pltpu.roll(v, 1, axis=0))
    left = [_pair(down(lo[s - 1]), down(hi[s - 1]))] + mid[0:s - 1]
    up = lambda v: jnp.where(sub == s - 1, 0.0, pltpu.roll(v, s - 1, axis=0))
    right = mid[1:s] + [_pair(up(lo[0]), up(hi[0]))]
    term = lambda kh: [left[b] * taps[3 * kh] + mid[b] * taps[3 * kh + 1] + right[b] * taps[3 * kh + 2]
                       for b in range(s)]
    above = None if first else term(2)
    here = term(1)
    below = None if last else [t + bias for t in term(0)]
    return above, here, below


def _ffn_down(a_ref, dn_ref, y_ref, row0, n_rows):
    tok = slice(row0 * GRID_W, (row0 + n_rows) * GRID_W)
    part = _dot(a_ref[tok, :].astype(BF16), dn_ref[...])
    for j in range(y_ref.shape[1]):
        y_ref[0, j, tok, :] += part[:, j * LANES:(j + 1) * LANES]


def _ffn_kernel(h_ref, up1_ref, up2_ref, cw1_ref, cw2_ref, cb1_ref, cb2_ref, dn_ref, y_ref,
                hp_ref, a_ref, wp_ref, tmp_ref, wup_ref, wdn_ref):
    t = pl.program_id(1)
    rows = hp_ref.shape[0] // GRID_W
    grp = FFN_GROUP
    s = SUBLANES

    @pl.when(t == 0)
    def _():
        _ffn_permute_tokens(h_ref, hp_ref, tmp_ref)
        y_ref[...] = jnp.zeros_like(y_ref)

    wup_ref[0] = up1_ref[...].astype(BF16)
    wup_ref[1] = up2_ref[...].astype(BF16)
    wdn_ref[...] = dn_ref[...].astype(BF16)
    up_refs = (wup_ref.at[0], wup_ref.at[1])

    for half, (cw_ref, cb_ref) in enumerate(((cw1_ref, cb1_ref), (cw2_ref, cb2_ref))):
        for k in range(10):
            row = cw_ref[k:k + 1, :] if k < 9 else cb_ref[...]
            wp_ref[half, k] = _pair(jnp.broadcast_to(row[:, 0:LANES], (s, LANES)),
                                    jnp.broadcast_to(row[:, LANES:2 * LANES], (s, LANES)))
    taps = [[wp_ref[half, k] for k in range(9)] for half in range(2)]
    bias = [wp_ref[half, 9] for half in range(2)]
    sub = lax.broadcasted_iota(jnp.int32, (s, LANES), 0)

    def gate(c, out_row):
        for b in range(s):
            act = (_silu(c[0][b]) * c[1][b]).astype(F32)
            tok_b = slice(out_row * GRID_W + s * b, out_row * GRID_W + s * b + s)
            a_ref[tok_b, 0:LANES] = act[0:s]
            a_ref[tok_b, LANES:2 * LANES] = act[s:2 * s]

    prev = [None, None]
    cur = [[bias[half]] * s for half in range(2)]
    z_next = _ffn_up(hp_ref, up_refs, 0, grp)
    for row0 in range(0, rows, grp):
        z_cur = z_next
        if row0 + grp < rows:
            z_next = _ffn_up(hp_ref, up_refs, row0 + grp, grp)
        for rr in range(grp):
            r = row0 + rr
            done = []
            for half in range(2):
                above, here, below = _ffn_row(z_cur[half], rr, taps[half], bias[half],
                                              r == 0, r == rows - 1, sub)
                if above is not None:
                    done.append([prev[half][b] + above[b] for b in range(s)])
                prev[half] = [cur[half][b] + here[b] for b in range(s)]
                cur[half] = below
            if done:
                gate(done, r - 1)
            if rr == 0 and row0 > 0:
                _ffn_down(a_ref, wdn_ref, y_ref, row0 - grp, grp)
    gate(prev, rows - 1)
    _ffn_down(a_ref, wdn_ref, y_ref, rows - grp, grp)


def _ffn(h2, up, conv_w, conv_b, down):
    b, l, d = h2.shape
    rows = l // GRID_W
    tc = FFN_TILE
    nt = D_FF // tc
    assert rows % FFN_GROUP == 0 and tc == 2 * LANES, "the conv pairs the two lane tiles of a channel tile"
    return pl.pallas_call(
        _ffn_kernel,
        grid=(b, nt),
        in_specs=[pl.BlockSpec((1, l, d), lambda i, t: (i, 0, 0)),
                  pl.BlockSpec((d, tc), lambda i, t: (0, t)),
                  pl.BlockSpec((d, tc), lambda i, t: (0, nt + t)),
                  pl.BlockSpec((9, tc), lambda i, t: (0, t)),
                  pl.BlockSpec((9, tc), lambda i, t: (0, nt + t)),
                  pl.BlockSpec((1, tc), lambda i, t: (0, t)),
                  pl.BlockSpec((1, tc), lambda i, t: (0, nt + t)),
                  pl.BlockSpec((tc, d), lambda i, t: (t, 0))],
        out_specs=pl.BlockSpec((1, d // LANES, l, LANES), lambda i, t: (i, 0, 0, 0)),
        out_shape=jax.ShapeDtypeStruct((b, d // LANES, l, LANES), F32),
        scratch_shapes=[pltpu.VMEM((l, d), BF16),
                        pltpu.VMEM((l, tc), F32),
                        pltpu.VMEM((2, 10, 2 * SUBLANES, LANES), BF16),
                        pltpu.VMEM((d // LANES, FFN_GROUP * GRID_W, LANES), F32),
                        pltpu.VMEM((2, d, tc), BF16),
                        pltpu.VMEM((tc, d), BF16)],
        compiler_params=_params("arbitrary", "arbitrary"),
        name="ffn",
    )(h2, up, up, conv_w, conv_w, conv_b, conv_b, down)


def _final_kernel(x1_ref, y_ref, mod_ref, g_ref, o_ref):
    n_slab = y_ref.shape[1]
    s = SUBLANES
    for r in range(y_ref.shape[2] // GRID_W):
        for a in range(s):
            tok = slice(r * GRID_W + s * a, r * GRID_W + s * a + s)
            y = jnp.concatenate([y_ref[0, j, pl.ds(r * GRID_W + a, s, stride=s), :]
                                 for j in range(n_slab)], axis=-1)
            x = x1_ref[0, tok, :] + mod_ref[0, 5:6, :] * y
            ms = jnp.mean(x * x, axis=-1, keepdims=True)
            o_ref[0, tok, :] = x * lax.rsqrt(ms + EPS) * g_ref[...]


def _final(x1, y, mod, g, *, tm):
    b, l, d = x1.shape
    tok = pl.BlockSpec((1, tm, d), lambda i, j: (i, j, 0))
    return pl.pallas_call(
        _final_kernel,
        grid=(b, l // tm),
        in_specs=[tok, pl.BlockSpec((1, d // LANES, tm, LANES), lambda i, j: (i, 0, j, 0)),
                  pl.BlockSpec((1, N_MOD, d), lambda i, j: (i, 0, 0)),
                  pl.BlockSpec((1, d), lambda i, j: (0, 0))],
        out_specs=tok,
        out_shape=jax.ShapeDtypeStruct((b, l, d), F32),
        compiler_params=_params("arbitrary", "arbitrary"),
        name="final",
    )(x1, y, mod, g)


def kernel(x, c, ctx, c_ctx, ada_w, ada_b, norm1_g, w_in, hg_lb, hg_onorm_g, w_a, w_b, w_out,
           norm2_g, ffn_up, ffn_conv_w, ffn_conv_b, ffn_down, final_g):
    b, seq, d = x.shape
    assert ada_w.shape[0] == 1, "single-layer kernel"
    assert (d, seq % GRID_W) == (D_MODEL, 0)

    lb = jnp.cumsum(jax.nn.softmax(hg_lb.astype(F32), axis=0), axis=0)[0].reshape(1, HG_DIM)

    n_rows = 16
    rows = jnp.concatenate([c, c_ctx[None, :], jnp.zeros((n_rows - b - 1, d), F32)], axis=0)
    mod = _ada(rows, ada_w[0], ada_b[0])
    mod_x = mod[:b].reshape(b, N_MOD, d)
    mod_c = mod[b:b + 1].reshape(1, N_MOD, d)

    w_in_b = w_in[0].astype(BF16)
    g1 = norm1_g[0].reshape(1, d)
    hg_x, u_f, gates = _inproj(x, mod_x, g1, lb, w_in_b, tm=1024, full=True)
    (hg_c,) = _inproj(ctx.reshape(1, -1, d), mod_c, g1, lb, w_in_b, tm=1024, full=False)
    hg_c = hg_c.reshape(b, ctx.shape[1], 4 * HG_DIM)

    s_ctx = _hgrn(hg_c, None, emit_o=False)
    o_f, o_b = _hgrn(hg_x, s_ctx, emit_o=True)

    chan, pos_c, pos_s = (jnp.asarray(t).astype(BF16) for t in _dft_tables(seq))
    y_f = _fnet(u_f, chan, pos_c, pos_s)

    x1, h2 = _merge(x, o_f, o_b, gates, y_f, mod_x, hg_onorm_g[0].reshape(1, HG_DIM),
                    norm2_g[0].reshape(1, d), w_a[0].astype(BF16), w_b[0].astype(BF16),
                    w_out[0].astype(BF16), tm=1024)

    y = _ffn(h2, ffn_up[0], ffn_conv_w[0].reshape(9, 2 * D_FF),
             ffn_conv_b[0].reshape(1, 2 * D_FF), ffn_down[0])
    return _final(x1, y, mod_x, final_g.reshape(1, d), tm=1024)
```

```python
import functools

import numpy as np
import jax
import jax.numpy as jnp
from jax import lax
from jax.experimental import pallas as pl
from jax.experimental.pallas import tpu as pltpu

D_MODEL = 1024
GRID_W = 64
FNET_DIM = 512
FNET_GROUPS = 4
FNET_GROUP_DIM = FNET_DIM // FNET_GROUPS
HG_DIM = 512
HG_HEADS = 4
HG_HEAD_DIM = HG_DIM // HG_HEADS
D_FF = 2816
N_MOD = 6
EPS = 1e-6

COL_FNET, COL_Q, COL_FF, COL_FB, COL_I, COL_G = 0, 1, 2, 3, 4, 5

INPROJ_SUB = 256
MERGE_SUB = 256
HG_BLOCK = 1024
HG_CHUNK = 64
HG_UNROLL = 4
HG_SUB = 16
FFN_TILE = 256

BF16 = jnp.bfloat16
F32 = jnp.float32
LANES = 128
SUBLANES = 8
VMEM_LIMIT = 56 * 1024 * 1024


def _sigmoid(x):
    return 1.0 / (1.0 + jnp.exp(-x))


def _silu(x):
    return x * _sigmoid(x)


def _dot(a, b):
    return jnp.dot(a, b, preferred_element_type=F32)


def _dot_nt(a, b):
    return lax.dot_general(a, b, (((1,), (1,)), ((), ())), preferred_element_type=F32)


def _params(*sem, flags=None):
    return pltpu.CompilerParams(dimension_semantics=sem, vmem_limit_bytes=VMEM_LIMIT, flags=flags)


def _ada_kernel(c_ref, w_ref, b_ref, o_ref):
    a = _silu(c_ref[...])
    o_ref[...] = _dot(a.astype(BF16), w_ref[...].astype(BF16)) + b_ref[...]


def _ada(rows, ada_w, ada_b):
    m, d = rows.shape
    n = ada_w.shape[1]
    tn = 1536
    return pl.pallas_call(
        _ada_kernel,
        grid=(n // tn,),
        in_specs=[pl.BlockSpec((m, d), lambda j: (0, 0)),
                  pl.BlockSpec((d, tn), lambda j: (0, j)),
                  pl.BlockSpec((1, tn), lambda j: (0, j))],
        out_specs=pl.BlockSpec((m, tn), lambda j: (0, j)),
        out_shape=jax.ShapeDtypeStruct((m, n), F32),
        compiler_params=_params("arbitrary"),
        name="ada",
    )(rows, ada_w, ada_b.reshape(1, n))


def _modulated_norm(x, g, shift, scale):
    ms = jnp.mean(x * x, axis=-1, keepdims=True)
    return (x * lax.rsqrt(ms + EPS) * g) * (1.0 + scale) + shift


def _inproj_kernel(x_ref, mod_ref, g_ref, lb_ref, w_ref, *o_refs, hg_col0, full):
    lb = lb_ref[...]
    hg_ref = o_refs[0]
    n_sub = x_ref.shape[1] // INPROJ_SUB

    def normed(i):
        rows = slice(i * INPROJ_SUB, (i + 1) * INPROJ_SUB)
        return _modulated_norm(x_ref[0, rows, :], g_ref[...], mod_ref[0, 0:1, :],
                               mod_ref[0, 1:2, :]).astype(BF16)

    hb_next = normed(0)
    for i in range(n_sub):
        hb = hb_next
        rows = slice(i * INPROJ_SUB, (i + 1) * INPROJ_SUB)

        def proj(col, width=HG_DIM):
            return _dot(hb, w_ref[:, col * HG_DIM:col * HG_DIM + width])

        hg_ref[0, rows, 0:HG_DIM] = _silu(proj(hg_col0))
        if i + 1 < n_sub:
            hb_next = normed(i + 1)
        for n in (1, 2):
            f = lb + (1.0 - lb) * _sigmoid(proj(hg_col0 + n))
            hg_ref[0, rows, n * HG_DIM:(n + 1) * HG_DIM] = jnp.log2(f)
        hg_ref[0, rows, 3 * HG_DIM:4 * HG_DIM] = proj(hg_col0 + 3)
        if full:
            o_refs[1][0, rows, :] = proj(COL_FNET).astype(BF16)


def _inproj(x, mod, g, lb, w, *, tm, full):
    b, l, d = x.shape
    n = COL_G * HG_DIM
    mod_map = (lambda i, j: (i, 0, 0)) if full else (lambda i, j: (0, 0, 0))
    tok = lambda width, dtype: (pl.BlockSpec((1, tm, width), lambda i, j: (i, j, 0)),
                                jax.ShapeDtypeStruct((b, l, width), dtype))
    outs = [tok(4 * HG_DIM, F32)]
    if full:
        outs += [tok(FNET_DIM, BF16)]
    return pl.pallas_call(
        functools.partial(_inproj_kernel, hg_col0=COL_Q, full=full),
        grid=(b, l // tm),
        in_specs=[pl.BlockSpec((1, tm, d), lambda i, j: (i, j, 0)),
                  pl.BlockSpec((1, N_MOD, d), mod_map),
                  pl.BlockSpec((1, d), lambda i, j: (0, 0)),
                  pl.BlockSpec((1, HG_DIM), lambda i, j: (0, 0)),
                  pl.BlockSpec((d, n), lambda i, j: (0, 0))],
        out_specs=tuple(o[0] for o in outs),
        out_shape=tuple(o[1] for o in outs),
        compiler_params=_params("arbitrary", "arbitrary"),
        name="inproj_x" if full else "inproj_ctx",
    )(x, mod, g, lb, w)


def _hgrn_step(dirs, st_ref):
    c = HG_CHUNK
    hd = HG_HEAD_DIM
    r_idx = lax.broadcasted_iota(jnp.int32, (c, c), 0)
    c_idx = lax.broadcasted_iota(jnp.int32, (c, c), 1)
    pairs = [slice(p * 2 * hd, (p + 1) * 2 * hd) for p in range(HG_HEADS // 2)]

    def block_diag(a0, a1):
        z0, z1 = jnp.zeros(a0.shape, a0.dtype), jnp.zeros(a1.shape, a1.dtype)
        return jnp.concatenate([jnp.concatenate([a0, z0], axis=1),
                                jnp.concatenate([z1, a1], axis=1)], axis=0)

    cums = []
    for q, l2f, v, reverse, o_ref, row0, base in dirs:
        tri = jnp.where((c_idx >= r_idx) if reverse else (c_idx <= r_idx), 1.0, 0.0).astype(BF16)
        hi = l2f.astype(BF16)
        lo = (l2f - hi.astype(F32)).astype(BF16)
        cums.append(_dot(tri, hi) + _dot(tri, lo))

    state_in, state_upd, intra = [], [], []
    for (q, l2f, v, reverse, o_ref, row0, base), cum in zip(dirs, cums):
        k = 1.0 - jnp.exp2(l2f)
        last = 0 if reverse else c - 1
        total = cum[last:last + 1, :]
        q_in = (q * jnp.exp2(cum)).astype(BF16)
        k_out = (k * jnp.exp2(total - cum)).astype(BF16)
        e_tot = jnp.exp2(total)
        v_b = v.astype(BF16)
        for p, ps in enumerate(pairs):
            h0, h1 = slice(ps.start, ps.start + hd), slice(ps.start + hd, ps.stop)
            v_t = jnp.concatenate([v[:, h0], v[:, h1]], axis=0).T.astype(BF16)
            state_in.append(q_in[:, ps])
            state_upd.append((base + p, e_tot[:, ps], v_t, block_diag(k_out[:, h0], k_out[:, h1])))
            if o_ref is None:
                continue
            for i in range(c // HG_SUB):
                r0 = i * HG_SUB
                rows = slice(r0, r0 + HG_SUB)
                cols = slice(r0, c) if reverse else slice(0, r0 + HG_SUB)
                mid = r0 + HG_SUB // 2 if reverse else r0 + HG_SUB // 2 - 1
                m = cum[mid:mid + 1, ps]
                q_t = (q[rows, ps] * jnp.exp2(cum[rows, ps] - m)).astype(BF16)
                k_t = (k[cols, ps] * jnp.exp2(m - cum[cols, ps])).astype(BF16)
                intra.append((o_ref, row0, rows, cols, ps, reverse, len(state_in) - 1, q_t,
                              block_diag(k_t[:, :hd], k_t[:, hd:]),
                              block_diag(v_b[cols, h0], v_b[cols, h1])))

    scores = [_dot_nt(item[7], item[8]) for item in intra]
    kv = [_dot(v_t, k_bd) for _, _, v_t, k_bd in state_upd]
    states = {}
    inter = []
    for n, (idx, e_tot_p, _, _) in enumerate(state_upd):
        st = states[idx] if idx in states else st_ref[idx]
        if intra:
            st_b = st.astype(BF16)
            inter.append(_dot_nt(state_in[n], block_diag(st_b[:, :hd], st_b[:, hd:])))
        states[idx] = st * e_tot_p + kv[n]
    for idx, st in states.items():
        st_ref[idx] = st

    for (o_ref, row0, rows, cols, ps, reverse, chain, _, _, v_bd), a in zip(intra, scores):
        n_cols = cols.stop - cols.start
        row_g = rows.start + lax.broadcasted_iota(jnp.int32, (HG_SUB, 2 * n_cols), 0)
        lane = lax.broadcasted_iota(jnp.int32, (HG_SUB, 2 * n_cols), 1)
        col_g = cols.start + jnp.where(lane >= n_cols, lane - n_cols, lane)
        a = jnp.where((col_g >= row_g) if reverse else (col_g <= row_g), a, 0.0)
        o = _dot(a.astype(BF16), v_bd) + inter[chain][rows]
        o_ref[0, pl.ds(row0 + rows.start, HG_SUB), ps] = o.astype(o_ref.dtype)


def _hgrn_kernel(*refs, has_s0, emit_o):
    it = iter(refs)
    fwd_refs = [next(it) for _ in range(3)]
    bwd_refs = [next(it) for _ in range(3)]
    qf_ref = fwd_refs[0]
    s0_ref = next(it) if has_s0 else None
    of_ref = next(it) if emit_o else None
    ob_ref = next(it) if emit_o else None
    sout_ref = None if emit_o else next(it)
    st_ref = next(it)
    j = pl.program_id(1)
    n_inner = qf_ref.shape[1] // HG_CHUNK

    @pl.when(j == 0)
    def _():
        if has_s0:
            st_ref[...] = s0_ref[0]
        else:
            st_ref[...] = jnp.zeros_like(st_ref)

    span = HG_UNROLL * HG_CHUNK

    def body(it, carry):
        dirs = []
        for jj in range(HG_UNROLL):
            rf = pl.multiple_of(it * span, span) + jj * HG_CHUNK
            rb = pl.multiple_of((n_inner // HG_UNROLL - 1 - it) * span, span) + (HG_UNROLL - 1 - jj) * HG_CHUNK
            sf, sb = pl.ds(rf, HG_CHUNK), pl.ds(rb, HG_CHUNK)
            dirs.append(tuple(r[0, sf, :] for r in fwd_refs) + (False, of_ref, rf, 0))
            dirs.append(tuple(r[0, sb, :] for r in bwd_refs) + (True, ob_ref, rb, HG_HEADS // 2))
        _hgrn_step(dirs, st_ref)
        return carry

    lax.fori_loop(0, n_inner // HG_UNROLL, body, 0)

    if not emit_o:
        @pl.when(j == pl.num_programs(1) - 1)
        def _():
            sout_ref[0] = st_ref[...]


def _hgrn(u, s0, *, emit_o):
    b, l, _ = u.shape
    c = min(HG_BLOCK, l)
    n = l // c
    blk = (1, c, HG_DIM)
    fwd = lambda col: pl.BlockSpec(blk, lambda i, j: (i, j, col))
    bwd = lambda col: pl.BlockSpec(blk, lambda i, j: (i, n - 1 - j, col))
    st_dims = (HG_HEADS, HG_HEAD_DIM, 2 * HG_HEAD_DIM)
    state_spec = pl.BlockSpec((1,) + st_dims, lambda i, j: (i, 0, 0, 0))
    state_shape = jax.ShapeDtypeStruct((b,) + st_dims, F32)
    in_specs = [fwd(0), fwd(1), fwd(3), bwd(0), bwd(2), bwd(3)]
    args = [u, u, u, u, u, u]
    if s0 is not None:
        in_specs.append(state_spec)
        args.append(s0)
    if emit_o:
        o_shape = jax.ShapeDtypeStruct((b, l, HG_DIM), BF16)
        out_shape = (o_shape, o_shape)
        out_specs = (pl.BlockSpec(blk, lambda i, j: (i, j, 0)),
                     pl.BlockSpec(blk, lambda i, j: (i, n - 1 - j, 0)))
    else:
        out_shape = state_shape
        out_specs = state_spec
    return pl.pallas_call(
        functools.partial(_hgrn_kernel, has_s0=s0 is not None, emit_o=emit_o),
        grid=(b, n),
        in_specs=in_specs,
        out_specs=out_specs,
        out_shape=out_shape,
        scratch_shapes=[pltpu.VMEM(st_dims, F32)],
        compiler_params=_params("arbitrary", "arbitrary"),
        name="hgrn_x" if emit_o else "hgrn_ctx",
    )(*args)


FNET_PAD = 2 * SUBLANES


def _dft_tables(seq):
    gd = FNET_GROUP_DIM
    kc = (np.arange(gd)[:, None] * np.arange(gd)[None, :]) % gd
    ang_c = 2.0 * np.pi * kc / gd
    scale = 1.0 / np.sqrt(float(seq) * gd)
    chan = np.concatenate([np.cos(ang_c), np.sin(ang_c)], axis=1) * scale
    half = seq // 2
    kl = (np.arange(half + FNET_PAD)[:, None] * np.arange(half)[None, :]) % seq
    ang_l = 2.0 * np.pi * kl / seq
    keep = (np.arange(half + FNET_PAD) <= half)[:, None]
    pos_c = np.where(keep, np.cos(ang_l), 0.0)
    pos_s = np.where(keep, -np.sin(ang_l), 0.0)
    return chan.astype(np.float32), pos_c.astype(np.float32), pos_s.astype(np.float32)


def _mirror_gather(width):
    sub = lax.broadcasted_iota(jnp.int32, (SUBLANES, width), 0)
    return jnp.where(sub == 0, 0, SUBLANES - sub), sub


def _row_tile(v, k):
    return v[SUBLANES * k:SUBLANES * (k + 1), :]


def _fold_positions(v, sign):
    seq = v.shape[0]
    n_tiles = seq // SUBLANES
    gather, sub = _mirror_gather(v.shape[1])
    tile = lambda k: _row_tile(v, k)
    out = []
    for k in range(n_tiles // 2):
        tail = jnp.take_along_axis(tile(n_tiles - 1 - k), gather, axis=0)
        if k == 0:
            mirror = jnp.where(sub == 0, 0.0, tail)
        else:
            mirror = jnp.where(sub == 0, jnp.take_along_axis(tile(n_tiles - k), gather, axis=0), tail)
        out.append(tile(k) + mirror if sign > 0 else tile(k) - mirror)
    return jnp.concatenate(out, axis=0)


def _fnet_kernel(u_ref, chan_ref, posc_ref, poss_ref, y_ref, pq_ref):
    seq = u_ref.shape[1]
    half = seq // 2
    gd = FNET_GROUP_DIM
    mids = []
    for g in range(FNET_GROUPS):
        gs = slice(g * gd, (g + 1) * gd)
        pq = _dot(u_ref[0, :, gs], chan_ref[...])
        pq_ref[0:half, gs] = _fold_positions(pq[:, :gd], 1).astype(BF16)
        pq_ref[half:seq, gs] = _fold_positions(pq[:, gd:], -1).astype(BF16)
        mids.append(pq[half:half + 1, :gd])
    gather, sub = _mirror_gather(FNET_DIM)
    mid = jnp.concatenate(mids, axis=1)
    mid_alt = jnp.where(sub % 2 == 0, mid, -mid)

    a = _dot(posc_ref[...], pq_ref[0:half, :])
    bm = _dot(poss_ref[...], pq_ref[half:seq, :])
    tile = _row_tile
    s = SUBLANES
    n_t = half // s
    for k in range(0, n_t, 2):
        pair = [tile(a, k + i) + tile(bm, k + i) + mid_alt for i in range(2)]
        y_ref[0, s * k:s * (k + 2), :] = jnp.concatenate(pair, axis=0).astype(BF16)
    flipped = [jnp.take_along_axis(tile(a, k) - tile(bm, k) + mid_alt, gather, axis=0) for k in range(n_t)]
    flipped.append(jnp.take_along_axis(tile(a, n_t) + tile(bm, n_t) + mid_alt, gather, axis=0))
    n_all = seq // s
    for m in range(n_t, n_all, 2):
        pair = [jnp.where(sub == 0, flipped[n_all - mm], flipped[n_all - 1 - mm]) for mm in (m, m + 1)]
        y_ref[0, s * m:s * (m + 2), :] = jnp.concatenate(pair, axis=0).astype(BF16)


def _fnet(u, chan, pos_c, pos_s):
    b, seq, _ = u.shape
    table = pl.BlockSpec(pos_c.shape, lambda i: (0, 0))
    return pl.pallas_call(
        _fnet_kernel,
        grid=(b,),
        in_specs=[pl.BlockSpec((1, seq, FNET_DIM), lambda i: (i, 0, 0)),
                  pl.BlockSpec((FNET_GROUP_DIM, 2 * FNET_GROUP_DIM), lambda i: (0, 0)),
                  table, table],
        out_specs=pl.BlockSpec((1, seq, FNET_DIM), lambda i: (i, 0, 0)),
        out_shape=jax.ShapeDtypeStruct((b, seq, FNET_DIM), BF16),
        scratch_shapes=[pltpu.VMEM((seq, FNET_DIM), BF16)],
        compiler_params=_params("arbitrary"),
        name="fnet",
    )(u, chan, pos_c, pos_s)


def _merge_kernel(x_ref, of_ref, ob_ref, yf_ref, mod_ref, n1g_ref, og_ref, n2g_ref,
                  wg_ref, wa_ref, wb_ref, wo_ref, x1_ref, h2_ref):
    n_sub = x_ref.shape[1] // MERGE_SUB

    def mix(rows):
        hb = _modulated_norm(x_ref[0, rows, :], n1g_ref[...], mod_ref[0, 0:1, :],
                             mod_ref[0, 1:2, :]).astype(BF16)
        gate_ab = _sigmoid(_dot(hb, wg_ref[:, HG_DIM:HG_DIM + 2 * D_MODEL]))
        sg = _silu(_dot(hb, wg_ref[:, 0:HG_DIM]))
        o = of_ref[0, rows, :].astype(F32) + ob_ref[0, rows, :].astype(F32)
        parts = []
        for h in range(HG_HEADS):
            oh = o[:, h * HG_HEAD_DIM:(h + 1) * HG_HEAD_DIM]
            parts.append(oh * lax.rsqrt(jnp.mean(oh * oh, axis=-1, keepdims=True) + EPS))
        on = jnp.concatenate(parts, axis=-1) * og_ref[...] * sg
        y_b = _dot(on.astype(BF16), wb_ref[...])
        y_a = _dot(yf_ref[0, rows, :], wa_ref[...])
        m = gate_ab[:, 0:D_MODEL] * y_a + gate_ab[:, D_MODEL:2 * D_MODEL] * y_b
        return _dot(m.astype(BF16), wo_ref[...])

    def finish(rows, yx):
        x1 = x_ref[0, rows, :] + mod_ref[0, 2:3, :] * yx
        x1_ref[0, rows, :] = x1
        h2 = _modulated_norm(x1, n2g_ref[...], mod_ref[0, 3:4, :], mod_ref[0, 4:5, :])
        h2_ref[0, rows, :] = h2.astype(BF16)

    pending = None
    for i in range(n_sub):
        rows = slice(i * MERGE_SUB, (i + 1) * MERGE_SUB)
        yx = mix(rows)
        if pending is not None:
            finish(*pending)
        pending = (rows, yx)
    finish(*pending)


def _merge(x, o_f, o_b, y_f, mod, n1g, og, n2g, w_in, w_a, w_b, w_out, *, tm):
    b, l, d = x.shape
    n_gate = HG_DIM + 2 * d
    assert w_in.shape[1] == 2 * n_gate
    tok = lambda w, col: pl.BlockSpec((1, tm, w), lambda i, j: (i, j, col))
    const = lambda shape: pl.BlockSpec(shape, lambda i, j: tuple(0 for _ in shape))
    return pl.pallas_call(
        _merge_kernel,
        grid=(b, l // tm),
        in_specs=[tok(d, 0), tok(HG_DIM, 0), tok(HG_DIM, 0), tok(FNET_DIM, 0),
                  pl.BlockSpec((1, N_MOD, d), lambda i, j: (i, 0, 0)),
                  const((1, d)), const((1, HG_DIM)), const((1, d)),
                  pl.BlockSpec((d, n_gate), lambda i, j: (0, 1)),
                  const((FNET_DIM, d)), const((HG_DIM, d)), const((d, d))],
        out_specs=(tok(d, 0), tok(d, 0)),
        out_shape=(jax.ShapeDtypeStruct((b, l, d), F32), jax.ShapeDtypeStruct((b, l, d), BF16)),
        compiler_params=_params("arbitrary", "arbitrary"),
        name="merge",
    )(x, o_f, o_b, y_f, mod, n1g, og, n2g, w_in, w_a, w_b, w_out)


assert GRID_W == SUBLANES * SUBLANES
FFN_GROUP = 8


def _ffn_permute_tokens(h_ref, hp_ref, tmp_ref):
    n_tok = tmp_ref.shape[1]
    s = SUBLANES

    def body(g, carry):
        tok = pl.ds(pl.multiple_of(g * n_tok, n_tok), n_tok)
        hf = h_ref[0, tok, :].astype(F32)
        for lt in range(tmp_ref.shape[0]):
            for r in range(n_tok // GRID_W):
                for a in range(s):
                    r0 = r * GRID_W + s * a
                    tmp_ref[lt, pl.ds(r * GRID_W + a, s, stride=s), :] = (
                        hf[r0:r0 + s, lt * LANES:(lt + 1) * LANES])
        for lt in range(tmp_ref.shape[0]):
            hp_ref[tok, lt * LANES:(lt + 1) * LANES] = tmp_ref[lt].astype(BF16)
        return carry

    lax.fori_loop(0, h_ref.shape[1] // n_tok, body, 0)


def _ffn_up(hp_ref, up_refs, row0, n_rows):
    h = hp_ref[row0 * GRID_W:(row0 + n_rows) * GRID_W, :]
    return [_dot(h, up_ref[...]) for up_ref in up_refs]


def _pair(lo, hi):
    return jnp.concatenate([lo, hi], axis=0).astype(BF16)


def _ffn_row(z, r, taps, bias, first, last, sub):
    s = SUBLANES
    rows_b = lambda b: slice(r * GRID_W + s * b, r * GRID_W + s * b + s)
    lo = [z[rows_b(b), 0:LANES] for b in range(s)]
    hi = [z[rows_b(b), LANES:2 * LANES] for b in range(s)]
    mid = [_pair(lo[b], hi[b]) for b in range(s)]
    down = lambda v: jnp.where(sub == 0, 0.0, pltpu.roll(v, 1, axis=0))
    left = [_pair(down(lo[s - 1]), down(hi[s - 1]))] + mid[0:s - 1]
    up = lambda v: jnp.where(sub == s - 1, 0.0, pltpu.roll(v, s - 1, axis=0))
    right = mid[1:s] + [_pair(up(lo[0]), up(hi[0]))]
    term = lambda kh: [left[b] * taps[3 * kh] + mid[b] * taps[3 * kh + 1] + right[b] * taps[3 * kh + 2]
                       for b in range(s)]
    above = None if first else term(2)
    here = term(1)
    below = None if last else [t + bias for t in term(0)]
    return above, here, below


def _ffn_down(a_ref, dn_ref, y_ref, row0, n_rows):
    tok = slice(row0 * GRID_W, (row0 + n_rows) * GRID_W)
    part = _dot(a_ref[tok, :].astype(BF16), dn_ref[...])
    for j in range(y_ref.shape[1]):
        y_ref[0, j, tok, :] += part[:, j * LANES:(j + 1) * LANES]


def _ffn_kernel(h_ref, up1_ref, up2_ref, cw1_ref, cw2_ref, cb1_ref, cb2_ref, dn_ref, y_ref,
                hp_ref, a_ref, wp_ref, tmp_ref, wup_ref, wdn_ref):
    t = pl.program_id(1)
    rows = hp_ref.shape[0] // GRID_W
    grp = FFN_GROUP
    s = SUBLANES

    @pl.when(t == 0)
    def _():
        _ffn_permute_tokens(h_ref, hp_ref, tmp_ref)
        y_ref[...] = jnp.zeros_like(y_ref)

    wup_ref[0] = up1_ref[...].astype(BF16)
    wup_ref[1] = up2_ref[...].astype(BF16)
    wdn_ref[...] = dn_ref[...].astype(BF16)
    up_refs = (wup_ref.at[0], wup_ref.at[1])

    for half, (cw_ref, cb_ref) in enumerate(((cw1_ref, cb1_ref), (cw2_ref, cb2_ref))):
        for k in range(10):
            row = cw_ref[k:k + 1, :] if k < 9 else cb_ref[...]
            wp_ref[half, k] = _pair(jnp.broadcast_to(row[:, 0:LANES], (s, LANES)),
                                    jnp.broadcast_to(row[:, LANES:2 * LANES], (s, LANES)))
    taps = [[wp_ref[half, k] for k in range(9)] for half in range(2)]
    bias = [wp_ref[half, 9] for half in range(2)]
    sub = lax.broadcasted_iota(jnp.int32, (s, LANES), 0)

    def gate(c, out_row):
        for b in range(s):
            act = (_silu(c[0][b]) * c[1][b]).astype(F32)
            tok_b = slice(out_row * GRID_W + s * b, out_row * GRID_W + s * b + s)
            a_ref[tok_b, 0:LANES] = act[0:s]
            a_ref[tok_b, LANES:2 * LANES] = act[s:2 * s]

    prev = [None, None]
    cur = [[bias[half]] * s for half in range(2)]
    z_next = _ffn_up(hp_ref, up_refs, 0, grp)
    for row0 in range(0, rows, grp):
        z_cur = z_next
        if row0 + grp < rows:
            z_next = _ffn_up(hp_ref, up_refs, row0 + grp, grp)
        for rr in range(grp):
            r = row0 + rr
            done = []
            for half in range(2):
                above, here, below = _ffn_row(z_cur[half], rr, taps[half], bias[half],
                                              r == 0, r == rows - 1, sub)
                if above is not None:
                    done.append([prev[half][b] + above[b] for b in range(s)])
                prev[half] = [cur[half][b] + here[b] for b in range(s)]
                cur[half] = below
            if done:
                gate(done, r - 1)
            if rr == 0 and row0 > 0:
                _ffn_down(a_ref, wdn_ref, y_ref, row0 - grp, grp)
    gate(prev, rows - 1)
    _ffn_down(a_ref, wdn_ref, y_ref, rows - grp, grp)


def _ffn(h2, up, conv_w, conv_b, down):
    b, l, d = h2.shape
    rows = l // GRID_W
    tc = FFN_TILE
    nt = D_FF // tc
    assert rows % FFN_GROUP == 0 and tc == 2 * LANES, "the conv pairs the two lane tiles of a channel tile"
    return pl.pallas_call(
        _ffn_kernel,
        grid=(b, nt),
        in_specs=[pl.BlockSpec((1, l, d), lambda i, t: (i, 0, 0)),
                  pl.BlockSpec((d, tc), lambda i, t: (0, t)),
                  pl.BlockSpec((d, tc), lambda i, t: (0, nt + t)),
                  pl.BlockSpec((9, tc), lambda i, t: (0, t)),
                  pl.BlockSpec((9, tc), lambda i, t: (0, nt + t)),
                  pl.BlockSpec((1, tc), lambda i, t: (0, t)),
                  pl.BlockSpec((1, tc), lambda i, t: (0, nt + t)),
                  pl.BlockSpec((tc, d), lambda i, t: (t, 0))],
        out_specs=pl.BlockSpec((1, d // LANES, l, LANES), lambda i, t: (i, 0, 0, 0)),
        out_shape=jax.ShapeDtypeStruct((b, d // LANES, l, LANES), F32),
        scratch_shapes=[pltpu.VMEM((l, d), BF16),
                        pltpu.VMEM((l, tc), F32),
                        pltpu.VMEM((2, 10, 2 * SUBLANES, LANES), BF16),
                        pltpu.VMEM((d // LANES, FFN_GROUP * GRID_W, LANES), F32),
                        pltpu.VMEM((2, d, tc), BF16),
                        pltpu.VMEM((tc, d), BF16)],
        compiler_params=_params("arbitrary", "arbitrary"),
        name="ffn",
    )(h2, up, up, conv_w, conv_w, conv_b, conv_b, down)


def _final_kernel(x1_ref, y_ref, mod_ref, g_ref, o_ref):
    n_slab = y_ref.shape[1]
    s = SUBLANES
    for r in range(y_ref.shape[2] // GRID_W):
        for a in range(s):
            tok = slice(r * GRID_W + s * a, r * GRID_W + s * a + s)
            y = jnp.concatenate([y_ref[0, j, pl.ds(r * GRID_W + a, s, stride=s), :]
                                 for j in range(n_slab)], axis=-1)
            x = x1_ref[0, tok, :] + mod_ref[0, 5:6, :] * y
            ms = jnp.mean(x * x, axis=-1, keepdims=True)
            o_ref[0, tok, :] = x * lax.rsqrt(ms + EPS) * g_ref[...]


def _final(x1, y, mod, g, *, tm):
    b, l, d = x1.shape
    tok = pl.BlockSpec((1, tm, d), lambda i, j: (i, j, 0))
    return pl.pallas_call(
        _final_kernel,
        grid=(b, l // tm),
        in_specs=[tok, pl.BlockSpec((1, d // LANES, tm, LANES), lambda i, j: (i, 0, j, 0)),
                  pl.BlockSpec((1, N_MOD, d), lambda i, j: (i, 0, 0)),
                  pl.BlockSpec((1, d), lambda i, j: (0, 0))],
        out_specs=tok,
        out_shape=jax.ShapeDtypeStruct((b, l, d), F32),
        compiler_params=_params("arbitrary", "arbitrary"),
        name="final",
    )(x1, y, mod, g)


def kernel(x, c, ctx, c_ctx, ada_w, ada_b, norm1_g, w_in, hg_lb, hg_onorm_g, w_a, w_b, w_out,
           norm2_g, ffn_up, ffn_conv_w, ffn_conv_b, ffn_down, final_g):
    b, seq, d = x.shape
    assert ada_w.shape[0] == 1, "single-layer kernel"
    assert (d, seq % GRID_W) == (D_MODEL, 0)

    lb = jnp.cumsum(jax.nn.softmax(hg_lb.astype(F32), axis=0), axis=0)[0].reshape(1, HG_DIM)

    n_rows = 16
    rows = jnp.concatenate([c, c_ctx[None, :], jnp.zeros((n_rows - b - 1, d), F32)], axis=0)
    mod = _ada(rows, ada_w[0], ada_b[0])
    mod_x = mod[:b].reshape(b, N_MOD, d)
    mod_c = mod[b:b + 1].reshape(1, N_MOD, d)

    w_in_b = w_in[0].astype(BF16)
    g1 = norm1_g[0].reshape(1, d)
    hg_x, u_f = _inproj(x, mod_x, g1, lb, w_in_b, tm=1024, full=True)
    (hg_c,) = _inproj(ctx.reshape(1, -1, d), mod_c, g1, lb, w_in_b, tm=1024, full=False)
    hg_c = hg_c.reshape(b, ctx.shape[1], 4 * HG_DIM)

    s_ctx = _hgrn(hg_c, None, emit_o=False)
    o_f, o_b = _hgrn(hg_x, s_ctx, emit_o=True)

    chan, pos_c, pos_s = (jnp.asarray(t).astype(BF16) for t in _dft_tables(seq))
    y_f = _fnet(u_f, chan, pos_c, pos_s)

    x1, h2 = _merge(x, o_f, o_b, y_f, mod_x, g1, hg_onorm_g[0].reshape(1, HG_DIM),
                    norm2_g[0].reshape(1, d), w_in_b, w_a[0].astype(BF16), w_b[0].astype(BF16),
                    w_out[0].astype(BF16), tm=1024)

    y = _ffn(h2, ffn_up[0], ffn_conv_w[0].reshape(9, 2 * D_FF),
             ffn_conv_b[0].reshape(1, 2 * D_FF), ffn_down[0])
    return _final(x1, y, mod_x, final_g.reshape(1, d), tm=1024)
```

```python
import functools

import numpy as np
import jax
import jax.numpy as jnp
from jax import lax
from jax.experimental import pallas as pl
from jax.experimental.pallas import tpu as pltpu

D_MODEL = 1024
GRID_W = 64
FNET_DIM = 512
FNET_GROUPS = 4
FNET_GROUP_DIM = FNET_DIM // FNET_GROUPS
HG_DIM = 512
HG_HEADS = 4
HG_HEAD_DIM = HG_DIM // HG_HEADS
D_FF = 2816
N_MOD = 6
EPS = 1e-6

COL_FNET, COL_Q, COL_FF, COL_FB, COL_I, COL_G = 0, 1, 2, 3, 4, 5

INPROJ_SUB = 256
MERGE_SUB = 512
HG_BLOCK = 1024
HG_CHUNK = 64
HG_UNROLL = 4
HG_SUB = 16
FFN_TILE = 256

BF16 = jnp.bfloat16
F32 = jnp.float32
LANES = 128
SUBLANES = 8
VMEM_LIMIT = 56 * 1024 * 1024


def _sigmoid(x):
    return 1.0 / (1.0 + jnp.exp(-x))


def _silu(x):
    return x * _sigmoid(x)


def _dot(a, b):
    return jnp.dot(a, b, preferred_element_type=F32)


def _dot_nt(a, b):
    return lax.dot_general(a, b, (((1,), (1,)), ((), ())), preferred_element_type=F32)


def _params(*sem, flags=None):
    return pltpu.CompilerParams(dimension_semantics=sem, vmem_limit_bytes=VMEM_LIMIT, flags=flags)


def _ada_kernel(c_ref, w_ref, b_ref, o_ref):
    a = _silu(c_ref[...])
    o_ref[...] = _dot(a.astype(BF16), w_ref[...].astype(BF16)) + b_ref[...]


def _ada(rows, ada_w, ada_b):
    m, d = rows.shape
    n = ada_w.shape[1]
    tn = 1536
    return pl.pallas_call(
        _ada_kernel,
        grid=(n // tn,),
        in_specs=[pl.BlockSpec((m, d), lambda j: (0, 0)),
                  pl.BlockSpec((d, tn), lambda j: (0, j)),
                  pl.BlockSpec((1, tn), lambda j: (0, j))],
        out_specs=pl.BlockSpec((m, tn), lambda j: (0, j)),
        out_shape=jax.ShapeDtypeStruct((m, n), F32),
        compiler_params=_params("arbitrary"),
        name="ada",
    )(rows, ada_w, ada_b.reshape(1, n))


def _modulated_norm(x, g, shift, scale):
    ms = jnp.mean(x * x, axis=-1, keepdims=True)
    return (x * lax.rsqrt(ms + EPS) * g) * (1.0 + scale) + shift


def _inproj_kernel(x_ref, mod_ref, g_ref, lb_ref, w_ref, *o_refs, hg_col0, full):
    lb = lb_ref[...]
    hg_ref = o_refs[0]
    n_sub = x_ref.shape[1] // INPROJ_SUB

    def normed(i):
        rows = slice(i * INPROJ_SUB, (i + 1) * INPROJ_SUB)
        return _modulated_norm(x_ref[0, rows, :], g_ref[...], mod_ref[0, 0:1, :],
                               mod_ref[0, 1:2, :]).astype(BF16)

    hb_next = normed(0)
    for i in range(n_sub):
        hb = hb_next
        rows = slice(i * INPROJ_SUB, (i + 1) * INPROJ_SUB)

        def proj(col, width=HG_DIM):
            return _dot(hb, w_ref[:, col * HG_DIM:col * HG_DIM + width])

        hg_ref[0, rows, 0:HG_DIM] = _silu(proj(hg_col0))
        if i + 1 < n_sub:
            hb_next = normed(i + 1)
        for n in (1, 2):
            f = lb + (1.0 - lb) * _sigmoid(proj(hg_col0 + n))
            hg_ref[0, rows, n * HG_DIM:(n + 1) * HG_DIM] = jnp.log2(f)
        hg_ref[0, rows, 3 * HG_DIM:4 * HG_DIM] = proj(hg_col0 + 3)
        if full:
            o_refs[1][0, rows, :] = proj(COL_FNET).astype(BF16)


def _inproj(x, mod, g, lb, w, *, tm, full):
    b, l, d = x.shape
    n = COL_G * HG_DIM
    mod_map = (lambda i, j: (i, 0, 0)) if full else (lambda i, j: (0, 0, 0))
    tok = lambda width, dtype: (pl.BlockSpec((1, tm, width), lambda i, j: (i, j, 0)),
                                jax.ShapeDtypeStruct((b, l, width), dtype))
    outs = [tok(4 * HG_DIM, F32)]
    if full:
        outs += [tok(FNET_DIM, BF16)]
    return pl.pallas_call(
        functools.partial(_inproj_kernel, hg_col0=COL_Q, full=full),
        grid=(b, l // tm),
        in_specs=[pl.BlockSpec((1, tm, d), lambda i, j: (i, j, 0)),
                  pl.BlockSpec((1, N_MOD, d), mod_map),
                  pl.BlockSpec((1, d), lambda i, j: (0, 0)),
                  pl.BlockSpec((1, HG_DIM), lambda i, j: (0, 0)),
                  pl.BlockSpec((d, n), lambda i, j: (0, 0))],
        out_specs=tuple(o[0] for o in outs),
        out_shape=tuple(o[1] for o in outs),
        compiler_params=_params("arbitrary", "arbitrary"),
        name="inproj_x" if full else "inproj_ctx",
    )(x, mod, g, lb, w)


def _hgrn_step(dirs, st_ref):
    c = HG_CHUNK
    hd = HG_HEAD_DIM
    r_idx = lax.broadcasted_iota(jnp.int32, (c, c), 0)
    c_idx = lax.broadcasted_iota(jnp.int32, (c, c), 1)
    pairs = [slice(p * 2 * hd, (p + 1) * 2 * hd) for p in range(HG_HEADS // 2)]

    def block_diag(a0, a1):
        z0, z1 = jnp.zeros(a0.shape, a0.dtype), jnp.zeros(a1.shape, a1.dtype)
        return jnp.concatenate([jnp.concatenate([a0, z0], axis=1),
                                jnp.concatenate([z1, a1], axis=1)], axis=0)

    cums = []
    for q, l2f, v, reverse, o_ref, row0, base in dirs:
        tri = jnp.where((c_idx >= r_idx) if reverse else (c_idx <= r_idx), 1.0, 0.0).astype(BF16)
        hi = l2f.astype(BF16)
        lo = (l2f - hi.astype(F32)).astype(BF16)
        cums.append(_dot(tri, hi) + _dot(tri, lo))

    state_in, state_upd, intra = [], [], []
    for (q, l2f, v, reverse, o_ref, row0, base), cum in zip(dirs, cums):
        k = 1.0 - jnp.exp2(l2f)
        last = 0 if reverse else c - 1
        total = cum[last:last + 1, :]
        q_in = (q * jnp.exp2(cum)).astype(BF16)
        k_out = (k * jnp.exp2(total - cum)).astype(BF16)
        e_tot = jnp.exp2(total)
        v_b = v.astype(BF16)
        for p, ps in enumerate(pairs):
            h0, h1 = slice(ps.start, ps.start + hd), slice(ps.start + hd, ps.stop)
            v_t = jnp.concatenate([v[:, h0], v[:, h1]], axis=0).T.astype(BF16)
            state_in.append(q_in[:, ps])
            state_upd.append((base + p, e_tot[:, ps], v_t, block_diag(k_out[:, h0], k_out[:, h1])))
            if o_ref is None:
                continue
            for i in range(c // HG_SUB):
                r0 = i * HG_SUB
                rows = slice(r0, r0 + HG_SUB)
                cols = slice(r0, c) if reverse else slice(0, r0 + HG_SUB)
                mid = r0 + HG_SUB // 2 if reverse else r0 + HG_SUB // 2 - 1
                m = cum[mid:mid + 1, ps]
                q_t = (q[rows, ps] * jnp.exp2(cum[rows, ps] - m)).astype(BF16)
                k_t = (k[cols, ps] * jnp.exp2(m - cum[cols, ps])).astype(BF16)
                intra.append((o_ref, row0, rows, cols, ps, reverse, len(state_in) - 1, q_t,
                              block_diag(k_t[:, :hd], k_t[:, hd:]),
                              block_diag(v_b[cols, h0], v_b[cols, h1])))

    scores = [_dot_nt(item[7], item[8]) for item in intra]
    kv = [_dot(v_t, k_bd) for _, _, v_t, k_bd in state_upd]
    states = {}
    inter = []
    for n, (idx, e_tot_p, _, _) in enumerate(state_upd):
        st = states[idx] if idx in states else st_ref[idx]
        if intra:
            st_b = st.astype(BF16)
            inter.append(_dot_nt(state_in[n], block_diag(st_b[:, :hd], st_b[:, hd:])))
        states[idx] = st * e_tot_p + kv[n]
    for idx, st in states.items():
        st_ref[idx] = st

    for (o_ref, row0, rows, cols, ps, reverse, chain, _, _, v_bd), a in zip(intra, scores):
        n_cols = cols.stop - cols.start
        row_g = rows.start + lax.broadcasted_iota(jnp.int32, (HG_SUB, 2 * n_cols), 0)
        lane = lax.broadcasted_iota(jnp.int32, (HG_SUB, 2 * n_cols), 1)
        col_g = cols.start + jnp.where(lane >= n_cols, lane - n_cols, lane)
        a = jnp.where((col_g >= row_g) if reverse else (col_g <= row_g), a, 0.0)
        o = _dot(a.astype(BF16), v_bd) + inter[chain][rows]
        o_ref[0, pl.ds(row0 + rows.start, HG_SUB), ps] = o.astype(o_ref.dtype)


def _hgrn_kernel(*refs, has_s0, emit_o):
    it = iter(refs)
    fwd_refs = [next(it) for _ in range(3)]
    bwd_refs = [next(it) for _ in range(3)]
    qf_ref = fwd_refs[0]
    s0_ref = next(it) if has_s0 else None
    of_ref = next(it) if emit_o else None
    ob_ref = next(it) if emit_o else None
    sout_ref = None if emit_o else next(it)
    st_ref = next(it)
    j = pl.program_id(1)
    n_inner = qf_ref.shape[1] // HG_CHUNK

    @pl.when(j == 0)
    def _():
        if has_s0:
            st_ref[...] = s0_ref[0]
        else:
            st_ref[...] = jnp.zeros_like(st_ref)

    span = HG_UNROLL * HG_CHUNK

    def body(it, carry):
        dirs = []
        for jj in range(HG_UNROLL):
            rf = pl.multiple_of(it * span, span) + jj * HG_CHUNK
            rb = pl.multiple_of((n_inner // HG_UNROLL - 1 - it) * span, span) + (HG_UNROLL - 1 - jj) * HG_CHUNK
            sf, sb = pl.ds(rf, HG_CHUNK), pl.ds(rb, HG_CHUNK)
            dirs.append(tuple(r[0, sf, :] for r in fwd_refs) + (False, of_ref, rf, 0))
            dirs.append(tuple(r[0, sb, :] for r in bwd_refs) + (True, ob_ref, rb, HG_HEADS // 2))
        _hgrn_step(dirs, st_ref)
        return carry

    lax.fori_loop(0, n_inner // HG_UNROLL, body, 0)

    if not emit_o:
        @pl.when(j == pl.num_programs(1) - 1)
        def _():
            sout_ref[0] = st_ref[...]


def _hgrn(u, s0, *, emit_o):
    b, l, _ = u.shape
    c = min(HG_BLOCK, l)
    n = l // c
    blk = (1, c, HG_DIM)
    fwd = lambda col: pl.BlockSpec(blk, lambda i, j: (i, j, col))
    bwd = lambda col: pl.BlockSpec(blk, lambda i, j: (i, n - 1 - j, col))
    st_dims = (HG_HEADS, HG_HEAD_DIM, 2 * HG_HEAD_DIM)
    state_spec = pl.BlockSpec((1,) + st_dims, lambda i, j: (i, 0, 0, 0))
    state_shape = jax.ShapeDtypeStruct((b,) + st_dims, F32)
    in_specs = [fwd(0), fwd(1), fwd(3), bwd(0), bwd(2), bwd(3)]
    args = [u, u, u, u, u, u]
    if s0 is not None:
        in_specs.append(state_spec)
        args.append(s0)
    if emit_o:
        o_shape = jax.ShapeDtypeStruct((b, l, HG_DIM), BF16)
        out_shape = (o_shape, o_shape)
        out_specs = (pl.BlockSpec(blk, lambda i, j: (i, j, 0)),
                     pl.BlockSpec(blk, lambda i, j: (i, n - 1 - j, 0)))
    else:
        out_shape = state_shape
        out_specs = state_spec
    return pl.pallas_call(
        functools.partial(_hgrn_kernel, has_s0=s0 is not None, emit_o=emit_o),
        grid=(b, n),
        in_specs=in_specs,
        out_specs=out_specs,
        out_shape=out_shape,
        scratch_shapes=[pltpu.VMEM(st_dims, F32)],
        compiler_params=_params("arbitrary", "arbitrary"),
        name="hgrn_x" if emit_o else "hgrn_ctx",
    )(*args)


FNET_PAD = 2 * SUBLANES


def _dft_tables(seq):
    gd = FNET_GROUP_DIM
    kc = (np.arange(gd)[:, None] * np.arange(gd)[None, :]) % gd
    ang_c = 2.0 * np.pi * kc / gd
    scale = 1.0 / np.sqrt(float(seq) * gd)
    chan = np.concatenate([np.cos(ang_c), np.sin(ang_c)], axis=1) * scale
    half = seq // 2
    kl = (np.arange(half + FNET_PAD)[:, None] * np.arange(half)[None, :]) % seq
    ang_l = 2.0 * np.pi * kl / seq
    keep = (np.arange(half + FNET_PAD) <= half)[:, None]
    pos_c = np.where(keep, np.cos(ang_l), 0.0)
    pos_s = np.where(keep, -np.sin(ang_l), 0.0)
    return chan.astype(np.float32), pos_c.astype(np.float32), pos_s.astype(np.float32)


def _mirror_gather(width):
    sub = lax.broadcasted_iota(jnp.int32, (SUBLANES, width), 0)
    return jnp.where(sub == 0, 0, SUBLANES - sub), sub


def _row_tile(v, k):
    return v[SUBLANES * k:SUBLANES * (k + 1), :]


def _fold_positions(v, sign):
    seq = v.shape[0]
    n_tiles = seq // SUBLANES
    gather, sub = _mirror_gather(v.shape[1])
    tile = lambda k: _row_tile(v, k)
    out = []
    for k in range(n_tiles // 2):
        tail = jnp.take_along_axis(tile(n_tiles - 1 - k), gather, axis=0)
        if k == 0:
            mirror = jnp.where(sub == 0, 0.0, tail)
        else:
            mirror = jnp.where(sub == 0, jnp.take_along_axis(tile(n_tiles - k), gather, axis=0), tail)
        out.append(tile(k) + mirror if sign > 0 else tile(k) - mirror)
    return jnp.concatenate(out, axis=0)


def _fnet_kernel(u_ref, chan_ref, posc_ref, poss_ref, y_ref, pq_ref):
    seq = u_ref.shape[1]
    half = seq // 2
    gd = FNET_GROUP_DIM
    mids = []
    for g in range(FNET_GROUPS):
        gs = slice(g * gd, (g + 1) * gd)
        pq = _dot(u_ref[0, :, gs], chan_ref[...])
        pq_ref[0:half, gs] = _fold_positions(pq[:, :gd], 1).astype(BF16)
        pq_ref[half:seq, gs] = _fold_positions(pq[:, gd:], -1).astype(BF16)
        mids.append(pq[half:half + 1, :gd])
    gather, sub = _mirror_gather(FNET_DIM)
    mid = jnp.concatenate(mids, axis=1)
    mid_alt = jnp.where(sub % 2 == 0, mid, -mid)

    a = _dot(posc_ref[...], pq_ref[0:half, :])
    bm = _dot(poss_ref[...], pq_ref[half:seq, :])
    tile = _row_tile
    s = SUBLANES
    n_t = half // s
    for k in range(0, n_t, 2):
        pair = [tile(a, k + i) + tile(bm, k + i) + mid_alt for i in range(2)]
        y_ref[0, s * k:s * (k + 2), :] = jnp.concatenate(pair, axis=0).astype(BF16)
    flipped = [jnp.take_along_axis(tile(a, k) - tile(bm, k) + mid_alt, gather, axis=0) for k in range(n_t)]
    flipped.append(jnp.take_along_axis(tile(a, n_t) + tile(bm, n_t) + mid_alt, gather, axis=0))
    n_all = seq // s
    for m in range(n_t, n_all, 2):
        pair = [jnp.where(sub == 0, flipped[n_all - mm], flipped[n_all - 1 - mm]) for mm in (m, m + 1)]
        y_ref[0, s * m:s * (m + 2), :] = jnp.concatenate(pair, axis=0).astype(BF16)


def _fnet(u, chan, pos_c, pos_s):
    b, seq, _ = u.shape
    table = pl.BlockSpec(pos_c.shape, lambda i: (0, 0))
    return pl.pallas_call(
        _fnet_kernel,
        grid=(b,),
        in_specs=[pl.BlockSpec((1, seq, FNET_DIM), lambda i: (i, 0, 0)),
                  pl.BlockSpec((FNET_GROUP_DIM, 2 * FNET_GROUP_DIM), lambda i: (0, 0)),
                  table, table],
        out_specs=pl.BlockSpec((1, seq, FNET_DIM), lambda i: (i, 0, 0)),
        out_shape=jax.ShapeDtypeStruct((b, seq, FNET_DIM), BF16),
        scratch_shapes=[pltpu.VMEM((seq, FNET_DIM), BF16)],
        compiler_params=_params("arbitrary"),
        name="fnet",
    )(u, chan, pos_c, pos_s)


def _merge_kernel(x_ref, of_ref, ob_ref, yf_ref, mod_ref, n1g_ref, og_ref, n2g_ref,
                  wg_ref, wa_ref, wb_ref, wo_ref, x1_ref, h2_ref):
    n_sub = x_ref.shape[1] // MERGE_SUB

    def mix(rows):
        hb = _modulated_norm(x_ref[0, rows, :], n1g_ref[...], mod_ref[0, 0:1, :],
                             mod_ref[0, 1:2, :]).astype(BF16)
        gate_ab = _sigmoid(_dot(hb, wg_ref[:, HG_DIM:HG_DIM + 2 * D_MODEL]))
        sg = _silu(_dot(hb, wg_ref[:, 0:HG_DIM]))
        o = of_ref[0, rows, :].astype(F32) + ob_ref[0, rows, :].astype(F32)
        parts = []
        for h in range(HG_HEADS):
            oh = o[:, h * HG_HEAD_DIM:(h + 1) * HG_HEAD_DIM]
            parts.append(oh * lax.rsqrt(jnp.mean(oh * oh, axis=-1, keepdims=True) + EPS))
        on = jnp.concatenate(parts, axis=-1) * og_ref[...] * sg
        y_b = _dot(on.astype(BF16), wb_ref[...])
        y_a = _dot(yf_ref[0, rows, :], wa_ref[...])
        m = gate_ab[:, 0:D_MODEL] * y_a + gate_ab[:, D_MODEL:2 * D_MODEL] * y_b
        return _dot(m.astype(BF16), wo_ref[...])

    def finish(rows, yx):
        x1 = x_ref[0, rows, :] + mod_ref[0, 2:3, :] * yx
        x1_ref[0, rows, :] = x1
        h2 = _modulated_norm(x1, n2g_ref[...], mod_ref[0, 3:4, :], mod_ref[0, 4:5, :])
        h2_ref[0, rows, :] = h2.astype(BF16)

    pending = None
    for i in range(n_sub):
        rows = slice(i * MERGE_SUB, (i + 1) * MERGE_SUB)
        yx = mix(rows)
        if pending is not None:
            finish(*pending)
        pending = (rows, yx)
    finish(*pending)


def _merge(x, o_f, o_b, y_f, mod, n1g, og, n2g, w_in, w_a, w_b, w_out, *, tm):
    b, l, d = x.shape
    n_gate = HG_DIM + 2 * d
    assert w_in.shape[1] == 2 * n_gate
    tok = lambda w, col: pl.BlockSpec((1, tm, w), lambda i, j: (i, j, col))
    const = lambda shape: pl.BlockSpec(shape, lambda i, j: tuple(0 for _ in shape))
    return pl.pallas_call(
        _merge_kernel,
        grid=(b, l // tm),
        in_specs=[tok(d, 0), tok(HG_DIM, 0), tok(HG_DIM, 0), tok(FNET_DIM, 0),
                  pl.BlockSpec((1, N_MOD, d), lambda i, j: (i, 0, 0)),
                  const((1, d)), const((1, HG_DIM)), const((1, d)),
                  pl.BlockSpec((d, n_gate), lambda i, j: (0, 1)),
                  const((FNET_DIM, d)), const((HG_DIM, d)), const((d, d))],
        out_specs=(tok(d, 0), tok(d, 0)),
        out_shape=(jax.ShapeDtypeStruct((b, l, d), F32), jax.ShapeDtypeStruct((b, l, d), BF16)),
        compiler_params=_params("arbitrary", "arbitrary"),
        name="merge",
    )(x, o_f, o_b, y_f, mod, n1g, og, n2g, w_in, w_a, w_b, w_out)


assert GRID_W == SUBLANES * SUBLANES
FFN_GROUP = 8


def _ffn_permute_tokens(h_ref, hp_ref, tmp_ref):
    n_tok = tmp_ref.shape[1]
    s = SUBLANES

    def body(g, carry):
        tok = pl.ds(pl.multiple_of(g * n_tok, n_tok), n_tok)
        hf = h_ref[0, tok, :].astype(F32)
        for lt in range(tmp_ref.shape[0]):
            for r in range(n_tok // GRID_W):
                for a in range(s):
                    r0 = r * GRID_W + s * a
                    tmp_ref[lt, pl.ds(r * GRID_W + a, s, stride=s), :] = (
                        hf[r0:r0 + s, lt * LANES:(lt + 1) * LANES])
        for lt in range(tmp_ref.shape[0]):
            hp_ref[tok, lt * LANES:(lt + 1) * LANES] = tmp_ref[lt].astype(BF16)
        return carry

    lax.fori_loop(0, h_ref.shape[1] // n_tok, body, 0)


def _ffn_up(hp_ref, up_refs, row0, n_rows):
    h = hp_ref[row0 * GRID_W:(row0 + n_rows) * GRID_W, :]
    return [_dot(h, up_ref[...]) for up_ref in up_refs]


def _pair(lo, hi):
    return jnp.concatenate([lo, hi], axis=0).astype(BF16)


def _ffn_row(z, r, taps, bias, first, last, sub):
    s = SUBLANES
    rows_b = lambda b: slice(r * GRID_W + s * b, r * GRID_W + s * b + s)
    lo = [z[rows_b(b), 0:LANES] for b in range(s)]
    hi = [z[rows_b(b), LANES:2 * LANES] for b in range(s)]
    mid = [_pair(lo[b], hi[b]) for b in range(s)]
    down = lambda v: jnp.where(sub == 0, 0.0, pltpu.roll(v, 1, axis=0))
    left = [_pair(down(lo[s - 1]), down(hi[s - 1]))] + mid[0:s - 1]
    up = lambda v: jnp.where(sub == s - 1, 0.0, pltpu.roll(v, s - 1, axis=0))
    right = mid[1:s] + [_pair(up(lo[0]), up(hi[0]))]
    term = lambda kh: [left[b] * taps[3 * kh] + mid[b] * taps[3 * kh + 1] + right[b] * taps[3 * kh + 2]
                       for b in range(s)]
    above = None if first else term(2)
    here = term(1)
    below = None if last else [t + bias for t in term(0)]
    return above, here, below


def _ffn_down(a_ref, dn_ref, y_ref, row0, n_rows):
    tok = slice(row0 * GRID_W, (row0 + n_rows) * GRID_W)
    part = _dot(a_ref[tok, :].astype(BF16), dn_ref[...])
    for j in range(y_ref.shape[1]):
        y_ref[0, j, tok, :] += part[:, j * LANES:(j + 1) * LANES]


def _ffn_kernel(h_ref, up1_ref, up2_ref, cw1_ref, cw2_ref, cb1_ref, cb2_ref, dn_ref, y_ref,
                hp_ref, a_ref, wp_ref, tmp_ref, wup_ref, wdn_ref):
    t = pl.program_id(1)
    rows = hp_ref.shape[0] // GRID_W
    grp = FFN_GROUP
    s = SUBLANES

    @pl.when(t == 0)
    def _():
        _ffn_permute_tokens(h_ref, hp_ref, tmp_ref)
        y_ref[...] = jnp.zeros_like(y_ref)

    wup_ref[0] = up1_ref[...].astype(BF16)
    wup_ref[1] = up2_ref[...].astype(BF16)
    wdn_ref[...] = dn_ref[...].astype(BF16)
    up_refs = (wup_ref.at[0], wup_ref.at[1])

    for half, (cw_ref, cb_ref) in enumerate(((cw1_ref, cb1_ref), (cw2_ref, cb2_ref))):
        for k in range(10):
            row = cw_ref[k:k + 1, :] if k < 9 else cb_ref[...]
            wp_ref[half, k] = _pair(jnp.broadcast_to(row[:, 0:LANES], (s, LANES)),
                                    jnp.broadcast_to(row[:, LANES:2 * LANES], (s, LANES)))
    taps = [[wp_ref[half, k] for k in range(9)] for half in range(2)]
    bias = [wp_ref[half, 9] for half in range(2)]
    sub = lax.broadcasted_iota(jnp.int32, (s, LANES), 0)

    def gate(c, out_row):
        for b in range(s):
            act = (_silu(c[0][b]) * c[1][b]).astype(F32)
            tok_b = slice(out_row * GRID_W + s * b, out_row * GRID_W + s * b + s)
            a_ref[tok_b, 0:LANES] = act[0:s]
            a_ref[tok_b, LANES:2 * LANES] = act[s:2 * s]

    prev = [None, None]
    cur = [[bias[half]] * s for half in range(2)]
    z_next = _ffn_up(hp_ref, up_refs, 0, grp)
    for row0 in range(0, rows, grp):
        z_cur = z_next
        if row0 + grp < rows:
            z_next = _ffn_up(hp_ref, up_refs, row0 + grp, grp)
        for rr in range(grp):
            r = row0 + rr
            done = []
            for half in range(2):
                above, here, below = _ffn_row(z_cur[half], rr, taps[half], bias[half],
                                              r == 0, r == rows - 1, sub)
                if above is not None:
                    done.append([prev[half][b] + above[b] for b in range(s)])
                prev[half] = [cur[half][b] + here[b] for b in range(s)]
                cur[half] = below
            if done:
                gate(done, r - 1)
            if rr == 0 and row0 > 0:
                _ffn_down(a_ref, wdn_ref, y_ref, row0 - grp, grp)
    gate(prev, rows - 1)
    _ffn_down(a_ref, wdn_ref, y_ref, rows - grp, grp)


def _ffn(h2, up, conv_w, conv_b, down):
    b, l, d = h2.shape
    rows = l // GRID_W
    tc = FFN_TILE
    nt = D_FF // tc
    assert rows % FFN_GROUP == 0 and tc == 2 * LANES, "the conv pairs the two lane tiles of a channel tile"
    return pl.pallas_call(
        _ffn_kernel,
        grid=(b, nt),
        in_specs=[pl.BlockSpec((1, l, d), lambda i, t: (i, 0, 0)),
                  pl.BlockSpec((d, tc), lambda i, t: (0, t)),
                  pl.BlockSpec((d, tc), lambda i, t: (0, nt + t)),
                  pl.BlockSpec((9, tc), lambda i, t: (0, t)),
                  pl.BlockSpec((9, tc), lambda i, t: (0, nt + t)),
                  pl.BlockSpec((1, tc), lambda i, t: (0, t)),
                  pl.BlockSpec((1, tc), lambda i, t: (0, nt + t)),
                  pl.BlockSpec((tc, d), lambda i, t: (t, 0))],
        out_specs=pl.BlockSpec((1, d // LANES, l, LANES), lambda i, t: (i, 0, 0, 0)),
        out_shape=jax.ShapeDtypeStruct((b, d // LANES, l, LANES), F32),
        scratch_shapes=[pltpu.VMEM((l, d), BF16),
                        pltpu.VMEM((l, tc), F32),
                        pltpu.VMEM((2, 10, 2 * SUBLANES, LANES), BF16),
                        pltpu.VMEM((d // LANES, FFN_GROUP * GRID_W, LANES), F32),
                        pltpu.VMEM((2, d, tc), BF16),
                        pltpu.VMEM((tc, d), BF16)],
        compiler_params=_params("arbitrary", "arbitrary"),
        name="ffn",
    )(h2, up, up, conv_w, conv_w, conv_b, conv_b, down)


def _final_kernel(x1_ref, y_ref, mod_ref, g_ref, o_ref):
    n_slab = y_ref.shape[1]
    s = SUBLANES
    for r in range(y_ref.shape[2] // GRID_W):
        for a in range(s):
            tok = slice(r * GRID_W + s * a, r * GRID_W + s * a + s)
            y = jnp.concatenate([y_ref[0, j, pl.ds(r * GRID_W + a, s, stride=s), :]
                                 for j in range(n_slab)], axis=-1)
            x = x1_ref[0, tok, :] + mod_ref[0, 5:6, :] * y
            ms = jnp.mean(x * x, axis=-1, keepdims=True)
            o_ref[0, tok, :] = x * lax.rsqrt(ms + EPS) * g_ref[...]


def _final(x1, y, mod, g, *, tm):
    b, l, d = x1.shape
    tok = pl.BlockSpec((1, tm, d), lambda i, j: (i, j, 0))
    return pl.pallas_call(
        _final_kernel,
        grid=(b, l // tm),
        in_specs=[tok, pl.BlockSpec((1, d // LANES, tm, LANES), lambda i, j: (i, 0, j, 0)),
                  pl.BlockSpec((1, N_MOD, d), lambda i, j: (i, 0, 0)),
                  pl.BlockSpec((1, d), lambda i, j: (0, 0))],
        out_specs=tok,
        out_shape=jax.ShapeDtypeStruct((b, l, d), F32),
        compiler_params=_params("arbitrary", "arbitrary"),
        name="final",
    )(x1, y, mod, g)


def kernel(x, c, ctx, c_ctx, ada_w, ada_b, norm1_g, w_in, hg_lb, hg_onorm_g, w_a, w_b, w_out,
           norm2_g, ffn_up, ffn_conv_w, ffn_conv_b, ffn_down, final_g):
    b, seq, d = x.shape
    assert ada_w.shape[0] == 1, "single-layer kernel"
    assert (d, seq % GRID_W) == (D_MODEL, 0)

    lb = jnp.cumsum(jax.nn.softmax(hg_lb.astype(F32), axis=0), axis=0)[0].reshape(1, HG_DIM)

    n_rows = 16
    rows = jnp.concatenate([c, c_ctx[None, :], jnp.zeros((n_rows - b - 1, d), F32)], axis=0)
    mod = _ada(rows, ada_w[0], ada_b[0])
    mod_x = mod[:b].reshape(b, N_MOD, d)
    mod_c = mod[b:b + 1].reshape(1, N_MOD, d)

    w_in_b = w_in[0].astype(BF16)
    g1 = norm1_g[0].reshape(1, d)
    hg_x, u_f = _inproj(x, mod_x, g1, lb, w_in_b, tm=1024, full=True)
    (hg_c,) = _inproj(ctx.reshape(1, -1, d), mod_c, g1, lb, w_in_b, tm=1024, full=False)
    hg_c = hg_c.reshape(b, ctx.shape[1], 4 * HG_DIM)

    s_ctx = _hgrn(hg_c, None, emit_o=False)
    o_f, o_b = _hgrn(hg_x, s_ctx, emit_o=True)

    chan, pos_c, pos_s = (jnp.asarray(t).astype(BF16) for t in _dft_tables(seq))
    y_f = _fnet(u_f, chan, pos_c, pos_s)

    x1, h2 = _merge(x, o_f, o_b, y_f, mod_x, g1, hg_onorm_g[0].reshape(1, HG_DIM),
                    norm2_g[0].reshape(1, d), w_in_b, w_a[0].astype(BF16), w_b[0].astype(BF16),
                    w_out[0].astype(BF16), tm=1024)

    y = _ffn(h2, ffn_up[0], ffn_conv_w[0].reshape(9, 2 * D_FF),
             ffn_conv_b[0].reshape(1, 2 * D_FF), ffn_down[0])
    return _final(x1, y, mod_x, final_g.reshape(1, d), tm=1024)
```
